```python
import math
import jax, jax.numpy as jnp
from jax import lax
import numpy as np

D_MODEL = 1024
BATCH = 16
SEQ = 2048
DEPTH = 2

N_MIXERS = 2
N_ATTN_LAYERS = (DEPTH + 1) // 2
N_SSM_LAYERS = DEPTH // 2

N_HEADS = 16
N_KV_HEADS = 4
HEAD_DIM = 64
GQA_GROUP = N_HEADS // N_KV_HEADS
WINDOW = 128
ATTN_BLOCK = 128
QKV_DIM = (N_HEADS + 2 * N_KV_HEADS) * HEAD_DIM

REL_BUCKETS = 32
REL_MAX_DIST = 128

SSM_EXPAND = 2
D_INNER = SSM_EXPAND * D_MODEL
SSM_HEAD_DIM = 64
SSM_HEADS = D_INNER // SSM_HEAD_DIM
SSM_GROUPS = 4
HEADS_PER_GROUP = SSM_HEADS // SSM_GROUPS
D_STATE = 128
CONV_WIDTH = 4
CONV_DIM = D_INNER + 2 * SSM_GROUPS * D_STATE
SSM_IN_DIM = 2 * D_INNER + 2 * SSM_GROUPS * D_STATE + SSM_HEADS
SSM_CHUNK = 128

N_EXPERT_GROUPS = 8
EXPERTS_PER_GROUP = 8
N_EXPERTS = N_EXPERT_GROUPS * EXPERTS_PER_GROUP
TOP_K_IN_GROUP = 2
EXPERT_FF = D_MODEL // 2
MOE_BLOCK = 256

NORM_EPS = 1e-6

kernel_name = "hybrid_swa_ssd_hmoe"


def rmsnorm(x, w):
    xf = x.astype(jnp.float32)
    y = xf * lax.rsqrt(jnp.mean(xf * xf, axis=-1, keepdims=True) + NORM_EPS)
    return (y * w.astype(jnp.float32)).astype(x.dtype)


def t5_causal_bucket(dist):
    n = jnp.maximum(dist, 0)
    max_exact = REL_BUCKETS // 2
    nf = jnp.maximum(n, 1).astype(jnp.float32)
    large = max_exact + (jnp.log(nf / max_exact) / math.log(REL_MAX_DIST / max_exact)
                         * (REL_BUCKETS - max_exact)).astype(jnp.int32)
    large = jnp.minimum(large, REL_BUCKETS - 1)
    return jnp.where(n < max_exact, n, large)


def swa_sink_attention(h, w_qkv, sinks, w_o, rel_bias):
    b, L, _ = h.shape
    nb = L // ATTN_BLOCK
    qkv = h @ w_qkv
    q = qkv[..., :N_HEADS * HEAD_DIM].reshape(b, nb, ATTN_BLOCK, N_KV_HEADS, GQA_GROUP, HEAD_DIM)
    k = qkv[..., N_HEADS * HEAD_DIM:(N_HEADS + N_KV_HEADS) * HEAD_DIM]
    v = qkv[..., (N_HEADS + N_KV_HEADS) * HEAD_DIM:]
    k = k.reshape(b, nb, ATTN_BLOCK, N_KV_HEADS, HEAD_DIM)
    v = v.reshape(b, nb, ATTN_BLOCK, N_KV_HEADS, HEAD_DIM)
    pad = ((0, 0), (1, 0), (0, 0), (0, 0), (0, 0))
    k_band = jnp.concatenate([jnp.pad(k[:, :-1], pad), k], axis=2)
    v_band = jnp.concatenate([jnp.pad(v[:, :-1], pad), v], axis=2)

    scores = jnp.einsum('bnqhgd,bnkhd->bhgnqk', q, k_band,
                        preferred_element_type=jnp.float32) * (HEAD_DIM ** -0.5)

    qi = jnp.arange(ATTN_BLOCK)[:, None]
    ki = jnp.arange(2 * ATTN_BLOCK)[None, :]
    dist = qi + ATTN_BLOCK - ki
    in_window = (dist >= 0) & (dist < WINDOW)
    first_block = (jnp.arange(nb)[:, None] == 0) & (jnp.arange(2 * ATTN_BLOCK)[None, :] < ATTN_BLOCK)
    mask = in_window[None] & ~first_block[:, None, :]

    bias = rel_bias[t5_causal_bucket(dist)].astype(jnp.float32)
    bias = jnp.transpose(bias, (2, 0, 1)).reshape(N_KV_HEADS, GQA_GROUP, ATTN_BLOCK, 2 * ATTN_BLOCK)
    logits = jnp.where(mask, scores + bias[None, :, :, None], -jnp.inf)

    sink = sinks.astype(jnp.float32).reshape(N_KV_HEADS, GQA_GROUP)[None, :, :, None, None, None]
    m = jnp.maximum(jnp.max(logits, axis=-1, keepdims=True), sink)
    p = jnp.exp(logits - m)
    probs = p / (jnp.sum(p, axis=-1, keepdims=True) + jnp.exp(sink - m))

    out = jnp.einsum('bhgnqk,bnkhd->bnqhgd', probs.astype(v_band.dtype), v_band)
    return out.reshape(b, L, N_HEADS * HEAD_DIM) @ w_o


def causal_depthwise_conv(u, w, bias):
    y = lax.conv_general_dilated(u, w[:, None, :].astype(u.dtype), window_strides=(1,),
                                 padding=[(CONV_WIDTH - 1, 0)],
                                 dimension_numbers=('NWC', 'WIO', 'NWC'),
                                 feature_group_count=u.shape[-1])
    return y + bias.astype(u.dtype)


def ssd_chunked(X, A, Bm, Cm):
    b, L = X.shape[:2]
    nc = L // SSM_CHUNK
    X = X.astype(jnp.float32).reshape(b, nc, SSM_CHUNK, SSM_GROUPS, HEADS_PER_GROUP, SSM_HEAD_DIM)
    A = A.astype(jnp.float32).reshape(b, nc, SSM_CHUNK, SSM_GROUPS, HEADS_PER_GROUP)
    Bm = Bm.astype(jnp.float32).reshape(b, nc, SSM_CHUNK, SSM_GROUPS, D_STATE)
    Cm = Cm.astype(jnp.float32).reshape(b, nc, SSM_CHUNK, SSM_GROUPS, D_STATE)

    a_cs = jnp.cumsum(A, axis=2)
    seg = a_cs[:, :, :, None] - a_cs[:, :, None, :]
    causal = (jnp.arange(SSM_CHUNK)[:, None] >= jnp.arange(SSM_CHUNK)[None, :])[None, None, :, :, None, None]
    decay_ls = jnp.exp(jnp.where(causal, seg, -jnp.inf))

    cb = jnp.einsum('bclgn,bcsgn->bclsg', Cm, Bm)
    y_diag = jnp.einsum('bclsge,bcsgep->bclgep', cb[..., None] * decay_ls, X)

    decay_to_end = jnp.exp(a_cs[:, :, -1:] - a_cs)
    chunk_states = jnp.einsum('bclgn,bclgep->bcgepn', Bm, X * decay_to_end[..., None])
    chunk_decay = jnp.exp(a_cs[:, :, -1])

    def step(state, inp):
        dec, new = inp
        return dec[..., None, None] * state + new, state

    init = jnp.zeros((b, SSM_GROUPS, HEADS_PER_GROUP, SSM_HEAD_DIM, D_STATE), jnp.float32)
    _, states_in = lax.scan(step, init, (jnp.moveaxis(chunk_decay, 1, 0), jnp.moveaxis(chunk_states, 1, 0)))
    states_in = jnp.moveaxis(states_in, 0, 1)

    y_off = jnp.einsum('bclgn,bcgepn->bclgep', Cm, states_in) * jnp.exp(a_cs)[..., None]
    return (y_diag + y_off).reshape(b, L, SSM_HEADS, SSM_HEAD_DIM)


def mamba2_mixer(h, w_in, conv_w, conv_b, dt_bias, a_log, d_skip, norm_w, w_out):
    b, L, _ = h.shape
    zxbcdt = h @ w_in
    z = zxbcdt[..., :D_INNER]
    xbc = zxbcdt[..., D_INNER:D_INNER + CONV_DIM]
    dt = zxbcdt[..., D_INNER + CONV_DIM:]
    xbc = jax.nn.silu(causal_depthwise_conv(xbc, conv_w, conv_b))
    xs = xbc[..., :D_INNER].reshape(b, L, SSM_HEADS, SSM_HEAD_DIM)
    Bm = xbc[..., D_INNER:D_INNER + SSM_GROUPS * D_STATE].reshape(b, L, SSM_GROUPS, D_STATE)
    Cm = xbc[..., D_INNER + SSM_GROUPS * D_STATE:].reshape(b, L, SSM_GROUPS, D_STATE)

    dt = jax.nn.softplus(dt.astype(jnp.float32) + dt_bias.astype(jnp.float32))
    A = -jnp.exp(a_log.astype(jnp.float32))
    xf = xs.astype(jnp.float32)
    y = ssd_chunked(xf * dt[..., None], dt * A, Bm, Cm)
    y = y + xf * d_skip.astype(jnp.float32)[:, None]

    g = y.reshape(b, L, D_INNER) * jax.nn.silu(z.astype(jnp.float32))
    g = g.reshape(b, L, SSM_GROUPS, D_INNER // SSM_GROUPS)
    g = g * lax.rsqrt(jnp.mean(g * g, axis=-1, keepdims=True) + NORM_EPS)
    g = g.reshape(b, L, D_INNER) * norm_w.astype(jnp.float32)
    return g.astype(h.dtype) @ w_out


def hierarchical_moe(h, w_group, b_group, w_expert, b_expert, w_gate, w_up, w_down):
    b, L, dm = h.shape
    T = b * L
    xf = h.reshape(T, dm)

    g_prob = jax.nn.softmax((xf @ w_group).astype(jnp.float32) + b_group.astype(jnp.float32), axis=-1)
    g_p, g_idx = lax.top_k(g_prob, 1)
    e_logits = ((xf @ w_expert).astype(jnp.float32) + b_expert.astype(jnp.float32))
    e_logits = e_logits.reshape(T, N_EXPERT_GROUPS, EXPERTS_PER_GROUP)
    sel = jnp.take_along_axis(e_logits, g_idx[:, :, None], axis=1)[:, 0]
    e_p, e_idx = lax.top_k(jax.nn.softmax(sel, axis=-1), TOP_K_IN_GROUP)
    gates = e_p / jnp.sum(e_p, axis=-1, keepdims=True) * g_p
    expert_id = g_idx * EXPERTS_PER_GROUP + e_idx

    n_assign = T * TOP_K_IN_GROUP
    n_blocks = -(-n_assign // MOE_BLOCK) + N_EXPERTS
    n_slots = n_blocks * MOE_BLOCK
    flat_e = expert_id.reshape(-1).astype(jnp.int32)
    flat_tok = jnp.repeat(jnp.arange(T, dtype=jnp.int32), TOP_K_IN_GROUP)
    flat_w = gates.reshape(-1)
    order = jnp.argsort(flat_e)
    s_e = flat_e[order]
    counts = jnp.zeros((N_EXPERTS,), jnp.int32).at[flat_e].add(1)
    starts = jnp.cumsum(counts) - counts
    padded = (counts + MOE_BLOCK - 1) // MOE_BLOCK * MOE_BLOCK
    p_ends = jnp.cumsum(padded)
    p_starts = p_ends - padded
    dest = p_starts[s_e] + (jnp.arange(n_assign, dtype=jnp.int32) - starts[s_e])
    slot_tok = jnp.full((n_slots,), T, jnp.int32).at[dest].set(flat_tok[order])
    slot_w = jnp.zeros((n_slots,), jnp.float32).at[dest].set(flat_w[order])
    block_start = jnp.arange(n_blocks, dtype=jnp.int32) * MOE_BLOCK
    block_e = jnp.clip(jnp.searchsorted(p_ends, block_start, side='right'), 0, N_EXPERTS - 1).astype(jnp.int32)

    x_pad = jnp.concatenate([xf, jnp.zeros((1, dm), xf.dtype)], axis=0)

    def expert_block(args):
        tok, e = args
        xb = x_pad[tok]
        hid = jax.nn.silu(xb @ w_gate[e]) * (xb @ w_up[e])
        return hid @ w_down[e]

    y = lax.map(expert_block, (slot_tok.reshape(n_blocks, MOE_BLOCK), block_e)).reshape(n_slots, dm)
    out = jnp.zeros((T + 1, dm), y.dtype).at[slot_tok].add(y * slot_w[:, None].astype(y.dtype))
    return out[:T].reshape(b, L, dm)


def setup_inputs(seed: int = 0) -> dict:
    key = jax.random.key(seed)
    k = jax.random.split(key, 24)
    D = D_MODEL
    nrm = jax.random.normal
    dt = jnp.exp(jax.random.uniform(k[12], (N_SSM_LAYERS, SSM_HEADS)) * (math.log(0.1) - math.log(0.001)) + math.log(0.001))
    return {
        "x": nrm(k[0], (BATCH, SEQ, D), jnp.float32),
        "rel_bias": 0.5 * nrm(k[1], (REL_BUCKETS, N_HEADS), jnp.float32),
        "ln_mix": 1.0 + 0.05 * nrm(k[2], (DEPTH, D), jnp.float32),
        "ln_ffn": 1.0 + 0.05 * nrm(k[3], (DEPTH, D), jnp.float32),
        "ln_final": 1.0 + 0.05 * nrm(k[4], (D,), jnp.float32),
        "attn_w_qkv": nrm(k[5], (N_ATTN_LAYERS, D, QKV_DIM), jnp.float32) * D ** -0.5,
        "attn_sinks": 0.5 * nrm(k[6], (N_ATTN_LAYERS, N_HEADS), jnp.float32),
        "attn_w_o": nrm(k[7], (N_ATTN_LAYERS, N_HEADS * HEAD_DIM, D), jnp.float32) * (N_HEADS * HEAD_DIM) ** -0.5,
        "ssm_w_in": nrm(k[8], (N_SSM_LAYERS, D, SSM_IN_DIM), jnp.float32) * D ** -0.5,
        "ssm_conv_w": nrm(k[9], (N_SSM_LAYERS, CONV_WIDTH, CONV_DIM), jnp.float32) * CONV_WIDTH ** -0.5,
        "ssm_conv_b": 0.02 * nrm(k[10], (N_SSM_LAYERS, CONV_DIM), jnp.float32),
        "ssm_dt_bias": dt + jnp.log(-jnp.expm1(-dt)),
        "ssm_a_log": jnp.log(jax.random.uniform(k[13], (N_SSM_LAYERS, SSM_HEADS), jnp.float32, 1.0, 16.0)),
        "ssm_d": 1.0 + 0.1 * nrm(k[14], (N_SSM_LAYERS, SSM_HEADS), jnp.float32),
        "ssm_norm_w": 1.0 + 0.05 * nrm(k[15], (N_SSM_LAYERS, D_INNER), jnp.float32),
        "ssm_w_out": nrm(k[16], (N_SSM_LAYERS, D_INNER, D), jnp.float32) * D_INNER ** -0.5,
        "moe_w_group": nrm(k[17], (DEPTH, D, N_EXPERT_GROUPS), jnp.float32) * D ** -0.5,
        "moe_b_group": 0.01 * nrm(k[18], (DEPTH, N_EXPERT_GROUPS), jnp.float32),
        "moe_w_expert": nrm(k[19], (DEPTH, D, N_EXPERTS), jnp.float32) * D ** -0.5,
        "moe_b_expert": 0.01 * nrm(k[20], (DEPTH, N_EXPERTS), jnp.float32),
        "moe_w_gate": nrm(k[21], (DEPTH, N_EXPERTS, D, EXPERT_FF), jnp.float32) * D ** -0.5,
        "moe_w_up": nrm(k[22], (DEPTH, N_EXPERTS, D, EXPERT_FF), jnp.float32) * D ** -0.5,
        "moe_w_down": nrm(k[23], (DEPTH, N_EXPERTS, EXPERT_FF, D), jnp.float32) * EXPERT_FF ** -0.5,
    }


def reference(x, rel_bias, ln_mix, ln_ffn, ln_final, attn_w_qkv, attn_sinks, attn_w_o,
              ssm_w_in, ssm_conv_w, ssm_conv_b, ssm_dt_bias, ssm_a_log, ssm_d, ssm_norm_w, ssm_w_out,
              moe_w_group, moe_b_group, moe_w_expert, moe_b_expert, moe_w_gate, moe_w_up, moe_w_down):
    for i in range(DEPTH):
        h = rmsnorm(x, ln_mix[i])
        j = i // N_MIXERS
        if i % N_MIXERS == 0:
            x = x + swa_sink_attention(h, attn_w_qkv[j], attn_sinks[j], attn_w_o[j], rel_bias)
        else:
            x = x + mamba2_mixer(h, ssm_w_in[j], ssm_conv_w[j], ssm_conv_b[j], ssm_dt_bias[j],
                                 ssm_a_log[j], ssm_d[j], ssm_norm_w[j], ssm_w_out[j])
        h = rmsnorm(x, ln_ffn[i])
        x = x + hierarchical_moe(h, moe_w_group[i], moe_b_group[i], moe_w_expert[i], moe_b_expert[i],
                                 moe_w_gate[i], moe_w_up[i], moe_w_down[i])
    return rmsnorm(x, ln_final)
```

```python
import functools
import math

import jax
import jax.numpy as jnp
from jax import lax
from jax.experimental import pallas as pl
from jax.experimental.pallas import tpu as pltpu

F32 = jnp.float32
BF16 = jnp.bfloat16

D_MODEL = 1024
N_HEADS = 16
N_KV_HEADS = 4
HEAD_DIM = 64
GQA_GROUP = N_HEADS // N_KV_HEADS
WINDOW = 128
ATTN_BLOCK = 128
Q_DIM = N_HEADS * HEAD_DIM
KV_DIM = N_KV_HEADS * HEAD_DIM
QKV_DIM = Q_DIM + 2 * KV_DIM
REL_BUCKETS = 32
REL_MAX_DIST = 128

D_INNER = 2048
SSM_HEAD_DIM = 64
SSM_HEADS = D_INNER // SSM_HEAD_DIM
SSM_GROUPS = 4
D_STATE = 128
CONV_WIDTH = 4
BC_DIM = 2 * SSM_GROUPS * D_STATE
CONV_DIM = D_INNER + BC_DIM
ZXBC_DIM = D_INNER + CONV_DIM
SSM_CHUNK = 128
GROUP_CH = D_INNER // SSM_GROUPS

N_EXPERT_GROUPS = 8
EXPERTS_PER_GROUP = 8
N_EXPERTS = 64
EXPERT_FF = 512
MOE_BLOCK = 256

NORM_EPS = 1e-6
LANES = 128
NEG_BIG = -1e30
VMEM_LIMIT = 56 * 1024 * 1024


def _cparams(sem):
    return pltpu.CompilerParams(dimension_semantics=sem, vmem_limit_bytes=VMEM_LIMIT)


def _rms(x, g):
    ms = jnp.mean(x * x, axis=-1, keepdims=True)
    return x * lax.rsqrt(ms + NORM_EPS) * g


def _silu(x):
    return x / (1.0 + jnp.exp(-x))


def _norm_matmul_kernel(x_ref, g_ref, w_ref, *rest, has_aux):
    if has_aux:
        wa_ref, o_ref, oa_ref, h_scr = rest
    else:
        o_ref, h_scr = rest

    @pl.when(pl.program_id(1) == 0)
    def _():
        h = _rms(x_ref[...], g_ref[...]).astype(BF16)
        h_scr[...] = h
        if has_aux:
            oa_ref[...] = jnp.dot(h, wa_ref[...], preferred_element_type=F32)

    o_ref[...] = jnp.dot(h_scr[...], w_ref[...], preferred_element_type=F32).astype(o_ref.dtype)


def _norm_matmul(x, g, w, w_aux=None, *, tm=1024, tn=512):
    t, d = x.shape
    n = w.shape[1]
    has_aux = w_aux is not None
    in_specs = [
        pl.BlockSpec((tm, d), lambda i, j: (i, 0)),
        pl.BlockSpec((1, d), lambda i, j: (0, 0)),
        pl.BlockSpec((d, tn), lambda i, j: (0, j)),
    ]
    out_shape = [jax.ShapeDtypeStruct((t, n), BF16)]
    out_specs = [pl.BlockSpec((tm, tn), lambda i, j: (i, j))]
    args = [x, g.reshape(1, d), w]
    if has_aux:
        na = w_aux.shape[1]
        in_specs.append(pl.BlockSpec((d, na), lambda i, j: (0, 0)))
        out_shape.append(jax.ShapeDtypeStruct((t, na), F32))
        out_specs.append(pl.BlockSpec((tm, na), lambda i, j: (i, 0)))
        args.append(w_aux)
    res = pl.pallas_call(
        functools.partial(_norm_matmul_kernel, has_aux=has_aux),
        grid=(t // tm, n // tn),
        in_specs=in_specs,
        out_specs=out_specs,
        out_shape=out_shape,
        scratch_shapes=[pltpu.VMEM((tm, d), BF16)],
        compiler_params=_cparams(("parallel", "arbitrary")),
        name="norm_matmul",
    )(*args)
    return res if has_aux else res[0]


def _matmul_residual_kernel(a_ref, w_ref, r_ref, o_ref):
    o_ref[...] = r_ref[...] + jnp.dot(a_ref[...], w_ref[...], preferred_element_type=F32)


def _matmul_residual(a, w, res, *, tm=512):
    t, k = a.shape
    n = w.shape[1]
    return pl.pallas_call(
        _matmul_residual_kernel,
        grid=(t // tm,),
        in_specs=[
            pl.BlockSpec((tm, k), lambda i: (i, 0)),
            pl.BlockSpec((k, n), lambda i: (0, 0)),
            pl.BlockSpec((tm, n), lambda i: (i, 0)),
        ],
        out_specs=pl.BlockSpec((tm, n), lambda i: (i, 0)),
        out_shape=jax.ShapeDtypeStruct((t, n), F32),
        compiler_params=_cparams(("parallel",)),
        name="matmul_residual",
    )(a, w, res)


def _attn_kernel(sinks_ref, q_ref, kp_ref, kc_ref, vp_ref, vc_ref, bias_ref, o_ref):
    first = pl.program_id(1) == 0
    col = lax.broadcasted_iota(jnp.int32, (ATTN_BLOCK, 2 * ATTN_BLOCK), 1)
    hide_prev = jnp.logical_and(first, col < ATTN_BLOCK)
    scale = HEAD_DIM ** -0.5
    for h in range(N_KV_HEADS):
        ks = slice(h * HEAD_DIM, (h + 1) * HEAD_DIM)
        kb = jnp.concatenate([kp_ref[:, ks], kc_ref[:, ks]], axis=0)
        vb = jnp.concatenate([vp_ref[:, ks], vc_ref[:, ks]], axis=0)
        for g in range(GQA_GROUP):
            hh = h * GQA_GROUP + g
            qh = q_ref[:, hh * HEAD_DIM:(hh + 1) * HEAD_DIM]
            s = lax.dot_general(qh, kb, (((1,), (1,)), ((), ())),
                                preferred_element_type=F32) * scale
            logits = jnp.where(hide_prev, NEG_BIG, s + bias_ref[hh])
            sink = sinks_ref[hh]
            m = jnp.maximum(jnp.max(logits, axis=-1, keepdims=True), sink)
            p = jnp.exp(logits - m)
            denom = jnp.sum(p, axis=-1, keepdims=True) + jnp.exp(sink - m)
            o = jnp.dot(p.astype(BF16), vb, preferred_element_type=F32) / denom
            o_ref[:, hh * HEAD_DIM:(hh + 1) * HEAD_DIM] = o.astype(o_ref.dtype)


def _t5_causal_bucket(dist):
    n = jnp.maximum(dist, 0)
    max_exact = REL_BUCKETS // 2
    nf = jnp.maximum(n, 1).astype(F32)
    large = max_exact + (jnp.log(nf / max_exact) / math.log(REL_MAX_DIST / max_exact)
                         * (REL_BUCKETS - max_exact)).astype(jnp.int32)
    large = jnp.minimum(large, REL_BUCKETS - 1)
    return jnp.where(n < max_exact, n, large)


def _attention(qkv, sinks, rel_bias, batch, seq):
    t = qkv.shape[0]
    nb = seq // ATTN_BLOCK
    qi = jnp.arange(ATTN_BLOCK)[:, None]
    ki = jnp.arange(2 * ATTN_BLOCK)[None, :]
    dist = qi + ATTN_BLOCK - ki
    in_window = (dist >= 0) & (dist < WINDOW)
    bias = rel_bias[_t5_causal_bucket(dist)].astype(F32)
    bias = jnp.where(in_window[None], jnp.transpose(bias, (2, 0, 1)), NEG_BIG)

    kcol = Q_DIM // KV_DIM
    vcol = kcol + 1

    def prev(b, n):
        return b * nb + jnp.maximum(n - 1, 0)

    return pl.pallas_call(
        _attn_kernel,
        grid=(batch, nb),
        in_specs=[
            pl.BlockSpec(memory_space=pltpu.SMEM),
            pl.BlockSpec((ATTN_BLOCK, Q_DIM), lambda b, n: (b * nb + n, 0)),
            pl.BlockSpec((ATTN_BLOCK, KV_DIM), lambda b, n: (prev(b, n), kcol)),
            pl.BlockSpec((ATTN_BLOCK, KV_DIM), lambda b, n: (b * nb + n, kcol)),
            pl.BlockSpec((ATTN_BLOCK, KV_DIM), lambda b, n: (prev(b, n), vcol)),
            pl.BlockSpec((ATTN_BLOCK, KV_DIM), lambda b, n: (b * nb + n, vcol)),
            pl.BlockSpec((N_HEADS, ATTN_BLOCK, 2 * ATTN_BLOCK), lambda b, n: (0, 0, 0)),
        ],
        out_specs=pl.BlockSpec((ATTN_BLOCK, Q_DIM), lambda b, n: (b * nb + n, 0)),
        out_shape=jax.ShapeDtypeStruct((t, Q_DIM), BF16),
        compiler_params=_cparams(("parallel", "arbitrary")),
        name="swa_attention",
    )(sinks.astype(F32), qkv, qkv, qkv, qkv, qkv, bias)


def _ssd_kernel(z_ref, x_ref, bc_ref, dt_ref, convw_ref, convb_ref, dtb_ref, alog_ref,
                dskip_ref, normw_ref, o_ref, state, xstage, bcstage):
    c = pl.program_id(1)
    L = SSM_CHUNK

    @pl.when(c == 0)
    def _():
        state[...] = jnp.zeros_like(state)
        xstage[0:8, :] = jnp.zeros((8, D_INNER), F32)
        bcstage[0:8, :] = jnp.zeros((8, BC_DIM), F32)

    def conv_silu(u_ref, stage, w_lo, width):
        cur = u_ref[...].astype(F32)
        stage[8:8 + L, :] = cur
        w = convw_ref[:, w_lo:w_lo + width]
        y = (cur * w[3:4] + stage[7:7 + L, :] * w[2:3] + stage[6:6 + L, :] * w[1:2]
             + stage[5:5 + L, :] * w[0:1] + convb_ref[:, w_lo:w_lo + width])
        stage[0:8, :] = cur[L - 8:L]
        return _silu(y)

    xs = conv_silu(x_ref, xstage, 0, D_INNER)
    bcm = conv_silu(bc_ref, bcstage, D_INNER, BC_DIM)

    xdt = dt_ref[...] + dtb_ref[...]
    dt = jnp.maximum(xdt, 0.0) + jnp.log1p(jnp.exp(-jnp.abs(xdt)))
    a = dt * (-jnp.exp(alog_ref[...]))
    row = lax.broadcasted_iota(jnp.int32, (L, L), 0)
    colm = lax.broadcasted_iota(jnp.int32, (L, L), 1)
    causal = row >= colm
    tri = causal.astype(F32)
    a_cs = jnp.dot(tri, a, preferred_element_type=F32, precision=lax.Precision.HIGHEST)
    a_cs_t = a_cs.T
    dt_t = dt.T
    a_last_col = a_cs_t[:, L - 1:L]
    w_end_t = jnp.exp(a_last_col - a_cs_t) * dt_t

    erow = lax.broadcasted_iota(jnp.int32, (LANES, D_INNER), 0)
    ecol = lax.broadcasted_iota(jnp.int32, (LANES, D_INNER), 1)
    expand = (jnp.right_shift(ecol, 6) == erow).astype(F32)
    cd_rows = jnp.broadcast_to(jnp.exp(a_cs[L - 1:L, :]), (8, LANES))
    cd_exp = jnp.dot(cd_rows, expand, preferred_element_type=F32,
                     precision=lax.Precision.HIGHEST)[0:1]

    lane = lax.broadcasted_iota(jnp.int32, (L, LANES), 1)
    low_half = lane < SSM_HEAD_DIM

    y_tiles = []
    for g in range(SSM_GROUPS):
        b_g = bcm[:, g * D_STATE:(g + 1) * D_STATE]
        c_g = bcm[:, (SSM_GROUPS + g) * D_STATE:(SSM_GROUPS + g + 1) * D_STATE]
        b_gb = b_g.astype(BF16)
        c_gb = c_g.astype(BF16)
        cb = lax.dot_general(c_gb, b_gb, (((1,), (1,)), ((), ())), preferred_element_type=F32)
        b_gt = b_g.T
        s_g = state[:, g * GROUP_CH:(g + 1) * GROUP_CH]
        y_off_g = jnp.dot(c_gb, s_g.astype(BF16), preferred_element_type=F32)
        for jj in range(GROUP_CH // LANES):
            j = g * (GROUP_CH // LANES) + jj
            lhs_top = []
            lhs_bot = []
            escale = []
            for e in (2 * j, 2 * j + 1):
                acs_b = jnp.broadcast_to(a_cs[:, e:e + 1], (L, L))
                seg = acs_b - a_cs_t[e:e + 1, :]
                dec = jnp.exp(jnp.where(causal, seg, -jnp.inf))
                lhs_top.append((cb * dec * dt_t[e:e + 1, :]).astype(BF16))
                lhs_bot.append((b_gt * w_end_t[e:e + 1, :]).astype(BF16))
                escale.append(jnp.exp(acs_b))
            x_tile = xs[:, j * LANES:(j + 1) * LANES]
            x_lo = jnp.where(low_half, x_tile, 0.0).astype(BF16)
            x_hi = jnp.where(low_half, 0.0, x_tile).astype(BF16)
            lhs = jnp.concatenate([jnp.concatenate(lhs_top, axis=1),
                                   jnp.concatenate(lhs_bot, axis=1)], axis=0)
            rhs = jnp.concatenate([x_lo, x_hi], axis=0)
            r = jnp.dot(lhs, rhs, preferred_element_type=F32)
            y_off = y_off_g[:, jj * LANES:(jj + 1) * LANES]
            y_tiles.append(r[0:L] + jnp.where(low_half, escale[0], escale[1]) * y_off)
            sl = slice(j * LANES, (j + 1) * LANES)
            state[:, sl] = cd_exp[:, sl] * state[:, sl] + r[L:2 * L]

    y = jnp.concatenate(y_tiles, axis=1)
    y = y + xs * dskip_ref[...]
    gated = y * _silu(z_ref[...].astype(F32))
    outs = []
    for g in range(SSM_GROUPS):
        gg = gated[:, g * GROUP_CH:(g + 1) * GROUP_CH]
        ms = jnp.mean(gg * gg, axis=-1, keepdims=True)
        outs.append(gg * lax.rsqrt(ms + NORM_EPS))
    o_ref[...] = (jnp.concatenate(outs, axis=1) * normw_ref[...]).astype(o_ref.dtype)


def _ssd(zxbc, dt_raw, conv_w, conv_b, dt_bias, a_log, d_skip, norm_w, batch, seq):
    t = zxbc.shape[0]
    nc = seq // SSM_CHUNK
    L = SSM_CHUNK

    def pad_heads(v):
        return jnp.pad(v.astype(F32), (0, LANES - SSM_HEADS)).reshape(1, LANES)

    def rowmap(col):
        return lambda b, c: (b * nc + c, col)

    def const2(b, c):
        return (0, 0)

    return pl.pallas_call(
        _ssd_kernel,
        grid=(batch, nc),
        in_specs=[
            pl.BlockSpec((L, D_INNER), rowmap(0)),
            pl.BlockSpec((L, D_INNER), rowmap(1)),
            pl.BlockSpec((L, BC_DIM), rowmap(2 * D_INNER // BC_DIM)),
            pl.BlockSpec((L, LANES), rowmap(0)),
            pl.BlockSpec((CONV_WIDTH, CONV_DIM), const2),
            pl.BlockSpec((1, CONV_DIM), const2),
            pl.BlockSpec((1, LANES), const2),
            pl.BlockSpec((1, LANES), const2),
            pl.BlockSpec((1, D_INNER), const2),
            pl.BlockSpec((1, D_INNER), const2),
        ],
        out_specs=pl.BlockSpec((L, D_INNER), rowmap(0)),
        out_shape=jax.ShapeDtypeStruct((t, D_INNER), BF16),
        scratch_shapes=[
            pltpu.VMEM((D_STATE, D_INNER), F32),
            pltpu.VMEM((L + 8, D_INNER), F32),
            pltpu.VMEM((L + 8, BC_DIM), F32),
        ],
        compiler_params=_cparams(("parallel", "arbitrary")),
        name="ssd_core",
    )(zxbc, zxbc, zxbc, dt_raw, conv_w.astype(F32), conv_b.astype(F32).reshape(1, CONV_DIM),
      pad_heads(dt_bias), pad_heads(a_log),
      jnp.repeat(d_skip.astype(F32), SSM_HEAD_DIM).reshape(1, D_INNER),
      norm_w.astype(F32).reshape(1, D_INNER))


GROUP_LANE0 = N_EXPERTS


def _router_kernel(x_ref, g_ref, w_ref, b_ref, tril_ref, ids_ref, gates_ref, cnt_ref, carry):
    i = pl.program_id(0)

    @pl.when(i == 0)
    def _():
        carry[...] = jnp.zeros_like(carry)

    h = _rms(x_ref[...], g_ref[...])
    logits = jnp.dot(h, w_ref[...], preferred_element_type=F32,
                     precision=lax.Precision.HIGHEST) + b_ref[...]
    tm = logits.shape[0]
    lane = lax.broadcasted_iota(jnp.int32, (tm, LANES), 1)
    ninf = -jnp.inf

    def first_argmax(v, vmax):
        return jnp.min(jnp.where(v == vmax, lane, LANES), axis=-1, keepdims=True)

    is_group = (lane >= GROUP_LANE0) & (lane < GROUP_LANE0 + N_EXPERT_GROUPS)
    glog = jnp.where(is_group, logits, ninf)
    gmax = jnp.max(glog, axis=-1, keepdims=True)
    g_idx = first_argmax(glog, gmax) - GROUP_LANE0
    g_p = 1.0 / jnp.sum(jnp.exp(glog - gmax), axis=-1, keepdims=True)

    in_group = (lane < N_EXPERTS) & (jnp.right_shift(lane, 3) == g_idx)
    elog = jnp.where(in_group, logits, ninf)
    m1 = jnp.max(elog, axis=-1, keepdims=True)
    e1 = first_argmax(elog, m1)
    elog2 = jnp.where(lane == e1, ninf, elog)
    m2 = jnp.max(elog2, axis=-1, keepdims=True)
    e2 = first_argmax(elog2, m2)
    zsum = jnp.sum(jnp.exp(elog - m1), axis=-1, keepdims=True)
    p1 = 1.0 / zsum
    p2 = jnp.exp(m2 - m1) / zsum
    psum = p1 + p2
    w1 = p1 / psum * g_p
    w2 = p2 / psum * g_p

    oh1 = lane == e1
    oh2 = lane == e2
    onehot = jnp.logical_or(oh1, oh2).astype(F32)
    before = jnp.dot(tril_ref[...], onehot.astype(BF16), preferred_element_type=F32) + carry[0:1, :]
    r1 = jnp.sum(jnp.where(oh1, before, 0.0), axis=-1, keepdims=True).astype(jnp.int32)
    r2 = jnp.sum(jnp.where(oh2, before, 0.0), axis=-1, keepdims=True).astype(jnp.int32)
    carry[0:1, :] = carry[0:1, :] + jnp.sum(onehot, axis=0, keepdims=True)

    ids_ref[...] = jnp.where(lane == 0, e1, jnp.where(lane == 1, e2,
                             jnp.where(lane == 2, r1, jnp.where(lane == 3, r2, 0))))
    gates_ref[...] = jnp.where(lane == 0, w1, jnp.where(lane == 1, w2, 0.0))
    cnt_ref[...] = carry[...]


def _router(x, ln_w, w_group, b_group, w_expert, b_expert, *, tm=512):
    t, d = x.shape
    pad = LANES - N_EXPERTS - N_EXPERT_GROUPS
    w_r = jnp.concatenate([w_expert, w_group, jnp.zeros((d, pad), F32)], axis=1).astype(F32)
    b_r = jnp.concatenate([b_expert, b_group, jnp.zeros((pad,), F32)]).astype(F32).reshape(1, LANES)
    tril = (jnp.arange(tm)[:, None] > jnp.arange(tm)[None, :]).astype(BF16)
    return pl.pallas_call(
        _router_kernel,
        grid=(t // tm,),
        in_specs=[
            pl.BlockSpec((tm, d), lambda i: (i, 0)),
            pl.BlockSpec((1, d), lambda i: (0, 0)),
            pl.BlockSpec((d, LANES), lambda i: (0, 0)),
            pl.BlockSpec((1, LANES), lambda i: (0, 0)),
            pl.BlockSpec((tm, tm), lambda i: (0, 0)),
        ],
        out_specs=[
            pl.BlockSpec((tm, LANES), lambda i: (i, 0)),
            pl.BlockSpec((tm, LANES), lambda i: (i, 0)),
            pl.BlockSpec((8, LANES), lambda i: (0, 0)),
        ],
        out_shape=[
            jax.ShapeDtypeStruct((t, LANES), jnp.int32),
            jax.ShapeDtypeStruct((t, LANES), F32),
            jax.ShapeDtypeStruct((8, LANES), F32),
        ],
        scratch_shapes=[pltpu.VMEM((8, LANES), F32)],
        compiler_params=_cparams(("arbitrary",)),
        name="moe_router",
    )(x, ln_w.reshape(1, d), w_r, b_r, tril)


def _dispatch_kernel(dest_ref, x_ref, g_ref, init_ref, xs_ref, hbuf, sem):
    del init_ref
    tb = hbuf.shape[0]
    hbuf[...] = _rms(x_ref[...], g_ref[...])

    def row_copy(r, k):
        d = dest_ref[0, 0, 2 * r + k]
        return pltpu.make_async_copy(hbuf.at[pl.ds(r, 1)], xs_ref.at[pl.ds(d, 1)], sem)

    def issue(r, carry):
        row_copy(r, 0).start()
        row_copy(r, 1).start()
        return carry

    def drain(r, carry):
        row_copy(r, 0).wait()
        row_copy(r, 1).wait()
        return carry

    lax.fori_loop(0, tb, issue, 0)
    lax.fori_loop(0, tb, drain, 0)


def _dispatch(x, ln_w, dest, n_slots, *, tb=256):
    t, d = x.shape
    nblk = t // tb
    init = jnp.zeros((n_slots, d), F32)
    return pl.pallas_call(
        _dispatch_kernel,
        grid=(nblk,),
        in_specs=[
            pl.BlockSpec((1, 1, 2 * tb), lambda i: (i, 0, 0), memory_space=pltpu.SMEM),
            pl.BlockSpec((tb, d), lambda i: (i, 0)),
            pl.BlockSpec((1, d), lambda i: (0, 0)),
            pl.BlockSpec(memory_space=pl.ANY),
        ],
        out_specs=pl.BlockSpec(memory_space=pl.ANY),
        out_shape=jax.ShapeDtypeStruct((n_slots, d), F32),
        scratch_shapes=[pltpu.VMEM((tb, d), F32), pltpu.SemaphoreType.DMA(())],
        input_output_aliases={3: 0},
        compiler_params=_cparams(("arbitrary",)),
        name="moe_dispatch",
    )(dest.reshape(nblk, 1, 2 * tb), x, ln_w.reshape(1, d), init)


def _expert_kernel(be_ref, na_ref, xs_ref, wg_ref, wu_ref, wd_ref, y_ref, wg_s, wu_s, wd_s):
    i = pl.program_id(0)
    active = i < na_ref[0]
    changed = jnp.logical_or(i == 0, be_ref[i] != be_ref[jnp.maximum(i - 1, 0)])

    @pl.when(jnp.logical_and(active, changed))
    def _():
        wg_s[...] = wg_ref[...].astype(BF16)
        wu_s[...] = wu_ref[...].astype(BF16)
        wd_s[...] = wd_ref[...].astype(BF16)

    @pl.when(active)
    def _():
        xb = xs_ref[...].astype(BF16)
        gate = jnp.dot(xb, wg_s[...], preferred_element_type=F32)
        up = jnp.dot(xb, wu_s[...], preferred_element_type=F32)
        hid = (_silu(gate) * up).astype(BF16)
        y_ref[...] = jnp.dot(hid, wd_s[...], preferred_element_type=F32)

    @pl.when(jnp.logical_not(active))
    def _():
        y_ref[...] = jnp.zeros_like(y_ref)


def _experts(xs, block_e, n_active, w_gate, w_up, w_down, layer):
    n_slots, d = xs.shape
    n_blocks = n_slots // MOE_BLOCK

    def xmap(i, be, na):
        return (jnp.minimum(i, na[0] - 1), 0)

    def wmap(i, be, na):
        return (layer, be[i], 0, 0)

    return pl.pallas_call(
        _expert_kernel,
        grid_spec=pltpu.PrefetchScalarGridSpec(
            num_scalar_prefetch=2,
            grid=(n_blocks,),
            in_specs=[
                pl.BlockSpec((MOE_BLOCK, d), xmap),
                pl.BlockSpec((None, None, d, EXPERT_FF), wmap),
                pl.BlockSpec((None, None, d, EXPERT_FF), wmap),
                pl.BlockSpec((None, None, EXPERT_FF, d), wmap),
            ],
            out_specs=pl.BlockSpec((MOE_BLOCK, d), lambda i, be, na: (i, 0)),
            scratch_shapes=[
                pltpu.VMEM((d, EXPERT_FF), BF16),
                pltpu.VMEM((d, EXPERT_FF), BF16),
                pltpu.VMEM((EXPERT_FF, d), BF16),
            ],
        ),
        out_shape=jax.ShapeDtypeStruct((n_slots, d), F32),
        compiler_params=_cparams(("arbitrary",)),
        name="moe_experts",
    )(block_e, n_active, xs, w_gate, w_up, w_down)


def _combine_kernel(dest_ref, x_ref, gates_ref, y_ref, *rest, final_norm):
    if final_norm:
        g_ref, o_ref, buf0, buf1, sem = rest
    else:
        o_ref, buf0, buf1, sem = rest
    tb = buf0.shape[0]
    bufs = (buf0, buf1)

    def row_copy(r, k):
        d = dest_ref[0, 0, 2 * r + k]
        return pltpu.make_async_copy(y_ref.at[pl.ds(d, 1)], bufs[k].at[pl.ds(r, 1)], sem)

    def issue(r, carry):
        row_copy(r, 0).start()
        row_copy(r, 1).start()
        return carry

    def drain(r, carry):
        row_copy(r, 0).wait()
        row_copy(r, 1).wait()
        return carry

    lax.fori_loop(0, tb, issue, 0)
    lax.fori_loop(0, tb, drain, 0)

    gt = gates_ref[...]
    out = x_ref[...] + (buf0[...] * gt[:, 0:1] + buf1[...] * gt[:, 1:2])
    if final_norm:
        out = _rms(out, g_ref[...])
    o_ref[...] = out


def _combine(x, gates, y, dest, final_w=None, *, tb=256):
    t, d = x.shape
    nblk = t // tb
    final_norm = final_w is not None
    in_specs = [
        pl.BlockSpec((1, 1, 2 * tb), lambda i: (i, 0, 0), memory_space=pltpu.SMEM),
        pl.BlockSpec((tb, d), lambda i: (i, 0)),
        pl.BlockSpec((tb, LANES), lambda i: (i, 0)),
        pl.BlockSpec(memory_space=pl.ANY),
    ]
    args = [dest.reshape(nblk, 1, 2 * tb), x, gates, y]
    if final_norm:
        in_specs.append(pl.BlockSpec((1, d), lambda i: (0, 0)))
        args.append(final_w.reshape(1, d))
    return pl.pallas_call(
        functools.partial(_combine_kernel, final_norm=final_norm),
        grid=(nblk,),
        in_specs=in_specs,
        out_specs=pl.BlockSpec((tb, d), lambda i: (i, 0)),
        out_shape=jax.ShapeDtypeStruct((t, d), F32),
        scratch_shapes=[pltpu.VMEM((tb, d), F32), pltpu.VMEM((tb, d), F32),
                        pltpu.SemaphoreType.DMA(())],
        compiler_params=_cparams(("arbitrary",)),
        name="moe_combine",
    )(*args)


def _moe(x, ln_w, w_group, b_group, w_expert, b_expert, w_gate, w_up, w_down, layer, final_w=None):
    t = x.shape[0]
    n_assign = t * 2
    n_blocks = -(-n_assign // MOE_BLOCK) + N_EXPERTS
    n_slots = n_blocks * MOE_BLOCK

    ids, gates, cnt = _router(x, ln_w, w_group, b_group, w_expert, b_expert)
    counts = cnt[0, :N_EXPERTS].astype(jnp.int32)
    padded = (counts + MOE_BLOCK - 1) // MOE_BLOCK * MOE_BLOCK
    p_ends = jnp.cumsum(padded)
    p_starts = p_ends - padded
    dest = (p_starts[ids[:, 0:2]] + ids[:, 2:4]).astype(jnp.int32)
    block_start = jnp.arange(n_blocks, dtype=jnp.int32) * MOE_BLOCK
    block_e = jnp.clip(jnp.searchsorted(p_ends, block_start, side='right'),
                       0, N_EXPERTS - 1).astype(jnp.int32)
    n_active = (p_ends[-1:] // MOE_BLOCK).astype(jnp.int32)

    xs = _dispatch(x, ln_w, dest, n_slots)
    y = _experts(xs, block_e, n_active, w_gate, w_up, w_down, layer)
    return _combine(x, gates, y, dest, final_w)


def kernel(x, rel_bias, ln_mix, ln_ffn, ln_final, attn_w_qkv, attn_sinks, attn_w_o, ssm_w_in,
           ssm_conv_w, ssm_conv_b, ssm_dt_bias, ssm_a_log, ssm_d, ssm_norm_w, ssm_w_out,
           moe_w_group, moe_b_group, moe_w_expert, moe_b_expert, moe_w_gate, moe_w_up, moe_w_down):
    batch, seq, d = x.shape
    t = batch * seq
    xf = x.reshape(t, d).astype(F32)

    qkv = _norm_matmul(xf, ln_mix[0], attn_w_qkv[0].astype(BF16))
    att = _attention(qkv, attn_sinks[0], rel_bias, batch, seq)
    xf = _matmul_residual(att, attn_w_o[0].astype(BF16), xf)
    xf = _moe(xf, ln_ffn[0], moe_w_group[0], moe_b_group[0], moe_w_expert[0], moe_b_expert[0],
              moe_w_gate, moe_w_up, moe_w_down, 0)

    w_in = ssm_w_in[0]
    w_dt = jnp.pad(w_in[:, ZXBC_DIM:], ((0, 0), (0, LANES - SSM_HEADS))).astype(BF16)
    zxbc, dt_raw = _norm_matmul(xf, ln_mix[1], w_in[:, :ZXBC_DIM].astype(BF16), w_dt)
    gated = _ssd(zxbc, dt_raw, ssm_conv_w[0], ssm_conv_b[0], ssm_dt_bias[0], ssm_a_log[0],
                 ssm_d[0], ssm_norm_w[0], batch, seq)
    xf = _matmul_residual(gated, ssm_w_out[0].astype(BF16), xf)
    xf = _moe(xf, ln_ffn[1], moe_w_group[1], moe_b_group[1], moe_w_expert[1], moe_b_expert[1],
              moe_w_gate, moe_w_up, moe_w_down, 1, final_w=ln_final)
    return xf.reshape(batch, seq, d).astype(x.dtype)
```

```python
import functools
import math

import jax
import jax.numpy as jnp
from jax import lax
from jax.experimental import pallas as pl
from jax.experimental.pallas import tpu as pltpu

F32 = jnp.float32
BF16 = jnp.bfloat16

D_MODEL = 1024
N_HEADS = 16
N_KV_HEADS = 4
HEAD_DIM = 64
GQA_GROUP = N_HEADS // N_KV_HEADS
WINDOW = 128
ATTN_BLOCK = 128
Q_DIM = N_HEADS * HEAD_DIM
KV_DIM = N_KV_HEADS * HEAD_DIM
QKV_DIM = Q_DIM + 2 * KV_DIM
REL_BUCKETS = 32
REL_MAX_DIST = 128

D_INNER = 2048
SSM_HEAD_DIM = 64
SSM_HEADS = D_INNER // SSM_HEAD_DIM
SSM_GROUPS = 4
D_STATE = 128
CONV_WIDTH = 4
BC_DIM = 2 * SSM_GROUPS * D_STATE
CONV_DIM = D_INNER + BC_DIM
ZXBC_DIM = D_INNER + CONV_DIM
SSM_CHUNK = 128
GROUP_CH = D_INNER // SSM_GROUPS

N_EXPERT_GROUPS = 8
EXPERTS_PER_GROUP = 8
N_EXPERTS = 64
EXPERT_FF = 512
MOE_BLOCK = 256

NORM_EPS = 1e-6
LANES = 128
NEG_BIG = -1e30
VMEM_LIMIT = 56 * 1024 * 1024


def _cparams(sem):
    return pltpu.CompilerParams(dimension_semantics=sem, vmem_limit_bytes=VMEM_LIMIT)


def _rms(x, g):
    ms = jnp.mean(x * x, axis=-1, keepdims=True)
    return x * lax.rsqrt(ms + NORM_EPS) * g


def _silu(x):
    return x / (1.0 + jnp.exp(-x))


def _norm_matmul_kernel(x_ref, g_ref, w_ref, *rest, has_aux):
    if has_aux:
        wa_ref, o_ref, oa_ref, h_scr = rest
    else:
        o_ref, h_scr = rest

    @pl.when(pl.program_id(1) == 0)
    def _():
        h = _rms(x_ref[...], g_ref[...]).astype(BF16)
        h_scr[...] = h
        if has_aux:
            oa_ref[...] = jnp.dot(h, wa_ref[...], preferred_element_type=F32)

    o_ref[...] = jnp.dot(h_scr[...], w_ref[...], preferred_element_type=F32).astype(o_ref.dtype)


def _norm_matmul(x, g, w, w_aux=None, *, tm=1024, tn=512):
    t, d = x.shape
    n = w.shape[1]
    has_aux = w_aux is not None
    in_specs = [
        pl.BlockSpec((tm, d), lambda i, j: (i, 0)),
        pl.BlockSpec((1, d), lambda i, j: (0, 0)),
        pl.BlockSpec((d, tn), lambda i, j: (0, j)),
    ]
    out_shape = [jax.ShapeDtypeStruct((t, n), BF16)]
    out_specs = [pl.BlockSpec((tm, tn), lambda i, j: (i, j))]
    args = [x, g.reshape(1, d), w]
    if has_aux:
        na = w_aux.shape[1]
        in_specs.append(pl.BlockSpec((d, na), lambda i, j: (0, 0)))
        out_shape.append(jax.ShapeDtypeStruct((t, na), F32))
        out_specs.append(pl.BlockSpec((tm, na), lambda i, j: (i, 0)))
        args.append(w_aux)
    res = pl.pallas_call(
        functools.partial(_norm_matmul_kernel, has_aux=has_aux),
        grid=(t // tm, n // tn),
        in_specs=in_specs,
        out_specs=out_specs,
        out_shape=out_shape,
        scratch_shapes=[pltpu.VMEM((tm, d), BF16)],
        compiler_params=_cparams(("parallel", "arbitrary")),
        name="norm_matmul",
    )(*args)
    return res if has_aux else res[0]


def _matmul_residual_kernel(a_ref, w_ref, r_ref, o_ref):
    o_ref[...] = r_ref[...] + jnp.dot(a_ref[...], w_ref[...], preferred_element_type=F32)


def _matmul_residual(a, w, res, *, tm=512):
    t, k = a.shape
    n = w.shape[1]
    return pl.pallas_call(
        _matmul_residual_kernel,
        grid=(t // tm,),
        in_specs=[
            pl.BlockSpec((tm, k), lambda i: (i, 0)),
            pl.BlockSpec((k, n), lambda i: (0, 0)),
            pl.BlockSpec((tm, n), lambda i: (i, 0)),
        ],
        out_specs=pl.BlockSpec((tm, n), lambda i: (i, 0)),
        out_shape=jax.ShapeDtypeStruct((t, n), F32),
        compiler_params=_cparams(("parallel",)),
        name="matmul_residual",
    )(a, w, res)


def _attn_kernel(sinks_ref, q_ref, kp_ref, kc_ref, vp_ref, vc_ref, bias_ref, o_ref):
    first = pl.program_id(1) == 0
    col = lax.broadcasted_iota(jnp.int32, (ATTN_BLOCK, 2 * ATTN_BLOCK), 1)
    hide_prev = jnp.logical_and(first, col < ATTN_BLOCK)
    scale = HEAD_DIM ** -0.5
    for h in range(N_KV_HEADS):
        ks = slice(h * HEAD_DIM, (h + 1) * HEAD_DIM)
        kb = jnp.concatenate([kp_ref[:, ks], kc_ref[:, ks]], axis=0)
        vb = jnp.concatenate([vp_ref[:, ks], vc_ref[:, ks]], axis=0)
        for g in range(GQA_GROUP):
            hh = h * GQA_GROUP + g
            qh = q_ref[:, hh * HEAD_DIM:(hh + 1) * HEAD_DIM]
            s = lax.dot_general(qh, kb, (((1,), (1,)), ((), ())),
                                preferred_element_type=F32) * scale
            logits = jnp.where(hide_prev, NEG_BIG, s + bias_ref[hh])
            sink = sinks_ref[hh]
            m = jnp.maximum(jnp.max(logits, axis=-1, keepdims=True), sink)
            p = jnp.exp(logits - m)
            denom = jnp.sum(p, axis=-1, keepdims=True) + jnp.exp(sink - m)
            o = jnp.dot(p.astype(BF16), vb, preferred_element_type=F32) / denom
            o_ref[:, hh * HEAD_DIM:(hh + 1) * HEAD_DIM] = o.astype(o_ref.dtype)


def _t5_causal_bucket(dist):
    n = jnp.maximum(dist, 0)
    max_exact = REL_BUCKETS // 2
    nf = jnp.maximum(n, 1).astype(F32)
    large = max_exact + (jnp.log(nf / max_exact) / math.log(REL_MAX_DIST / max_exact)
                         * (REL_BUCKETS - max_exact)).astype(jnp.int32)
    large = jnp.minimum(large, REL_BUCKETS - 1)
    return jnp.where(n < max_exact, n, large)


def _attention(qkv, sinks, rel_bias, batch, seq):
    t = qkv.shape[0]
    nb = seq // ATTN_BLOCK
    qi = jnp.arange(ATTN_BLOCK)[:, None]
    ki = jnp.arange(2 * ATTN_BLOCK)[None, :]
    dist = qi + ATTN_BLOCK - ki
    in_window = (dist >= 0) & (dist < WINDOW)
    onehot = (_t5_causal_bucket(dist)[None] == jnp.arange(REL_BUCKETS)[:, None, None]).astype(F32)
    bias = jnp.einsum('hr,rqk->hqk', rel_bias.astype(F32).T, onehot, precision=lax.Precision.HIGHEST)
    bias = jnp.where(in_window[None], bias, NEG_BIG)

    kcol = Q_DIM // KV_DIM
    vcol = kcol + 1

    def prev(b, n):
        return b * nb + jnp.maximum(n - 1, 0)

    return pl.pallas_call(
        _attn_kernel,
        grid=(batch, nb),
        in_specs=[
            pl.BlockSpec(memory_space=pltpu.SMEM),
            pl.BlockSpec((ATTN_BLOCK, Q_DIM), lambda b, n: (b * nb + n, 0)),
            pl.BlockSpec((ATTN_BLOCK, KV_DIM), lambda b, n: (prev(b, n), kcol)),
            pl.BlockSpec((ATTN_BLOCK, KV_DIM), lambda b, n: (b * nb + n, kcol)),
            pl.BlockSpec((ATTN_BLOCK, KV_DIM), lambda b, n: (prev(b, n), vcol)),
            pl.BlockSpec((ATTN_BLOCK, KV_DIM), lambda b, n: (b * nb + n, vcol)),
            pl.BlockSpec((N_HEADS, ATTN_BLOCK, 2 * ATTN_BLOCK), lambda b, n: (0, 0, 0)),
        ],
        out_specs=pl.BlockSpec((ATTN_BLOCK, Q_DIM), lambda b, n: (b * nb + n, 0)),
        out_shape=jax.ShapeDtypeStruct((t, Q_DIM), BF16),
        compiler_params=_cparams(("parallel", "arbitrary")),
        name="swa_attention",
    )(sinks.astype(F32), qkv, qkv, qkv, qkv, qkv, bias)


def _ssd_kernel(z_ref, x_ref, bc_ref, dt_ref, convw_ref, convb_ref, dtb_ref, alog_ref,
                dskip_ref, normw_ref, o_ref, state, xstage, bcstage):
    c = pl.program_id(1)
    L = SSM_CHUNK

    @pl.when(c == 0)
    def _():
        state[...] = jnp.zeros_like(state)
        xstage[0:8, :] = jnp.zeros((8, D_INNER), F32)
        bcstage[0:8, :] = jnp.zeros((8, BC_DIM), F32)

    def conv_silu(u_ref, stage, w_lo, width):
        cur = u_ref[...].astype(F32)
        stage[8:8 + L, :] = cur
        w = convw_ref[:, w_lo:w_lo + width]
        y = (cur * w[3:4] + stage[7:7 + L, :] * w[2:3] + stage[6:6 + L, :] * w[1:2]
             + stage[5:5 + L, :] * w[0:1] + convb_ref[:, w_lo:w_lo + width])
        stage[0:8, :] = cur[L - 8:L]
        return _silu(y)

    xs = conv_silu(x_ref, xstage, 0, D_INNER)
    bcm = conv_silu(bc_ref, bcstage, D_INNER, BC_DIM)

    xdt = dt_ref[...] + dtb_ref[...]
    dt = jnp.maximum(xdt, 0.0) + jnp.log1p(jnp.exp(-jnp.abs(xdt)))
    a = dt * (-jnp.exp(alog_ref[...]))
    row = lax.broadcasted_iota(jnp.int32, (L, L), 0)
    colm = lax.broadcasted_iota(jnp.int32, (L, L), 1)
    causal = row >= colm
    tri = causal.astype(F32)
    a_cs = jnp.dot(tri, a, preferred_element_type=F32, precision=lax.Precision.HIGHEST)
    a_cs_t = a_cs.T
    dt_t = dt.T
    a_last_col = a_cs_t[:, L - 1:L]
    w_end_t = jnp.exp(a_last_col - a_cs_t) * dt_t

    erow = lax.broadcasted_iota(jnp.int32, (LANES, D_INNER), 0)
    ecol = lax.broadcasted_iota(jnp.int32, (LANES, D_INNER), 1)
    expand = (jnp.right_shift(ecol, 6) == erow).astype(F32)
    cd_rows = jnp.broadcast_to(jnp.exp(a_cs[L - 1:L, :]), (8, LANES))
    cd_exp = jnp.dot(cd_rows, expand, preferred_element_type=F32,
                     precision=lax.Precision.HIGHEST)[0:1]

    lane = lax.broadcasted_iota(jnp.int32, (L, LANES), 1)
    low_half = lane < SSM_HEAD_DIM

    y_tiles = []
    for g in range(SSM_GROUPS):
        b_g = bcm[:, g * D_STATE:(g + 1) * D_STATE]
        c_g = bcm[:, (SSM_GROUPS + g) * D_STATE:(SSM_GROUPS + g + 1) * D_STATE]
        b_gb = b_g.astype(BF16)
        c_gb = c_g.astype(BF16)
        cb = lax.dot_general(c_gb, b_gb, (((1,), (1,)), ((), ())), preferred_element_type=F32)
        b_gt = b_g.T
        s_g = state[:, g * GROUP_CH:(g + 1) * GROUP_CH]
        y_off_g = jnp.dot(c_gb, s_g.astype(BF16), preferred_element_type=F32)
        for jj in range(GROUP_CH // LANES):
            j = g * (GROUP_CH // LANES) + jj
            lhs_top = []
            lhs_bot = []
            escale = []
            for e in (2 * j, 2 * j + 1):
                acs_b = jnp.broadcast_to(a_cs[:, e:e + 1], (L, L))
                seg = acs_b - a_cs_t[e:e + 1, :]
                dec = jnp.exp(jnp.where(causal, seg, -jnp.inf))
                lhs_top.append((cb * dec * dt_t[e:e + 1, :]).astype(BF16))
                lhs_bot.append((b_gt * w_end_t[e:e + 1, :]).astype(BF16))
                escale.append(jnp.exp(acs_b))
            x_tile = xs[:, j * LANES:(j + 1) * LANES]
            x_lo = jnp.where(low_half, x_tile, 0.0).astype(BF16)
            x_hi = jnp.where(low_half, 0.0, x_tile).astype(BF16)
            lhs = jnp.concatenate([jnp.concatenate(lhs_top, axis=1),
                                   jnp.concatenate(lhs_bot, axis=1)], axis=0)
            rhs = jnp.concatenate([x_lo, x_hi], axis=0)
            r = jnp.dot(lhs, rhs, preferred_element_type=F32)
            y_off = y_off_g[:, jj * LANES:(jj + 1) * LANES]
            y_tiles.append(r[0:L] + jnp.where(low_half, escale[0], escale[1]) * y_off)
            sl = slice(j * LANES, (j + 1) * LANES)
            state[:, sl] = cd_exp[:, sl] * state[:, sl] + r[L:2 * L]

    y = jnp.concatenate(y_tiles, axis=1)
    y = y + xs * dskip_ref[...]
    gated = y * _silu(z_ref[...].astype(F32))
    outs = []
    for g in range(SSM_GROUPS):
        gg = gated[:, g * GROUP_CH:(g + 1) * GROUP_CH]
        ms = jnp.mean(gg * gg, axis=-1, keepdims=True)
        outs.append(gg * lax.rsqrt(ms + NORM_EPS))
    o_ref[...] = (jnp.concatenate(outs, axis=1) * normw_ref[...]).astype(o_ref.dtype)


def _ssd(zxbc, dt_raw, conv_w, conv_b, dt_bias, a_log, d_skip, norm_w, batch, seq):
    t = zxbc.shape[0]
    nc = seq // SSM_CHUNK
    L = SSM_CHUNK

    def pad_heads(v):
        return jnp.pad(v.astype(F32), (0, LANES - SSM_HEADS)).reshape(1, LANES)

    def rowmap(col):
        return lambda b, c: (b * nc + c, col)

    def const2(b, c):
        return (0, 0)

    return pl.pallas_call(
        _ssd_kernel,
        grid=(batch, nc),
        in_specs=[
            pl.BlockSpec((L, D_INNER), rowmap(0)),
            pl.BlockSpec((L, D_INNER), rowmap(1)),
            pl.BlockSpec((L, BC_DIM), rowmap(2 * D_INNER // BC_DIM)),
            pl.BlockSpec((L, LANES), rowmap(0)),
            pl.BlockSpec((CONV_WIDTH, CONV_DIM), const2),
            pl.BlockSpec((1, CONV_DIM), const2),
            pl.BlockSpec((1, LANES), const2),
            pl.BlockSpec((1, LANES), const2),
            pl.BlockSpec((1, D_INNER), const2),
            pl.BlockSpec((1, D_INNER), const2),
        ],
        out_specs=pl.BlockSpec((L, D_INNER), rowmap(0)),
        out_shape=jax.ShapeDtypeStruct((t, D_INNER), BF16),
        scratch_shapes=[
            pltpu.VMEM((D_STATE, D_INNER), F32),
            pltpu.VMEM((L + 8, D_INNER), F32),
            pltpu.VMEM((L + 8, BC_DIM), F32),
        ],
        compiler_params=_cparams(("parallel", "arbitrary")),
        name="ssd_core",
    )(zxbc, zxbc, zxbc, dt_raw, conv_w.astype(F32), conv_b.astype(F32).reshape(1, CONV_DIM),
      pad_heads(dt_bias), pad_heads(a_log),
      jnp.repeat(d_skip.astype(F32), SSM_HEAD_DIM).reshape(1, D_INNER),
      norm_w.astype(F32).reshape(1, D_INNER))


GROUP_LANE0 = N_EXPERTS


def _router_kernel(x_ref, g_ref, w_ref, b_ref, tril_ref, ids_ref, gates_ref, cnt_ref, carry):
    i = pl.program_id(0)

    @pl.when(i == 0)
    def _():
        carry[...] = jnp.zeros_like(carry)

    h = _rms(x_ref[...], g_ref[...])
    logits = jnp.dot(h, w_ref[...], preferred_element_type=F32,
                     precision=lax.Precision.HIGHEST) + b_ref[...]
    tm = logits.shape[0]
    lane = lax.broadcasted_iota(jnp.int32, (tm, LANES), 1)
    ninf = -jnp.inf

    def first_argmax(v, vmax):
        return jnp.min(jnp.where(v == vmax, lane, LANES), axis=-1, keepdims=True)

    is_group = (lane >= GROUP_LANE0) & (lane < GROUP_LANE0 + N_EXPERT_GROUPS)
    glog = jnp.where(is_group, logits, ninf)
    gmax = jnp.max(glog, axis=-1, keepdims=True)
    g_idx = first_argmax(glog, gmax) - GROUP_LANE0
    g_p = 1.0 / jnp.sum(jnp.exp(glog - gmax), axis=-1, keepdims=True)

    in_group = (lane < N_EXPERTS) & (jnp.right_shift(lane, 3) == g_idx)
    elog = jnp.where(in_group, logits, ninf)
    m1 = jnp.max(elog, axis=-1, keepdims=True)
    e1 = first_argmax(elog, m1)
    elog2 = jnp.where(lane == e1, ninf, elog)
    m2 = jnp.max(elog2, axis=-1, keepdims=True)
    e2 = first_argmax(elog2, m2)
    zsum = jnp.sum(jnp.exp(elog - m1), axis=-1, keepdims=True)
    p1 = 1.0 / zsum
    p2 = jnp.exp(m2 - m1) / zsum
    psum = p1 + p2
    w1 = p1 / psum * g_p
    w2 = p2 / psum * g_p

    oh1 = lane == e1
    oh2 = lane == e2
    onehot = jnp.logical_or(oh1, oh2).astype(F32)
    before = jnp.dot(tril_ref[...], onehot.astype(BF16), preferred_element_type=F32) + carry[0:1, :]
    r1 = jnp.sum(jnp.where(oh1, before, 0.0), axis=-1, keepdims=True).astype(jnp.int32)
    r2 = jnp.sum(jnp.where(oh2, before, 0.0), axis=-1, keepdims=True).astype(jnp.int32)
    carry[0:1, :] = carry[0:1, :] + jnp.sum(onehot, axis=0, keepdims=True)

    ids_ref[...] = jnp.where(lane == 0, e1, jnp.where(lane == 1, e2,
                             jnp.where(lane == 2, r1, jnp.where(lane == 3, r2, 0))))
    gates_ref[...] = jnp.where(lane == 0, w1, jnp.where(lane == 1, w2, 0.0))
    cnt_ref[...] = carry[...]


def _router(x, ln_w, w_group, b_group, w_expert, b_expert, *, tm=512):
    t, d = x.shape
    pad = LANES - N_EXPERTS - N_EXPERT_GROUPS
    w_r = jnp.concatenate([w_expert, w_group, jnp.zeros((d, pad), F32)], axis=1).astype(F32)
    b_r = jnp.concatenate([b_expert, b_group, jnp.zeros((pad,), F32)]).astype(F32).reshape(1, LANES)
    tril = (jnp.arange(tm)[:, None] > jnp.arange(tm)[None, :]).astype(BF16)
    return pl.pallas_call(
        _router_kernel,
        grid=(t // tm,),
        in_specs=[
            pl.BlockSpec((tm, d), lambda i: (i, 0)),
            pl.BlockSpec((1, d), lambda i: (0, 0)),
            pl.BlockSpec((d, LANES), lambda i: (0, 0)),
            pl.BlockSpec((1, LANES), lambda i: (0, 0)),
            pl.BlockSpec((tm, tm), lambda i: (0, 0)),
        ],
        out_specs=[
            pl.BlockSpec((tm, LANES), lambda i: (i, 0)),
            pl.BlockSpec((tm, LANES), lambda i: (i, 0)),
            pl.BlockSpec((8, LANES), lambda i: (0, 0)),
        ],
        out_shape=[
            jax.ShapeDtypeStruct((t, LANES), jnp.int32),
            jax.ShapeDtypeStruct((t, LANES), F32),
            jax.ShapeDtypeStruct((8, LANES), F32),
        ],
        scratch_shapes=[pltpu.VMEM((8, LANES), F32)],
        compiler_params=_cparams(("arbitrary",)),
        name="moe_router",
    )(x, ln_w.reshape(1, d), w_r, b_r, tril)


def _dispatch_kernel(zf_ref, dest_ref, x_ref, g_ref, xs_ref, hbuf, zbuf, sem, zsem):
    tb = hbuf.shape[0]
    n_blocks = zf_ref.shape[0]

    @pl.when(pl.program_id(0) == 0)
    def _():
        zbuf[...] = jnp.zeros_like(zbuf)

        def zero_copy(b):
            start = pl.multiple_of(b * MOE_BLOCK, MOE_BLOCK)
            return pltpu.make_async_copy(zbuf, xs_ref.at[pl.ds(start, MOE_BLOCK)], zsem)

        def zissue(b, carry):
            @pl.when(zf_ref[b] != 0)
            def _():
                zero_copy(b).start()
            return carry

        def zdrain(b, carry):
            @pl.when(zf_ref[b] != 0)
            def _():
                zero_copy(b).wait()
            return carry

        lax.fori_loop(0, n_blocks, zissue, 0)
        lax.fori_loop(0, n_blocks, zdrain, 0)

    hbuf[...] = _rms(x_ref[...], g_ref[...])
    for r in range(tb):
        for k in range(2):
            d = dest_ref[0, 0, 2 * r + k]
            pltpu.make_async_copy(hbuf.at[pl.ds(r, 1)], xs_ref.at[pl.ds(d, 1)], sem).start()
    for k in range(2):
        pltpu.make_async_copy(hbuf, xs_ref.at[pl.ds(0, tb)], sem).wait()


def _dispatch(x, ln_w, dest, zero_flag, n_slots, *, tb=MOE_BLOCK):
    t, d = x.shape
    nblk = t // tb
    return pl.pallas_call(
        _dispatch_kernel,
        grid_spec=pltpu.PrefetchScalarGridSpec(
            num_scalar_prefetch=1,
            grid=(nblk,),
            in_specs=[
                pl.BlockSpec((1, 1, 2 * tb), lambda i, zf: (i, 0, 0), memory_space=pltpu.SMEM),
                pl.BlockSpec((tb, d), lambda i, zf: (i, 0)),
                pl.BlockSpec((1, d), lambda i, zf: (0, 0)),
            ],
            out_specs=pl.BlockSpec(memory_space=pl.ANY),
            scratch_shapes=[pltpu.VMEM((tb, d), F32), pltpu.VMEM((MOE_BLOCK, d), F32),
                            pltpu.SemaphoreType.DMA(()), pltpu.SemaphoreType.DMA(())],
        ),
        out_shape=jax.ShapeDtypeStruct((n_slots, d), F32),
        compiler_params=_cparams(("arbitrary",)),
        name="moe_dispatch",
    )(zero_flag, dest.reshape(nblk, 1, 2 * tb), x, ln_w.reshape(1, d))


def _expert_kernel(be_ref, na_ref, xs_ref, wg_ref, wu_ref, wd_ref, y_ref, wg_s, wu_s, wd_s):
    i = pl.program_id(0)
    active = i < na_ref[0]
    changed = jnp.logical_or(i == 0, be_ref[i] != be_ref[jnp.maximum(i - 1, 0)])

    @pl.when(jnp.logical_and(active, changed))
    def _():
        wg_s[...] = wg_ref[...].astype(BF16)
        wu_s[...] = wu_ref[...].astype(BF16)
        wd_s[...] = wd_ref[...].astype(BF16)

    @pl.when(active)
    def _():
        xb = xs_ref[...].astype(BF16)
        gate = jnp.dot(xb, wg_s[...], preferred_element_type=F32)
        up = jnp.dot(xb, wu_s[...], preferred_element_type=F32)
        hid = (_silu(gate) * up).astype(BF16)
        y_ref[...] = jnp.dot(hid, wd_s[...], preferred_element_type=F32)

    @pl.when(jnp.logical_not(active))
    def _():
        y_ref[...] = jnp.zeros_like(y_ref)


def _experts(xs, block_e, n_active, w_gate, w_up, w_down, layer):
    n_slots, d = xs.shape
    n_blocks = n_slots // MOE_BLOCK

    def xmap(i, be, na):
        return (jnp.minimum(i, na[0] - 1), 0)

    def wmap(i, be, na):
        return (layer, be[i], 0, 0)

    return pl.pallas_call(
        _expert_kernel,
        grid_spec=pltpu.PrefetchScalarGridSpec(
            num_scalar_prefetch=2,
            grid=(n_blocks,),
            in_specs=[
                pl.BlockSpec((MOE_BLOCK, d), xmap),
                pl.BlockSpec((None, None, d, EXPERT_FF), wmap),
                pl.BlockSpec((None, None, d, EXPERT_FF), wmap),
                pl.BlockSpec((None, None, EXPERT_FF, d), wmap),
            ],
            out_specs=pl.BlockSpec((MOE_BLOCK, d), lambda i, be, na: (i, 0)),
            scratch_shapes=[
                pltpu.VMEM((d, EXPERT_FF), BF16),
                pltpu.VMEM((d, EXPERT_FF), BF16),
                pltpu.VMEM((EXPERT_FF, d), BF16),
            ],
        ),
        out_shape=jax.ShapeDtypeStruct((n_slots, d), F32),
        compiler_params=_cparams(("arbitrary",)),
        name="moe_experts",
    )(block_e, n_active, xs, w_gate, w_up, w_down)


def _combine_kernel(dest_ref, x_ref, gates_ref, y_ref, *rest, final_norm):
    if final_norm:
        g_ref, o_ref, buf0, buf1, sem = rest
    else:
        o_ref, buf0, buf1, sem = rest
    tb = buf0.shape[0]
    bufs = (buf0, buf1)

    for r in range(tb):
        for k in range(2):
            d = dest_ref[0, 0, 2 * r + k]
            pltpu.make_async_copy(y_ref.at[pl.ds(d, 1)], bufs[k].at[pl.ds(r, 1)], sem).start()
    for k in range(2):
        pltpu.make_async_copy(y_ref.at[pl.ds(0, tb)], bufs[k], sem).wait()

    gt = gates_ref[...]
    out = x_ref[...] + (buf0[...] * gt[:, 0:1] + buf1[...] * gt[:, 1:2])
    if final_norm:
        out = _rms(out, g_ref[...])
    o_ref[...] = out


def _combine(x, gates, y, dest, final_w=None, *, tb=256):
    t, d = x.shape
    nblk = t // tb
    final_norm = final_w is not None
    in_specs = [
        pl.BlockSpec((1, 1, 2 * tb), lambda i: (i, 0, 0), memory_space=pltpu.SMEM),
        pl.BlockSpec((tb, d), lambda i: (i, 0)),
        pl.BlockSpec((tb, LANES), lambda i: (i, 0)),
        pl.BlockSpec(memory_space=pl.ANY),
    ]
    args = [dest.reshape(nblk, 1, 2 * tb), x, gates, y]
    if final_norm:
        in_specs.append(pl.BlockSpec((1, d), lambda i: (0, 0)))
        args.append(final_w.reshape(1, d))
    return pl.pallas_call(
        functools.partial(_combine_kernel, final_norm=final_norm),
        grid=(nblk,),
        in_specs=in_specs,
        out_specs=pl.BlockSpec((tb, d), lambda i: (i, 0)),
        out_shape=jax.ShapeDtypeStruct((t, d), F32),
        scratch_shapes=[pltpu.VMEM((tb, d), F32), pltpu.VMEM((tb, d), F32),
                        pltpu.SemaphoreType.DMA(())],
        compiler_params=_cparams(("arbitrary",)),
        name="moe_combine",
    )(*args)


def _moe(x, ln_w, w_group, b_group, w_expert, b_expert, w_gate, w_up, w_down, layer, final_w=None):
    t = x.shape[0]
    n_assign = t * 2
    n_blocks = -(-n_assign // MOE_BLOCK) + N_EXPERTS
    n_slots = n_blocks * MOE_BLOCK

    ids, gates, cnt = _router(x, ln_w, w_group, b_group, w_expert, b_expert)
    counts = cnt[0, :N_EXPERTS].astype(jnp.int32)
    padded = (counts + MOE_BLOCK - 1) // MOE_BLOCK * MOE_BLOCK
    p_ends = jnp.cumsum(padded)
    p_starts = p_ends - padded
    experts = jnp.arange(N_EXPERTS, dtype=jnp.int32)
    start_of = jnp.sum(jnp.where(ids[:, 0:2, None] == experts, p_starts, 0), axis=-1)
    dest = (start_of + ids[:, 2:4]).astype(jnp.int32)
    block_start = jnp.arange(n_blocks, dtype=jnp.int32) * MOE_BLOCK
    block_e = jnp.minimum(jnp.sum((p_ends[None, :] <= block_start[:, None]).astype(jnp.int32), axis=1),
                          N_EXPERTS - 1)
    n_active = (p_ends[-1:] // MOE_BLOCK).astype(jnp.int32)
    valid_end = jnp.sum(jnp.where(block_e[:, None] == experts, p_starts + counts, 0), axis=-1)
    zero_flag = jnp.logical_or(block_start + MOE_BLOCK > valid_end,
                               block_start >= p_ends[-1]).astype(jnp.int32)

    xs = _dispatch(x, ln_w, dest, zero_flag, n_slots)
    y = _experts(xs, block_e, n_active, w_gate, w_up, w_down, layer)
    return _combine(x, gates, y, dest, final_w)


def kernel(x, rel_bias, ln_mix, ln_ffn, ln_final, attn_w_qkv, attn_sinks, attn_w_o, ssm_w_in,
           ssm_conv_w, ssm_conv_b, ssm_dt_bias, ssm_a_log, ssm_d, ssm_norm_w, ssm_w_out,
           moe_w_group, moe_b_group, moe_w_expert, moe_b_expert, moe_w_gate, moe_w_up, moe_w_down):
    batch, seq, d = x.shape
    t = batch * seq
    xf = x.reshape(t, d).astype(F32)

    qkv = _norm_matmul(xf, ln_mix[0], attn_w_qkv[0].astype(BF16))
    att = _attention(qkv, attn_sinks[0], rel_bias, batch, seq)
    xf = _matmul_residual(att, attn_w_o[0].astype(BF16), xf)
    xf = _moe(xf, ln_ffn[0], moe_w_group[0], moe_b_group[0], moe_w_expert[0], moe_b_expert[0],
              moe_w_gate, moe_w_up, moe_w_down, 0)

    w_in = ssm_w_in[0]
    w_dt = jnp.pad(w_in[:, ZXBC_DIM:], ((0, 0), (0, LANES - SSM_HEADS))).astype(BF16)
    zxbc, dt_raw = _norm_matmul(xf, ln_mix[1], w_in[:, :ZXBC_DIM].astype(BF16), w_dt)
    gated = _ssd(zxbc, dt_raw, ssm_conv_w[0], ssm_conv_b[0], ssm_dt_bias[0], ssm_a_log[0],
                 ssm_d[0], ssm_norm_w[0], batch, seq)
    xf = _matmul_residual(gated, ssm_w_out[0].astype(BF16), xf)
    xf = _moe(xf, ln_ffn[1], moe_w_group[1], moe_b_group[1], moe_w_expert[1], moe_b_expert[1],
              moe_w_gate, moe_w_up, moe_w_down, 1, final_w=ln_final)
    return xf.reshape(batch, seq, d).astype(x.dtype)
```

```python
import functools
import math

import jax
import jax.numpy as jnp
from jax import lax
from jax.experimental import pallas as pl
from jax.experimental.pallas import tpu as pltpu

F32 = jnp.float32
BF16 = jnp.bfloat16

D_MODEL = 1024
N_HEADS = 16
N_KV_HEADS = 4
HEAD_DIM = 64
GQA_GROUP = N_HEADS // N_KV_HEADS
WINDOW = 128
ATTN_BLOCK = 128
Q_DIM = N_HEADS * HEAD_DIM
KV_DIM = N_KV_HEADS * HEAD_DIM
QKV_DIM = Q_DIM + 2 * KV_DIM
REL_BUCKETS = 32
REL_MAX_DIST = 128

D_INNER = 2048
SSM_HEAD_DIM = 64
SSM_HEADS = D_INNER // SSM_HEAD_DIM
SSM_GROUPS = 4
D_STATE = 128
CONV_WIDTH = 4
BC_DIM = 2 * SSM_GROUPS * D_STATE
CONV_DIM = D_INNER + BC_DIM
ZXBC_DIM = D_INNER + CONV_DIM
SSM_CHUNK = 128
GROUP_CH = D_INNER // SSM_GROUPS

N_EXPERT_GROUPS = 8
EXPERTS_PER_GROUP = 8
N_EXPERTS = 64
EXPERT_FF = 512
MOE_BLOCK = 256

NORM_EPS = 1e-6
LANES = 128
NEG_BIG = -1e30
VMEM_LIMIT = 56 * 1024 * 1024


def _cparams(sem):
    return pltpu.CompilerParams(dimension_semantics=sem, vmem_limit_bytes=VMEM_LIMIT)


def _rms(x, g):
    ms = jnp.mean(x * x, axis=-1, keepdims=True)
    return x * lax.rsqrt(ms + NORM_EPS) * g


def _silu(x):
    return x / (1.0 + jnp.exp(-x))


def _norm_matmul_kernel(x_ref, g_ref, w_ref, *rest, has_aux):
    if has_aux:
        wa_ref, o_ref, oa_ref, h_scr = rest
    else:
        o_ref, h_scr = rest

    @pl.when(pl.program_id(1) == 0)
    def _():
        h = _rms(x_ref[...], g_ref[...]).astype(BF16)
        h_scr[...] = h
        if has_aux:
            oa_ref[...] = jnp.dot(h, wa_ref[...], preferred_element_type=F32)

    o_ref[...] = jnp.dot(h_scr[...], w_ref[...], preferred_element_type=F32).astype(o_ref.dtype)


def _norm_matmul(x, g, w, w_aux=None, *, tm=1024, tn=512):
    t, d = x.shape
    n = w.shape[1]
    has_aux = w_aux is not None
    in_specs = [
        pl.BlockSpec((tm, d), lambda i, j: (i, 0)),
        pl.BlockSpec((1, d), lambda i, j: (0, 0)),
        pl.BlockSpec((d, tn), lambda i, j: (0, j)),
    ]
    out_shape = [jax.ShapeDtypeStruct((t, n), BF16)]
    out_specs = [pl.BlockSpec((tm, tn), lambda i, j: (i, j))]
    args = [x, g.reshape(1, d), w]
    if has_aux:
        na = w_aux.shape[1]
        in_specs.append(pl.BlockSpec((d, na), lambda i, j: (0, 0)))
        out_shape.append(jax.ShapeDtypeStruct((t, na), F32))
        out_specs.append(pl.BlockSpec((tm, na), lambda i, j: (i, 0)))
        args.append(w_aux)
    res = pl.pallas_call(
        functools.partial(_norm_matmul_kernel, has_aux=has_aux),
        grid=(t // tm, n // tn),
        in_specs=in_specs,
        out_specs=out_specs,
        out_shape=out_shape,
        scratch_shapes=[pltpu.VMEM((tm, d), BF16)],
        compiler_params=_cparams(("parallel", "arbitrary")),
        name="norm_matmul",
    )(*args)
    return res if has_aux else res[0]


def _matmul_residual_kernel(a_ref, w_ref, r_ref, o_ref):
    o_ref[...] = r_ref[...] + jnp.dot(a_ref[...], w_ref[...], preferred_element_type=F32)


def _matmul_residual(a, w, res, *, tm=512):
    t, k = a.shape
    n = w.shape[1]
    return pl.pallas_call(
        _matmul_residual_kernel,
        grid=(t // tm,),
        in_specs=[
            pl.BlockSpec((tm, k), lambda i: (i, 0)),
            pl.BlockSpec((k, n), lambda i: (0, 0)),
            pl.BlockSpec((tm, n), lambda i: (i, 0)),
        ],
        out_specs=pl.BlockSpec((tm, n), lambda i: (i, 0)),
        out_shape=jax.ShapeDtypeStruct((t, n), F32),
        compiler_params=_cparams(("parallel",)),
        name="matmul_residual",
    )(a, w, res)


def _attn_kernel(sinks_ref, q_ref, kp_ref, kc_ref, vp_ref, vc_ref, bias_ref, o_ref):
    first = pl.program_id(1) == 0
    col = lax.broadcasted_iota(jnp.int32, (ATTN_BLOCK, 2 * ATTN_BLOCK), 1)
    hide_prev = jnp.logical_and(first, col < ATTN_BLOCK)
    scale = HEAD_DIM ** -0.5
    for h in range(N_KV_HEADS):
        ks = slice(h * HEAD_DIM, (h + 1) * HEAD_DIM)
        kb = jnp.concatenate([kp_ref[:, ks], kc_ref[:, ks]], axis=0)
        vb = jnp.concatenate([vp_ref[:, ks], vc_ref[:, ks]], axis=0)
        for g in range(GQA_GROUP):
            hh = h * GQA_GROUP + g
            qh = q_ref[:, hh * HEAD_DIM:(hh + 1) * HEAD_DIM]
            s = lax.dot_general(qh, kb, (((1,), (1,)), ((), ())),
                                preferred_element_type=F32) * scale
            logits = jnp.where(hide_prev, NEG_BIG, s + bias_ref[hh])
            sink = sinks_ref[hh]
            m = jnp.maximum(jnp.max(logits, axis=-1, keepdims=True), sink)
            p = jnp.exp(logits - m)
            denom = jnp.sum(p, axis=-1, keepdims=True) + jnp.exp(sink - m)
            o = jnp.dot(p.astype(BF16), vb, preferred_element_type=F32) / denom
            o_ref[:, hh * HEAD_DIM:(hh + 1) * HEAD_DIM] = o.astype(o_ref.dtype)


def _t5_causal_bucket(dist):
    n = jnp.maximum(dist, 0)
    max_exact = REL_BUCKETS // 2
    nf = jnp.maximum(n, 1).astype(F32)
    large = max_exact + (jnp.log(nf / max_exact) / math.log(REL_MAX_DIST / max_exact)
                         * (REL_BUCKETS - max_exact)).astype(jnp.int32)
    large = jnp.minimum(large, REL_BUCKETS - 1)
    return jnp.where(n < max_exact, n, large)


def _attention(qkv, sinks, rel_bias, batch, seq):
    t = qkv.shape[0]
    nb = seq // ATTN_BLOCK
    qi = jnp.arange(ATTN_BLOCK)[:, None]
    ki = jnp.arange(2 * ATTN_BLOCK)[None, :]
    dist = qi + ATTN_BLOCK - ki
    in_window = (dist >= 0) & (dist < WINDOW)
    onehot = (_t5_causal_bucket(dist)[None] == jnp.arange(REL_BUCKETS)[:, None, None]).astype(F32)
    bias = jnp.einsum('hr,rqk->hqk', rel_bias.astype(F32).T, onehot, precision=lax.Precision.HIGHEST)
    bias = jnp.where(in_window[None], bias, NEG_BIG)

    kcol = Q_DIM // KV_DIM
    vcol = kcol + 1

    def prev(b, n):
        return b * nb + jnp.maximum(n - 1, 0)

    return pl.pallas_call(
        _attn_kernel,
        grid=(batch, nb),
        in_specs=[
            pl.BlockSpec(memory_space=pltpu.SMEM),
            pl.BlockSpec((ATTN_BLOCK, Q_DIM), lambda b, n: (b * nb + n, 0)),
            pl.BlockSpec((ATTN_BLOCK, KV_DIM), lambda b, n: (prev(b, n), kcol)),
            pl.BlockSpec((ATTN_BLOCK, KV_DIM), lambda b, n: (b * nb + n, kcol)),
            pl.BlockSpec((ATTN_BLOCK, KV_DIM), lambda b, n: (prev(b, n), vcol)),
            pl.BlockSpec((ATTN_BLOCK, KV_DIM), lambda b, n: (b * nb + n, vcol)),
            pl.BlockSpec((N_HEADS, ATTN_BLOCK, 2 * ATTN_BLOCK), lambda b, n: (0, 0, 0)),
        ],
        out_specs=pl.BlockSpec((ATTN_BLOCK, Q_DIM), lambda b, n: (b * nb + n, 0)),
        out_shape=jax.ShapeDtypeStruct((t, Q_DIM), BF16),
        compiler_params=_cparams(("parallel", "arbitrary")),
        name="swa_attention",
    )(sinks.astype(F32), qkv, qkv, qkv, qkv, qkv, bias)


def _ssd_kernel(z_ref, x_ref, bc_ref, dt_ref, convw_ref, convb_ref, dtb_ref, alog_ref,
                dskip_ref, normw_ref, o_ref, state, xstage, bcstage):
    c = pl.program_id(1)
    L = SSM_CHUNK

    @pl.when(c == 0)
    def _():
        state[...] = jnp.zeros_like(state)
        xstage[0:8, :] = jnp.zeros((8, D_INNER), F32)
        bcstage[0:8, :] = jnp.zeros((8, BC_DIM), F32)

    def conv_silu(u_ref, stage, w_lo, width):
        cur = u_ref[...].astype(F32)
        stage[8:8 + L, :] = cur
        w = convw_ref[:, w_lo:w_lo + width]
        y = (cur * w[3:4] + stage[7:7 + L, :] * w[2:3] + stage[6:6 + L, :] * w[1:2]
             + stage[5:5 + L, :] * w[0:1] + convb_ref[:, w_lo:w_lo + width])
        stage[0:8, :] = cur[L - 8:L]
        return _silu(y)

    xs = conv_silu(x_ref, xstage, 0, D_INNER)
    bcm = conv_silu(bc_ref, bcstage, D_INNER, BC_DIM)

    xdt = dt_ref[...] + dtb_ref[...]
    dt = jnp.maximum(xdt, 0.0) + jnp.log1p(jnp.exp(-jnp.abs(xdt)))
    a = dt * (-jnp.exp(alog_ref[...]))
    row = lax.broadcasted_iota(jnp.int32, (L, L), 0)
    colm = lax.broadcasted_iota(jnp.int32, (L, L), 1)
    causal = row >= colm
    tri = causal.astype(F32)
    a_cs = jnp.dot(tri, a, preferred_element_type=F32, precision=lax.Precision.HIGHEST)
    a_cs_t = a_cs.T
    dt_t = dt.T
    a_last_col = a_cs_t[:, L - 1:L]
    w_end_t = jnp.exp(a_last_col - a_cs_t) * dt_t

    erow = lax.broadcasted_iota(jnp.int32, (LANES, D_INNER), 0)
    ecol = lax.broadcasted_iota(jnp.int32, (LANES, D_INNER), 1)
    expand = (jnp.right_shift(ecol, 6) == erow).astype(F32)
    cd_rows = jnp.broadcast_to(jnp.exp(a_cs[L - 1:L, :]), (8, LANES))
    cd_exp = jnp.dot(cd_rows, expand, preferred_element_type=F32,
                     precision=lax.Precision.HIGHEST)[0:1]

    lane = lax.broadcasted_iota(jnp.int32, (L, LANES), 1)
    low_half = lane < SSM_HEAD_DIM

    y_tiles = []
    for g in range(SSM_GROUPS):
        b_g = bcm[:, g * D_STATE:(g + 1) * D_STATE]
        c_g = bcm[:, (SSM_GROUPS + g) * D_STATE:(SSM_GROUPS + g + 1) * D_STATE]
        b_gb = b_g.astype(BF16)
        c_gb = c_g.astype(BF16)
        cb = lax.dot_general(c_gb, b_gb, (((1,), (1,)), ((), ())), preferred_element_type=F32)
        b_gt = b_g.T
        s_g = state[:, g * GROUP_CH:(g + 1) * GROUP_CH]
        y_off_g = jnp.dot(c_gb, s_g.astype(BF16), preferred_element_type=F32)
        for jj in range(GROUP_CH // LANES):
            j = g * (GROUP_CH // LANES) + jj
            lhs_top = []
            lhs_bot = []
            escale = []
            for e in (2 * j, 2 * j + 1):
                acs_b = jnp.broadcast_to(a_cs[:, e:e + 1], (L, L))
                seg = acs_b - a_cs_t[e:e + 1, :]
                dec = jnp.exp(jnp.where(causal, seg, -jnp.inf))
                lhs_top.append((cb * dec * dt_t[e:e + 1, :]).astype(BF16))
                lhs_bot.append((b_gt * w_end_t[e:e + 1, :]).astype(BF16))
                escale.append(jnp.exp(acs_b))
            x_tile = xs[:, j * LANES:(j + 1) * LANES]
            x_lo = jnp.where(low_half, x_tile, 0.0).astype(BF16)
            x_hi = jnp.where(low_half, 0.0, x_tile).astype(BF16)
            lhs = jnp.concatenate([jnp.concatenate(lhs_top, axis=1),
                                   jnp.concatenate(lhs_bot, axis=1)], axis=0)
            rhs = jnp.concatenate([x_lo, x_hi], axis=0)
            r = jnp.dot(lhs, rhs, preferred_element_type=F32)
            y_off = y_off_g[:, jj * LANES:(jj + 1) * LANES]
            y_tiles.append(r[0:L] + jnp.where(low_half, escale[0], escale[1]) * y_off)
            sl = slice(j * LANES, (j + 1) * LANES)
            state[:, sl] = cd_exp[:, sl] * state[:, sl] + r[L:2 * L]

    y = jnp.concatenate(y_tiles, axis=1)
    y = y + xs * dskip_ref[...]
    gated = y * _silu(z_ref[...].astype(F32))
    outs = []
    for g in range(SSM_GROUPS):
        gg = gated[:, g * GROUP_CH:(g + 1) * GROUP_CH]
        ms = jnp.mean(gg * gg, axis=-1, keepdims=True)
        outs.append(gg * lax.rsqrt(ms + NORM_EPS))
    o_ref[...] = (jnp.concatenate(outs, axis=1) * normw_ref[...]).astype(o_ref.dtype)


def _ssd(zxbc, dt_raw, conv_w, conv_b, dt_bias, a_log, d_skip, norm_w, batch, seq):
    t = zxbc.shape[0]
    nc = seq // SSM_CHUNK
    L = SSM_CHUNK

    def pad_heads(v):
        return jnp.pad(v.astype(F32), (0, LANES - SSM_HEADS)).reshape(1, LANES)

    def rowmap(col):
        return lambda b, c: (b * nc + c, col)

    def const2(b, c):
        return (0, 0)

    return pl.pallas_call(
        _ssd_kernel,
        grid=(batch, nc),
        in_specs=[
            pl.BlockSpec((L, D_INNER), rowmap(0)),
            pl.BlockSpec((L, D_INNER), rowmap(1)),
            pl.BlockSpec((L, BC_DIM), rowmap(2 * D_INNER // BC_DIM)),
            pl.BlockSpec((L, LANES), rowmap(0)),
            pl.BlockSpec((CONV_WIDTH, CONV_DIM), const2),
            pl.BlockSpec((1, CONV_DIM), const2),
            pl.BlockSpec((1, LANES), const2),
            pl.BlockSpec((1, LANES), const2),
            pl.BlockSpec((1, D_INNER), const2),
            pl.BlockSpec((1, D_INNER), const2),
        ],
        out_specs=pl.BlockSpec((L, D_INNER), rowmap(0)),
        out_shape=jax.ShapeDtypeStruct((t, D_INNER), BF16),
        scratch_shapes=[
            pltpu.VMEM((D_STATE, D_INNER), F32),
            pltpu.VMEM((L + 8, D_INNER), F32),
            pltpu.VMEM((L + 8, BC_DIM), F32),
        ],
        compiler_params=_cparams(("parallel", "arbitrary")),
        name="ssd_core",
    )(zxbc, zxbc, zxbc, dt_raw, conv_w.astype(F32), conv_b.astype(F32).reshape(1, CONV_DIM),
      pad_heads(dt_bias), pad_heads(a_log),
      jnp.repeat(d_skip.astype(F32), SSM_HEAD_DIM).reshape(1, D_INNER),
      norm_w.astype(F32).reshape(1, D_INNER))


GROUP_LANE0 = N_EXPERTS


def _router_kernel(x_ref, g_ref, w_ref, b_ref, tril_ref, ids_ref, gates_ref, cnt_ref, carry):
    i = pl.program_id(0)

    @pl.when(i == 0)
    def _():
        carry[...] = jnp.zeros_like(carry)

    h = _rms(x_ref[...], g_ref[...])
    logits = jnp.dot(h, w_ref[...], preferred_element_type=F32,
                     precision=lax.Precision.HIGHEST) + b_ref[...]
    tm = logits.shape[0]
    lane = lax.broadcasted_iota(jnp.int32, (tm, LANES), 1)
    ninf = -jnp.inf

    def first_argmax(v, vmax):
        return jnp.min(jnp.where(v == vmax, lane, LANES), axis=-1, keepdims=True)

    is_group = (lane >= GROUP_LANE0) & (lane < GROUP_LANE0 + N_EXPERT_GROUPS)
    glog = jnp.where(is_group, logits, ninf)
    gmax = jnp.max(glog, axis=-1, keepdims=True)
    g_idx = first_argmax(glog, gmax) - GROUP_LANE0
    g_p = 1.0 / jnp.sum(jnp.exp(glog - gmax), axis=-1, keepdims=True)

    in_group = (lane < N_EXPERTS) & (jnp.right_shift(lane, 3) == g_idx)
    elog = jnp.where(in_group, logits, ninf)
    m1 = jnp.max(elog, axis=-1, keepdims=True)
    e1 = first_argmax(elog, m1)
    elog2 = jnp.where(lane == e1, ninf, elog)
    m2 = jnp.max(elog2, axis=-1, keepdims=True)
    e2 = first_argmax(elog2, m2)
    zsum = jnp.sum(jnp.exp(elog - m1), axis=-1, keepdims=True)
    p1 = 1.0 / zsum
    p2 = jnp.exp(m2 - m1) / zsum
    psum = p1 + p2
    w1 = p1 / psum * g_p
    w2 = p2 / psum * g_p

    oh1 = lane == e1
    oh2 = lane == e2
    onehot = jnp.logical_or(oh1, oh2).astype(F32)
    before = jnp.dot(tril_ref[...], onehot.astype(BF16), preferred_element_type=F32) + carry[0:1, :]
    r1 = jnp.sum(jnp.where(oh1, before, 0.0), axis=-1, keepdims=True).astype(jnp.int32)
    r2 = jnp.sum(jnp.where(oh2, before, 0.0), axis=-1, keepdims=True).astype(jnp.int32)
    carry[0:1, :] = carry[0:1, :] + jnp.sum(onehot, axis=0, keepdims=True)

    ids_ref[...] = jnp.where(lane == 0, e1, jnp.where(lane == 1, e2,
                             jnp.where(lane == 2, r1, jnp.where(lane == 3, r2, 0))))
    gates_ref[...] = jnp.where(lane == 0, w1, jnp.where(lane == 1, w2, 0.0))
    cnt_ref[...] = carry[...]


def _router(x, ln_w, w_group, b_group, w_expert, b_expert, *, tm=512):
    t, d = x.shape
    pad = LANES - N_EXPERTS - N_EXPERT_GROUPS
    w_r = jnp.concatenate([w_expert, w_group, jnp.zeros((d, pad), F32)], axis=1).astype(F32)
    b_r = jnp.concatenate([b_expert, b_group, jnp.zeros((pad,), F32)]).astype(F32).reshape(1, LANES)
    tril = (jnp.arange(tm)[:, None] > jnp.arange(tm)[None, :]).astype(BF16)
    return pl.pallas_call(
        _router_kernel,
        grid=(t // tm,),
        in_specs=[
            pl.BlockSpec((tm, d), lambda i: (i, 0)),
            pl.BlockSpec((1, d), lambda i: (0, 0)),
            pl.BlockSpec((d, LANES), lambda i: (0, 0)),
            pl.BlockSpec((1, LANES), lambda i: (0, 0)),
            pl.BlockSpec((tm, tm), lambda i: (0, 0)),
        ],
        out_specs=[
            pl.BlockSpec((tm, LANES), lambda i: (i, 0)),
            pl.BlockSpec((tm, LANES), lambda i: (i, 0)),
            pl.BlockSpec((8, LANES), lambda i: (0, 0)),
        ],
        out_shape=[
            jax.ShapeDtypeStruct((t, LANES), jnp.int32),
            jax.ShapeDtypeStruct((t, LANES), F32),
            jax.ShapeDtypeStruct((8, LANES), F32),
        ],
        scratch_shapes=[pltpu.VMEM((8, LANES), F32)],
        compiler_params=_cparams(("arbitrary",)),
        name="moe_router",
    )(x, ln_w.reshape(1, d), w_r, b_r, tril)


U32 = jnp.uint32
ROW_WORDS = D_MODEL // 2
ROW_SUB = ROW_WORDS // LANES
HIGH_HALF = 0xFFFF0000


def _pack_rows(v):
    lo = lax.bitcast_convert_type(v[:, :ROW_WORDS].astype(BF16).astype(F32), U32)
    hi = lax.bitcast_convert_type(v[:, ROW_WORDS:].astype(BF16).astype(F32), U32)
    return jnp.right_shift(lo, jnp.uint32(16)) | (hi & jnp.uint32(HIGH_HALF))


def _unpack_rows(w):
    lo = lax.bitcast_convert_type(jnp.left_shift(w, jnp.uint32(16)), F32)
    hi = lax.bitcast_convert_type(w & jnp.uint32(HIGH_HALF), F32)
    return lo, hi


def _store_rows(ref, packed):
    n = packed.shape[0]
    for c in range(ROW_SUB):
        ref[pl.ds(c, n, stride=ROW_SUB), :] = packed[:, c * LANES:(c + 1) * LANES]


def _load_rows(ref, n):
    return jnp.concatenate([ref[pl.ds(c, n, stride=ROW_SUB), :] for c in range(ROW_SUB)], axis=1)


def _row_slice(ref, row):
    return ref.at[pl.ds(pl.multiple_of(row * ROW_SUB, ROW_SUB), ROW_SUB)]


def _dispatch_kernel(zf_ref, dest_ref, x_ref, g_ref, xs_ref, hbuf, zbuf, sem, zsem):
    tb = x_ref.shape[0]
    n_blocks = zf_ref.shape[0]
    blk_rows = MOE_BLOCK * ROW_SUB

    @pl.when(pl.program_id(0) == 0)
    def _():
        zbuf[...] = jnp.zeros_like(zbuf)

        def zero_copy(b):
            start = pl.multiple_of(b * blk_rows, blk_rows)
            return pltpu.make_async_copy(zbuf, xs_ref.at[pl.ds(start, blk_rows)], zsem)

        def zissue(b, carry):
            @pl.when(zf_ref[b] != 0)
            def _():
                zero_copy(b).start()
            return carry

        def zdrain(b, carry):
            @pl.when(zf_ref[b] != 0)
            def _():
                zero_copy(b).wait()
            return carry

        lax.fori_loop(0, n_blocks, zissue, 0)
        lax.fori_loop(0, n_blocks, zdrain, 0)

    _store_rows(hbuf, _pack_rows(_rms(x_ref[...], g_ref[...])))
    for r in range(tb):
        for k in range(2):
            d = dest_ref[0, 0, 2 * r + k]
            pltpu.make_async_copy(_row_slice(hbuf, r), _row_slice(xs_ref, d), sem).start()
    for k in range(2):
        pltpu.make_async_copy(hbuf, xs_ref.at[pl.ds(0, tb * ROW_SUB)], sem).wait()


def _dispatch(x, ln_w, dest, zero_flag, n_slots, *, tb=MOE_BLOCK):
    t, d = x.shape
    nblk = t // tb
    return pl.pallas_call(
        _dispatch_kernel,
        grid_spec=pltpu.PrefetchScalarGridSpec(
            num_scalar_prefetch=1,
            grid=(nblk,),
            in_specs=[
                pl.BlockSpec((1, 1, 2 * tb), lambda i, zf: (i, 0, 0), memory_space=pltpu.SMEM),
                pl.BlockSpec((tb, d), lambda i, zf: (i, 0)),
                pl.BlockSpec((1, d), lambda i, zf: (0, 0)),
            ],
            out_specs=pl.BlockSpec(memory_space=pl.ANY),
            scratch_shapes=[pltpu.VMEM((tb * ROW_SUB, LANES), U32),
                            pltpu.VMEM((MOE_BLOCK * ROW_SUB, LANES), U32),
                            pltpu.SemaphoreType.DMA(()), pltpu.SemaphoreType.DMA(())],
        ),
        out_shape=jax.ShapeDtypeStruct((n_slots * ROW_SUB, LANES), U32),
        compiler_params=_cparams(("arbitrary",)),
        name="moe_dispatch",
    )(zero_flag, dest.reshape(nblk, 1, 2 * tb), x, ln_w.reshape(1, d))


def _expert_kernel(be_ref, na_ref, xs_ref, wg_ref, wu_ref, wd_ref, y_ref, wg_s, wu_s, wd_s):
    i = pl.program_id(0)
    active = i < na_ref[0]
    changed = jnp.logical_or(i == 0, be_ref[i] != be_ref[jnp.maximum(i - 1, 0)])

    @pl.when(jnp.logical_and(active, changed))
    def _():
        wg_s[...] = wg_ref[...].astype(BF16)
        wu_s[...] = wu_ref[...].astype(BF16)
        wd_s[...] = wd_ref[...].astype(BF16)

    @pl.when(active)
    def _():
        lo, hi = _unpack_rows(_load_rows(xs_ref, MOE_BLOCK))
        xb = jnp.concatenate([lo.astype(BF16), hi.astype(BF16)], axis=1)
        gate = jnp.dot(xb, wg_s[...], preferred_element_type=F32)
        up = jnp.dot(xb, wu_s[...], preferred_element_type=F32)
        hid = (_silu(gate) * up).astype(BF16)
        _store_rows(y_ref, _pack_rows(jnp.dot(hid, wd_s[...], preferred_element_type=F32)))

    @pl.when(jnp.logical_not(active))
    def _():
        y_ref[...] = jnp.zeros_like(y_ref)


def _experts(xs, block_e, n_active, w_gate, w_up, w_down, layer):
    blk_rows = MOE_BLOCK * ROW_SUB
    n_blocks = xs.shape[0] // blk_rows
    d = D_MODEL

    def xmap(i, be, na):
        return (jnp.maximum(jnp.minimum(i, na[0] - 1), 0), 0)

    def wmap(i, be, na):
        return (layer, be[i], 0, 0)

    return pl.pallas_call(
        _expert_kernel,
        grid_spec=pltpu.PrefetchScalarGridSpec(
            num_scalar_prefetch=2,
            grid=(n_blocks,),
            in_specs=[
                pl.BlockSpec((blk_rows, LANES), xmap),
                pl.BlockSpec((None, None, d, EXPERT_FF), wmap),
                pl.BlockSpec((None, None, d, EXPERT_FF), wmap),
                pl.BlockSpec((None, None, EXPERT_FF, d), wmap),
            ],
            out_specs=pl.BlockSpec((blk_rows, LANES), lambda i, be, na: (i, 0)),
            scratch_shapes=[
                pltpu.VMEM((d, EXPERT_FF), BF16),
                pltpu.VMEM((d, EXPERT_FF), BF16),
                pltpu.VMEM((EXPERT_FF, d), BF16),
            ],
        ),
        out_shape=jax.ShapeDtypeStruct(xs.shape, U32),
        compiler_params=_cparams(("arbitrary",)),
        name="moe_experts",
    )(block_e, n_active, xs, w_gate, w_up, w_down)


def _combine_kernel(dest_ref, x_ref, gates_ref, y_ref, *rest, final_norm):
    if final_norm:
        g_ref, o_ref, buf0, buf1, sem = rest
    else:
        o_ref, buf0, buf1, sem = rest
    tb = x_ref.shape[0]
    bufs = (buf0, buf1)

    for r in range(tb):
        for k in range(2):
            d = dest_ref[0, 0, 2 * r + k]
            pltpu.make_async_copy(_row_slice(y_ref, d), _row_slice(bufs[k], r), sem).start()
    for k in range(2):
        pltpu.make_async_copy(y_ref.at[pl.ds(0, tb * ROW_SUB)], bufs[k], sem).wait()

    gt = gates_ref[...]
    lo0, hi0 = _unpack_rows(_load_rows(buf0, tb))
    lo1, hi1 = _unpack_rows(_load_rows(buf1, tb))
    w0 = gt[:, 0:1]
    w1 = gt[:, 1:2]
    out = x_ref[...] + jnp.concatenate([lo0 * w0 + lo1 * w1, hi0 * w0 + hi1 * w1], axis=1)
    if final_norm:
        out = _rms(out, g_ref[...])
    o_ref[...] = out


def _combine(x, gates, y, dest, final_w=None, *, tb=256):
    t, d = x.shape
    nblk = t // tb
    final_norm = final_w is not None
    in_specs = [
        pl.BlockSpec((1, 1, 2 * tb), lambda i: (i, 0, 0), memory_space=pltpu.SMEM),
        pl.BlockSpec((tb, d), lambda i: (i, 0)),
        pl.BlockSpec((tb, LANES), lambda i: (i, 0)),
        pl.BlockSpec(memory_space=pl.ANY),
    ]
    args = [dest.reshape(nblk, 1, 2 * tb), x, gates, y]
    if final_norm:
        in_specs.append(pl.BlockSpec((1, d), lambda i: (0, 0)))
        args.append(final_w.reshape(1, d))
    return pl.pallas_call(
        functools.partial(_combine_kernel, final_norm=final_norm),
        grid=(nblk,),
        in_specs=in_specs,
        out_specs=pl.BlockSpec((tb, d), lambda i: (i, 0)),
        out_shape=jax.ShapeDtypeStruct((t, d), F32),
        scratch_shapes=[pltpu.VMEM((tb * ROW_SUB, LANES), U32), pltpu.VMEM((tb * ROW_SUB, LANES), U32),
                        pltpu.SemaphoreType.DMA(())],
        compiler_params=_cparams(("arbitrary",)),
        name="moe_combine",
    )(*args)


def _moe(x, ln_w, w_group, b_group, w_expert, b_expert, w_gate, w_up, w_down, layer, final_w=None):
    t = x.shape[0]
    n_assign = t * 2
    n_blocks = -(-n_assign // MOE_BLOCK) + N_EXPERTS
    n_slots = n_blocks * MOE_BLOCK

    ids, gates, cnt = _router(x, ln_w, w_group, b_group, w_expert, b_expert)
    counts = cnt[0, :N_EXPERTS].astype(jnp.int32)
    padded = (counts + MOE_BLOCK - 1) // MOE_BLOCK * MOE_BLOCK
    p_ends = jnp.cumsum(padded)
    p_starts = p_ends - padded
    experts = jnp.arange(N_EXPERTS, dtype=jnp.int32)
    start_of = jnp.sum(jnp.where(ids[:, 0:2, None] == experts, p_starts, 0), axis=-1)
    dest = (start_of + ids[:, 2:4]).astype(jnp.int32)
    block_start = jnp.arange(n_blocks, dtype=jnp.int32) * MOE_BLOCK
    block_e = jnp.minimum(jnp.sum((p_ends[None, :] <= block_start[:, None]).astype(jnp.int32), axis=1),
                          N_EXPERTS - 1)
    n_active = (p_ends[-1:] // MOE_BLOCK).astype(jnp.int32)
    valid_end = jnp.sum(jnp.where(block_e[:, None] == experts, p_starts + counts, 0), axis=-1)
    zero_flag = jnp.logical_or(block_start + MOE_BLOCK > valid_end,
                               block_start >= p_ends[-1]).astype(jnp.int32)

    xs = _dispatch(x, ln_w, dest, zero_flag, n_slots)
    y = _experts(xs, block_e, n_active, w_gate, w_up, w_down, layer)
    return _combine(x, gates, y, dest, final_w)


def kernel(x, rel_bias, ln_mix, ln_ffn, ln_final, attn_w_qkv, attn_sinks, attn_w_o, ssm_w_in,
           ssm_conv_w, ssm_conv_b, ssm_dt_bias, ssm_a_log, ssm_d, ssm_norm_w, ssm_w_out,
           moe_w_group, moe_b_group, moe_w_expert, moe_b_expert, moe_w_gate, moe_w_up, moe_w_down):
    batch, seq, d = x.shape
    t = batch * seq
    xf = x.reshape(t, d).astype(F32)

    qkv = _norm_matmul(xf, ln_mix[0], attn_w_qkv[0].astype(BF16))
    att = _attention(qkv, attn_sinks[0], rel_bias, batch, seq)
    xf = _matmul_residual(att, attn_w_o[0].astype(BF16), xf)
    xf = _moe(xf, ln_ffn[0], moe_w_group[0], moe_b_group[0], moe_w_expert[0], moe_b_expert[0],
              moe_w_gate, moe_w_up, moe_w_down, 0)

    w_in = ssm_w_in[0]
    w_dt = jnp.pad(w_in[:, ZXBC_DIM:], ((0, 0), (0, LANES - SSM_HEADS))).astype(BF16)
    zxbc, dt_raw = _norm_matmul(xf, ln_mix[1], w_in[:, :ZXBC_DIM].astype(BF16), w_dt)
    gated = _ssd(zxbc, dt_raw, ssm_conv_w[0], ssm_conv_b[0], ssm_dt_bias[0], ssm_a_log[0],
                 ssm_d[0], ssm_norm_w[0], batch, seq)
    xf = _matmul_residual(gated, ssm_w_out[0].astype(BF16), xf)
    xf = _moe(xf, ln_ffn[1], moe_w_group[1], moe_b_group[1], moe_w_expert[1], moe_b_expert[1],
              moe_w_gate, moe_w_up, moe_w_down, 1, final_w=ln_final)
    return xf.reshape(batch, seq, d).astype(x.dtype)
```

```python
import functools
import math

import jax
import jax.numpy as jnp
from jax import lax
from jax.experimental import pallas as pl
from jax.experimental.pallas import tpu as pltpu

F32 = jnp.float32
BF16 = jnp.bfloat16

D_MODEL = 1024
N_HEADS = 16
N_KV_HEADS = 4
HEAD_DIM = 64
GQA_GROUP = N_HEADS // N_KV_HEADS
WINDOW = 128
ATTN_BLOCK = 128
Q_DIM = N_HEADS * HEAD_DIM
KV_DIM = N_KV_HEADS * HEAD_DIM
QKV_DIM = Q_DIM + 2 * KV_DIM
REL_BUCKETS = 32
REL_MAX_DIST = 128

D_INNER = 2048
SSM_HEAD_DIM = 64
SSM_HEADS = D_INNER // SSM_HEAD_DIM
SSM_GROUPS = 4
D_STATE = 128
CONV_WIDTH = 4
BC_DIM = 2 * SSM_GROUPS * D_STATE
CONV_DIM = D_INNER + BC_DIM
ZXBC_DIM = D_INNER + CONV_DIM
SSM_CHUNK = 128
GROUP_CH = D_INNER // SSM_GROUPS

N_EXPERT_GROUPS = 8
EXPERTS_PER_GROUP = 8
N_EXPERTS = 64
EXPERT_FF = 512
MOE_BLOCK = 256

NORM_EPS = 1e-6
LANES = 128
NEG_BIG = -1e30
VMEM_LIMIT = 56 * 1024 * 1024


def _cparams(sem):
    return pltpu.CompilerParams(dimension_semantics=sem, vmem_limit_bytes=VMEM_LIMIT)


def _rms(x, g):
    ms = jnp.mean(x * x, axis=-1, keepdims=True)
    return x * lax.rsqrt(ms + NORM_EPS) * g


def _silu(x):
    return x / (1.0 + jnp.exp(-x))


def _norm_matmul_kernel(x_ref, g_ref, w_ref, *rest, has_aux):
    if has_aux:
        wa_ref, o_ref, oa_ref, h_scr = rest
    else:
        o_ref, h_scr = rest

    @pl.when(pl.program_id(1) == 0)
    def _():
        h = _rms(x_ref[...], g_ref[...]).astype(BF16)
        h_scr[...] = h
        if has_aux:
            oa_ref[...] = jnp.dot(h, wa_ref[...], preferred_element_type=F32)

    o_ref[...] = jnp.dot(h_scr[...], w_ref[...], preferred_element_type=F32).astype(o_ref.dtype)


def _norm_matmul(x, g, w, w_aux=None, *, tm=1024, tn=512):
    t, d = x.shape
    n = w.shape[1]
    has_aux = w_aux is not None
    in_specs = [
        pl.BlockSpec((tm, d), lambda i, j: (i, 0)),
        pl.BlockSpec((1, d), lambda i, j: (0, 0)),
        pl.BlockSpec((d, tn), lambda i, j: (0, j)),
    ]
    out_shape = [jax.ShapeDtypeStruct((t, n), BF16)]
    out_specs = [pl.BlockSpec((tm, tn), lambda i, j: (i, j))]
    args = [x, g.reshape(1, d), w]
    if has_aux:
        na = w_aux.shape[1]
        in_specs.append(pl.BlockSpec((d, na), lambda i, j: (0, 0)))
        out_shape.append(jax.ShapeDtypeStruct((t, na), F32))
        out_specs.append(pl.BlockSpec((tm, na), lambda i, j: (i, 0)))
        args.append(w_aux)
    res = pl.pallas_call(
        functools.partial(_norm_matmul_kernel, has_aux=has_aux),
        grid=(t // tm, n // tn),
        in_specs=in_specs,
        out_specs=out_specs,
        out_shape=out_shape,
        scratch_shapes=[pltpu.VMEM((tm, d), BF16)],
        compiler_params=_cparams(("parallel", "arbitrary")),
        name="norm_matmul",
    )(*args)
    return res if has_aux else res[0]


def _matmul_residual_kernel(a_ref, w_ref, r_ref, o_ref):
    o_ref[...] = r_ref[...] + jnp.dot(a_ref[...], w_ref[...], preferred_element_type=F32)


def _matmul_residual(a, w, res, *, tm=512):
    t, k = a.shape
    n = w.shape[1]
    return pl.pallas_call(
        _matmul_residual_kernel,
        grid=(t // tm,),
        in_specs=[
            pl.BlockSpec((tm, k), lambda i: (i, 0)),
            pl.BlockSpec((k, n), lambda i: (0, 0)),
            pl.BlockSpec((tm, n), lambda i: (i, 0)),
        ],
        out_specs=pl.BlockSpec((tm, n), lambda i: (i, 0)),
        out_shape=jax.ShapeDtypeStruct((t, n), F32),
        compiler_params=_cparams(("parallel",)),
        name="matmul_residual",
    )(a, w, res)


def _attn_kernel(sinks_ref, q_ref, kp_ref, kc_ref, vp_ref, vc_ref, bias_ref, o_ref):
    first = pl.program_id(1) == 0
    col = lax.broadcasted_iota(jnp.int32, (ATTN_BLOCK, 2 * ATTN_BLOCK), 1)
    hide_prev = jnp.logical_and(first, col < ATTN_BLOCK)
    scale = HEAD_DIM ** -0.5
    for h in range(N_KV_HEADS):
        ks = slice(h * HEAD_DIM, (h + 1) * HEAD_DIM)
        kb = jnp.concatenate([kp_ref[:, ks], kc_ref[:, ks]], axis=0)
        vb = jnp.concatenate([vp_ref[:, ks], vc_ref[:, ks]], axis=0)
        for g in range(GQA_GROUP):
            hh = h * GQA_GROUP + g
            qh = q_ref[:, hh * HEAD_DIM:(hh + 1) * HEAD_DIM]
            s = lax.dot_general(qh, kb, (((1,), (1,)), ((), ())),
                                preferred_element_type=F32) * scale
            logits = jnp.where(hide_prev, NEG_BIG, s + bias_ref[hh])
            sink = sinks_ref[hh]
            m = jnp.maximum(jnp.max(logits, axis=-1, keepdims=True), sink)
            p = jnp.exp(logits - m)
            denom = jnp.sum(p, axis=-1, keepdims=True) + jnp.exp(sink - m)
            o = jnp.dot(p.astype(BF16), vb, preferred_element_type=F32) / denom
            o_ref[:, hh * HEAD_DIM:(hh + 1) * HEAD_DIM] = o.astype(o_ref.dtype)


def _t5_causal_bucket(dist):
    n = jnp.maximum(dist, 0)
    max_exact = REL_BUCKETS // 2
    nf = jnp.maximum(n, 1).astype(F32)
    large = max_exact + (jnp.log(nf / max_exact) / math.log(REL_MAX_DIST / max_exact)
                         * (REL_BUCKETS - max_exact)).astype(jnp.int32)
    large = jnp.minimum(large, REL_BUCKETS - 1)
    return jnp.where(n < max_exact, n, large)


def _attention(qkv, sinks, rel_bias, batch, seq):
    t = qkv.shape[0]
    nb = seq // ATTN_BLOCK
    qi = jnp.arange(ATTN_BLOCK)[:, None]
    ki = jnp.arange(2 * ATTN_BLOCK)[None, :]
    dist = qi + ATTN_BLOCK - ki
    in_window = (dist >= 0) & (dist < WINDOW)
    onehot = (_t5_causal_bucket(dist)[None] == jnp.arange(REL_BUCKETS)[:, None, None]).astype(F32)
    bias = jnp.einsum('hr,rqk->hqk', rel_bias.astype(F32).T, onehot, precision=lax.Precision.HIGHEST)
    bias = jnp.where(in_window[None], bias, NEG_BIG)

    kcol = Q_DIM // KV_DIM
    vcol = kcol + 1

    def prev(b, n):
        return b * nb + jnp.maximum(n - 1, 0)

    return pl.pallas_call(
        _attn_kernel,
        grid=(batch, nb),
        in_specs=[
            pl.BlockSpec(memory_space=pltpu.SMEM),
            pl.BlockSpec((ATTN_BLOCK, Q_DIM), lambda b, n: (b * nb + n, 0)),
            pl.BlockSpec((ATTN_BLOCK, KV_DIM), lambda b, n: (prev(b, n), kcol)),
            pl.BlockSpec((ATTN_BLOCK, KV_DIM), lambda b, n: (b * nb + n, kcol)),
            pl.BlockSpec((ATTN_BLOCK, KV_DIM), lambda b, n: (prev(b, n), vcol)),
            pl.BlockSpec((ATTN_BLOCK, KV_DIM), lambda b, n: (b * nb + n, vcol)),
            pl.BlockSpec((N_HEADS, ATTN_BLOCK, 2 * ATTN_BLOCK), lambda b, n: (0, 0, 0)),
        ],
        out_specs=pl.BlockSpec((ATTN_BLOCK, Q_DIM), lambda b, n: (b * nb + n, 0)),
        out_shape=jax.ShapeDtypeStruct((t, Q_DIM), BF16),
        compiler_params=_cparams(("parallel", "arbitrary")),
        name="swa_attention",
    )(sinks.astype(F32), qkv, qkv, qkv, qkv, qkv, bias)


def _ssd_kernel(z_ref, x_ref, bc_ref, dt_ref, convw_ref, convb_ref, dtb_ref, alog_ref,
                dskip_ref, normw_ref, o_ref, state, xstage, bcstage):
    c = pl.program_id(1)
    L = SSM_CHUNK

    @pl.when(c == 0)
    def _():
        state[...] = jnp.zeros_like(state)
        xstage[0:8, :] = jnp.zeros((8, D_INNER), F32)
        bcstage[0:8, :] = jnp.zeros((8, BC_DIM), F32)

    def conv_silu(u_ref, stage, w_lo, width):
        cur = u_ref[...].astype(F32)
        stage[8:8 + L, :] = cur
        w = convw_ref[:, w_lo:w_lo + width]
        y = (cur * w[3:4] + stage[7:7 + L, :] * w[2:3] + stage[6:6 + L, :] * w[1:2]
             + stage[5:5 + L, :] * w[0:1] + convb_ref[:, w_lo:w_lo + width])
        stage[0:8, :] = cur[L - 8:L]
        return _silu(y)

    xs = conv_silu(x_ref, xstage, 0, D_INNER)
    bcm = conv_silu(bc_ref, bcstage, D_INNER, BC_DIM)

    xdt = dt_ref[...] + dtb_ref[...]
    dt = jnp.maximum(xdt, 0.0) + jnp.log1p(jnp.exp(-jnp.abs(xdt)))
    a = dt * (-jnp.exp(alog_ref[...]))
    row = lax.broadcasted_iota(jnp.int32, (L, L), 0)
    colm = lax.broadcasted_iota(jnp.int32, (L, L), 1)
    causal = row >= colm
    tri = causal.astype(F32)
    a_cs = jnp.dot(tri, a, preferred_element_type=F32, precision=lax.Precision.HIGHEST)
    a_cs_t = a_cs.T
    dt_t = dt.T
    a_last_col = a_cs_t[:, L - 1:L]
    w_end_t = jnp.exp(a_last_col - a_cs_t) * dt_t

    erow = lax.broadcasted_iota(jnp.int32, (LANES, D_INNER), 0)
    ecol = lax.broadcasted_iota(jnp.int32, (LANES, D_INNER), 1)
    expand = (jnp.right_shift(ecol, 6) == erow).astype(F32)
    cd_rows = jnp.broadcast_to(jnp.exp(a_cs[L - 1:L, :]), (8, LANES))
    cd_exp = jnp.dot(cd_rows, expand, preferred_element_type=F32,
                     precision=lax.Precision.HIGHEST)[0:1]

    lane = lax.broadcasted_iota(jnp.int32, (L, LANES), 1)
    low_half = lane < SSM_HEAD_DIM

    y_tiles = []
    for g in range(SSM_GROUPS):
        b_g = bcm[:, g * D_STATE:(g + 1) * D_STATE]
        c_g = bcm[:, (SSM_GROUPS + g) * D_STATE:(SSM_GROUPS + g + 1) * D_STATE]
        b_gb = b_g.astype(BF16)
        c_gb = c_g.astype(BF16)
        cb = lax.dot_general(c_gb, b_gb, (((1,), (1,)), ((), ())), preferred_element_type=F32)
        b_gt = b_g.T
        s_g = state[:, g * GROUP_CH:(g + 1) * GROUP_CH]
        y_off_g = jnp.dot(c_gb, s_g.astype(BF16), preferred_element_type=F32)
        for jj in range(GROUP_CH // LANES):
            j = g * (GROUP_CH // LANES) + jj
            lhs_top = []
            lhs_bot = []
            escale = []
            for e in (2 * j, 2 * j + 1):
                acs_b = jnp.broadcast_to(a_cs[:, e:e + 1], (L, L))
                seg = acs_b - a_cs_t[e:e + 1, :]
                dec = jnp.exp(jnp.where(causal, seg, -jnp.inf))
                lhs_top.append((cb * dec * dt_t[e:e + 1, :]).astype(BF16))
                lhs_bot.append((b_gt * w_end_t[e:e + 1, :]).astype(BF16))
                escale.append(jnp.exp(acs_b))
            x_tile = xs[:, j * LANES:(j + 1) * LANES]
            x_lo = jnp.where(low_half, x_tile, 0.0).astype(BF16)
            x_hi = jnp.where(low_half, 0.0, x_tile).astype(BF16)
            lhs = jnp.concatenate([jnp.concatenate(lhs_top, axis=1),
                                   jnp.concatenate(lhs_bot, axis=1)], axis=0)
            rhs = jnp.concatenate([x_lo, x_hi], axis=0)
            r = jnp.dot(lhs, rhs, preferred_element_type=F32)
            y_off = y_off_g[:, jj * LANES:(jj + 1) * LANES]
            y_tiles.append(r[0:L] + jnp.where(low_half, escale[0], escale[1]) * y_off)
            sl = slice(j * LANES, (j + 1) * LANES)
            state[:, sl] = cd_exp[:, sl] * state[:, sl] + r[L:2 * L]

    y = jnp.concatenate(y_tiles, axis=1)
    y = y + xs * dskip_ref[...]
    gated = y * _silu(z_ref[...].astype(F32))
    outs = []
    for g in range(SSM_GROUPS):
        gg = gated[:, g * GROUP_CH:(g + 1) * GROUP_CH]
        ms = jnp.mean(gg * gg, axis=-1, keepdims=True)
        outs.append(gg * lax.rsqrt(ms + NORM_EPS))
    o_ref[...] = (jnp.concatenate(outs, axis=1) * normw_ref[...]).astype(o_ref.dtype)


def _ssd(zxbc, dt_raw, conv_w, conv_b, dt_bias, a_log, d_skip, norm_w, batch, seq):
    t = zxbc.shape[0]
    nc = seq // SSM_CHUNK
    L = SSM_CHUNK

    def pad_heads(v):
        return jnp.pad(v.astype(F32), (0, LANES - SSM_HEADS)).reshape(1, LANES)

    def rowmap(col):
        return lambda b, c: (b * nc + c, col)

    def const2(b, c):
        return (0, 0)

    return pl.pallas_call(
        _ssd_kernel,
        grid=(batch, nc),
        in_specs=[
            pl.BlockSpec((L, D_INNER), rowmap(0)),
            pl.BlockSpec((L, D_INNER), rowmap(1)),
            pl.BlockSpec((L, BC_DIM), rowmap(2 * D_INNER // BC_DIM)),
            pl.BlockSpec((L, LANES), rowmap(0)),
            pl.BlockSpec((CONV_WIDTH, CONV_DIM), const2),
            pl.BlockSpec((1, CONV_DIM), const2),
            pl.BlockSpec((1, LANES), const2),
            pl.BlockSpec((1, LANES), const2),
            pl.BlockSpec((1, D_INNER), const2),
            pl.BlockSpec((1, D_INNER), const2),
        ],
        out_specs=pl.BlockSpec((L, D_INNER), rowmap(0)),
        out_shape=jax.ShapeDtypeStruct((t, D_INNER), BF16),
        scratch_shapes=[
            pltpu.VMEM((D_STATE, D_INNER), F32),
            pltpu.VMEM((L + 8, D_INNER), F32),
            pltpu.VMEM((L + 8, BC_DIM), F32),
        ],
        compiler_params=_cparams(("parallel", "arbitrary")),
        name="ssd_core",
    )(zxbc, zxbc, zxbc, dt_raw, conv_w.astype(F32), conv_b.astype(F32).reshape(1, CONV_DIM),
      pad_heads(dt_bias), pad_heads(a_log),
      jnp.repeat(d_skip.astype(F32), SSM_HEAD_DIM).reshape(1, D_INNER),
      norm_w.astype(F32).reshape(1, D_INNER))


GROUP_LANE0 = N_EXPERTS


def _router_kernel(x_ref, g_ref, w_ref, b_ref, tril_ref, ids_ref, gates_ref, cnt_ref, carry, w_split):
    i = pl.program_id(0)

    @pl.when(i == 0)
    def _():
        carry[...] = jnp.zeros_like(carry)
        w = w_ref[...]
        w_hi = w.astype(BF16)
        w_split[:, :LANES] = w_hi
        w_split[:, LANES:] = (w - w_hi.astype(F32)).astype(BF16)

    h = _rms(x_ref[...], g_ref[...])
    h_hi = h.astype(BF16)
    h_mid = (h - h_hi.astype(F32)).astype(BF16)
    top = jnp.dot(h_hi, w_split[...], preferred_element_type=F32)
    cross = jnp.dot(h_mid, w_split[:, :LANES], preferred_element_type=F32)
    logits = top[:, :LANES] + (top[:, LANES:] + cross) + b_ref[...]
    tm = logits.shape[0]
    lane = lax.broadcasted_iota(jnp.int32, (tm, LANES), 1)
    ninf = -jnp.inf

    def first_argmax(v, vmax):
        return jnp.min(jnp.where(v == vmax, lane, LANES), axis=-1, keepdims=True)

    is_group = (lane >= GROUP_LANE0) & (lane < GROUP_LANE0 + N_EXPERT_GROUPS)
    glog = jnp.where(is_group, logits, ninf)
    gmax = jnp.max(glog, axis=-1, keepdims=True)
    g_idx = first_argmax(glog, gmax) - GROUP_LANE0
    g_p = 1.0 / jnp.sum(jnp.exp(glog - gmax), axis=-1, keepdims=True)

    in_group = (lane < N_EXPERTS) & (jnp.right_shift(lane, 3) == g_idx)
    elog = jnp.where(in_group, logits, ninf)
    m1 = jnp.max(elog, axis=-1, keepdims=True)
    e1 = first_argmax(elog, m1)
    elog2 = jnp.where(lane == e1, ninf, elog)
    m2 = jnp.max(elog2, axis=-1, keepdims=True)
    e2 = first_argmax(elog2, m2)
    zsum = jnp.sum(jnp.exp(elog - m1), axis=-1, keepdims=True)
    p1 = 1.0 / zsum
    p2 = jnp.exp(m2 - m1) / zsum
    psum = p1 + p2
    w1 = p1 / psum * g_p
    w2 = p2 / psum * g_p

    oh1 = lane == e1
    oh2 = lane == e2
    onehot = jnp.logical_or(oh1, oh2).astype(F32)
    before = jnp.dot(tril_ref[...], onehot.astype(BF16), preferred_element_type=F32) + carry[0:1, :]
    r1 = jnp.sum(jnp.where(oh1, before, 0.0), axis=-1, keepdims=True).astype(jnp.int32)
    r2 = jnp.sum(jnp.where(oh2, before, 0.0), axis=-1, keepdims=True).astype(jnp.int32)
    carry[0:1, :] = carry[0:1, :] + jnp.sum(onehot, axis=0, keepdims=True)

    ids_ref[...] = jnp.where(lane == 0, e1, jnp.where(lane == 1, e2,
                             jnp.where(lane == 2, r1, jnp.where(lane == 3, r2, 0))))
    gates_ref[...] = jnp.where(lane == 0, w1, jnp.where(lane == 1, w2, 0.0))
    cnt_ref[...] = carry[...]


def _router(x, ln_w, w_group, b_group, w_expert, b_expert, *, tm=512):
    t, d = x.shape
    pad = LANES - N_EXPERTS - N_EXPERT_GROUPS
    w_r = jnp.concatenate([w_expert, w_group, jnp.zeros((d, pad), F32)], axis=1).astype(F32)
    b_r = jnp.concatenate([b_expert, b_group, jnp.zeros((pad,), F32)]).astype(F32).reshape(1, LANES)
    tril = (jnp.arange(tm)[:, None] > jnp.arange(tm)[None, :]).astype(BF16)
    return pl.pallas_call(
        _router_kernel,
        grid=(t // tm,),
        in_specs=[
            pl.BlockSpec((tm, d), lambda i: (i, 0)),
            pl.BlockSpec((1, d), lambda i: (0, 0)),
            pl.BlockSpec((d, LANES), lambda i: (0, 0)),
            pl.BlockSpec((1, LANES), lambda i: (0, 0)),
            pl.BlockSpec((tm, tm), lambda i: (0, 0)),
        ],
        out_specs=[
            pl.BlockSpec((tm, LANES), lambda i: (i, 0)),
            pl.BlockSpec((tm, LANES), lambda i: (i, 0)),
            pl.BlockSpec((8, LANES), lambda i: (0, 0)),
        ],
        out_shape=[
            jax.ShapeDtypeStruct((t, LANES), jnp.int32),
            jax.ShapeDtypeStruct((t, LANES), F32),
            jax.ShapeDtypeStruct((8, LANES), F32),
        ],
        scratch_shapes=[pltpu.VMEM((8, LANES), F32), pltpu.VMEM((d, 2 * LANES), BF16)],
        compiler_params=_cparams(("arbitrary",)),
        name="moe_router",
    )(x, ln_w.reshape(1, d), w_r, b_r, tril)


U32 = jnp.uint32
ROW_WORDS = D_MODEL // 2
ROW_SUB = ROW_WORDS // LANES
HIGH_HALF = 0xFFFF0000


def _pack_rows(v):
    lo = lax.bitcast_convert_type(v[:, :ROW_WORDS].astype(BF16).astype(F32), U32)
    hi = lax.bitcast_convert_type(v[:, ROW_WORDS:].astype(BF16).astype(F32), U32)
    return jnp.right_shift(lo, jnp.uint32(16)) | (hi & jnp.uint32(HIGH_HALF))


def _unpack_rows(w):
    lo = lax.bitcast_convert_type(jnp.left_shift(w, jnp.uint32(16)), F32)
    hi = lax.bitcast_convert_type(w & jnp.uint32(HIGH_HALF), F32)
    return lo, hi


def _store_rows(ref, packed):
    n = packed.shape[0]
    for c in range(ROW_SUB):
        ref[pl.ds(c, n, stride=ROW_SUB), :] = packed[:, c * LANES:(c + 1) * LANES]


def _load_rows(ref, n):
    return jnp.concatenate([ref[pl.ds(c, n, stride=ROW_SUB), :] for c in range(ROW_SUB)], axis=1)


def _row_slice(ref, row):
    return ref.at[pl.ds(pl.multiple_of(row * ROW_SUB, ROW_SUB), ROW_SUB)]


def _dispatch_kernel(zf_ref, dest_ref, x_ref, g_ref, xs_ref, hbuf, zbuf, sem, zsem):
    tb = x_ref.shape[0]
    n_blocks = zf_ref.shape[0]
    blk_rows = MOE_BLOCK * ROW_SUB

    @pl.when(pl.program_id(0) == 0)
    def _():
        zbuf[...] = jnp.zeros_like(zbuf)

        def zero_copy(b):
            start = pl.multiple_of(b * blk_rows, blk_rows)
            return pltpu.make_async_copy(zbuf, xs_ref.at[pl.ds(start, blk_rows)], zsem)

        def zissue(b, carry):
            @pl.when(zf_ref[b] != 0)
            def _():
                zero_copy(b).start()
            return carry

        def zdrain(b, carry):
            @pl.when(zf_ref[b] != 0)
            def _():
                zero_copy(b).wait()
            return carry

        lax.fori_loop(0, n_blocks, zissue, 0)
        lax.fori_loop(0, n_blocks, zdrain, 0)

    _store_rows(hbuf, _pack_rows(_rms(x_ref[...], g_ref[...])))
    for r in range(tb):
        for k in range(2):
            d = dest_ref[0, 0, 2 * r + k]
            pltpu.make_async_copy(_row_slice(hbuf, r), _row_slice(xs_ref, d), sem).start(priority=k)
    for k in range(2):
        pltpu.make_async_copy(hbuf, xs_ref.at[pl.ds(0, tb * ROW_SUB)], sem).wait()


def _dispatch(x, ln_w, dest, zero_flag, n_slots, *, tb=MOE_BLOCK):
    t, d = x.shape
    nblk = t // tb
    return pl.pallas_call(
        _dispatch_kernel,
        grid_spec=pltpu.PrefetchScalarGridSpec(
            num_scalar_prefetch=1,
            grid=(nblk,),
            in_specs=[
                pl.BlockSpec((1, 1, 2 * tb), lambda i, zf: (i, 0, 0), memory_space=pltpu.SMEM),
                pl.BlockSpec((tb, d), lambda i, zf: (i, 0)),
                pl.BlockSpec((1, d), lambda i, zf: (0, 0)),
            ],
            out_specs=pl.BlockSpec(memory_space=pl.ANY),
            scratch_shapes=[pltpu.VMEM((tb * ROW_SUB, LANES), U32),
                            pltpu.VMEM((MOE_BLOCK * ROW_SUB, LANES), U32),
                            pltpu.SemaphoreType.DMA(()), pltpu.SemaphoreType.DMA(())],
        ),
        out_shape=jax.ShapeDtypeStruct((n_slots * ROW_SUB, LANES), U32),
        compiler_params=_cparams(("arbitrary",)),
        name="moe_dispatch",
    )(zero_flag, dest.reshape(nblk, 1, 2 * tb), x, ln_w.reshape(1, d))


def _expert_kernel(be_ref, na_ref, xs_ref, wg_ref, wu_ref, wd_ref, y_ref, wg_s, wu_s, wd_s):
    i = pl.program_id(0)
    active = i < na_ref[0]
    changed = jnp.logical_or(i == 0, be_ref[i] != be_ref[jnp.maximum(i - 1, 0)])

    @pl.when(jnp.logical_and(active, changed))
    def _():
        wg_s[...] = wg_ref[...].astype(BF16)
        wu_s[...] = wu_ref[...].astype(BF16)
        wd_s[...] = wd_ref[...].astype(BF16)

    @pl.when(active)
    def _():
        lo, hi = _unpack_rows(_load_rows(xs_ref, MOE_BLOCK))
        xb = jnp.concatenate([lo.astype(BF16), hi.astype(BF16)], axis=1)
        gate = jnp.dot(xb, wg_s[...], preferred_element_type=F32)
        up = jnp.dot(xb, wu_s[...], preferred_element_type=F32)
        hid = (_silu(gate) * up).astype(BF16)
        _store_rows(y_ref, _pack_rows(jnp.dot(hid, wd_s[...], preferred_element_type=F32)))

    @pl.when(jnp.logical_not(active))
    def _():
        y_ref[...] = jnp.zeros_like(y_ref)


def _experts(xs, block_e, n_active, w_gate, w_up, w_down, layer):
    blk_rows = MOE_BLOCK * ROW_SUB
    n_blocks = xs.shape[0] // blk_rows
    d = D_MODEL

    def xmap(i, be, na):
        return (jnp.maximum(jnp.minimum(i, na[0] - 1), 0), 0)

    def wmap(i, be, na):
        return (layer, be[i], 0, 0)

    return pl.pallas_call(
        _expert_kernel,
        grid_spec=pltpu.PrefetchScalarGridSpec(
            num_scalar_prefetch=2,
            grid=(n_blocks,),
            in_specs=[
                pl.BlockSpec((blk_rows, LANES), xmap),
                pl.BlockSpec((None, None, d, EXPERT_FF), wmap),
                pl.BlockSpec((None, None, d, EXPERT_FF), wmap),
                pl.BlockSpec((None, None, EXPERT_FF, d), wmap),
            ],
            out_specs=pl.BlockSpec((blk_rows, LANES), lambda i, be, na: (i, 0)),
            scratch_shapes=[
                pltpu.VMEM((d, EXPERT_FF), BF16),
                pltpu.VMEM((d, EXPERT_FF), BF16),
                pltpu.VMEM((EXPERT_FF, d), BF16),
            ],
        ),
        out_shape=jax.ShapeDtypeStruct(xs.shape, U32),
        compiler_params=_cparams(("arbitrary",)),
        name="moe_experts",
    )(block_e, n_active, xs, w_gate, w_up, w_down)


def _combine_kernel(dest_ref, x_ref, gates_ref, y_ref, *rest, final_norm):
    if final_norm:
        g_ref, o_ref, buf0, buf1, sem = rest
    else:
        o_ref, buf0, buf1, sem = rest
    tb = x_ref.shape[0]
    bufs = (buf0, buf1)

    for r in range(tb):
        for k in range(2):
            d = dest_ref[0, 0, 2 * r + k]
            pltpu.make_async_copy(_row_slice(y_ref, d), _row_slice(bufs[k], r), sem).start(priority=k)
    for k in range(2):
        pltpu.make_async_copy(y_ref.at[pl.ds(0, tb * ROW_SUB)], bufs[k], sem).wait()

    gt = gates_ref[...]
    lo0, hi0 = _unpack_rows(_load_rows(buf0, tb))
    lo1, hi1 = _unpack_rows(_load_rows(buf1, tb))
    w0 = gt[:, 0:1]
    w1 = gt[:, 1:2]
    out = x_ref[...] + jnp.concatenate([lo0 * w0 + lo1 * w1, hi0 * w0 + hi1 * w1], axis=1)
    if final_norm:
        out = _rms(out, g_ref[...])
    o_ref[...] = out


def _combine(x, gates, y, dest, final_w=None, *, tb=256):
    t, d = x.shape
    nblk = t // tb
    final_norm = final_w is not None
    in_specs = [
        pl.BlockSpec((1, 1, 2 * tb), lambda i: (i, 0, 0), memory_space=pltpu.SMEM),
        pl.BlockSpec((tb, d), lambda i: (i, 0)),
        pl.BlockSpec((tb, LANES), lambda i: (i, 0)),
        pl.BlockSpec(memory_space=pl.ANY),
    ]
    args = [dest.reshape(nblk, 1, 2 * tb), x, gates, y]
    if final_norm:
        in_specs.append(pl.BlockSpec((1, d), lambda i: (0, 0)))
        args.append(final_w.reshape(1, d))
    return pl.pallas_call(
        functools.partial(_combine_kernel, final_norm=final_norm),
        grid=(nblk,),
        in_specs=in_specs,
        out_specs=pl.BlockSpec((tb, d), lambda i: (i, 0)),
        out_shape=jax.ShapeDtypeStruct((t, d), F32),
        scratch_shapes=[pltpu.VMEM((tb * ROW_SUB, LANES), U32), pltpu.VMEM((tb * ROW_SUB, LANES), U32),
                        pltpu.SemaphoreType.DMA(())],
        compiler_params=_cparams(("arbitrary",)),
        name="moe_combine",
    )(*args)


def _moe(x, ln_w, w_group, b_group, w_expert, b_expert, w_gate, w_up, w_down, layer, final_w=None):
    t = x.shape[0]
    n_assign = t * 2
    n_blocks = -(-n_assign // MOE_BLOCK) + N_EXPERTS
    n_slots = n_blocks * MOE_BLOCK

    ids, gates, cnt = _router(x, ln_w, w_group, b_group, w_expert, b_expert)
    counts = cnt[0, :N_EXPERTS].astype(jnp.int32)
    padded = (counts + MOE_BLOCK - 1) // MOE_BLOCK * MOE_BLOCK
    p_ends = jnp.cumsum(padded)
    p_starts = p_ends - padded
    experts = jnp.arange(N_EXPERTS, dtype=jnp.int32)
    start_of = jnp.sum(jnp.where(ids[:, 0:2, None] == experts, p_starts, 0), axis=-1)
    dest = (start_of + ids[:, 2:4]).astype(jnp.int32)
    block_start = jnp.arange(n_blocks, dtype=jnp.int32) * MOE_BLOCK
    block_e = jnp.minimum(jnp.sum((p_ends[None, :] <= block_start[:, None]).astype(jnp.int32), axis=1),
                          N_EXPERTS - 1)
    n_active = (p_ends[-1:] // MOE_BLOCK).astype(jnp.int32)
    valid_end = jnp.sum(jnp.where(block_e[:, None] == experts, p_starts + counts, 0), axis=-1)
    zero_flag = jnp.logical_or(block_start + MOE_BLOCK > valid_end,
                               block_start >= p_ends[-1]).astype(jnp.int32)

    xs = _dispatch(x, ln_w, dest, zero_flag, n_slots)
    y = _experts(xs, block_e, n_active, w_gate, w_up, w_down, layer)
    return _combine(x, gates, y, dest, final_w)


def kernel(x, rel_bias, ln_mix, ln_ffn, ln_final, attn_w_qkv, attn_sinks, attn_w_o, ssm_w_in,
           ssm_conv_w, ssm_conv_b, ssm_dt_bias, ssm_a_log, ssm_d, ssm_norm_w, ssm_w_out,
           moe_w_group, moe_b_group, moe_w_expert, moe_b_expert, moe_w_gate, moe_w_up, moe_w_down):
    batch, seq, d = x.shape
    t = batch * seq
    xf = x.reshape(t, d).astype(F32)

    qkv = _norm_matmul(xf, ln_mix[0], attn_w_qkv[0].astype(BF16))
    att = _attention(qkv, attn_sinks[0], rel_bias, batch, seq)
    xf = _matmul_residual(att, attn_w_o[0].astype(BF16), xf)
    xf = _moe(xf, ln_ffn[0], moe_w_group[0], moe_b_group[0], moe_w_expert[0], moe_b_expert[0],
              moe_w_gate, moe_w_up, moe_w_down, 0)

    w_in = ssm_w_in[0]
    w_dt = jnp.pad(w_in[:, ZXBC_DIM:], ((0, 0), (0, LANES - SSM_HEADS))).astype(BF16)
    zxbc, dt_raw = _norm_matmul(xf, ln_mix[1], w_in[:, :ZXBC_DIM].astype(BF16), w_dt)
    gated = _ssd(zxbc, dt_raw, ssm_conv_w[0], ssm_conv_b[0], ssm_dt_bias[0], ssm_a_log[0],
                 ssm_d[0], ssm_norm_w[0], batch, seq)
    xf = _matmul_residual(gated, ssm_w_out[0].astype(BF16), xf)
    xf = _moe(xf, ln_ffn[1], moe_w_group[1], moe_b_group[1], moe_w_expert[1], moe_b_expert[1],
              moe_w_gate, moe_w_up, moe_w_down, 1, final_w=ln_final)
    return xf.reshape(batch, seq, d).astype(x.dtype)
```

```python
import functools
import math

import jax
import jax.numpy as jnp
from jax import lax
from jax.experimental import pallas as pl
from jax.experimental.pallas import tpu as pltpu

F32 = jnp.float32
BF16 = jnp.bfloat16

D_MODEL = 1024
N_HEADS = 16
N_KV_HEADS = 4
HEAD_DIM = 64
GQA_GROUP = N_HEADS // N_KV_HEADS
WINDOW = 128
ATTN_BLOCK = 128
Q_DIM = N_HEADS * HEAD_DIM
KV_DIM = N_KV_HEADS * HEAD_DIM
QKV_DIM = Q_DIM + 2 * KV_DIM
REL_BUCKETS = 32
REL_MAX_DIST = 128

D_INNER = 2048
SSM_HEAD_DIM = 64
SSM_HEADS = D_INNER // SSM_HEAD_DIM
SSM_GROUPS = 4
D_STATE = 128
CONV_WIDTH = 4
BC_DIM = 2 * SSM_GROUPS * D_STATE
CONV_DIM = D_INNER + BC_DIM
ZXBC_DIM = D_INNER + CONV_DIM
SSM_CHUNK = 128
GROUP_CH = D_INNER // SSM_GROUPS

N_EXPERT_GROUPS = 8
EXPERTS_PER_GROUP = 8
N_EXPERTS = 64
EXPERT_FF = 512
MOE_BLOCK = 256

NORM_EPS = 1e-6
LOG2E = 1.4426950408889634
CONV_TAIL = 16
LANES = 128
NEG_BIG = -1e30
VMEM_LIMIT = 56 * 1024 * 1024


def _cparams(sem):
    return pltpu.CompilerParams(dimension_semantics=sem, vmem_limit_bytes=VMEM_LIMIT)


def _rms(x, g):
    ms = jnp.mean(x * x, axis=-1, keepdims=True)
    return x * lax.rsqrt(ms + NORM_EPS) * g


def _silu(x):
    return x / (1.0 + jnp.exp(-x))


def _norm_matmul_kernel(x_ref, g_ref, w_ref, *rest, has_aux):
    if has_aux:
        wa_ref, o_ref, oa_ref, h_scr = rest
    else:
        o_ref, h_scr = rest

    @pl.when(pl.program_id(1) == 0)
    def _():
        h = _rms(x_ref[...], g_ref[...]).astype(BF16)
        h_scr[...] = h
        if has_aux:
            oa_ref[...] = jnp.dot(h, wa_ref[...], preferred_element_type=F32)

    o_ref[...] = jnp.dot(h_scr[...], w_ref[...], preferred_element_type=F32).astype(o_ref.dtype)


def _norm_matmul(x, g, w, w_aux=None, *, tm=1024, tn=512):
    t, d = x.shape
    n = w.shape[1]
    has_aux = w_aux is not None
    in_specs = [
        pl.BlockSpec((tm, d), lambda i, j: (i, 0)),
        pl.BlockSpec((1, d), lambda i, j: (0, 0)),
        pl.BlockSpec((d, tn), lambda i, j: (0, j)),
    ]
    out_shape = [jax.ShapeDtypeStruct((t, n), BF16)]
    out_specs = [pl.BlockSpec((tm, tn), lambda i, j: (i, j))]
    args = [x, g.reshape(1, d), w]
    if has_aux:
        na = w_aux.shape[1]
        in_specs.append(pl.BlockSpec((d, na), lambda i, j: (0, 0)))
        out_shape.append(jax.ShapeDtypeStruct((t, na), F32))
        out_specs.append(pl.BlockSpec((tm, na), lambda i, j: (i, 0)))
        args.append(w_aux)
    res = pl.pallas_call(
        functools.partial(_norm_matmul_kernel, has_aux=has_aux),
        grid=(t // tm, n // tn),
        in_specs=in_specs,
        out_specs=out_specs,
        out_shape=out_shape,
        scratch_shapes=[pltpu.VMEM((tm, d), BF16)],
        compiler_params=_cparams(("parallel", "arbitrary")),
        name="norm_matmul",
    )(*args)
    return res if has_aux else res[0]


def _matmul_residual_kernel(a_ref, w_ref, r_ref, o_ref):
    o_ref[...] = r_ref[...] + jnp.dot(a_ref[...], w_ref[...], preferred_element_type=F32)


def _matmul_residual(a, w, res, *, tm=512):
    t, k = a.shape
    n = w.shape[1]
    return pl.pallas_call(
        _matmul_residual_kernel,
        grid=(t // tm,),
        in_specs=[
            pl.BlockSpec((tm, k), lambda i: (i, 0)),
            pl.BlockSpec((k, n), lambda i: (0, 0)),
            pl.BlockSpec((tm, n), lambda i: (i, 0)),
        ],
        out_specs=pl.BlockSpec((tm, n), lambda i: (i, 0)),
        out_shape=jax.ShapeDtypeStruct((t, n), F32),
        compiler_params=_cparams(("parallel",)),
        name="matmul_residual",
    )(a, w, res)


def _attn_kernel(sinks_ref, q_ref, kp_ref, kc_ref, vp_ref, vc_ref, bias_ref, o_ref):
    first = pl.program_id(1) == 0
    col = lax.broadcasted_iota(jnp.int32, (ATTN_BLOCK, 2 * ATTN_BLOCK), 1)
    hide_prev = jnp.logical_and(first, col < ATTN_BLOCK)
    scale = HEAD_DIM ** -0.5
    for h in range(N_KV_HEADS):
        ks = slice(h * HEAD_DIM, (h + 1) * HEAD_DIM)
        kb = jnp.concatenate([kp_ref[:, ks], kc_ref[:, ks]], axis=0)
        vb = jnp.concatenate([vp_ref[:, ks], vc_ref[:, ks]], axis=0)
        for g in range(GQA_GROUP):
            hh = h * GQA_GROUP + g
            qh = q_ref[:, hh * HEAD_DIM:(hh + 1) * HEAD_DIM]
            s = lax.dot_general(qh, kb, (((1,), (1,)), ((), ())),
                                preferred_element_type=F32) * scale
            logits = jnp.where(hide_prev, NEG_BIG, s + bias_ref[hh])
            sink = sinks_ref[hh]
            m = jnp.maximum(jnp.max(logits, axis=-1, keepdims=True), sink)
            p = jnp.exp(logits - m)
            denom = jnp.sum(p, axis=-1, keepdims=True) + jnp.exp(sink - m)
            o = jnp.dot(p.astype(BF16), vb, preferred_element_type=F32) / denom
            o_ref[:, hh * HEAD_DIM:(hh + 1) * HEAD_DIM] = o.astype(o_ref.dtype)


def _t5_causal_bucket(dist):
    n = jnp.maximum(dist, 0)
    max_exact = REL_BUCKETS // 2
    nf = jnp.maximum(n, 1).astype(F32)
    large = max_exact + (jnp.log(nf / max_exact) / math.log(REL_MAX_DIST / max_exact)
                         * (REL_BUCKETS - max_exact)).astype(jnp.int32)
    large = jnp.minimum(large, REL_BUCKETS - 1)
    return jnp.where(n < max_exact, n, large)


def _attention(qkv, sinks, rel_bias, batch, seq):
    t = qkv.shape[0]
    nb = seq // ATTN_BLOCK
    qi = jnp.arange(ATTN_BLOCK)[:, None]
    ki = jnp.arange(2 * ATTN_BLOCK)[None, :]
    dist = qi + ATTN_BLOCK - ki
    in_window = (dist >= 0) & (dist < WINDOW)
    onehot = (_t5_causal_bucket(dist)[None] == jnp.arange(REL_BUCKETS)[:, None, None]).astype(F32)
    bias = jnp.einsum('hr,rqk->hqk', rel_bias.astype(F32).T, onehot, precision=lax.Precision.HIGHEST)
    bias = jnp.where(in_window[None], bias, NEG_BIG)

    kcol = Q_DIM // KV_DIM
    vcol = kcol + 1

    def prev(b, n):
        return b * nb + jnp.maximum(n - 1, 0)

    return pl.pallas_call(
        _attn_kernel,
        grid=(batch, nb),
        in_specs=[
            pl.BlockSpec(memory_space=pltpu.SMEM),
            pl.BlockSpec((ATTN_BLOCK, Q_DIM), lambda b, n: (b * nb + n, 0)),
            pl.BlockSpec((ATTN_BLOCK, KV_DIM), lambda b, n: (prev(b, n), kcol)),
            pl.BlockSpec((ATTN_BLOCK, KV_DIM), lambda b, n: (b * nb + n, kcol)),
            pl.BlockSpec((ATTN_BLOCK, KV_DIM), lambda b, n: (prev(b, n), vcol)),
            pl.BlockSpec((ATTN_BLOCK, KV_DIM), lambda b, n: (b * nb + n, vcol)),
            pl.BlockSpec((N_HEADS, ATTN_BLOCK, 2 * ATTN_BLOCK), lambda b, n: (0, 0, 0)),
        ],
        out_specs=pl.BlockSpec((ATTN_BLOCK, Q_DIM), lambda b, n: (b * nb + n, 0)),
        out_shape=jax.ShapeDtypeStruct((t, Q_DIM), BF16),
        compiler_params=_cparams(("parallel", "arbitrary")),
        name="swa_attention",
    )(sinks.astype(F32), qkv, qkv, qkv, qkv, qkv, bias)


def _split3(v):
    hi = v.astype(BF16)
    r = v - hi.astype(F32)
    mid = r.astype(BF16)
    lo = (r - mid.astype(F32)).astype(BF16)
    return hi, mid, lo


def _ssd_kernel(z_ref, x_ref, bc_ref, dt_ref, convw_ref, convb_ref, dtb_ref, alog_ref,
                dskip_ref, normw_ref, shift_ref, shift_tail_ref, expand_ref,
                o_ref, state, xtail, bctail):
    c = pl.program_id(1)
    L = SSM_CHUNK

    @pl.when(c == 0)
    def _():
        state[...] = jnp.zeros_like(state)
        xtail[...] = jnp.zeros_like(xtail)
        bctail[...] = jnp.zeros_like(bctail)

    def conv_silu(u_ref, tail, w_lo, width):
        cur = u_ref[...]
        prev = tail[...]
        w = [convw_ref[k:k + 1, w_lo:w_lo + width].astype(BF16) for k in range(CONV_WIDTH)]
        taps = jnp.concatenate([cur * w[k] for k in range(CONV_WIDTH)], axis=0)
        taps_prev = jnp.concatenate([prev * w[k] for k in range(CONV_WIDTH)], axis=0)
        y = (jnp.dot(shift_ref[...], taps, preferred_element_type=F32)
             + jnp.dot(shift_tail_ref[...], taps_prev, preferred_element_type=F32)
             + convb_ref[:, w_lo:w_lo + width])
        tail[...] = cur[L - CONV_TAIL:L]
        return _silu(y)

    xs = conv_silu(x_ref, xtail, 0, D_INNER)
    bcm = conv_silu(bc_ref, bctail, D_INNER, BC_DIM)

    xdt = dt_ref[...] + dtb_ref[...]
    dt = jnp.maximum(xdt, 0.0) + jnp.log1p(jnp.exp(-jnp.abs(xdt)))
    a = dt * (-jnp.exp(alog_ref[...]) * LOG2E)
    row = lax.broadcasted_iota(jnp.int32, (L, L), 0)
    colm = lax.broadcasted_iota(jnp.int32, (L, L), 1)
    causal = row >= colm
    cs3 = jnp.dot(causal.astype(BF16), jnp.concatenate(_split3(a), axis=1), preferred_element_type=F32)
    a_cs = cs3[:, :LANES] + cs3[:, LANES:2 * LANES] + cs3[:, 2 * LANES:]
    a_cs_t = a_cs.T
    dt_t = dt.T
    a_last_col = a_cs_t[:, L - 1:L]
    w_end_t = jnp.exp2(a_last_col - a_cs_t) * dt_t
    exp_acs = jnp.exp2(a_cs)

    cd3 = [jnp.broadcast_to(part.astype(F32), (16, LANES)).astype(BF16)
           for part in _split3(exp_acs[L - 1:L, :])]
    cdx = jnp.dot(jnp.concatenate(cd3, axis=0), expand_ref[...], preferred_element_type=F32)
    cd_exp = cdx[0:1] + cdx[16:17] + cdx[32:33]

    lane = lax.broadcasted_iota(jnp.int32, (L, LANES), 1)
    low_half = lane < SSM_HEAD_DIM

    y_tiles = []
    for g in range(SSM_GROUPS):
        b_g = bcm[:, g * D_STATE:(g + 1) * D_STATE]
        c_g = bcm[:, (SSM_GROUPS + g) * D_STATE:(SSM_GROUPS + g + 1) * D_STATE]
        b_gb = b_g.astype(BF16)
        c_gb = c_g.astype(BF16)
        cb = lax.dot_general(c_gb, b_gb, (((1,), (1,)), ((), ())), preferred_element_type=F32)
        b_gt = b_g.T
        s_g = state[:, g * GROUP_CH:(g + 1) * GROUP_CH]
        y_off_g = jnp.dot(c_gb, s_g.astype(BF16), preferred_element_type=F32)
        for jj in range(GROUP_CH // LANES):
            j = g * (GROUP_CH // LANES) + jj
            lhs_top = []
            lhs_bot = []
            escale = []
            for e in (2 * j, 2 * j + 1):
                acs_b = jnp.broadcast_to(a_cs[:, e:e + 1], (L, L))
                seg = acs_b - a_cs_t[e:e + 1, :]
                dec = jnp.exp2(jnp.where(causal, seg, -jnp.inf))
                lhs_top.append((cb * dec * dt_t[e:e + 1, :]).astype(BF16))
                lhs_bot.append((b_gt * w_end_t[e:e + 1, :]).astype(BF16))
                escale.append(jnp.broadcast_to(exp_acs[:, e:e + 1], (L, LANES)))
            x_tile = xs[:, j * LANES:(j + 1) * LANES]
            x_lo = jnp.where(low_half, x_tile, 0.0).astype(BF16)
            x_hi = jnp.where(low_half, 0.0, x_tile).astype(BF16)
            lhs = jnp.concatenate([jnp.concatenate(lhs_top, axis=1),
                                   jnp.concatenate(lhs_bot, axis=1)], axis=0)
            rhs = jnp.concatenate([x_lo, x_hi], axis=0)
            r = jnp.dot(lhs, rhs, preferred_element_type=F32)
            y_off = y_off_g[:, jj * LANES:(jj + 1) * LANES]
            y_tiles.append(r[0:L] + jnp.where(low_half, escale[0], escale[1]) * y_off)
            sl = slice(j * LANES, (j + 1) * LANES)
            state[:, sl] = cd_exp[:, sl] * state[:, sl] + r[L:2 * L]

    y = jnp.concatenate(y_tiles, axis=1)
    y = y + xs * dskip_ref[...]
    gated = y * _silu(z_ref[...].astype(F32))
    outs = []
    for g in range(SSM_GROUPS):
        gg = gated[:, g * GROUP_CH:(g + 1) * GROUP_CH]
        ms = jnp.mean(gg * gg, axis=-1, keepdims=True)
        outs.append(gg * lax.rsqrt(ms + NORM_EPS))
    o_ref[...] = (jnp.concatenate(outs, axis=1) * normw_ref[...]).astype(o_ref.dtype)


def _ssd(zxbc, dt_raw, conv_w, conv_b, dt_bias, a_log, d_skip, norm_w, batch, seq):
    t = zxbc.shape[0]
    nc = seq // SSM_CHUNK
    L = SSM_CHUNK

    def pad_heads(v):
        return jnp.pad(v.astype(F32), (0, LANES - SSM_HEADS)).reshape(1, LANES)

    def rowmap(col):
        return lambda b, c: (b * nc + c, col)

    def const2(b, c):
        return (0, 0)

    l_idx = jnp.arange(L)[:, None]
    shift = jnp.concatenate(
        [(jnp.arange(L)[None, :] == l_idx - (CONV_WIDTH - 1) + k) for k in range(CONV_WIDTH)],
        axis=1).astype(BF16)
    shift_tail = jnp.concatenate(
        [(jnp.arange(-CONV_TAIL, 0)[None, :] == l_idx - (CONV_WIDTH - 1) + k) for k in range(CONV_WIDTH)],
        axis=1).astype(BF16)
    expand = (jnp.arange(D_INNER)[None, :] // SSM_HEAD_DIM == jnp.arange(LANES)[:, None]).astype(BF16)

    return pl.pallas_call(
        _ssd_kernel,
        grid=(batch, nc),
        in_specs=[
            pl.BlockSpec((L, D_INNER), rowmap(0)),
            pl.BlockSpec((L, D_INNER), rowmap(1)),
            pl.BlockSpec((L, BC_DIM), rowmap(2 * D_INNER // BC_DIM)),
            pl.BlockSpec((L, LANES), rowmap(0)),
            pl.BlockSpec((CONV_WIDTH, CONV_DIM), const2),
            pl.BlockSpec((1, CONV_DIM), const2),
            pl.BlockSpec((1, LANES), const2),
            pl.BlockSpec((1, LANES), const2),
            pl.BlockSpec((1, D_INNER), const2),
            pl.BlockSpec((1, D_INNER), const2),
            pl.BlockSpec((L, CONV_WIDTH * L), const2),
            pl.BlockSpec((L, CONV_WIDTH * CONV_TAIL), const2),
            pl.BlockSpec((LANES, D_INNER), const2),
        ],
        out_specs=pl.BlockSpec((L, D_INNER), rowmap(0)),
        out_shape=jax.ShapeDtypeStruct((t, D_INNER), BF16),
        scratch_shapes=[
            pltpu.VMEM((D_STATE, D_INNER), F32),
            pltpu.VMEM((CONV_TAIL, D_INNER), BF16),
            pltpu.VMEM((CONV_TAIL, BC_DIM), BF16),
        ],
        compiler_params=_cparams(("parallel", "arbitrary")),
        name="ssd_core",
    )(zxbc, zxbc, zxbc, dt_raw, conv_w.astype(F32), conv_b.astype(F32).reshape(1, CONV_DIM),
      pad_heads(dt_bias), pad_heads(a_log),
      jnp.repeat(d_skip.astype(F32), SSM_HEAD_DIM).reshape(1, D_INNER),
      norm_w.astype(F32).reshape(1, D_INNER), shift, shift_tail, expand)


GROUP_LANE0 = N_EXPERTS


def _router_kernel(x_ref, g_ref, w_ref, b_ref, tril_ref, ids_ref, gates_ref, cnt_ref, carry, w_split):
    i = pl.program_id(0)

    @pl.when(i == 0)
    def _():
        carry[...] = jnp.zeros_like(carry)
        w = w_ref[...]
        w_hi = w.astype(BF16)
        w_split[:, :LANES] = w_hi
        w_split[:, LANES:] = (w - w_hi.astype(F32)).astype(BF16)

    h = _rms(x_ref[...], g_ref[...])
    h_hi = h.astype(BF16)
    h_mid = (h - h_hi.astype(F32)).astype(BF16)
    top = jnp.dot(h_hi, w_split[...], preferred_element_type=F32)
    cross = jnp.dot(h_mid, w_split[:, :LANES], preferred_element_type=F32)
    logits = top[:, :LANES] + (top[:, LANES:] + cross) + b_ref[...]
    tm = logits.shape[0]
    lane = lax.broadcasted_iota(jnp.int32, (tm, LANES), 1)
    ninf = -jnp.inf

    def first_argmax(v, vmax):
        return jnp.min(jnp.where(v == vmax, lane, LANES), axis=-1, keepdims=True)

    is_group = (lane >= GROUP_LANE0) & (lane < GROUP_LANE0 + N_EXPERT_GROUPS)
    glog = jnp.where(is_group, logits, ninf)
    gmax = jnp.max(glog, axis=-1, keepdims=True)
    g_idx = first_argmax(glog, gmax) - GROUP_LANE0
    g_p = 1.0 / jnp.sum(jnp.exp(glog - gmax), axis=-1, keepdims=True)

    in_group = (lane < N_EXPERTS) & (jnp.right_shift(lane, 3) == g_idx)
    elog = jnp.where(in_group, logits, ninf)
    m1 = jnp.max(elog, axis=-1, keepdims=True)
    e1 = first_argmax(elog, m1)
    elog2 = jnp.where(lane == e1, ninf, elog)
    m2 = jnp.max(elog2, axis=-1, keepdims=True)
    e2 = first_argmax(elog2, m2)
    zsum = jnp.sum(jnp.exp(elog - m1), axis=-1, keepdims=True)
    p1 = 1.0 / zsum
    p2 = jnp.exp(m2 - m1) / zsum
    psum = p1 + p2
    w1 = p1 / psum * g_p
    w2 = p2 / psum * g_p

    oh1 = lane == e1
    oh2 = lane == e2
    onehot = jnp.logical_or(oh1, oh2).astype(F32)
    before = jnp.dot(tril_ref[...], onehot.astype(BF16), preferred_element_type=F32) + carry[0:1, :]
    r1 = jnp.sum(jnp.where(oh1, before, 0.0), axis=-1, keepdims=True).astype(jnp.int32)
    r2 = jnp.sum(jnp.where(oh2, before, 0.0), axis=-1, keepdims=True).astype(jnp.int32)
    carry[0:1, :] = carry[0:1, :] + jnp.sum(onehot, axis=0, keepdims=True)

    ids_ref[...] = jnp.where(lane == 0, e1, jnp.where(lane == 1, e2,
                             jnp.where(lane == 2, r1, jnp.where(lane == 3, r2, 0))))
    gates_ref[...] = jnp.where(lane == 0, w1, jnp.where(lane == 1, w2, 0.0))
    cnt_ref[...] = carry[...]


def _router(x, ln_w, w_group, b_group, w_expert, b_expert, *, tm=512):
    t, d = x.shape
    pad = LANES - N_EXPERTS - N_EXPERT_GROUPS
    w_r = jnp.concatenate([w_expert, w_group, jnp.zeros((d, pad), F32)], axis=1).astype(F32)
    b_r = jnp.concatenate([b_expert, b_group, jnp.zeros((pad,), F32)]).astype(F32).reshape(1, LANES)
    tril = (jnp.arange(tm)[:, None] > jnp.arange(tm)[None, :]).astype(BF16)
    return pl.pallas_call(
        _router_kernel,
        grid=(t // tm,),
        in_specs=[
            pl.BlockSpec((tm, d), lambda i: (i, 0)),
            pl.BlockSpec((1, d), lambda i: (0, 0)),
            pl.BlockSpec((d, LANES), lambda i: (0, 0)),
            pl.BlockSpec((1, LANES), lambda i: (0, 0)),
            pl.BlockSpec((tm, tm), lambda i: (0, 0)),
        ],
        out_specs=[
            pl.BlockSpec((tm, LANES), lambda i: (i, 0)),
            pl.BlockSpec((tm, LANES), lambda i: (i, 0)),
            pl.BlockSpec((8, LANES), lambda i: (0, 0)),
        ],
        out_shape=[
            jax.ShapeDtypeStruct((t, LANES), jnp.int32),
            jax.ShapeDtypeStruct((t, LANES), F32),
            jax.ShapeDtypeStruct((8, LANES), F32),
        ],
        scratch_shapes=[pltpu.VMEM((8, LANES), F32), pltpu.VMEM((d, 2 * LANES), BF16)],
        compiler_params=_cparams(("arbitrary",)),
        name="moe_router",
    )(x, ln_w.reshape(1, d), w_r, b_r, tril)


U32 = jnp.uint32
ROW_WORDS = D_MODEL // 2
ROW_SUB = ROW_WORDS // LANES
HIGH_HALF = 0xFFFF0000


def _pack_rows(v):
    lo = lax.bitcast_convert_type(v[:, :ROW_WORDS].astype(BF16).astype(F32), U32)
    hi = lax.bitcast_convert_type(v[:, ROW_WORDS:].astype(BF16).astype(F32), U32)
    return jnp.right_shift(lo, jnp.uint32(16)) | (hi & jnp.uint32(HIGH_HALF))


def _unpack_rows(w):
    lo = lax.bitcast_convert_type(jnp.left_shift(w, jnp.uint32(16)), F32)
    hi = lax.bitcast_convert_type(w & jnp.uint32(HIGH_HALF), F32)
    return lo, hi


def _store_rows(ref, packed):
    n = packed.shape[0]
    for c in range(ROW_SUB):
        ref[pl.ds(c, n, stride=ROW_SUB), :] = packed[:, c * LANES:(c + 1) * LANES]


def _load_rows(ref, n):
    return jnp.concatenate([ref[pl.ds(c, n, stride=ROW_SUB), :] for c in range(ROW_SUB)], axis=1)


def _row_slice(ref, row):
    return ref.at[pl.ds(pl.multiple_of(row * ROW_SUB, ROW_SUB), ROW_SUB)]


def _dispatch_kernel(zf_ref, dest_ref, x_ref, g_ref, xs_ref, hbuf, zbuf, sem, zsem):
    tb = x_ref.shape[0]
    n_blocks = zf_ref.shape[0]
    blk_rows = MOE_BLOCK * ROW_SUB

    @pl.when(pl.program_id(0) == 0)
    def _():
        zbuf[...] = jnp.zeros_like(zbuf)

        def zero_copy(b):
            start = pl.multiple_of(b * blk_rows, blk_rows)
            return pltpu.make_async_copy(zbuf, xs_ref.at[pl.ds(start, blk_rows)], zsem)

        def zissue(b, carry):
            @pl.when(zf_ref[b] != 0)
            def _():
                zero_copy(b).start()
            return carry

        def zdrain(b, carry):
            @pl.when(zf_ref[b] != 0)
            def _():
                zero_copy(b).wait()
            return carry

        lax.fori_loop(0, n_blocks, zissue, 0)
        lax.fori_loop(0, n_blocks, zdrain, 0)

    _store_rows(hbuf, _pack_rows(_rms(x_ref[...], g_ref[...])))
    for r in range(tb):
        for k in range(2):
            d = dest_ref[0, 0, 2 * r + k]
            pltpu.make_async_copy(_row_slice(hbuf, r), _row_slice(xs_ref, d), sem).start(priority=k)
    for k in range(2):
        pltpu.make_async_copy(hbuf, xs_ref.at[pl.ds(0, tb * ROW_SUB)], sem).wait()


def _dispatch(x, ln_w, dest, zero_flag, n_slots, *, tb=MOE_BLOCK):
    t, d = x.shape
    nblk = t // tb
    return pl.pallas_call(
        _dispatch_kernel,
        grid_spec=pltpu.PrefetchScalarGridSpec(
            num_scalar_prefetch=1,
            grid=(nblk,),
            in_specs=[
                pl.BlockSpec((1, 1, 2 * tb), lambda i, zf: (i, 0, 0), memory_space=pltpu.SMEM),
                pl.BlockSpec((tb, d), lambda i, zf: (i, 0)),
                pl.BlockSpec((1, d), lambda i, zf: (0, 0)),
            ],
            out_specs=pl.BlockSpec(memory_space=pl.ANY),
            scratch_shapes=[pltpu.VMEM((tb * ROW_SUB, LANES), U32),
                            pltpu.VMEM((MOE_BLOCK * ROW_SUB, LANES), U32),
                            pltpu.SemaphoreType.DMA(()), pltpu.SemaphoreType.DMA(())],
        ),
        out_shape=jax.ShapeDtypeStruct((n_slots * ROW_SUB, LANES), U32),
        compiler_params=_cparams(("arbitrary",)),
        name="moe_dispatch",
    )(zero_flag, dest.reshape(nblk, 1, 2 * tb), x, ln_w.reshape(1, d))


def _expert_kernel(be_ref, na_ref, xs_ref, wg_ref, wu_ref, wd_ref, y_ref, wg_s, wu_s, wd_s):
    i = pl.program_id(0)
    active = i < na_ref[0]
    changed = jnp.logical_or(i == 0, be_ref[i] != be_ref[jnp.maximum(i - 1, 0)])

    @pl.when(jnp.logical_and(active, changed))
    def _():
        wg_s[...] = wg_ref[...].astype(BF16)
        wu_s[...] = wu_ref[...].astype(BF16)
        wd_s[...] = wd_ref[...].astype(BF16)

    @pl.when(active)
    def _():
        lo, hi = _unpack_rows(_load_rows(xs_ref, MOE_BLOCK))
        xb = jnp.concatenate([lo.astype(BF16), hi.astype(BF16)], axis=1)
        gate = jnp.dot(xb, wg_s[...], preferred_element_type=F32)
        up = jnp.dot(xb, wu_s[...], preferred_element_type=F32)
        hid = (_silu(gate) * up).astype(BF16)
        _store_rows(y_ref, _pack_rows(jnp.dot(hid, wd_s[...], preferred_element_type=F32)))

    @pl.when(jnp.logical_not(active))
    def _():
        y_ref[...] = jnp.zeros_like(y_ref)


def _experts(xs, block_e, n_active, w_gate, w_up, w_down, layer):
    blk_rows = MOE_BLOCK * ROW_SUB
    n_blocks = xs.shape[0] // blk_rows
    d = D_MODEL

    def xmap(i, be, na):
        return (jnp.maximum(jnp.minimum(i, na[0] - 1), 0), 0)

    def wmap(i, be, na):
        return (layer, be[i], 0, 0)

    return pl.pallas_call(
        _expert_kernel,
        grid_spec=pltpu.PrefetchScalarGridSpec(
            num_scalar_prefetch=2,
            grid=(n_blocks,),
            in_specs=[
                pl.BlockSpec((blk_rows, LANES), xmap),
                pl.BlockSpec((None, None, d, EXPERT_FF), wmap),
                pl.BlockSpec((None, None, d, EXPERT_FF), wmap),
                pl.BlockSpec((None, None, EXPERT_FF, d), wmap),
            ],
            out_specs=pl.BlockSpec((blk_rows, LANES), lambda i, be, na: (i, 0)),
            scratch_shapes=[
                pltpu.VMEM((d, EXPERT_FF), BF16),
                pltpu.VMEM((d, EXPERT_FF), BF16),
                pltpu.VMEM((EXPERT_FF, d), BF16),
            ],
        ),
        out_shape=jax.ShapeDtypeStruct(xs.shape, U32),
        compiler_params=_cparams(("arbitrary",)),
        name="moe_experts",
    )(block_e, n_active, xs, w_gate, w_up, w_down)


def _combine_kernel(dest_ref, x_ref, gates_ref, y_ref, *rest, final_norm):
    if final_norm:
        g_ref, o_ref, buf0, buf1, sem = rest
    else:
        o_ref, buf0, buf1, sem = rest
    tb = x_ref.shape[0]
    bufs = (buf0, buf1)

    for r in range(tb):
        for k in range(2):
            d = dest_ref[0, 0, 2 * r + k]
            pltpu.make_async_copy(_row_slice(y_ref, d), _row_slice(bufs[k], r), sem).start(priority=k)
    for k in range(2):
        pltpu.make_async_copy(y_ref.at[pl.ds(0, tb * ROW_SUB)], bufs[k], sem).wait()

    gt = gates_ref[...]
    lo0, hi0 = _unpack_rows(_load_rows(buf0, tb))
    lo1, hi1 = _unpack_rows(_load_rows(buf1, tb))
    w0 = gt[:, 0:1]
    w1 = gt[:, 1:2]
    out = x_ref[...] + jnp.concatenate([lo0 * w0 + lo1 * w1, hi0 * w0 + hi1 * w1], axis=1)
    if final_norm:
        out = _rms(out, g_ref[...])
    o_ref[...] = out


def _combine(x, gates, y, dest, final_w=None, *, tb=256):
    t, d = x.shape
    nblk = t // tb
    final_norm = final_w is not None
    in_specs = [
        pl.BlockSpec((1, 1, 2 * tb), lambda i: (i, 0, 0), memory_space=pltpu.SMEM),
        pl.BlockSpec((tb, d), lambda i: (i, 0)),
        pl.BlockSpec((tb, LANES), lambda i: (i, 0)),
        pl.BlockSpec(memory_space=pl.ANY),
    ]
    args = [dest.reshape(nblk, 1, 2 * tb), x, gates, y]
    if final_norm:
        in_specs.append(pl.BlockSpec((1, d), lambda i: (0, 0)))
        args.append(final_w.reshape(1, d))
    return pl.pallas_call(
        functools.partial(_combine_kernel, final_norm=final_norm),
        grid=(nblk,),
        in_specs=in_specs,
        out_specs=pl.BlockSpec((tb, d), lambda i: (i, 0)),
        out_shape=jax.ShapeDtypeStruct((t, d), F32),
        scratch_shapes=[pltpu.VMEM((tb * ROW_SUB, LANES), U32), pltpu.VMEM((tb * ROW_SUB, LANES), U32),
                        pltpu.SemaphoreType.DMA(())],
        compiler_params=_cparams(("arbitrary",)),
        name="moe_combine",
    )(*args)


def _moe(x, ln_w, w_group, b_group, w_expert, b_expert, w_gate, w_up, w_down, layer, final_w=None):
    t = x.shape[0]
    n_assign = t * 2
    n_blocks = -(-n_assign // MOE_BLOCK) + N_EXPERTS
    n_slots = n_blocks * MOE_BLOCK

    ids, gates, cnt = _router(x, ln_w, w_group, b_group, w_expert, b_expert)
    counts = cnt[0, :N_EXPERTS].astype(jnp.int32)
    padded = (counts + MOE_BLOCK - 1) // MOE_BLOCK * MOE_BLOCK
    p_ends = jnp.cumsum(padded)
    p_starts = p_ends - padded
    experts = jnp.arange(N_EXPERTS, dtype=jnp.int32)
    start_of = jnp.sum(jnp.where(ids[:, 0:2, None] == experts, p_starts, 0), axis=-1)
    dest = (start_of + ids[:, 2:4]).astype(jnp.int32)
    block_start = jnp.arange(n_blocks, dtype=jnp.int32) * MOE_BLOCK
    block_e = jnp.minimum(jnp.sum((p_ends[None, :] <= block_start[:, None]).astype(jnp.int32), axis=1),
                          N_EXPERTS - 1)
    n_active = (p_ends[-1:] // MOE_BLOCK).astype(jnp.int32)
    valid_end = jnp.sum(jnp.where(block_e[:, None] == experts, p_starts + counts, 0), axis=-1)
    zero_flag = jnp.logical_or(block_start + MOE_BLOCK > valid_end,
                               block_start >= p_ends[-1]).astype(jnp.int32)

    xs = _dispatch(x, ln_w, dest, zero_flag, n_slots)
    y = _experts(xs, block_e, n_active, w_gate, w_up, w_down, layer)
    return _combine(x, gates, y, dest, final_w)


def kernel(x, rel_bias, ln_mix, ln_ffn, ln_final, attn_w_qkv, attn_sinks, attn_w_o, ssm_w_in,
           ssm_conv_w, ssm_conv_b, ssm_dt_bias, ssm_a_log, ssm_d, ssm_norm_w, ssm_w_out,
           moe_w_group, moe_b_group, moe_w_expert, moe_b_expert, moe_w_gate, moe_w_up, moe_w_down):
    batch, seq, d = x.shape
    t = batch * seq
    xf = x.reshape(t, d).astype(F32)

    qkv = _norm_matmul(xf, ln_mix[0], attn_w_qkv[0].astype(BF16), tm=512, tn=QKV_DIM)
    att = _attention(qkv, attn_sinks[0], rel_bias, batch, seq)
    xf = _matmul_residual(att, attn_w_o[0].astype(BF16), xf)
    xf = _moe(xf, ln_ffn[0], moe_w_group[0], moe_b_group[0], moe_w_expert[0], moe_b_expert[0],
              moe_w_gate, moe_w_up, moe_w_down, 0)

    w_in = ssm_w_in[0]
    w_dt = jnp.pad(w_in[:, ZXBC_DIM:], ((0, 0), (0, LANES - SSM_HEADS))).astype(BF16)
    zxbc, dt_raw = _norm_matmul(xf, ln_mix[1], w_in[:, :ZXBC_DIM].astype(BF16), w_dt, tm=512, tn=ZXBC_DIM)
    gated = _ssd(zxbc, dt_raw, ssm_conv_w[0], ssm_conv_b[0], ssm_dt_bias[0], ssm_a_log[0],
                 ssm_d[0], ssm_norm_w[0], batch, seq)
    xf = _matmul_residual(gated, ssm_w_out[0].astype(BF16), xf)
    xf = _moe(xf, ln_ffn[1], moe_w_group[1], moe_b_group[1], moe_w_expert[1], moe_b_expert[1],
              moe_w_gate, moe_w_up, moe_w_down, 1, final_w=ln_final)
    return xf.reshape(batch, seq, d).astype(x.dtype)
```

```python
import functools
import math

import jax
import jax.numpy as jnp
from jax import lax
from jax.experimental import pallas as pl
from jax.experimental.pallas import tpu as pltpu

F32 = jnp.float32
BF16 = jnp.bfloat16

D_MODEL = 1024
N_HEADS = 16
N_KV_HEADS = 4
HEAD_DIM = 64
GQA_GROUP = N_HEADS // N_KV_HEADS
WINDOW = 128
ATTN_BLOCK = 128
Q_DIM = N_HEADS * HEAD_DIM
KV_DIM = N_KV_HEADS * HEAD_DIM
QKV_DIM = Q_DIM + 2 * KV_DIM
REL_BUCKETS = 32
REL_MAX_DIST = 128

D_INNER = 2048
SSM_HEAD_DIM = 64
SSM_HEADS = D_INNER // SSM_HEAD_DIM
SSM_GROUPS = 4
D_STATE = 128
CONV_WIDTH = 4
BC_DIM = 2 * SSM_GROUPS * D_STATE
CONV_DIM = D_INNER + BC_DIM
ZXBC_DIM = D_INNER + CONV_DIM
SSM_CHUNK = 128
GROUP_CH = D_INNER // SSM_GROUPS

N_EXPERT_GROUPS = 8
EXPERTS_PER_GROUP = 8
N_EXPERTS = 64
EXPERT_FF = 512
MOE_BLOCK = 256

NORM_EPS = 1e-6
LOG2E = 1.4426950408889634
CONV_TAIL = 16
LANES = 128
NEG_BIG = -1e30
VMEM_LIMIT = 56 * 1024 * 1024


def _cparams(sem):
    return pltpu.CompilerParams(dimension_semantics=sem, vmem_limit_bytes=VMEM_LIMIT)


def _rms(x, g):
    ms = jnp.mean(x * x, axis=-1, keepdims=True)
    return x * lax.rsqrt(ms + NORM_EPS) * g


def _silu(x):
    return x / (1.0 + jnp.exp(-x))


def _norm_matmul_kernel(x_ref, g_ref, w_ref, *rest, has_aux):
    if has_aux:
        wa_ref, o_ref, oa_ref, h_scr = rest
    else:
        o_ref, h_scr = rest

    @pl.when(pl.program_id(1) == 0)
    def _():
        h = _rms(x_ref[...], g_ref[...]).astype(BF16)
        h_scr[...] = h
        if has_aux:
            oa_ref[...] = jnp.dot(h, wa_ref[...], preferred_element_type=F32)

    o_ref[...] = jnp.dot(h_scr[...], w_ref[...], preferred_element_type=F32).astype(o_ref.dtype)


def _norm_matmul(x, g, w, w_aux=None, *, tm=1024, tn=512):
    t, d = x.shape
    n = w.shape[1]
    has_aux = w_aux is not None
    in_specs = [
        pl.BlockSpec((tm, d), lambda i, j: (i, 0)),
        pl.BlockSpec((1, d), lambda i, j: (0, 0)),
        pl.BlockSpec((d, tn), lambda i, j: (0, j)),
    ]
    out_shape = [jax.ShapeDtypeStruct((t, n), BF16)]
    out_specs = [pl.BlockSpec((tm, tn), lambda i, j: (i, j))]
    args = [x, g.reshape(1, d), w]
    if has_aux:
        na = w_aux.shape[1]
        in_specs.append(pl.BlockSpec((d, na), lambda i, j: (0, 0)))
        out_shape.append(jax.ShapeDtypeStruct((t, na), F32))
        out_specs.append(pl.BlockSpec((tm, na), lambda i, j: (i, 0)))
        args.append(w_aux)
    res = pl.pallas_call(
        functools.partial(_norm_matmul_kernel, has_aux=has_aux),
        grid=(t // tm, n // tn),
        in_specs=in_specs,
        out_specs=out_specs,
        out_shape=out_shape,
        scratch_shapes=[pltpu.VMEM((tm, d), BF16)],
        compiler_params=_cparams(("parallel", "arbitrary")),
        name="norm_matmul",
    )(*args)
    return res if has_aux else res[0]


def _matmul_residual_kernel(a_ref, w_ref, r_ref, o_ref):
    o_ref[...] = r_ref[...] + jnp.dot(a_ref[...], w_ref[...], preferred_element_type=F32)


def _matmul_residual(a, w, res, *, tm=512):
    t, k = a.shape
    n = w.shape[1]
    return pl.pallas_call(
        _matmul_residual_kernel,
        grid=(t // tm,),
        in_specs=[
            pl.BlockSpec((tm, k), lambda i: (i, 0)),
            pl.BlockSpec((k, n), lambda i: (0, 0)),
            pl.BlockSpec((tm, n), lambda i: (i, 0)),
        ],
        out_specs=pl.BlockSpec((tm, n), lambda i: (i, 0)),
        out_shape=jax.ShapeDtypeStruct((t, n), F32),
        compiler_params=_cparams(("parallel",)),
        name="matmul_residual",
    )(a, w, res)


def _attn_kernel(sinks_ref, q_ref, kp_ref, kc_ref, vp_ref, vc_ref, bias_ref, o_ref):
    first = pl.program_id(1) == 0
    col = lax.broadcasted_iota(jnp.int32, (ATTN_BLOCK, 2 * ATTN_BLOCK), 1)
    hide_prev = jnp.logical_and(first, col < ATTN_BLOCK)
    scale = HEAD_DIM ** -0.5
    for h in range(N_KV_HEADS):
        ks = slice(h * HEAD_DIM, (h + 1) * HEAD_DIM)
        kb = jnp.concatenate([kp_ref[:, ks], kc_ref[:, ks]], axis=0)
        vb = jnp.concatenate([vp_ref[:, ks], vc_ref[:, ks]], axis=0)
        for g in range(GQA_GROUP):
            hh = h * GQA_GROUP + g
            qh = q_ref[:, hh * HEAD_DIM:(hh + 1) * HEAD_DIM]
            s = lax.dot_general(qh, kb, (((1,), (1,)), ((), ())),
                                preferred_element_type=F32) * scale
            logits = jnp.where(hide_prev, NEG_BIG, s + bias_ref[hh])
            sink = sinks_ref[hh]
            m = jnp.maximum(jnp.max(logits, axis=-1, keepdims=True), sink)
            p = jnp.exp(logits - m)
            denom = jnp.sum(p, axis=-1, keepdims=True) + jnp.exp(sink - m)
            o = jnp.dot(p.astype(BF16), vb, preferred_element_type=F32) / denom
            o_ref[:, hh * HEAD_DIM:(hh + 1) * HEAD_DIM] = o.astype(o_ref.dtype)


def _t5_causal_bucket(dist):
    n = jnp.maximum(dist, 0)
    max_exact = REL_BUCKETS // 2
    nf = jnp.maximum(n, 1).astype(F32)
    large = max_exact + (jnp.log(nf / max_exact) / math.log(REL_MAX_DIST / max_exact)
                         * (REL_BUCKETS - max_exact)).astype(jnp.int32)
    large = jnp.minimum(large, REL_BUCKETS - 1)
    return jnp.where(n < max_exact, n, large)


def _attention(qkv, sinks, rel_bias, batch, seq):
    t = qkv.shape[0]
    nb = seq // ATTN_BLOCK
    qi = jnp.arange(ATTN_BLOCK)[:, None]
    ki = jnp.arange(2 * ATTN_BLOCK)[None, :]
    dist = qi + ATTN_BLOCK - ki
    in_window = (dist >= 0) & (dist < WINDOW)
    onehot = (_t5_causal_bucket(dist)[None] == jnp.arange(REL_BUCKETS)[:, None, None]).astype(F32)
    bias = jnp.einsum('hr,rqk->hqk', rel_bias.astype(F32).T, onehot, precision=lax.Precision.HIGHEST)
    bias = jnp.where(in_window[None], bias, NEG_BIG)

    kcol = Q_DIM // KV_DIM
    vcol = kcol + 1

    def prev(b, n):
        return b * nb + jnp.maximum(n - 1, 0)

    return pl.pallas_call(
        _attn_kernel,
        grid=(batch, nb),
        in_specs=[
            pl.BlockSpec(memory_space=pltpu.SMEM),
            pl.BlockSpec((ATTN_BLOCK, Q_DIM), lambda b, n: (b * nb + n, 0)),
            pl.BlockSpec((ATTN_BLOCK, KV_DIM), lambda b, n: (prev(b, n), kcol)),
            pl.BlockSpec((ATTN_BLOCK, KV_DIM), lambda b, n: (b * nb + n, kcol)),
            pl.BlockSpec((ATTN_BLOCK, KV_DIM), lambda b, n: (prev(b, n), vcol)),
            pl.BlockSpec((ATTN_BLOCK, KV_DIM), lambda b, n: (b * nb + n, vcol)),
            pl.BlockSpec((N_HEADS, ATTN_BLOCK, 2 * ATTN_BLOCK), lambda b, n: (0, 0, 0)),
        ],
        out_specs=pl.BlockSpec((ATTN_BLOCK, Q_DIM), lambda b, n: (b * nb + n, 0)),
        out_shape=jax.ShapeDtypeStruct((t, Q_DIM), BF16),
        compiler_params=_cparams(("parallel", "arbitrary")),
        name="swa_attention",
    )(sinks.astype(F32), qkv, qkv, qkv, qkv, qkv, bias)


def _split3(v):
    hi = v.astype(BF16)
    r = v - hi.astype(F32)
    mid = r.astype(BF16)
    lo = (r - mid.astype(F32)).astype(BF16)
    return hi, mid, lo


def _ssd_kernel(z_ref, x_ref, bc_ref, dt_ref, convw_ref, convb_ref, dtb_ref, alog_ref,
                dskip_ref, normw_ref, shift_ref, shift_tail_ref, expand_ref,
                o_ref, state, xtail, bctail):
    c = pl.program_id(1)
    L = SSM_CHUNK

    @pl.when(c == 0)
    def _():
        state[...] = jnp.zeros_like(state)
        xtail[...] = jnp.zeros_like(xtail)
        bctail[...] = jnp.zeros_like(bctail)

    def conv_silu(u_ref, tail, w_lo, width):
        cur = u_ref[...]
        prev = tail[...]
        w = [convw_ref[k:k + 1, w_lo:w_lo + width].astype(BF16) for k in range(CONV_WIDTH)]
        taps = jnp.concatenate([cur * w[k] for k in range(CONV_WIDTH)], axis=0)
        taps_prev = jnp.concatenate([prev * w[k] for k in range(CONV_WIDTH)], axis=0)
        y = (jnp.dot(shift_ref[...], taps, preferred_element_type=F32)
             + jnp.dot(shift_tail_ref[...], taps_prev, preferred_element_type=F32)
             + convb_ref[:, w_lo:w_lo + width])
        tail[...] = cur[L - CONV_TAIL:L]
        return _silu(y)

    xs = conv_silu(x_ref, xtail, 0, D_INNER)
    bcm = conv_silu(bc_ref, bctail, D_INNER, BC_DIM)

    xdt = dt_ref[...] + dtb_ref[...]
    dt = jnp.maximum(xdt, 0.0) + jnp.log1p(jnp.exp(-jnp.abs(xdt)))
    a = dt * (-jnp.exp(alog_ref[...]) * LOG2E)
    row = lax.broadcasted_iota(jnp.int32, (L, L), 0)
    colm = lax.broadcasted_iota(jnp.int32, (L, L), 1)
    causal = row >= colm
    cs3 = jnp.dot(causal.astype(BF16), jnp.concatenate(_split3(a), axis=1), preferred_element_type=F32)
    a_cs = cs3[:, :LANES] + cs3[:, LANES:2 * LANES] + cs3[:, 2 * LANES:]
    a_cs_t = a_cs.T
    dt_t = dt.T
    a_last_col = a_cs_t[:, L - 1:L]
    w_end_t = jnp.exp2(a_last_col - a_cs_t) * dt_t
    exp_acs = jnp.exp2(a_cs)

    cd3 = [jnp.broadcast_to(part.astype(F32), (16, LANES)).astype(BF16)
           for part in _split3(exp_acs[L - 1:L, :])]
    cdx = jnp.dot(jnp.concatenate(cd3, axis=0), expand_ref[...], preferred_element_type=F32)
    cd_exp = cdx[0:1] + cdx[16:17] + cdx[32:33]

    lane = lax.broadcasted_iota(jnp.int32, (L, LANES), 1)
    low_half = lane < SSM_HEAD_DIM

    y_tiles = []
    for g in range(SSM_GROUPS):
        b_g = bcm[:, g * D_STATE:(g + 1) * D_STATE]
        c_g = bcm[:, (SSM_GROUPS + g) * D_STATE:(SSM_GROUPS + g + 1) * D_STATE]
        b_gb = b_g.astype(BF16)
        c_gb = c_g.astype(BF16)
        cb = lax.dot_general(c_gb, b_gb, (((1,), (1,)), ((), ())), preferred_element_type=F32)
        b_gt = b_g.T
        s_g = state[:, g * GROUP_CH:(g + 1) * GROUP_CH]
        y_off_g = jnp.dot(c_gb, s_g.astype(BF16), preferred_element_type=F32)
        for jj in range(GROUP_CH // LANES):
            j = g * (GROUP_CH // LANES) + jj
            lhs_top = []
            lhs_bot = []
            escale = []
            for e in (2 * j, 2 * j + 1):
                acs_b = jnp.broadcast_to(a_cs[:, e:e + 1], (L, L))
                seg = acs_b - a_cs_t[e:e + 1, :]
                dec = jnp.exp2(jnp.where(causal, seg, -jnp.inf))
                lhs_top.append((cb * dec * dt_t[e:e + 1, :]).astype(BF16))
                lhs_bot.append((b_gt * w_end_t[e:e + 1, :]).astype(BF16))
                escale.append(jnp.broadcast_to(exp_acs[:, e:e + 1], (L, LANES)))
            x_tile = xs[:, j * LANES:(j + 1) * LANES]
            x_lo = jnp.where(low_half, x_tile, 0.0).astype(BF16)
            x_hi = jnp.where(low_half, 0.0, x_tile).astype(BF16)
            lhs = jnp.concatenate([jnp.concatenate(lhs_top, axis=1),
                                   jnp.concatenate(lhs_bot, axis=1)], axis=0)
            rhs = jnp.concatenate([x_lo, x_hi], axis=0)
            r = jnp.dot(lhs, rhs, preferred_element_type=F32)
            y_off = y_off_g[:, jj * LANES:(jj + 1) * LANES]
            y_tiles.append(r[0:L] + jnp.where(low_half, escale[0], escale[1]) * y_off)
            sl = slice(j * LANES, (j + 1) * LANES)
            state[:, sl] = cd_exp[:, sl] * state[:, sl] + r[L:2 * L]

    y = jnp.concatenate(y_tiles, axis=1)
    y = y + xs * dskip_ref[...]
    gated = y * _silu(z_ref[...].astype(F32))
    outs = []
    for g in range(SSM_GROUPS):
        gg = gated[:, g * GROUP_CH:(g + 1) * GROUP_CH]
        ms = jnp.mean(gg * gg, axis=-1, keepdims=True)
        outs.append(gg * lax.rsqrt(ms + NORM_EPS))
    o_ref[...] = (jnp.concatenate(outs, axis=1) * normw_ref[...]).astype(o_ref.dtype)


def _ssd(zxbc, dt_raw, conv_w, conv_b, dt_bias, a_log, d_skip, norm_w, batch, seq):
    t = zxbc.shape[0]
    nc = seq // SSM_CHUNK
    L = SSM_CHUNK

    def pad_heads(v):
        return jnp.pad(v.astype(F32), (0, LANES - SSM_HEADS)).reshape(1, LANES)

    def rowmap(col):
        return lambda b, c: (b * nc + c, col)

    def const2(b, c):
        return (0, 0)

    l_idx = jnp.arange(L)[:, None]
    shift = jnp.concatenate(
        [(jnp.arange(L)[None, :] == l_idx - (CONV_WIDTH - 1) + k) for k in range(CONV_WIDTH)],
        axis=1).astype(BF16)
    shift_tail = jnp.concatenate(
        [(jnp.arange(-CONV_TAIL, 0)[None, :] == l_idx - (CONV_WIDTH - 1) + k) for k in range(CONV_WIDTH)],
        axis=1).astype(BF16)
    expand = (jnp.arange(D_INNER)[None, :] // SSM_HEAD_DIM == jnp.arange(LANES)[:, None]).astype(BF16)

    return pl.pallas_call(
        _ssd_kernel,
        grid=(batch, nc),
        in_specs=[
            pl.BlockSpec((L, D_INNER), rowmap(0)),
            pl.BlockSpec((L, D_INNER), rowmap(1)),
            pl.BlockSpec((L, BC_DIM), rowmap(2 * D_INNER // BC_DIM)),
            pl.BlockSpec((L, LANES), rowmap(0)),
            pl.BlockSpec((CONV_WIDTH, CONV_DIM), const2),
            pl.BlockSpec((1, CONV_DIM), const2),
            pl.BlockSpec((1, LANES), const2),
            pl.BlockSpec((1, LANES), const2),
            pl.BlockSpec((1, D_INNER), const2),
            pl.BlockSpec((1, D_INNER), const2),
            pl.BlockSpec((L, CONV_WIDTH * L), const2),
            pl.BlockSpec((L, CONV_WIDTH * CONV_TAIL), const2),
            pl.BlockSpec((LANES, D_INNER), const2),
        ],
        out_specs=pl.BlockSpec((L, D_INNER), rowmap(0)),
        out_shape=jax.ShapeDtypeStruct((t, D_INNER), BF16),
        scratch_shapes=[
            pltpu.VMEM((D_STATE, D_INNER), F32),
            pltpu.VMEM((CONV_TAIL, D_INNER), BF16),
            pltpu.VMEM((CONV_TAIL, BC_DIM), BF16),
        ],
        compiler_params=_cparams(("parallel", "arbitrary")),
        name="ssd_core",
    )(zxbc, zxbc, zxbc, dt_raw, conv_w.astype(F32), conv_b.astype(F32).reshape(1, CONV_DIM),
      pad_heads(dt_bias), pad_heads(a_log),
      jnp.repeat(d_skip.astype(F32), SSM_HEAD_DIM).reshape(1, D_INNER),
      norm_w.astype(F32).reshape(1, D_INNER), shift, shift_tail, expand)


GROUP_LANE0 = N_EXPERTS


def _router_kernel(x_ref, g_ref, w_ref, b_ref, tril_ref, ids_ref, gates_ref, cnt_ref, carry, w_split):
    i = pl.program_id(0)

    @pl.when(i == 0)
    def _():
        carry[...] = jnp.zeros_like(carry)
        w = w_ref[...]
        w_hi = w.astype(BF16)
        w_split[:, :LANES] = w_hi
        w_split[:, LANES:] = (w - w_hi.astype(F32)).astype(BF16)

    h = _rms(x_ref[...], g_ref[...])
    h_hi = h.astype(BF16)
    h_mid = (h - h_hi.astype(F32)).astype(BF16)
    top = jnp.dot(h_hi, w_split[...], preferred_element_type=F32)
    cross = jnp.dot(h_mid, w_split[:, :LANES], preferred_element_type=F32)
    logits = top[:, :LANES] + (top[:, LANES:] + cross) + b_ref[...]
    tm = logits.shape[0]
    lane = lax.broadcasted_iota(jnp.int32, (tm, LANES), 1)
    ninf = -jnp.inf

    def first_argmax(v, vmax):
        return jnp.min(jnp.where(v == vmax, lane, LANES), axis=-1, keepdims=True)

    is_group = (lane >= GROUP_LANE0) & (lane < GROUP_LANE0 + N_EXPERT_GROUPS)
    glog = jnp.where(is_group, logits, ninf)
    gmax = jnp.max(glog, axis=-1, keepdims=True)
    g_idx = first_argmax(glog, gmax) - GROUP_LANE0
    g_p = 1.0 / jnp.sum(jnp.exp(glog - gmax), axis=-1, keepdims=True)

    in_group = (lane < N_EXPERTS) & (jnp.right_shift(lane, 3) == g_idx)
    elog = jnp.where(in_group, logits, ninf)
    m1 = jnp.max(elog, axis=-1, keepdims=True)
    e1 = first_argmax(elog, m1)
    elog2 = jnp.where(lane == e1, ninf, elog)
    m2 = jnp.max(elog2, axis=-1, keepdims=True)
    e2 = first_argmax(elog2, m2)
    zsum = jnp.sum(jnp.exp(elog - m1), axis=-1, keepdims=True)
    p1 = 1.0 / zsum
    p2 = jnp.exp(m2 - m1) / zsum
    psum = p1 + p2
    w1 = p1 / psum * g_p
    w2 = p2 / psum * g_p

    oh1 = lane == e1
    oh2 = lane == e2
    onehot = jnp.logical_or(oh1, oh2).astype(F32)
    before = jnp.dot(tril_ref[...], onehot.astype(BF16), preferred_element_type=F32) + carry[0:1, :]
    r1 = jnp.sum(jnp.where(oh1, before, 0.0), axis=-1, keepdims=True).astype(jnp.int32)
    r2 = jnp.sum(jnp.where(oh2, before, 0.0), axis=-1, keepdims=True).astype(jnp.int32)
    carry[0:1, :] = carry[0:1, :] + jnp.sum(onehot, axis=0, keepdims=True)

    ids_ref[...] = jnp.where(lane == 0, e1, jnp.where(lane == 1, e2,
                             jnp.where(lane == 2, r1, jnp.where(lane == 3, r2, 0))))
    gates_ref[...] = jnp.where(lane == 0, w1, jnp.where(lane == 1, w2, 0.0))
    cnt_ref[...] = carry[...]


def _router(x, ln_w, w_group, b_group, w_expert, b_expert, *, tm=512):
    t, d = x.shape
    pad = LANES - N_EXPERTS - N_EXPERT_GROUPS
    w_r = jnp.concatenate([w_expert, w_group, jnp.zeros((d, pad), F32)], axis=1).astype(F32)
    b_r = jnp.concatenate([b_expert, b_group, jnp.zeros((pad,), F32)]).astype(F32).reshape(1, LANES)
    tril = (jnp.arange(tm)[:, None] > jnp.arange(tm)[None, :]).astype(BF16)
    return pl.pallas_call(
        _router_kernel,
        grid=(t // tm,),
        in_specs=[
            pl.BlockSpec((tm, d), lambda i: (i, 0)),
            pl.BlockSpec((1, d), lambda i: (0, 0)),
            pl.BlockSpec((d, LANES), lambda i: (0, 0)),
            pl.BlockSpec((1, LANES), lambda i: (0, 0)),
            pl.BlockSpec((tm, tm), lambda i: (0, 0)),
        ],
        out_specs=[
            pl.BlockSpec((tm, LANES), lambda i: (i, 0)),
            pl.BlockSpec((tm, LANES), lambda i: (i, 0)),
            pl.BlockSpec((8, LANES), lambda i: (0, 0)),
        ],
        out_shape=[
            jax.ShapeDtypeStruct((t, LANES), jnp.int32),
            jax.ShapeDtypeStruct((t, LANES), F32),
            jax.ShapeDtypeStruct((8, LANES), F32),
        ],
        scratch_shapes=[pltpu.VMEM((8, LANES), F32), pltpu.VMEM((d, 2 * LANES), BF16)],
        compiler_params=_cparams(("arbitrary",)),
        name="moe_router",
    )(x, ln_w.reshape(1, d), w_r, b_r, tril)


U32 = jnp.uint32
ROW_WORDS = D_MODEL // 2
ROW_SUB = ROW_WORDS // LANES
HIGH_HALF = 0xFFFF0000


def _pack_rows(v):
    lo = lax.bitcast_convert_type(v[:, :ROW_WORDS].astype(BF16).astype(F32), U32)
    hi = lax.bitcast_convert_type(v[:, ROW_WORDS:].astype(BF16).astype(F32), U32)
    return jnp.right_shift(lo, jnp.uint32(16)) | (hi & jnp.uint32(HIGH_HALF))


def _unpack_rows(w):
    lo = lax.bitcast_convert_type(jnp.left_shift(w, jnp.uint32(16)), F32)
    hi = lax.bitcast_convert_type(w & jnp.uint32(HIGH_HALF), F32)
    return lo, hi


def _store_rows(ref, packed):
    n = packed.shape[0]
    for c in range(ROW_SUB):
        ref[pl.ds(c, n, stride=ROW_SUB), :] = packed[:, c * LANES:(c + 1) * LANES]


def _load_rows(ref, n):
    return jnp.concatenate([ref[pl.ds(c, n, stride=ROW_SUB), :] for c in range(ROW_SUB)], axis=1)


def _row_slice(ref, row):
    return ref.at[pl.ds(pl.multiple_of(row * ROW_SUB, ROW_SUB), ROW_SUB)]


def _dispatch_kernel(zf_ref, dest_ref, x_ref, g_ref, xs_ref, hbuf, zbuf, sem, zsem):
    tb = x_ref.shape[0]
    n_blocks = zf_ref.shape[0]
    blk_rows = MOE_BLOCK * ROW_SUB

    @pl.when(pl.program_id(0) == 0)
    def _():
        zbuf[...] = jnp.zeros_like(zbuf)

        def zero_copy(b):
            start = pl.multiple_of(b * blk_rows, blk_rows)
            return pltpu.make_async_copy(zbuf, xs_ref.at[pl.ds(start, blk_rows)], zsem)

        def zissue(b, carry):
            @pl.when(zf_ref[b] != 0)
            def _():
                zero_copy(b).start()
            return carry

        def zdrain(b, carry):
            @pl.when(zf_ref[b] != 0)
            def _():
                zero_copy(b).wait()
            return carry

        lax.fori_loop(0, n_blocks, zissue, 0)
        lax.fori_loop(0, n_blocks, zdrain, 0)

    _store_rows(hbuf, _pack_rows(_rms(x_ref[...], g_ref[...])))
    for r in range(tb):
        for k in range(2):
            d = dest_ref[0, 0, 2 * r + k]
            pltpu.make_async_copy(_row_slice(hbuf, r), _row_slice(xs_ref, d), sem).start(priority=k)
    for k in range(2):
        pltpu.make_async_copy(hbuf, xs_ref.at[pl.ds(0, tb * ROW_SUB)], sem).wait()


def _dispatch(x, ln_w, dest, zero_flag, n_slots, *, tb=MOE_BLOCK):
    t, d = x.shape
    nblk = t // tb
    return pl.pallas_call(
        _dispatch_kernel,
        grid_spec=pltpu.PrefetchScalarGridSpec(
            num_scalar_prefetch=1,
            grid=(nblk,),
            in_specs=[
                pl.BlockSpec((1, 1, 2 * tb), lambda i, zf: (i, 0, 0), memory_space=pltpu.SMEM),
                pl.BlockSpec((tb, d), lambda i, zf: (i, 0)),
                pl.BlockSpec((1, d), lambda i, zf: (0, 0)),
            ],
            out_specs=pl.BlockSpec(memory_space=pl.ANY),
            scratch_shapes=[pltpu.VMEM((tb * ROW_SUB, LANES), U32),
                            pltpu.VMEM((MOE_BLOCK * ROW_SUB, LANES), U32),
                            pltpu.SemaphoreType.DMA(()), pltpu.SemaphoreType.DMA(())],
        ),
        out_shape=jax.ShapeDtypeStruct((n_slots * ROW_SUB, LANES), U32),
        compiler_params=_cparams(("arbitrary",)),
        name="moe_dispatch",
    )(zero_flag, dest.reshape(nblk, 1, 2 * tb), x, ln_w.reshape(1, d))


def _expert_kernel(be_ref, bpos_ref, elist_ref, meta_ref, xs_ref, wg_hbm, wu_hbm, wd_hbm, y_ref,
                   wg_s, wu_s, wd_s, wg_f, wu_f, wd_f, sems, *, layer):
    i = pl.program_id(0)
    active = i < meta_ref[0]
    pos = bpos_ref[i]
    changed = jnp.logical_or(i == 0, be_ref[i] != be_ref[jnp.maximum(i - 1, 0)])
    streams = ((wg_hbm, wg_f), (wu_hbm, wu_f), (wd_hbm, wd_f))

    def fetch(p, slot):
        e = elist_ref[p]
        return [pltpu.make_async_copy(hbm.at[layer, e], buf.at[slot], sems.at[slot, t])
                for t, (hbm, buf) in enumerate(streams)]

    @pl.when(jnp.logical_and(active, i == 0))
    def _():
        for cp in fetch(0, 0):
            cp.start()

    @pl.when(jnp.logical_and(active, changed))
    def _():
        slot = lax.rem(pos, 2)
        for cp in fetch(pos, slot):
            cp.wait()

        @pl.when(pos + 1 < meta_ref[1])
        def _():
            for cp in fetch(pos + 1, 1 - slot):
                cp.start()

        wg_s[...] = wg_f[slot].astype(BF16)
        wu_s[...] = wu_f[slot].astype(BF16)
        wd_s[...] = wd_f[slot].astype(BF16)

    @pl.when(active)
    def _():
        lo, hi = _unpack_rows(_load_rows(xs_ref, MOE_BLOCK))
        xb = jnp.concatenate([lo.astype(BF16), hi.astype(BF16)], axis=1)
        gate = jnp.dot(xb, wg_s[...], preferred_element_type=F32)
        up = jnp.dot(xb, wu_s[...], preferred_element_type=F32)
        hid = (_silu(gate) * up).astype(BF16)
        _store_rows(y_ref, _pack_rows(jnp.dot(hid, wd_s[...], preferred_element_type=F32)))

    @pl.when(jnp.logical_not(active))
    def _():
        y_ref[...] = jnp.zeros_like(y_ref)


def _experts(xs, block_e, block_pos, expert_list, meta, w_gate, w_up, w_down, layer):
    blk_rows = MOE_BLOCK * ROW_SUB
    n_blocks = xs.shape[0] // blk_rows
    d = D_MODEL

    def xmap(i, be, bpos, elist, meta):
        return (jnp.maximum(jnp.minimum(i, meta[0] - 1), 0), 0)

    return pl.pallas_call(
        functools.partial(_expert_kernel, layer=layer),
        grid_spec=pltpu.PrefetchScalarGridSpec(
            num_scalar_prefetch=4,
            grid=(n_blocks,),
            in_specs=[
                pl.BlockSpec((blk_rows, LANES), xmap),
                pl.BlockSpec(memory_space=pl.ANY),
                pl.BlockSpec(memory_space=pl.ANY),
                pl.BlockSpec(memory_space=pl.ANY),
            ],
            out_specs=pl.BlockSpec((blk_rows, LANES), lambda i, be, bpos, elist, meta: (i, 0)),
            scratch_shapes=[
                pltpu.VMEM((d, EXPERT_FF), BF16),
                pltpu.VMEM((d, EXPERT_FF), BF16),
                pltpu.VMEM((EXPERT_FF, d), BF16),
                pltpu.VMEM((2, d, EXPERT_FF), F32),
                pltpu.VMEM((2, d, EXPERT_FF), F32),
                pltpu.VMEM((2, EXPERT_FF, d), F32),
                pltpu.SemaphoreType.DMA((2, 3)),
            ],
        ),
        out_shape=jax.ShapeDtypeStruct(xs.shape, U32),
        compiler_params=_cparams(("arbitrary",)),
        name="moe_experts",
    )(block_e, block_pos, expert_list, meta, xs, w_gate, w_up, w_down)


def _combine_kernel(dest_ref, x_ref, gates_ref, y_ref, *rest, final_norm):
    if final_norm:
        g_ref, o_ref, buf0, buf1, sem = rest
    else:
        o_ref, buf0, buf1, sem = rest
    tb = x_ref.shape[0]
    bufs = (buf0, buf1)

    for r in range(tb):
        for k in range(2):
            d = dest_ref[0, 0, 2 * r + k]
            pltpu.make_async_copy(_row_slice(y_ref, d), _row_slice(bufs[k], r), sem).start(priority=k)
    for k in range(2):
        pltpu.make_async_copy(y_ref.at[pl.ds(0, tb * ROW_SUB)], bufs[k], sem).wait()

    gt = gates_ref[...]
    lo0, hi0 = _unpack_rows(_load_rows(buf0, tb))
    lo1, hi1 = _unpack_rows(_load_rows(buf1, tb))
    w0 = gt[:, 0:1]
    w1 = gt[:, 1:2]
    out = x_ref[...] + jnp.concatenate([lo0 * w0 + lo1 * w1, hi0 * w0 + hi1 * w1], axis=1)
    if final_norm:
        out = _rms(out, g_ref[...])
    o_ref[...] = out


def _combine(x, gates, y, dest, final_w=None, *, tb=256):
    t, d = x.shape
    nblk = t // tb
    final_norm = final_w is not None
    in_specs = [
        pl.BlockSpec((1, 1, 2 * tb), lambda i: (i, 0, 0), memory_space=pltpu.SMEM),
        pl.BlockSpec((tb, d), lambda i: (i, 0)),
        pl.BlockSpec((tb, LANES), lambda i: (i, 0)),
        pl.BlockSpec(memory_space=pl.ANY),
    ]
    args = [dest.reshape(nblk, 1, 2 * tb), x, gates, y]
    if final_norm:
        in_specs.append(pl.BlockSpec((1, d), lambda i: (0, 0)))
        args.append(final_w.reshape(1, d))
    return pl.pallas_call(
        functools.partial(_combine_kernel, final_norm=final_norm),
        grid=(nblk,),
        in_specs=in_specs,
        out_specs=pl.BlockSpec((tb, d), lambda i: (i, 0)),
        out_shape=jax.ShapeDtypeStruct((t, d), F32),
        scratch_shapes=[pltpu.VMEM((tb * ROW_SUB, LANES), U32), pltpu.VMEM((tb * ROW_SUB, LANES), U32),
                        pltpu.SemaphoreType.DMA(())],
        compiler_params=_cparams(("arbitrary",)),
        name="moe_combine",
    )(*args)


def _moe(x, ln_w, w_group, b_group, w_expert, b_expert, w_gate, w_up, w_down, layer, final_w=None):
    t = x.shape[0]
    n_assign = t * 2
    n_blocks = -(-n_assign // MOE_BLOCK) + N_EXPERTS
    n_slots = n_blocks * MOE_BLOCK

    ids, gates, cnt = _router(x, ln_w, w_group, b_group, w_expert, b_expert)
    counts = cnt[0, :N_EXPERTS].astype(jnp.int32)
    padded = (counts + MOE_BLOCK - 1) // MOE_BLOCK * MOE_BLOCK
    p_ends = jnp.cumsum(padded)
    p_starts = p_ends - padded
    experts = jnp.arange(N_EXPERTS, dtype=jnp.int32)
    start_of = jnp.sum(jnp.where(ids[:, 0:2, None] == experts, p_starts, 0), axis=-1)
    dest = (start_of + ids[:, 2:4]).astype(jnp.int32)
    block_start = jnp.arange(n_blocks, dtype=jnp.int32) * MOE_BLOCK
    block_e = jnp.minimum(jnp.sum((p_ends[None, :] <= block_start[:, None]).astype(jnp.int32), axis=1),
                          N_EXPERTS - 1)
    n_active = (p_ends[-1:] // MOE_BLOCK).astype(jnp.int32)
    valid_end = jnp.sum(jnp.where(block_e[:, None] == experts, p_starts + counts, 0), axis=-1)
    zero_flag = jnp.logical_or(block_start + MOE_BLOCK > valid_end,
                               block_start >= p_ends[-1]).astype(jnp.int32)

    has_tokens = counts > 0
    list_pos = jnp.cumsum(has_tokens.astype(jnp.int32)) - 1
    expert_list = jnp.sum(jnp.where(jnp.logical_and(has_tokens[None, :], list_pos[None, :] == experts[:, None]),
                                    experts[None, :], 0), axis=1).astype(jnp.int32)
    block_pos = jnp.sum(jnp.where(block_e[:, None] == experts, list_pos, 0), axis=-1).astype(jnp.int32)
    meta = jnp.concatenate([n_active, jnp.sum(has_tokens.astype(jnp.int32), keepdims=True)]).astype(jnp.int32)

    xs = _dispatch(x, ln_w, dest, zero_flag, n_slots)
    y = _experts(xs, block_e, block_pos, expert_list, meta, w_gate, w_up, w_down, layer)
    return _combine(x, gates, y, dest, final_w)


def kernel(x, rel_bias, ln_mix, ln_ffn, ln_final, attn_w_qkv, attn_sinks, attn_w_o, ssm_w_in,
           ssm_conv_w, ssm_conv_b, ssm_dt_bias, ssm_a_log, ssm_d, ssm_norm_w, ssm_w_out,
           moe_w_group, moe_b_group, moe_w_expert, moe_b_expert, moe_w_gate, moe_w_up, moe_w_down):
    batch, seq, d = x.shape
    t = batch * seq
    xf = x.reshape(t, d).astype(F32)

    qkv = _norm_matmul(xf, ln_mix[0], attn_w_qkv[0].astype(BF16), tm=512, tn=QKV_DIM)
    att = _attention(qkv, attn_sinks[0], rel_bias, batch, seq)
    xf = _matmul_residual(att, attn_w_o[0].astype(BF16), xf)
    xf = _moe(xf, ln_ffn[0], moe_w_group[0], moe_b_group[0], moe_w_expert[0], moe_b_expert[0],
              moe_w_gate, moe_w_up, moe_w_down, 0)

    w_in = ssm_w_in[0]
    w_dt = jnp.pad(w_in[:, ZXBC_DIM:], ((0, 0), (0, LANES - SSM_HEADS))).astype(BF16)
    zxbc, dt_raw = _norm_matmul(xf, ln_mix[1], w_in[:, :ZXBC_DIM].astype(BF16), w_dt, tm=512, tn=ZXBC_DIM)
    gated = _ssd(zxbc, dt_raw, ssm_conv_w[0], ssm_conv_b[0], ssm_dt_bias[0], ssm_a_log[0],
                 ssm_d[0], ssm_norm_w[0], batch, seq)
    xf = _matmul_residual(gated, ssm_w_out[0].astype(BF16), xf)
    xf = _moe(xf, ln_ffn[1], moe_w_group[1], moe_b_group[1], moe_w_expert[1], moe_b_expert[1],
              moe_w_gate, moe_w_up, moe_w_down, 1, final_w=ln_final)
    return xf.reshape(batch, seq, d).astype(x.dtype)
```

```python
import functools
import math

import jax
import jax.numpy as jnp
from jax import lax
from jax.experimental import pallas as pl
from jax.experimental.pallas import tpu as pltpu

F32 = jnp.float32
BF16 = jnp.bfloat16

D_MODEL = 1024
N_HEADS = 16
N_KV_HEADS = 4
HEAD_DIM = 64
GQA_GROUP = N_HEADS // N_KV_HEADS
WINDOW = 128
ATTN_BLOCK = 128
ATTN_STEP_BLOCKS = 1
Q_DIM = N_HEADS * HEAD_DIM
KV_DIM = N_KV_HEADS * HEAD_DIM
QKV_DIM = Q_DIM + 2 * KV_DIM
REL_BUCKETS = 32
REL_MAX_DIST = 128

D_INNER = 2048
SSM_HEAD_DIM = 64
SSM_HEADS = D_INNER // SSM_HEAD_DIM
SSM_GROUPS = 4
D_STATE = 128
CONV_WIDTH = 4
BC_DIM = 2 * SSM_GROUPS * D_STATE
CONV_DIM = D_INNER + BC_DIM
ZXBC_DIM = D_INNER + CONV_DIM
SSM_CHUNK = 128
SSD_STEP_CHUNKS = 4
GROUP_CH = D_INNER // SSM_GROUPS

N_EXPERT_GROUPS = 8
EXPERTS_PER_GROUP = 8
N_EXPERTS = 64
EXPERT_FF = 512
MOE_BLOCK = 256

NORM_EPS = 1e-6
LOG2E = 1.4426950408889634
CONV_TAIL = 16
LANES = 128
NEG_BIG = -1e30
VMEM_LIMIT = 56 * 1024 * 1024


def _cparams(sem):
    return pltpu.CompilerParams(dimension_semantics=sem, vmem_limit_bytes=VMEM_LIMIT)


def _rms(x, g):
    ms = jnp.mean(x * x, axis=-1, keepdims=True)
    return x * lax.rsqrt(ms + NORM_EPS) * g


def _silu(x):
    h = 0.5 * x
    return h + h * jnp.tanh(h)


def _norm_matmul_kernel(x_ref, g_ref, w_ref, *rest, has_aux):
    if has_aux:
        wa_ref, o_ref, oa_ref, h_scr = rest
    else:
        o_ref, h_scr = rest

    @pl.when(pl.program_id(1) == 0)
    def _():
        h = _rms(x_ref[...], g_ref[...]).astype(BF16)
        h_scr[...] = h
        if has_aux:
            oa_ref[...] = jnp.dot(h, wa_ref[...], preferred_element_type=F32)

    o_ref[...] = jnp.dot(h_scr[...], w_ref[...], preferred_element_type=F32).astype(o_ref.dtype)


def _norm_matmul(x, g, w, w_aux=None, *, tm=1024, tn=512):
    t, d = x.shape
    n = w.shape[1]
    has_aux = w_aux is not None
    in_specs = [
        pl.BlockSpec((tm, d), lambda i, j: (i, 0)),
        pl.BlockSpec((1, d), lambda i, j: (0, 0)),
        pl.BlockSpec((d, tn), lambda i, j: (0, j)),
    ]
    out_shape = [jax.ShapeDtypeStruct((t, n), BF16)]
    out_specs = [pl.BlockSpec((tm, tn), lambda i, j: (i, j))]
    args = [x, g.reshape(1, d), w]
    if has_aux:
        na = w_aux.shape[1]
        in_specs.append(pl.BlockSpec((d, na), lambda i, j: (0, 0)))
        out_shape.append(jax.ShapeDtypeStruct((t, na), F32))
        out_specs.append(pl.BlockSpec((tm, na), lambda i, j: (i, 0)))
        args.append(w_aux)
    res = pl.pallas_call(
        functools.partial(_norm_matmul_kernel, has_aux=has_aux),
        grid=(t // tm, n // tn),
        in_specs=in_specs,
        out_specs=out_specs,
        out_shape=out_shape,
        scratch_shapes=[pltpu.VMEM((tm, d), BF16)],
        compiler_params=_cparams(("parallel", "arbitrary")),
        name="norm_matmul",
    )(*args)
    return res if has_aux else res[0]


def _matmul_residual_kernel(a_ref, w_ref, r_ref, o_ref):
    o_ref[...] = r_ref[...] + jnp.dot(a_ref[...], w_ref[...], preferred_element_type=F32)


def _matmul_residual(a, w, res, *, tm=512):
    t, k = a.shape
    n = w.shape[1]
    return pl.pallas_call(
        _matmul_residual_kernel,
        grid=(t // tm,),
        in_specs=[
            pl.BlockSpec((tm, k), lambda i: (i, 0)),
            pl.BlockSpec((k, n), lambda i: (0, 0)),
            pl.BlockSpec((tm, n), lambda i: (i, 0)),
        ],
        out_specs=pl.BlockSpec((tm, n), lambda i: (i, 0)),
        out_shape=jax.ShapeDtypeStruct((t, n), F32),
        compiler_params=_cparams(("parallel",)),
        name="matmul_residual",
    )(a, w, res)


def _attn_kernel(sinks_ref, q_ref, kp_ref, kc_ref, vp_ref, vc_ref, bias_ref, o_ref):
    first = pl.program_id(1) == 0
    col = lax.broadcasted_iota(jnp.int32, (ATTN_BLOCK, 2 * ATTN_BLOCK), 1)
    hide_prev = jnp.logical_and(first, col < ATTN_BLOCK)
    scale = HEAD_DIM ** -0.5
    B = ATTN_BLOCK
    for j in range(q_ref.shape[0] // B):
        rows = slice(j * B, (j + 1) * B)
        for h in range(N_KV_HEADS):
            ks = slice(h * HEAD_DIM, (h + 1) * HEAD_DIM)
            if j == 0:
                kb = jnp.concatenate([kp_ref[:, ks], kc_ref[0:B, ks]], axis=0)
                vb = jnp.concatenate([vp_ref[:, ks], vc_ref[0:B, ks]], axis=0)
            else:
                kb = kc_ref[(j - 1) * B:(j + 1) * B, ks]
                vb = vc_ref[(j - 1) * B:(j + 1) * B, ks]
            for g in range(GQA_GROUP):
                hh = h * GQA_GROUP + g
                qh = q_ref[rows, hh * HEAD_DIM:(hh + 1) * HEAD_DIM]
                s = lax.dot_general(qh, kb, (((1,), (1,)), ((), ())),
                                    preferred_element_type=F32) * scale
                logits = s + bias_ref[hh]
                if j == 0:
                    logits = jnp.where(hide_prev, NEG_BIG, logits)
                sink = sinks_ref[hh]
                m = jnp.maximum(jnp.max(logits, axis=-1, keepdims=True), sink)
                p = jnp.exp(logits - m)
                denom = jnp.sum(p, axis=-1, keepdims=True) + jnp.exp(sink - m)
                o = jnp.dot(p.astype(BF16), vb, preferred_element_type=F32) / denom
                o_ref[rows, hh * HEAD_DIM:(hh + 1) * HEAD_DIM] = o.astype(o_ref.dtype)


def _t5_causal_bucket(dist):
    n = jnp.maximum(dist, 0)
    max_exact = REL_BUCKETS // 2
    nf = jnp.maximum(n, 1).astype(F32)
    large = max_exact + (jnp.log(nf / max_exact) / math.log(REL_MAX_DIST / max_exact)
                         * (REL_BUCKETS - max_exact)).astype(jnp.int32)
    large = jnp.minimum(large, REL_BUCKETS - 1)
    return jnp.where(n < max_exact, n, large)


def _attention(qkv, sinks, rel_bias, batch, seq):
    t = qkv.shape[0]
    nb = seq // ATTN_BLOCK
    qi = jnp.arange(ATTN_BLOCK)[:, None]
    ki = jnp.arange(2 * ATTN_BLOCK)[None, :]
    dist = qi + ATTN_BLOCK - ki
    in_window = (dist >= 0) & (dist < WINDOW)
    onehot = (_t5_causal_bucket(dist)[None] == jnp.arange(REL_BUCKETS)[:, None, None]).astype(F32)
    bias = jnp.einsum('hr,rqk->hqk', rel_bias.astype(F32).T, onehot, precision=lax.Precision.HIGHEST)
    bias = jnp.where(in_window[None], bias, NEG_BIG)

    kcol = Q_DIM // KV_DIM
    vcol = kcol + 1

    spb = ATTN_STEP_BLOCKS
    ns = nb // spb
    rows = spb * ATTN_BLOCK

    def prev(b, n):
        return b * nb + jnp.maximum(n * spb - 1, 0)

    return pl.pallas_call(
        _attn_kernel,
        grid=(batch, ns),
        in_specs=[
            pl.BlockSpec(memory_space=pltpu.SMEM),
            pl.BlockSpec((rows, Q_DIM), lambda b, n: (b * ns + n, 0)),
            pl.BlockSpec((ATTN_BLOCK, KV_DIM), lambda b, n: (prev(b, n), kcol)),
            pl.BlockSpec((rows, KV_DIM), lambda b, n: (b * ns + n, kcol)),
            pl.BlockSpec((ATTN_BLOCK, KV_DIM), lambda b, n: (prev(b, n), vcol)),
            pl.BlockSpec((rows, KV_DIM), lambda b, n: (b * ns + n, vcol)),
            pl.BlockSpec((N_HEADS, ATTN_BLOCK, 2 * ATTN_BLOCK), lambda b, n: (0, 0, 0)),
        ],
        out_specs=pl.BlockSpec((rows, Q_DIM), lambda b, n: (b * ns + n, 0)),
        out_shape=jax.ShapeDtypeStruct((t, Q_DIM), BF16),
        compiler_params=_cparams(("parallel", "arbitrary")),
        name="swa_attention",
    )(sinks.astype(F32), qkv, qkv, qkv, qkv, qkv, bias)


def _split3(v):
    hi = v.astype(BF16)
    r = v - hi.astype(F32)
    mid = r.astype(BF16)
    lo = (r - mid.astype(F32)).astype(BF16)
    return hi, mid, lo


def _ssd_kernel(z_ref, x_ref, bc_ref, dt_ref, convw_ref, convb_ref, dtb_ref, alog_ref,
                dskip_ref, normw_ref, shift_ref, shift_tail_ref, expand_ref,
                o_ref, state, xtail, bctail):
    @pl.when(pl.program_id(1) == 0)
    def _():
        state[...] = jnp.zeros_like(state)
        xtail[...] = jnp.zeros_like(xtail)
        bctail[...] = jnp.zeros_like(bctail)

    for k in range(z_ref.shape[0] // SSM_CHUNK):
        _ssd_chunk(pl.ds(k * SSM_CHUNK, SSM_CHUNK), z_ref, x_ref, bc_ref, dt_ref, convw_ref, convb_ref,
                   dtb_ref, alog_ref, dskip_ref, normw_ref, shift_ref, shift_tail_ref, expand_ref,
                   o_ref, state, xtail, bctail)


def _ssd_chunk(rows, z_ref, x_ref, bc_ref, dt_ref, convw_ref, convb_ref, dtb_ref, alog_ref,
               dskip_ref, normw_ref, shift_ref, shift_tail_ref, expand_ref,
               o_ref, state, xtail, bctail):
    L = SSM_CHUNK

    def conv_silu(u_ref, tail, w_lo, width):
        cur = u_ref[rows, :]
        prev = tail[...]
        w = [convw_ref[k:k + 1, w_lo:w_lo + width].astype(BF16) for k in range(CONV_WIDTH)]
        taps = jnp.concatenate([cur * w[k] for k in range(CONV_WIDTH)], axis=0)
        taps_prev = jnp.concatenate([prev * w[k] for k in range(CONV_WIDTH)], axis=0)
        y = (jnp.dot(shift_ref[...], taps, preferred_element_type=F32)
             + jnp.dot(shift_tail_ref[...], taps_prev, preferred_element_type=F32)
             + convb_ref[:, w_lo:w_lo + width])
        tail[...] = cur[L - CONV_TAIL:L]
        return _silu(y)

    xs = conv_silu(x_ref, xtail, 0, D_INNER)
    bcm = conv_silu(bc_ref, bctail, D_INNER, BC_DIM)

    xdt = dt_ref[rows, :] + dtb_ref[...]
    dt = jnp.maximum(xdt, 0.0) + jnp.log1p(jnp.exp(-jnp.abs(xdt)))
    a = dt * (-jnp.exp(alog_ref[...]) * LOG2E)
    row = lax.broadcasted_iota(jnp.int32, (L, L), 0)
    colm = lax.broadcasted_iota(jnp.int32, (L, L), 1)
    causal = row >= colm
    cs3 = jnp.dot(causal.astype(BF16), jnp.concatenate(_split3(a), axis=1), preferred_element_type=F32)
    a_cs = cs3[:, :LANES] + cs3[:, LANES:2 * LANES] + cs3[:, 2 * LANES:]
    a_cs_t = a_cs.T
    dt_t = dt.T
    a_last_col = a_cs_t[:, L - 1:L]
    w_end_t = jnp.exp2(a_last_col - a_cs_t) * dt_t
    exp_acs = jnp.exp2(a_cs)

    cd3 = [jnp.broadcast_to(part.astype(F32), (16, LANES)).astype(BF16)
           for part in _split3(exp_acs[L - 1:L, :])]
    cdx = jnp.dot(jnp.concatenate(cd3, axis=0), expand_ref[...], preferred_element_type=F32)
    cd_exp = cdx[0:1] + cdx[16:17] + cdx[32:33]

    lane = lax.broadcasted_iota(jnp.int32, (L, LANES), 1)
    low_half = lane < SSM_HEAD_DIM

    y_tiles = []
    for g in range(SSM_GROUPS):
        b_g = bcm[:, g * D_STATE:(g + 1) * D_STATE]
        c_g = bcm[:, (SSM_GROUPS + g) * D_STATE:(SSM_GROUPS + g + 1) * D_STATE]
        b_gb = b_g.astype(BF16)
        c_gb = c_g.astype(BF16)
        cb = lax.dot_general(c_gb, b_gb, (((1,), (1,)), ((), ())), preferred_element_type=F32)
        b_gt = b_g.T
        s_g = state[:, g * GROUP_CH:(g + 1) * GROUP_CH]
        y_off_g = jnp.dot(c_gb, s_g.astype(BF16), preferred_element_type=F32)
        for jj in range(GROUP_CH // LANES):
            j = g * (GROUP_CH // LANES) + jj
            lhs_top = []
            lhs_bot = []
            escale = []
            for e in (2 * j, 2 * j + 1):
                acs_b = jnp.broadcast_to(a_cs[:, e:e + 1], (L, L))
                seg = acs_b - a_cs_t[e:e + 1, :]
                dec = jnp.exp2(jnp.where(causal, seg, -jnp.inf))
                lhs_top.append((cb * dec * dt_t[e:e + 1, :]).astype(BF16))
                lhs_bot.append((b_gt * w_end_t[e:e + 1, :]).astype(BF16))
                escale.append(jnp.broadcast_to(exp_acs[:, e:e + 1], (L, LANES)))
            x_tile = xs[:, j * LANES:(j + 1) * LANES]
            x_lo = jnp.where(low_half, x_tile, 0.0).astype(BF16)
            x_hi = jnp.where(low_half, 0.0, x_tile).astype(BF16)
            lhs = jnp.concatenate([jnp.concatenate(lhs_top, axis=1),
                                   jnp.concatenate(lhs_bot, axis=1)], axis=0)
            rhs = jnp.concatenate([x_lo, x_hi], axis=0)
            r = jnp.dot(lhs, rhs, preferred_element_type=F32)
            y_off = y_off_g[:, jj * LANES:(jj + 1) * LANES]
            y_tiles.append(r[0:L] + jnp.where(low_half, escale[0], escale[1]) * y_off)
            sl = slice(j * LANES, (j + 1) * LANES)
            state[:, sl] = cd_exp[:, sl] * state[:, sl] + r[L:2 * L]

    y = jnp.concatenate(y_tiles, axis=1)
    y = y + xs * dskip_ref[...]
    gated = y * _silu(z_ref[rows, :].astype(F32))
    outs = []
    for g in range(SSM_GROUPS):
        gg = gated[:, g * GROUP_CH:(g + 1) * GROUP_CH]
        ms = jnp.mean(gg * gg, axis=-1, keepdims=True)
        outs.append(gg * lax.rsqrt(ms + NORM_EPS))
    o_ref[rows, :] = (jnp.concatenate(outs, axis=1) * normw_ref[...]).astype(o_ref.dtype)


def _ssd(zxbc, dt_raw, conv_w, conv_b, dt_bias, a_log, d_skip, norm_w, batch, seq):
    t = zxbc.shape[0]
    L = SSM_CHUNK
    rows = SSD_STEP_CHUNKS * L
    nc = seq // rows

    def pad_heads(v):
        return jnp.pad(v.astype(F32), (0, LANES - SSM_HEADS)).reshape(1, LANES)

    def rowmap(col):
        return lambda b, c: (b * nc + c, col)

    def const2(b, c):
        return (0, 0)

    l_idx = jnp.arange(L)[:, None]
    shift = jnp.concatenate(
        [(jnp.arange(L)[None, :] == l_idx - (CONV_WIDTH - 1) + k) for k in range(CONV_WIDTH)],
        axis=1).astype(BF16)
    shift_tail = jnp.concatenate(
        [(jnp.arange(-CONV_TAIL, 0)[None, :] == l_idx - (CONV_WIDTH - 1) + k) for k in range(CONV_WIDTH)],
        axis=1).astype(BF16)
    expand = (jnp.arange(D_INNER)[None, :] // SSM_HEAD_DIM == jnp.arange(LANES)[:, None]).astype(BF16)

    return pl.pallas_call(
        _ssd_kernel,
        grid=(batch, nc),
        in_specs=[
            pl.BlockSpec((rows, D_INNER), rowmap(0)),
            pl.BlockSpec((rows, D_INNER), rowmap(1)),
            pl.BlockSpec((rows, BC_DIM), rowmap(2 * D_INNER // BC_DIM)),
            pl.BlockSpec((rows, LANES), rowmap(0)),
            pl.BlockSpec((CONV_WIDTH, CONV_DIM), const2),
            pl.BlockSpec((1, CONV_DIM), const2),
            pl.BlockSpec((1, LANES), const2),
            pl.BlockSpec((1, LANES), const2),
            pl.BlockSpec((1, D_INNER), const2),
            pl.BlockSpec((1, D_INNER), const2),
            pl.BlockSpec((L, CONV_WIDTH * L), const2),
            pl.BlockSpec((L, CONV_WIDTH * CONV_TAIL), const2),
            pl.BlockSpec((LANES, D_INNER), const2),
        ],
        out_specs=pl.BlockSpec((rows, D_INNER), rowmap(0)),
        out_shape=jax.ShapeDtypeStruct((t, D_INNER), BF16),
        scratch_shapes=[
            pltpu.VMEM((D_STATE, D_INNER), F32),
            pltpu.VMEM((CONV_TAIL, D_INNER), BF16),
            pltpu.VMEM((CONV_TAIL, BC_DIM), BF16),
        ],
        compiler_params=_cparams(("parallel", "arbitrary")),
        name="ssd_core",
    )(zxbc, zxbc, zxbc, dt_raw, conv_w.astype(F32), conv_b.astype(F32).reshape(1, CONV_DIM),
      pad_heads(dt_bias), pad_heads(a_log),
      jnp.repeat(d_skip.astype(F32), SSM_HEAD_DIM).reshape(1, D_INNER),
      norm_w.astype(F32).reshape(1, D_INNER), shift, shift_tail, expand)


GROUP_LANE0 = N_EXPERTS


def _router_kernel(x_ref, g_ref, w_ref, b_ref, tril_ref, ids_ref, gates_ref, cnt_ref, carry, w_split):
    i = pl.program_id(0)

    @pl.when(i == 0)
    def _():
        carry[...] = jnp.zeros_like(carry)
        w = w_ref[...]
        w_hi = w.astype(BF16)
        w_split[:, :LANES] = w_hi
        w_split[:, LANES:] = (w - w_hi.astype(F32)).astype(BF16)

    h = _rms(x_ref[...], g_ref[...])
    h_hi = h.astype(BF16)
    h_mid = (h - h_hi.astype(F32)).astype(BF16)
    top = jnp.dot(h_hi, w_split[...], preferred_element_type=F32)
    cross = jnp.dot(h_mid, w_split[:, :LANES], preferred_element_type=F32)
    logits = top[:, :LANES] + (top[:, LANES:] + cross) + b_ref[...]
    tm = logits.shape[0]
    lane = lax.broadcasted_iota(jnp.int32, (tm, LANES), 1)
    ninf = -jnp.inf

    def first_argmax(v, vmax):
        return jnp.min(jnp.where(v == vmax, lane, LANES), axis=-1, keepdims=True)

    is_group = (lane >= GROUP_LANE0) & (lane < GROUP_LANE0 + N_EXPERT_GROUPS)
    glog = jnp.where(is_group, logits, ninf)
    gmax = jnp.max(glog, axis=-1, keepdims=True)
    g_idx = first_argmax(glog, gmax) - GROUP_LANE0
    g_p = 1.0 / jnp.sum(jnp.exp(glog - gmax), axis=-1, keepdims=True)

    in_group = (lane < N_EXPERTS) & (jnp.right_shift(lane, 3) == g_idx)
    elog = jnp.where(in_group, logits, ninf)
    m1 = jnp.max(elog, axis=-1, keepdims=True)
    e1 = first_argmax(elog, m1)
    elog2 = jnp.where(lane == e1, ninf, elog)
    m2 = jnp.max(elog2, axis=-1, keepdims=True)
    e2 = first_argmax(elog2, m2)
    zsum = jnp.sum(jnp.exp(elog - m1), axis=-1, keepdims=True)
    p1 = 1.0 / zsum
    p2 = jnp.exp(m2 - m1) / zsum
    psum = p1 + p2
    w1 = p1 / psum * g_p
    w2 = p2 / psum * g_p

    oh1 = lane == e1
    oh2 = lane == e2
    onehot = jnp.logical_or(oh1, oh2).astype(F32)
    before = jnp.dot(tril_ref[...], onehot.astype(BF16), preferred_element_type=F32) + carry[0:1, :]
    r1 = jnp.sum(jnp.where(oh1, before, 0.0), axis=-1, keepdims=True).astype(jnp.int32)
    r2 = jnp.sum(jnp.where(oh2, before, 0.0), axis=-1, keepdims=True).astype(jnp.int32)
    carry[0:1, :] = carry[0:1, :] + jnp.sum(onehot, axis=0, keepdims=True)

    ids_ref[...] = jnp.where(lane == 0, e1, jnp.where(lane == 1, e2,
                             jnp.where(lane == 2, r1, jnp.where(lane == 3, r2, 0))))
    gates_ref[...] = jnp.where(lane == 0, w1, jnp.where(lane == 1, w2, 0.0))
    cnt_ref[...] = carry[...]


def _router(x, ln_w, w_group, b_group, w_expert, b_expert, *, tm=512):
    t, d = x.shape
    pad = LANES - N_EXPERTS - N_EXPERT_GROUPS
    w_r = jnp.concatenate([w_expert, w_group, jnp.zeros((d, pad), F32)], axis=1).astype(F32)
    b_r = jnp.concatenate([b_expert, b_group, jnp.zeros((pad,), F32)]).astype(F32).reshape(1, LANES)
    tril = (jnp.arange(tm)[:, None] > jnp.arange(tm)[None, :]).astype(BF16)
    return pl.pallas_call(
        _router_kernel,
        grid=(t // tm,),
        in_specs=[
            pl.BlockSpec((tm, d), lambda i: (i, 0)),
            pl.BlockSpec((1, d), lambda i: (0, 0)),
            pl.BlockSpec((d, LANES), lambda i: (0, 0)),
            pl.BlockSpec((1, LANES), lambda i: (0, 0)),
            pl.BlockSpec((tm, tm), lambda i: (0, 0)),
        ],
        out_specs=[
            pl.BlockSpec((tm, LANES), lambda i: (i, 0)),
            pl.BlockSpec((tm, LANES), lambda i: (i, 0)),
            pl.BlockSpec((8, LANES), lambda i: (0, 0)),
        ],
        out_shape=[
            jax.ShapeDtypeStruct((t, LANES), jnp.int32),
            jax.ShapeDtypeStruct((t, LANES), F32),
            jax.ShapeDtypeStruct((8, LANES), F32),
        ],
        scratch_shapes=[pltpu.VMEM((8, LANES), F32), pltpu.VMEM((d, 2 * LANES), BF16)],
        compiler_params=_cparams(("arbitrary",)),
        name="moe_router",
    )(x, ln_w.reshape(1, d), w_r, b_r, tril)


U32 = jnp.uint32
ROW_WORDS = D_MODEL // 2
ROW_SUB = ROW_WORDS // LANES
HIGH_HALF = 0xFFFF0000


def _pack_rows(v):
    lo = lax.bitcast_convert_type(v[:, :ROW_WORDS].astype(BF16).astype(F32), U32)
    hi = lax.bitcast_convert_type(v[:, ROW_WORDS:].astype(BF16).astype(F32), U32)
    return jnp.right_shift(lo, jnp.uint32(16)) | (hi & jnp.uint32(HIGH_HALF))


def _unpack_rows(w):
    lo = lax.bitcast_convert_type(jnp.left_shift(w, jnp.uint32(16)), F32)
    hi = lax.bitcast_convert_type(w & jnp.uint32(HIGH_HALF), F32)
    return lo, hi


def _store_rows(ref, packed):
    n = packed.shape[0]
    for c in range(ROW_SUB):
        ref[pl.ds(c, n, stride=ROW_SUB), :] = packed[:, c * LANES:(c + 1) * LANES]


def _load_rows(ref, n):
    return jnp.concatenate([ref[pl.ds(c, n, stride=ROW_SUB), :] for c in range(ROW_SUB)], axis=1)


def _row_slice(ref, row):
    return ref.at[pl.ds(pl.multiple_of(row * ROW_SUB, ROW_SUB), ROW_SUB)]


def _dispatch_kernel(zf_ref, dest_ref, x_ref, g_ref, xs_ref, hbuf, zbuf, sem, zsem):
    tb = x_ref.shape[0]
    n_blocks = zf_ref.shape[0]
    blk_rows = MOE_BLOCK * ROW_SUB

    @pl.when(pl.program_id(0) == 0)
    def _():
        zbuf[...] = jnp.zeros_like(zbuf)

        def zero_copy(b):
            start = pl.multiple_of(b * blk_rows, blk_rows)
            return pltpu.make_async_copy(zbuf, xs_ref.at[pl.ds(start, blk_rows)], zsem)

        def zissue(b, carry):
            @pl.when(zf_ref[b] != 0)
            def _():
                zero_copy(b).start()
            return carry

        def zdrain(b, carry):
            @pl.when(zf_ref[b] != 0)
            def _():
                zero_copy(b).wait()
            return carry

        lax.fori_loop(0, n_blocks, zissue, 0)
        lax.fori_loop(0, n_blocks, zdrain, 0)

    _store_rows(hbuf, _pack_rows(_rms(x_ref[...], g_ref[...])))
    for r in range(tb):
        for k in range(2):
            d = dest_ref[0, 0, 2 * r + k]
            pltpu.make_async_copy(_row_slice(hbuf, r), _row_slice(xs_ref, d), sem).start(priority=k)
    for k in range(2):
        pltpu.make_async_copy(hbuf, xs_ref.at[pl.ds(0, tb * ROW_SUB)], sem).wait()


def _dispatch(x, ln_w, dest, zero_flag, n_slots, *, tb=MOE_BLOCK):
    t, d = x.shape
    nblk = t // tb
    return pl.pallas_call(
        _dispatch_kernel,
        grid_spec=pltpu.PrefetchScalarGridSpec(
            num_scalar_prefetch=1,
            grid=(nblk,),
            in_specs=[
                pl.BlockSpec((1, 1, 2 * tb), lambda i, zf: (i, 0, 0), memory_space=pltpu.SMEM),
                pl.BlockSpec((tb, d), lambda i, zf: (i, 0)),
                pl.BlockSpec((1, d), lambda i, zf: (0, 0)),
            ],
            out_specs=pl.BlockSpec(memory_space=pl.ANY),
            scratch_shapes=[pltpu.VMEM((tb * ROW_SUB, LANES), U32),
                            pltpu.VMEM((MOE_BLOCK * ROW_SUB, LANES), U32),
                            pltpu.SemaphoreType.DMA(()), pltpu.SemaphoreType.DMA(())],
        ),
        out_shape=jax.ShapeDtypeStruct((n_slots * ROW_SUB, LANES), U32),
        compiler_params=_cparams(("arbitrary",)),
        name="moe_dispatch",
    )(zero_flag, dest.reshape(nblk, 1, 2 * tb), x, ln_w.reshape(1, d))


def _expert_kernel(be_ref, bpos_ref, elist_ref, meta_ref, xs_ref, wg_hbm, wu_hbm, wd_hbm, y_ref,
                   wg_s, wu_s, wd_s, wg_f, wu_f, wd_f, sems, *, layer):
    i = pl.program_id(0)
    active = i < meta_ref[0]
    pos = bpos_ref[i]
    changed = jnp.logical_or(i == 0, be_ref[i] != be_ref[jnp.maximum(i - 1, 0)])
    streams = ((wg_hbm, wg_f), (wu_hbm, wu_f), (wd_hbm, wd_f))

    def fetch(p, slot):
        e = elist_ref[p]
        return [pltpu.make_async_copy(hbm.at[layer, e], buf.at[slot], sems.at[slot, t])
                for t, (hbm, buf) in enumerate(streams)]

    @pl.when(jnp.logical_and(active, i == 0))
    def _():
        for cp in fetch(0, 0):
            cp.start()

    @pl.when(jnp.logical_and(active, changed))
    def _():
        slot = lax.rem(pos, 2)
        for cp in fetch(pos, slot):
            cp.wait()

        @pl.when(pos + 1 < meta_ref[1])
        def _():
            for cp in fetch(pos + 1, 1 - slot):
                cp.start()

        wg_s[...] = wg_f[slot].astype(BF16)
        wu_s[...] = wu_f[slot].astype(BF16)
        wd_s[...] = wd_f[slot].astype(BF16)

    @pl.when(active)
    def _():
        lo, hi = _unpack_rows(_load_rows(xs_ref, MOE_BLOCK))
        xb = jnp.concatenate([lo.astype(BF16), hi.astype(BF16)], axis=1)
        gate = jnp.dot(xb, wg_s[...], preferred_element_type=F32)
        up = jnp.dot(xb, wu_s[...], preferred_element_type=F32)
        hid = (_silu(gate) * up).astype(BF16)
        _store_rows(y_ref, _pack_rows(jnp.dot(hid, wd_s[...], preferred_element_type=F32)))

    @pl.when(jnp.logical_not(active))
    def _():
        y_ref[...] = jnp.zeros_like(y_ref)


def _experts(xs, block_e, block_pos, expert_list, meta, w_gate, w_up, w_down, layer):
    blk_rows = MOE_BLOCK * ROW_SUB
    n_blocks = xs.shape[0] // blk_rows
    d = D_MODEL

    def xmap(i, be, bpos, elist, meta):
        return (jnp.maximum(jnp.minimum(i, meta[0] - 1), 0), 0)

    return pl.pallas_call(
        functools.partial(_expert_kernel, layer=layer),
        grid_spec=pltpu.PrefetchScalarGridSpec(
            num_scalar_prefetch=4,
            grid=(n_blocks,),
            in_specs=[
                pl.BlockSpec((blk_rows, LANES), xmap),
                pl.BlockSpec(memory_space=pl.ANY),
                pl.BlockSpec(memory_space=pl.ANY),
                pl.BlockSpec(memory_space=pl.ANY),
            ],
            out_specs=pl.BlockSpec((blk_rows, LANES), lambda i, be, bpos, elist, meta: (i, 0)),
            scratch_shapes=[
                pltpu.VMEM((d, EXPERT_FF), BF16),
                pltpu.VMEM((d, EXPERT_FF), BF16),
                pltpu.VMEM((EXPERT_FF, d), BF16),
                pltpu.VMEM((2, d, EXPERT_FF), F32),
                pltpu.VMEM((2, d, EXPERT_FF), F32),
                pltpu.VMEM((2, EXPERT_FF, d), F32),
                pltpu.SemaphoreType.DMA((2, 3)),
            ],
        ),
        out_shape=jax.ShapeDtypeStruct(xs.shape, U32),
        compiler_params=_cparams(("arbitrary",)),
        name="moe_experts",
    )(block_e, block_pos, expert_list, meta, xs, w_gate, w_up, w_down)


def _combine_kernel(dest_ref, x_ref, gates_ref, y_ref, *rest, final_norm):
    if final_norm:
        g_ref, o_ref, buf0, buf1, sem = rest
    else:
        o_ref, buf0, buf1, sem = rest
    tb = x_ref.shape[0]
    bufs = (buf0, buf1)

    for r in range(tb):
        for k in range(2):
            d = dest_ref[0, 0, 2 * r + k]
            pltpu.make_async_copy(_row_slice(y_ref, d), _row_slice(bufs[k], r), sem).start(priority=k)
    for k in range(2):
        pltpu.make_async_copy(y_ref.at[pl.ds(0, tb * ROW_SUB)], bufs[k], sem).wait()

    gt = gates_ref[...]
    lo0, hi0 = _unpack_rows(_load_rows(buf0, tb))
    lo1, hi1 = _unpack_rows(_load_rows(buf1, tb))
    w0 = gt[:, 0:1]
    w1 = gt[:, 1:2]
    out = x_ref[...] + jnp.concatenate([lo0 * w0 + lo1 * w1, hi0 * w0 + hi1 * w1], axis=1)
    if final_norm:
        out = _rms(out, g_ref[...])
    o_ref[...] = out


def _combine(x, gates, y, dest, final_w=None, *, tb=256):
    t, d = x.shape
    nblk = t // tb
    final_norm = final_w is not None
    in_specs = [
        pl.BlockSpec((1, 1, 2 * tb), lambda i: (i, 0, 0), memory_space=pltpu.SMEM),
        pl.BlockSpec((tb, d), lambda i: (i, 0)),
        pl.BlockSpec((tb, LANES), lambda i: (i, 0)),
        pl.BlockSpec(memory_space=pl.ANY),
    ]
    args = [dest.reshape(nblk, 1, 2 * tb), x, gates, y]
    if final_norm:
        in_specs.append(pl.BlockSpec((1, d), lambda i: (0, 0)))
        args.append(final_w.reshape(1, d))
    return pl.pallas_call(
        functools.partial(_combine_kernel, final_norm=final_norm),
        grid=(nblk,),
        in_specs=in_specs,
        out_specs=pl.BlockSpec((tb, d), lambda i: (i, 0)),
        out_shape=jax.ShapeDtypeStruct((t, d), F32),
        scratch_shapes=[pltpu.VMEM((tb * ROW_SUB, LANES), U32), pltpu.VMEM((tb * ROW_SUB, LANES), U32),
                        pltpu.SemaphoreType.DMA(())],
        compiler_params=_cparams(("arbitrary",)),
        name="moe_combine",
    )(*args)


def _moe(x, ln_w, w_group, b_group, w_expert, b_expert, w_gate, w_up, w_down, layer, final_w=None):
    t = x.shape[0]
    n_assign = t * 2
    n_blocks = -(-n_assign // MOE_BLOCK) + N_EXPERTS
    n_slots = n_blocks * MOE_BLOCK

    ids, gates, cnt = _router(x, ln_w, w_group, b_group, w_expert, b_expert)
    counts = cnt[0, :N_EXPERTS].astype(jnp.int32)
    padded = (counts + MOE_BLOCK - 1) // MOE_BLOCK * MOE_BLOCK
    p_ends = jnp.cumsum(padded)
    p_starts = p_ends - padded
    experts = jnp.arange(N_EXPERTS, dtype=jnp.int32)
    start_of = jnp.sum(jnp.where(ids[:, 0:2, None] == experts, p_starts, 0), axis=-1)
    dest = (start_of + ids[:, 2:4]).astype(jnp.int32)
    block_start = jnp.arange(n_blocks, dtype=jnp.int32) * MOE_BLOCK
    block_e = jnp.minimum(jnp.sum((p_ends[None, :] <= block_start[:, None]).astype(jnp.int32), axis=1),
                          N_EXPERTS - 1)
    n_active = (p_ends[-1:] // MOE_BLOCK).astype(jnp.int32)
    valid_end = jnp.sum(jnp.where(block_e[:, None] == experts, p_starts + counts, 0), axis=-1)
    zero_flag = jnp.logical_or(block_start + MOE_BLOCK > valid_end,
                               block_start >= p_ends[-1]).astype(jnp.int32)

    has_tokens = counts > 0
    list_pos = jnp.cumsum(has_tokens.astype(jnp.int32)) - 1
    expert_list = jnp.sum(jnp.where(jnp.logical_and(has_tokens[None, :], list_pos[None, :] == experts[:, None]),
                                    experts[None, :], 0), axis=1).astype(jnp.int32)
    block_pos = jnp.sum(jnp.where(block_e[:, None] == experts, list_pos, 0), axis=-1).astype(jnp.int32)
    meta = jnp.concatenate([n_active, jnp.sum(has_tokens.astype(jnp.int32), keepdims=True)]).astype(jnp.int32)

    xs = _dispatch(x, ln_w, dest, zero_flag, n_slots)
    y = _experts(xs, block_e, block_pos, expert_list, meta, w_gate, w_up, w_down, layer)
    return _combine(x, gates, y, dest, final_w)


def kernel(x, rel_bias, ln_mix, ln_ffn, ln_final, attn_w_qkv, attn_sinks, attn_w_o, ssm_w_in,
           ssm_conv_w, ssm_conv_b, ssm_dt_bias, ssm_a_log, ssm_d, ssm_norm_w, ssm_w_out,
           moe_w_group, moe_b_group, moe_w_expert, moe_b_expert, moe_w_gate, moe_w_up, moe_w_down):
    batch, seq, d = x.shape
    t = batch * seq
    xf = x.reshape(t, d).astype(F32)

    qkv = _norm_matmul(xf, ln_mix[0], attn_w_qkv[0].astype(BF16), tm=512, tn=QKV_DIM)
    att = _attention(qkv, attn_sinks[0], rel_bias, batch, seq)
    xf = _matmul_residual(att, attn_w_o[0].astype(BF16), xf)
    xf = _moe(xf, ln_ffn[0], moe_w_group[0], moe_b_group[0], moe_w_expert[0], moe_b_expert[0],
              moe_w_gate, moe_w_up, moe_w_down, 0)

    w_in = ssm_w_in[0]
    w_dt = jnp.pad(w_in[:, ZXBC_DIM:], ((0, 0), (0, LANES - SSM_HEADS))).astype(BF16)
    zxbc, dt_raw = _norm_matmul(xf, ln_mix[1], w_in[:, :ZXBC_DIM].astype(BF16), w_dt, tm=512, tn=ZXBC_DIM)
    gated = _ssd(zxbc, dt_raw, ssm_conv_w[0], ssm_conv_b[0], ssm_dt_bias[0], ssm_a_log[0],
                 ssm_d[0], ssm_norm_w[0], batch, seq)
    xf = _matmul_residual(gated, ssm_w_out[0].astype(BF16), xf)
    xf = _moe(xf, ln_ffn[1], moe_w_group[1], moe_b_group[1], moe_w_expert[1], moe_b_expert[1],
              moe_w_gate, moe_w_up, moe_w_down, 1, final_w=ln_final)
    return xf.reshape(batch, seq, d).astype(x.dtype)
```

```python
import functools
import math

import jax
import jax.numpy as jnp
from jax import lax
from jax.experimental import pallas as pl
from jax.experimental.pallas import tpu as pltpu

F32 = jnp.float32
BF16 = jnp.bfloat16

D_MODEL = 1024
N_HEADS = 16
N_KV_HEADS = 4
HEAD_DIM = 64
GQA_GROUP = N_HEADS // N_KV_HEADS
WINDOW = 128
ATTN_BLOCK = 128
ATTN_STEP_BLOCKS = 1
ATTN_ROWS = 128
Q_DIM = N_HEADS * HEAD_DIM
KV_DIM = N_KV_HEADS * HEAD_DIM
QKV_DIM = Q_DIM + 2 * KV_DIM
REL_BUCKETS = 32
REL_MAX_DIST = 128

D_INNER = 2048
SSM_HEAD_DIM = 64
SSM_HEADS = D_INNER // SSM_HEAD_DIM
SSM_GROUPS = 4
D_STATE = 128
CONV_WIDTH = 4
BC_DIM = 2 * SSM_GROUPS * D_STATE
CONV_DIM = D_INNER + BC_DIM
ZXBC_DIM = D_INNER + CONV_DIM
SSM_CHUNK = 128
SSD_STEP_CHUNKS = 4
GROUP_CH = D_INNER // SSM_GROUPS

N_EXPERT_GROUPS = 8
EXPERTS_PER_GROUP = 8
N_EXPERTS = 64
EXPERT_FF = 512
MOE_BLOCK = 256

NORM_EPS = 1e-6
LOG2E = 1.4426950408889634
CONV_TAIL = 16
LANES = 128
NEG_BIG = -1e30
VMEM_LIMIT = 56 * 1024 * 1024


def _cparams(sem, flags=None):
    return pltpu.CompilerParams(dimension_semantics=sem, vmem_limit_bytes=VMEM_LIMIT, flags=flags)


def _rms(x, g):
    ms = jnp.mean(x * x, axis=-1, keepdims=True)
    return x * lax.rsqrt(ms + NORM_EPS) * g


def _silu(x):
    h = 0.5 * x
    return h + h * jnp.tanh(h)


def _norm_matmul_kernel(x_ref, g_ref, w_ref, *rest, has_aux):
    if has_aux:
        wa_ref, o_ref, oa_ref, h_scr = rest
    else:
        o_ref, h_scr = rest

    @pl.when(pl.program_id(1) == 0)
    def _():
        h = _rms(x_ref[...], g_ref[...]).astype(BF16)
        h_scr[...] = h
        if has_aux:
            oa_ref[...] = jnp.dot(h, wa_ref[...], preferred_element_type=F32)

    o_ref[...] = jnp.dot(h_scr[...], w_ref[...], preferred_element_type=F32).astype(o_ref.dtype)


def _norm_matmul(x, g, w, w_aux=None, *, tm=1024, tn=512):
    t, d = x.shape
    n = w.shape[1]
    has_aux = w_aux is not None
    in_specs = [
        pl.BlockSpec((tm, d), lambda i, j: (i, 0)),
        pl.BlockSpec((1, d), lambda i, j: (0, 0)),
        pl.BlockSpec((d, tn), lambda i, j: (0, j)),
    ]
    out_shape = [jax.ShapeDtypeStruct((t, n), BF16)]
    out_specs = [pl.BlockSpec((tm, tn), lambda i, j: (i, j))]
    args = [x, g.reshape(1, d), w]
    if has_aux:
        na = w_aux.shape[1]
        in_specs.append(pl.BlockSpec((d, na), lambda i, j: (0, 0)))
        out_shape.append(jax.ShapeDtypeStruct((t, na), F32))
        out_specs.append(pl.BlockSpec((tm, na), lambda i, j: (i, 0)))
        args.append(w_aux)
    res = pl.pallas_call(
        functools.partial(_norm_matmul_kernel, has_aux=has_aux),
        grid=(t // tm, n // tn),
        in_specs=in_specs,
        out_specs=out_specs,
        out_shape=out_shape,
        scratch_shapes=[pltpu.VMEM((tm, d), BF16)],
        compiler_params=_cparams(("parallel", "arbitrary")),
        name="norm_matmul",
    )(*args)
    return res if has_aux else res[0]


def _matmul_residual_kernel(a_ref, w_ref, r_ref, o_ref):
    o_ref[...] = r_ref[...] + jnp.dot(a_ref[...], w_ref[...], preferred_element_type=F32)


def _matmul_residual(a, w, res, *, tm=512):
    t, k = a.shape
    n = w.shape[1]
    return pl.pallas_call(
        _matmul_residual_kernel,
        grid=(t // tm,),
        in_specs=[
            pl.BlockSpec((tm, k), lambda i: (i, 0)),
            pl.BlockSpec((k, n), lambda i: (0, 0)),
            pl.BlockSpec((tm, n), lambda i: (i, 0)),
        ],
        out_specs=pl.BlockSpec((tm, n), lambda i: (i, 0)),
        out_shape=jax.ShapeDtypeStruct((t, n), F32),
        compiler_params=_cparams(("parallel",)),
        name="matmul_residual",
    )(a, w, res)


def _attn_kernel(sinks_ref, q_ref, kp_ref, kc_ref, vp_ref, vc_ref, bias_ref, o_ref):
    first = pl.program_id(1) == 0
    col = lax.broadcasted_iota(jnp.int32, (ATTN_ROWS, 2 * ATTN_BLOCK), 1)
    hide_prev = jnp.logical_and(first, col < ATTN_BLOCK)
    scale = HEAD_DIM ** -0.5
    B = ATTN_BLOCK
    for j in range(q_ref.shape[0] // B):
        rows = slice(j * B, (j + 1) * B)
        for h in range(N_KV_HEADS):
            ks = slice(h * HEAD_DIM, (h + 1) * HEAD_DIM)
            if j == 0:
                kb = jnp.concatenate([kp_ref[:, ks], kc_ref[0:B, ks]], axis=0)
                vb = jnp.concatenate([vp_ref[:, ks], vc_ref[0:B, ks]], axis=0)
            else:
                kb = kc_ref[(j - 1) * B:(j + 1) * B, ks]
                vb = vc_ref[(j - 1) * B:(j + 1) * B, ks]
            for g in range(GQA_GROUP):
                hh = h * GQA_GROUP + g
                for part in range(B // ATTN_ROWS):
                    r0 = part * ATTN_ROWS
                    rs = slice(j * B + r0, j * B + r0 + ATTN_ROWS)
                    qh = q_ref[rs, hh * HEAD_DIM:(hh + 1) * HEAD_DIM]
                    s = lax.dot_general(qh, kb, (((1,), (1,)), ((), ())),
                                        preferred_element_type=F32) * scale
                    logits = s + bias_ref[hh, r0:r0 + ATTN_ROWS, :]
                    if j == 0:
                        logits = jnp.where(hide_prev, NEG_BIG, logits)
                    sink = sinks_ref[hh]
                    m = jnp.maximum(jnp.max(logits, axis=-1, keepdims=True), sink)
                    p = jnp.exp(logits - m)
                    denom = jnp.sum(p, axis=-1, keepdims=True) + jnp.exp(sink - m)
                    o = jnp.dot(p.astype(BF16), vb, preferred_element_type=F32) / denom
                    o_ref[rs, hh * HEAD_DIM:(hh + 1) * HEAD_DIM] = o.astype(o_ref.dtype)


def _t5_causal_bucket(dist):
    n = jnp.maximum(dist, 0)
    max_exact = REL_BUCKETS // 2
    nf = jnp.maximum(n, 1).astype(F32)
    large = max_exact + (jnp.log(nf / max_exact) / math.log(REL_MAX_DIST / max_exact)
                         * (REL_BUCKETS - max_exact)).astype(jnp.int32)
    large = jnp.minimum(large, REL_BUCKETS - 1)
    return jnp.where(n < max_exact, n, large)


def _attention(qkv, sinks, rel_bias, batch, seq):
    t = qkv.shape[0]
    nb = seq // ATTN_BLOCK
    qi = jnp.arange(ATTN_BLOCK)[:, None]
    ki = jnp.arange(2 * ATTN_BLOCK)[None, :]
    dist = qi + ATTN_BLOCK - ki
    in_window = (dist >= 0) & (dist < WINDOW)
    onehot = (_t5_causal_bucket(dist)[None] == jnp.arange(REL_BUCKETS)[:, None, None]).astype(F32)
    bias = jnp.einsum('hr,rqk->hqk', rel_bias.astype(F32).T, onehot, precision=lax.Precision.HIGHEST)
    bias = jnp.where(in_window[None], bias, NEG_BIG)

    kcol = Q_DIM // KV_DIM
    vcol = kcol + 1

    spb = ATTN_STEP_BLOCKS
    ns = nb // spb
    rows = spb * ATTN_BLOCK

    def prev(b, n):
        return b * nb + jnp.maximum(n * spb - 1, 0)

    return pl.pallas_call(
        _attn_kernel,
        grid=(batch, ns),
        in_specs=[
            pl.BlockSpec(memory_space=pltpu.SMEM),
            pl.BlockSpec((rows, Q_DIM), lambda b, n: (b * ns + n, 0)),
            pl.BlockSpec((ATTN_BLOCK, KV_DIM), lambda b, n: (prev(b, n), kcol)),
            pl.BlockSpec((rows, KV_DIM), lambda b, n: (b * ns + n, kcol)),
            pl.BlockSpec((ATTN_BLOCK, KV_DIM), lambda b, n: (prev(b, n), vcol)),
            pl.BlockSpec((rows, KV_DIM), lambda b, n: (b * ns + n, vcol)),
            pl.BlockSpec((N_HEADS, ATTN_BLOCK, 2 * ATTN_BLOCK), lambda b, n: (0, 0, 0)),
        ],
        out_specs=pl.BlockSpec((rows, Q_DIM), lambda b, n: (b * ns + n, 0)),
        out_shape=jax.ShapeDtypeStruct((t, Q_DIM), BF16),
        compiler_params=_cparams(("parallel", "arbitrary")),
        name="swa_attention",
    )(sinks.astype(F32), qkv, qkv, qkv, qkv, qkv, bias)


def _split3(v):
    hi = v.astype(BF16)
    r = v - hi.astype(F32)
    mid = r.astype(BF16)
    lo = (r - mid.astype(F32)).astype(BF16)
    return hi, mid, lo


def _ssd_kernel(z_ref, x_ref, bc_ref, dt_ref, convw_ref, convb_ref, dtb_ref, alog_ref,
                dskip_ref, normw_ref, shift_ref, shift_tail_ref, expand_ref,
                o_ref, state, xtail, bctail):
    @pl.when(pl.program_id(1) == 0)
    def _():
        state[...] = jnp.zeros_like(state)
        xtail[...] = jnp.zeros_like(xtail)
        bctail[...] = jnp.zeros_like(bctail)

    for k in range(z_ref.shape[0] // SSM_CHUNK):
        _ssd_chunk(pl.ds(k * SSM_CHUNK, SSM_CHUNK), z_ref, x_ref, bc_ref, dt_ref, convw_ref, convb_ref,
                   dtb_ref, alog_ref, dskip_ref, normw_ref, shift_ref, shift_tail_ref, expand_ref,
                   o_ref, state, xtail, bctail)


def _ssd_chunk(rows, z_ref, x_ref, bc_ref, dt_ref, convw_ref, convb_ref, dtb_ref, alog_ref,
               dskip_ref, normw_ref, shift_ref, shift_tail_ref, expand_ref,
               o_ref, state, xtail, bctail):
    L = SSM_CHUNK

    def conv_silu(u_ref, tail, w_lo, width):
        cur = u_ref[rows, :]
        prev = tail[...]
        w = [convw_ref[k:k + 1, w_lo:w_lo + width].astype(BF16) for k in range(CONV_WIDTH)]
        taps = jnp.concatenate([cur * w[k] for k in range(CONV_WIDTH)], axis=0)
        taps_prev = jnp.concatenate([prev * w[k] for k in range(CONV_WIDTH)], axis=0)
        y = (jnp.dot(shift_ref[...], taps, preferred_element_type=F32)
             + jnp.dot(shift_tail_ref[...], taps_prev, preferred_element_type=F32)
             + convb_ref[:, w_lo:w_lo + width])
        tail[...] = cur[L - CONV_TAIL:L]
        return _silu(y)

    xs = conv_silu(x_ref, xtail, 0, D_INNER)
    bcm = conv_silu(bc_ref, bctail, D_INNER, BC_DIM)

    xdt = dt_ref[rows, :] + dtb_ref[...]
    dt = jnp.maximum(xdt, 0.0) + jnp.log1p(jnp.exp(-jnp.abs(xdt)))
    a = dt * (-jnp.exp(alog_ref[...]) * LOG2E)
    row = lax.broadcasted_iota(jnp.int32, (L, L), 0)
    colm = lax.broadcasted_iota(jnp.int32, (L, L), 1)
    causal = row >= colm
    cs3 = jnp.dot(causal.astype(BF16), jnp.concatenate(_split3(a), axis=1), preferred_element_type=F32)
    a_cs = cs3[:, :LANES] + cs3[:, LANES:2 * LANES] + cs3[:, 2 * LANES:]
    a_cs_t = a_cs.T
    dt_t = dt.T
    a_last_col = a_cs_t[:, L - 1:L]
    w_end_t = jnp.exp2(a_last_col - a_cs_t) * dt_t
    exp_acs = jnp.exp2(a_cs)

    cd3 = [jnp.broadcast_to(part.astype(F32), (16, LANES)).astype(BF16)
           for part in _split3(exp_acs[L - 1:L, :])]
    cdx = jnp.dot(jnp.concatenate(cd3, axis=0), expand_ref[...], preferred_element_type=F32)
    cd_exp = cdx[0:1] + cdx[16:17] + cdx[32:33]

    lane = lax.broadcasted_iota(jnp.int32, (L, LANES), 1)
    low_half = lane < SSM_HEAD_DIM

    y_tiles = []
    for g in range(SSM_GROUPS):
        b_g = bcm[:, g * D_STATE:(g + 1) * D_STATE]
        c_g = bcm[:, (SSM_GROUPS + g) * D_STATE:(SSM_GROUPS + g + 1) * D_STATE]
        b_gb = b_g.astype(BF16)
        c_gb = c_g.astype(BF16)
        cb = lax.dot_general(c_gb, b_gb, (((1,), (1,)), ((), ())), preferred_element_type=F32)
        b_gt = b_g.T
        s_g = state[:, g * GROUP_CH:(g + 1) * GROUP_CH]
        y_off_g = jnp.dot(c_gb, s_g.astype(BF16), preferred_element_type=F32)
        for jj in range(GROUP_CH // LANES):
            j = g * (GROUP_CH // LANES) + jj
            lhs_top = []
            lhs_bot = []
            escale = []
            for e in (2 * j, 2 * j + 1):
                acs_b = jnp.broadcast_to(a_cs[:, e:e + 1], (L, L))
                seg = acs_b - a_cs_t[e:e + 1, :]
                dec = jnp.exp2(jnp.where(causal, seg, -jnp.inf))
                lhs_top.append((cb * dec * dt_t[e:e + 1, :]).astype(BF16))
                lhs_bot.append((b_gt * w_end_t[e:e + 1, :]).astype(BF16))
                escale.append(jnp.broadcast_to(exp_acs[:, e:e + 1], (L, LANES)))
            x_tile = xs[:, j * LANES:(j + 1) * LANES]
            x_lo = jnp.where(low_half, x_tile, 0.0).astype(BF16)
            x_hi = jnp.where(low_half, 0.0, x_tile).astype(BF16)
            lhs = jnp.concatenate([jnp.concatenate(lhs_top, axis=1),
                                   jnp.concatenate(lhs_bot, axis=1)], axis=0)
            rhs = jnp.concatenate([x_lo, x_hi], axis=0)
            r = jnp.dot(lhs, rhs, preferred_element_type=F32)
            y_off = y_off_g[:, jj * LANES:(jj + 1) * LANES]
            y_tiles.append(r[0:L] + jnp.where(low_half, escale[0], escale[1]) * y_off)
            sl = slice(j * LANES, (j + 1) * LANES)
            state[:, sl] = cd_exp[:, sl] * state[:, sl] + r[L:2 * L]

    y = jnp.concatenate(y_tiles, axis=1)
    y = y + xs * dskip_ref[...]
    gated = y * _silu(z_ref[rows, :].astype(F32))
    outs = []
    for g in range(SSM_GROUPS):
        gg = gated[:, g * GROUP_CH:(g + 1) * GROUP_CH]
        ms = jnp.mean(gg * gg, axis=-1, keepdims=True)
        outs.append(gg * lax.rsqrt(ms + NORM_EPS))
    o_ref[rows, :] = (jnp.concatenate(outs, axis=1) * normw_ref[...]).astype(o_ref.dtype)


def _ssd(zxbc, dt_raw, conv_w, conv_b, dt_bias, a_log, d_skip, norm_w, batch, seq):
    t = zxbc.shape[0]
    L = SSM_CHUNK
    rows = SSD_STEP_CHUNKS * L
    nc = seq // rows

    def pad_heads(v):
        return jnp.pad(v.astype(F32), (0, LANES - SSM_HEADS)).reshape(1, LANES)

    def rowmap(col):
        return lambda b, c: (b * nc + c, col)

    def const2(b, c):
        return (0, 0)

    l_idx = jnp.arange(L)[:, None]
    shift = jnp.concatenate(
        [(jnp.arange(L)[None, :] == l_idx - (CONV_WIDTH - 1) + k) for k in range(CONV_WIDTH)],
        axis=1).astype(BF16)
    shift_tail = jnp.concatenate(
        [(jnp.arange(-CONV_TAIL, 0)[None, :] == l_idx - (CONV_WIDTH - 1) + k) for k in range(CONV_WIDTH)],
        axis=1).astype(BF16)
    expand = (jnp.arange(D_INNER)[None, :] // SSM_HEAD_DIM == jnp.arange(LANES)[:, None]).astype(BF16)

    return pl.pallas_call(
        _ssd_kernel,
        grid=(batch, nc),
        in_specs=[
            pl.BlockSpec((rows, D_INNER), rowmap(0)),
            pl.BlockSpec((rows, D_INNER), rowmap(1)),
            pl.BlockSpec((rows, BC_DIM), rowmap(2 * D_INNER // BC_DIM)),
            pl.BlockSpec((rows, LANES), rowmap(0)),
            pl.BlockSpec((CONV_WIDTH, CONV_DIM), const2),
            pl.BlockSpec((1, CONV_DIM), const2),
            pl.BlockSpec((1, LANES), const2),
            pl.BlockSpec((1, LANES), const2),
            pl.BlockSpec((1, D_INNER), const2),
            pl.BlockSpec((1, D_INNER), const2),
            pl.BlockSpec((L, CONV_WIDTH * L), const2),
            pl.BlockSpec((L, CONV_WIDTH * CONV_TAIL), const2),
            pl.BlockSpec((LANES, D_INNER), const2),
        ],
        out_specs=pl.BlockSpec((rows, D_INNER), rowmap(0)),
        out_shape=jax.ShapeDtypeStruct((t, D_INNER), BF16),
        scratch_shapes=[
            pltpu.VMEM((D_STATE, D_INNER), F32),
            pltpu.VMEM((CONV_TAIL, D_INNER), BF16),
            pltpu.VMEM((CONV_TAIL, BC_DIM), BF16),
        ],
        compiler_params=_cparams(("parallel", "arbitrary")),
        name="ssd_core",
    )(zxbc, zxbc, zxbc, dt_raw, conv_w.astype(F32), conv_b.astype(F32).reshape(1, CONV_DIM),
      pad_heads(dt_bias), pad_heads(a_log),
      jnp.repeat(d_skip.astype(F32), SSM_HEAD_DIM).reshape(1, D_INNER),
      norm_w.astype(F32).reshape(1, D_INNER), shift, shift_tail, expand)


GROUP_ROW0 = N_EXPERTS


def _router_kernel(x_ref, g_ref, w_ref, b_ref, earlier_ref, ids_ref, gates_ref, cnt_ref, carry, wt_split):
    i = pl.program_id(0)

    @pl.when(i == 0)
    def _():
        carry[...] = jnp.zeros_like(carry)
        wt = w_ref[...].T
        wt_hi = wt.astype(BF16)
        wt_split[0:LANES, :] = wt_hi
        wt_split[LANES:, :] = (wt - wt_hi.astype(F32)).astype(BF16)

    h = _rms(x_ref[...], g_ref[...])
    h_hi = h.astype(BF16)
    h_mid = (h - h_hi.astype(F32)).astype(BF16)
    nt = (((1,), (1,)), ((), ()))
    top = lax.dot_general(wt_split[...], h_hi, nt, preferred_element_type=F32)
    cross = lax.dot_general(wt_split[0:LANES, :], h_mid, nt, preferred_element_type=F32)
    logits = top[:LANES] + (top[LANES:] + cross) + b_ref[...]
    tm = logits.shape[1]
    row = lax.broadcasted_iota(jnp.int32, (LANES, tm), 0)
    ninf = -jnp.inf

    def first_argmax(v, vmax):
        return jnp.min(jnp.where(v == vmax, row, LANES), axis=0, keepdims=True)

    is_group = (row >= GROUP_ROW0) & (row < GROUP_ROW0 + N_EXPERT_GROUPS)
    glog = jnp.where(is_group, logits, ninf)
    gmax = jnp.max(glog, axis=0, keepdims=True)
    g_idx = first_argmax(glog, gmax) - GROUP_ROW0
    g_p = 1.0 / jnp.sum(jnp.exp(glog - gmax), axis=0, keepdims=True)

    in_group = (row < N_EXPERTS) & (jnp.right_shift(row, 3) == g_idx)
    elog = jnp.where(in_group, logits, ninf)
    m1 = jnp.max(elog, axis=0, keepdims=True)
    e1 = first_argmax(elog, m1)
    elog2 = jnp.where(row == e1, ninf, elog)
    m2 = jnp.max(elog2, axis=0, keepdims=True)
    e2 = first_argmax(elog2, m2)
    zsum = jnp.sum(jnp.exp(elog - m1), axis=0, keepdims=True)
    p1 = 1.0 / zsum
    p2 = jnp.exp(m2 - m1) / zsum
    psum = p1 + p2
    w1 = p1 / psum * g_p
    w2 = p2 / psum * g_p

    oh1 = row == e1
    oh2 = row == e2
    onehot = jnp.logical_or(oh1, oh2).astype(F32)
    before = jnp.dot(onehot.astype(BF16), earlier_ref[...], preferred_element_type=F32) + carry[:, 0:1]
    r1 = jnp.sum(jnp.where(oh1, before, 0.0), axis=0, keepdims=True).astype(jnp.int32)
    r2 = jnp.sum(jnp.where(oh2, before, 0.0), axis=0, keepdims=True).astype(jnp.int32)
    carry[...] = carry[...] + jnp.sum(onehot, axis=1, keepdims=True)

    row8 = lax.broadcasted_iota(jnp.int32, (8, tm), 0)
    ids_ref[...] = jnp.where(row8 == 0, e1, jnp.where(row8 == 1, e2,
                             jnp.where(row8 == 2, r1, jnp.where(row8 == 3, r2, 0))))
    gates_ref[...] = jnp.where(row == 0, w1, jnp.where(row == 1, w2, 0.0)).T
    cnt_ref[...] = carry[...]


def _router(x, ln_w, w_group, b_group, w_expert, b_expert, *, tm=512):
    t, d = x.shape
    pad = LANES - N_EXPERTS - N_EXPERT_GROUPS
    w_r = jnp.concatenate([w_expert, w_group, jnp.zeros((d, pad), F32)], axis=1).astype(F32)
    b_r = jnp.concatenate([b_expert, b_group, jnp.zeros((pad,), F32)]).astype(F32).reshape(LANES, 1)
    earlier = (jnp.arange(tm)[:, None] < jnp.arange(tm)[None, :]).astype(BF16)
    return pl.pallas_call(
        _router_kernel,
        grid=(t // tm,),
        in_specs=[
            pl.BlockSpec((tm, d), lambda i: (i, 0)),
            pl.BlockSpec((1, d), lambda i: (0, 0)),
            pl.BlockSpec((d, LANES), lambda i: (0, 0)),
            pl.BlockSpec((LANES, 1), lambda i: (0, 0)),
            pl.BlockSpec((tm, tm), lambda i: (0, 0)),
        ],
        out_specs=[
            pl.BlockSpec((8, tm), lambda i: (0, i)),
            pl.BlockSpec((tm, LANES), lambda i: (i, 0)),
            pl.BlockSpec((LANES, LANES), lambda i: (0, 0)),
        ],
        out_shape=[
            jax.ShapeDtypeStruct((8, t), jnp.int32),
            jax.ShapeDtypeStruct((t, LANES), F32),
            jax.ShapeDtypeStruct((LANES, LANES), F32),
        ],
        scratch_shapes=[pltpu.VMEM((LANES, LANES), F32), pltpu.VMEM((2 * LANES, d), BF16)],
        compiler_params=_cparams(("arbitrary",)),
        name="moe_router",
    )(x, ln_w.reshape(1, d), w_r, b_r, earlier)


U32 = jnp.uint32
ROW_WORDS = D_MODEL // 2
ROW_SUB = ROW_WORDS // LANES
HIGH_HALF = 0xFFFF0000
ROW_MOVE_TOKENS = 512


def _pack_rows(v):
    lo = lax.bitcast_convert_type(v[:, :ROW_WORDS].astype(BF16).astype(F32), U32)
    hi = lax.bitcast_convert_type(v[:, ROW_WORDS:].astype(BF16).astype(F32), U32)
    return jnp.right_shift(lo, jnp.uint32(16)) | (hi & jnp.uint32(HIGH_HALF))


def _unpack_rows(w):
    lo = lax.bitcast_convert_type(jnp.left_shift(w, jnp.uint32(16)), F32)
    hi = lax.bitcast_convert_type(w & jnp.uint32(HIGH_HALF), F32)
    return lo, hi


def _store_rows(ref, packed):
    n = packed.shape[0]
    for c in range(ROW_SUB):
        ref[pl.ds(c, n, stride=ROW_SUB), :] = packed[:, c * LANES:(c + 1) * LANES]


def _load_rows(ref, n):
    return jnp.concatenate([ref[pl.ds(c, n, stride=ROW_SUB), :] for c in range(ROW_SUB)], axis=1)


def _row_slice(ref, row):
    return ref.at[pl.ds(pl.multiple_of(row * ROW_SUB, ROW_SUB), ROW_SUB)]


def _dispatch_kernel(zf_ref, dest_ref, x_ref, g_ref, xs_ref, hbuf, zbuf, sem, zsem):
    tb = x_ref.shape[0]
    n_blocks = zf_ref.shape[0]
    blk_rows = MOE_BLOCK * ROW_SUB

    @pl.when(pl.program_id(0) == 0)
    def _():
        zbuf[...] = jnp.zeros_like(zbuf)

        def zero_copy(b):
            start = pl.multiple_of(b * blk_rows, blk_rows)
            return pltpu.make_async_copy(zbuf, xs_ref.at[pl.ds(start, blk_rows)], zsem)

        def zissue(b, carry):
            @pl.when(zf_ref[b] != 0)
            def _():
                zero_copy(b).start()
            return carry

        def zdrain(b, carry):
            @pl.when(zf_ref[b] != 0)
            def _():
                zero_copy(b).wait()
            return carry

        lax.fori_loop(0, n_blocks, zissue, 0)
        lax.fori_loop(0, n_blocks, zdrain, 0)

    _store_rows(hbuf, _pack_rows(_rms(x_ref[...], g_ref[...])))
    for r in range(tb):
        for k in range(2):
            d = dest_ref[0, 0, 2 * r + k]
            pltpu.make_async_copy(_row_slice(hbuf, r), _row_slice(xs_ref, d), sem).start(priority=k)
    for k in range(2):
        pltpu.make_async_copy(hbuf, xs_ref.at[pl.ds(0, tb * ROW_SUB)], sem).wait()


def _dispatch(x, ln_w, dest, zero_flag, n_slots, *, tb=ROW_MOVE_TOKENS):
    t, d = x.shape
    nblk = t // tb
    return pl.pallas_call(
        _dispatch_kernel,
        grid_spec=pltpu.PrefetchScalarGridSpec(
            num_scalar_prefetch=1,
            grid=(nblk,),
            in_specs=[
                pl.BlockSpec((1, 1, 2 * tb), lambda i, zf: (i, 0, 0), memory_space=pltpu.SMEM),
                pl.BlockSpec((tb, d), lambda i, zf: (i, 0)),
                pl.BlockSpec((1, d), lambda i, zf: (0, 0)),
            ],
            out_specs=pl.BlockSpec(memory_space=pl.ANY),
            scratch_shapes=[pltpu.VMEM((tb * ROW_SUB, LANES), U32),
                            pltpu.VMEM((MOE_BLOCK * ROW_SUB, LANES), U32),
                            pltpu.SemaphoreType.DMA(()), pltpu.SemaphoreType.DMA(())],
        ),
        out_shape=jax.ShapeDtypeStruct((n_slots * ROW_SUB, LANES), U32),
        compiler_params=_cparams(("arbitrary",)),
        name="moe_dispatch",
    )(zero_flag, dest.reshape(nblk, 1, 2 * tb), x, ln_w.reshape(1, d))


def _expert_kernel(be_ref, bpos_ref, elist_ref, meta_ref, xs_ref, wg_hbm, wu_hbm, wd_hbm, y_ref,
                   wg_s, wu_s, wd_s, wg_f, wu_f, wd_f, sems, *, layer):
    i = pl.program_id(0)
    active = i < meta_ref[0]
    pos = bpos_ref[i]
    changed = jnp.logical_or(i == 0, be_ref[i] != be_ref[jnp.maximum(i - 1, 0)])
    streams = ((wg_hbm, wg_f), (wu_hbm, wu_f), (wd_hbm, wd_f))

    def fetch(p, slot):
        e = elist_ref[p]
        return [pltpu.make_async_copy(hbm.at[layer, e], buf.at[slot], sems.at[slot, t])
                for t, (hbm, buf) in enumerate(streams)]

    @pl.when(jnp.logical_and(active, i == 0))
    def _():
        for cp in fetch(0, 0):
            cp.start()

    @pl.when(jnp.logical_and(active, changed))
    def _():
        slot = lax.rem(pos, 2)
        for cp in fetch(pos, slot):
            cp.wait()

        @pl.when(pos + 1 < meta_ref[1])
        def _():
            for cp in fetch(pos + 1, 1 - slot):
                cp.start()

        wg_s[...] = wg_f[slot].astype(BF16)
        wu_s[...] = wu_f[slot].astype(BF16)
        wd_s[...] = wd_f[slot].astype(BF16)

    @pl.when(active)
    def _():
        lo, hi = _unpack_rows(_load_rows(xs_ref, MOE_BLOCK))
        xb = jnp.concatenate([lo.astype(BF16), hi.astype(BF16)], axis=1)
        gate = jnp.dot(xb, wg_s[...], preferred_element_type=F32)
        up = jnp.dot(xb, wu_s[...], preferred_element_type=F32)
        hid = (_silu(gate) * up).astype(BF16)
        _store_rows(y_ref, _pack_rows(jnp.dot(hid, wd_s[...], preferred_element_type=F32)))

    @pl.when(jnp.logical_not(active))
    def _():
        y_ref[...] = jnp.zeros_like(y_ref)


def _experts(xs, block_e, block_pos, expert_list, meta, w_gate, w_up, w_down, layer):
    blk_rows = MOE_BLOCK * ROW_SUB
    n_blocks = xs.shape[0] // blk_rows
    d = D_MODEL

    def xmap(i, be, bpos, elist, meta):
        return (jnp.maximum(jnp.minimum(i, meta[0] - 1), 0), 0)

    return pl.pallas_call(
        functools.partial(_expert_kernel, layer=layer),
        grid_spec=pltpu.PrefetchScalarGridSpec(
            num_scalar_prefetch=4,
            grid=(n_blocks,),
            in_specs=[
                pl.BlockSpec((blk_rows, LANES), xmap),
                pl.BlockSpec(memory_space=pl.ANY),
                pl.BlockSpec(memory_space=pl.ANY),
                pl.BlockSpec(memory_space=pl.ANY),
            ],
            out_specs=pl.BlockSpec((blk_rows, LANES), lambda i, be, bpos, elist, meta: (i, 0)),
            scratch_shapes=[
                pltpu.VMEM((d, EXPERT_FF), BF16),
                pltpu.VMEM((d, EXPERT_FF), BF16),
                pltpu.VMEM((EXPERT_FF, d), BF16),
                pltpu.VMEM((2, d, EXPERT_FF), F32),
                pltpu.VMEM((2, d, EXPERT_FF), F32),
                pltpu.VMEM((2, EXPERT_FF, d), F32),
                pltpu.SemaphoreType.DMA((2, 3)),
            ],
        ),
        out_shape=jax.ShapeDtypeStruct(xs.shape, U32),
        compiler_params=_cparams(("arbitrary",)),
        name="moe_experts",
    )(block_e, block_pos, expert_list, meta, xs, w_gate, w_up, w_down)


def _combine_kernel(dest_ref, x_ref, gates_ref, y_ref, *rest, final_norm):
    if final_norm:
        g_ref, o_ref, buf0, buf1, sem = rest
    else:
        o_ref, buf0, buf1, sem = rest
    tb = x_ref.shape[0]
    bufs = (buf0, buf1)

    for r in range(tb):
        for k in range(2):
            d = dest_ref[0, 0, 2 * r + k]
            pltpu.make_async_copy(_row_slice(y_ref, d), _row_slice(bufs[k], r), sem).start(priority=k)
    for k in range(2):
        pltpu.make_async_copy(y_ref.at[pl.ds(0, tb * ROW_SUB)], bufs[k], sem).wait()

    gt = gates_ref[...]
    lo0, hi0 = _unpack_rows(_load_rows(buf0, tb))
    lo1, hi1 = _unpack_rows(_load_rows(buf1, tb))
    w0 = gt[:, 0:1]
    w1 = gt[:, 1:2]
    out = x_ref[...] + jnp.concatenate([lo0 * w0 + lo1 * w1, hi0 * w0 + hi1 * w1], axis=1)
    if final_norm:
        out = _rms(out, g_ref[...])
    o_ref[...] = out


def _combine(x, gates, y, dest, final_w=None, *, tb=ROW_MOVE_TOKENS):
    t, d = x.shape
    nblk = t // tb
    final_norm = final_w is not None
    in_specs = [
        pl.BlockSpec((1, 1, 2 * tb), lambda i: (i, 0, 0), memory_space=pltpu.SMEM),
        pl.BlockSpec((tb, d), lambda i: (i, 0)),
        pl.BlockSpec((tb, LANES), lambda i: (i, 0)),
        pl.BlockSpec(memory_space=pl.ANY),
    ]
    args = [dest.reshape(nblk, 1, 2 * tb), x, gates, y]
    if final_norm:
        in_specs.append(pl.BlockSpec((1, d), lambda i: (0, 0)))
        args.append(final_w.reshape(1, d))
    return pl.pallas_call(
        functools.partial(_combine_kernel, final_norm=final_norm),
        grid=(nblk,),
        in_specs=in_specs,
        out_specs=pl.BlockSpec((tb, d), lambda i: (i, 0)),
        out_shape=jax.ShapeDtypeStruct((t, d), F32),
        scratch_shapes=[pltpu.VMEM((tb * ROW_SUB, LANES), U32), pltpu.VMEM((tb * ROW_SUB, LANES), U32),
                        pltpu.SemaphoreType.DMA(())],
        compiler_params=_cparams(("arbitrary",)),
        name="moe_combine",
    )(*args)


def _moe(x, ln_w, w_group, b_group, w_expert, b_expert, w_gate, w_up, w_down, layer, final_w=None):
    t = x.shape[0]
    n_assign = t * 2
    n_blocks = -(-n_assign // MOE_BLOCK) + N_EXPERTS
    n_slots = n_blocks * MOE_BLOCK

    ids, gates, cnt = _router(x, ln_w, w_group, b_group, w_expert, b_expert)
    counts = cnt[:N_EXPERTS, 0].astype(jnp.int32)
    padded = (counts + MOE_BLOCK - 1) // MOE_BLOCK * MOE_BLOCK
    p_ends = jnp.cumsum(padded)
    p_starts = p_ends - padded
    experts = jnp.arange(N_EXPERTS, dtype=jnp.int32)
    start_of = jnp.sum(jnp.where(ids[0:2, :, None] == experts, p_starts, 0), axis=-1)
    dest = (start_of + ids[2:4]).astype(jnp.int32).T
    block_start = jnp.arange(n_blocks, dtype=jnp.int32) * MOE_BLOCK
    block_e = jnp.minimum(jnp.sum((p_ends[None, :] <= block_start[:, None]).astype(jnp.int32), axis=1),
                          N_EXPERTS - 1)
    n_active = (p_ends[-1:] // MOE_BLOCK).astype(jnp.int32)
    valid_end = jnp.sum(jnp.where(block_e[:, None] == experts, p_starts + counts, 0), axis=-1)
    zero_flag = jnp.logical_or(block_start + MOE_BLOCK > valid_end,
                               block_start >= p_ends[-1]).astype(jnp.int32)

    has_tokens = counts > 0
    list_pos = jnp.cumsum(has_tokens.astype(jnp.int32)) - 1
    expert_list = jnp.sum(jnp.where(jnp.logical_and(has_tokens[None, :], list_pos[None, :] == experts[:, None]),
                                    experts[None, :], 0), axis=1).astype(jnp.int32)
    block_pos = jnp.sum(jnp.where(block_e[:, None] == experts, list_pos, 0), axis=-1).astype(jnp.int32)
    meta = jnp.concatenate([n_active, jnp.sum(has_tokens.astype(jnp.int32), keepdims=True)]).astype(jnp.int32)

    xs = _dispatch(x, ln_w, dest, zero_flag, n_slots)
    y = _experts(xs, block_e, block_pos, expert_list, meta, w_gate, w_up, w_down, layer)
    return _combine(x, gates, y, dest, final_w)


def kernel(x, rel_bias, ln_mix, ln_ffn, ln_final, attn_w_qkv, attn_sinks, attn_w_o, ssm_w_in,
           ssm_conv_w, ssm_conv_b, ssm_dt_bias, ssm_a_log, ssm_d, ssm_norm_w, ssm_w_out,
           moe_w_group, moe_b_group, moe_w_expert, moe_b_expert, moe_w_gate, moe_w_up, moe_w_down):
    batch, seq, d = x.shape
    t = batch * seq
    xf = x.reshape(t, d).astype(F32)

    qkv = _norm_matmul(xf, ln_mix[0], attn_w_qkv[0].astype(BF16), tm=512, tn=QKV_DIM)
    att = _attention(qkv, attn_sinks[0], rel_bias, batch, seq)
    xf = _matmul_residual(att, attn_w_o[0].astype(BF16), xf)
    xf = _moe(xf, ln_ffn[0], moe_w_group[0], moe_b_group[0], moe_w_expert[0], moe_b_expert[0],
              moe_w_gate, moe_w_up, moe_w_down, 0)

    w_in = ssm_w_in[0]
    w_dt = jnp.pad(w_in[:, ZXBC_DIM:], ((0, 0), (0, LANES - SSM_HEADS))).astype(BF16)
    zxbc, dt_raw = _norm_matmul(xf, ln_mix[1], w_in[:, :ZXBC_DIM].astype(BF16), w_dt, tm=512, tn=ZXBC_DIM)
    gated = _ssd(zxbc, dt_raw, ssm_conv_w[0], ssm_conv_b[0], ssm_dt_bias[0], ssm_a_log[0],
                 ssm_d[0], ssm_norm_w[0], batch, seq)
    xf = _matmul_residual(gated, ssm_w_out[0].astype(BF16), xf)
    xf = _moe(xf, ln_ffn[1], moe_w_group[1], moe_b_group[1], moe_w_expert[1], moe_b_expert[1],
              moe_w_gate, moe_w_up, moe_w_down, 1, final_w=ln_final)
    return xf.reshape(batch, seq, d).astype(x.dtype)
```

```python
import functools
import math

import jax
import jax.numpy as jnp
from jax import lax
from jax.experimental import pallas as pl
from jax.experimental.pallas import tpu as pltpu

F32 = jnp.float32
BF16 = jnp.bfloat16

D_MODEL = 1024
N_HEADS = 16
N_KV_HEADS = 4
HEAD_DIM = 64
GQA_GROUP = N_HEADS // N_KV_HEADS
WINDOW = 128
ATTN_BLOCK = 128
Q_DIM = N_HEADS * HEAD_DIM
KV_DIM = N_KV_HEADS * HEAD_DIM
QKV_DIM = Q_DIM + 2 * KV_DIM
REL_BUCKETS = 32
REL_MAX_DIST = 128

D_INNER = 2048
SSM_HEAD_DIM = 64
SSM_HEADS = D_INNER // SSM_HEAD_DIM
SSM_GROUPS = 4
D_STATE = 128
CONV_WIDTH = 4
BC_DIM = 2 * SSM_GROUPS * D_STATE
CONV_DIM = D_INNER + BC_DIM
ZXBC_DIM = D_INNER + CONV_DIM
SSM_CHUNK = 128
SSD_STEP_CHUNKS = 4
GROUP_CH = D_INNER // SSM_GROUPS

N_EXPERT_GROUPS = 8
EXPERTS_PER_GROUP = 8
N_EXPERTS = 64
EXPERT_FF = 512
MOE_BLOCK = 256

NORM_EPS = 1e-6
LOG2E = 1.4426950408889634
CONV_TAIL = 16
LANES = 128
NEG_BIG = -1e30
VMEM_LIMIT = 56 * 1024 * 1024


def _cparams(sem, flags=None):
    return pltpu.CompilerParams(dimension_semantics=sem, vmem_limit_bytes=VMEM_LIMIT, flags=flags)


def _rms(x, g):
    ms = jnp.mean(x * x, axis=-1, keepdims=True)
    return x * lax.rsqrt(ms + NORM_EPS) * g


def _silu(x):
    h = 0.5 * x
    return h + h * jnp.tanh(h)


def _norm_matmul_kernel(x_ref, g_ref, w_ref, *rest, has_aux):
    if has_aux:
        wa_ref, o_ref, oa_ref, h_scr = rest
    else:
        o_ref, h_scr = rest

    @pl.when(pl.program_id(1) == 0)
    def _():
        h = _rms(x_ref[...], g_ref[...]).astype(BF16)
        h_scr[...] = h
        if has_aux:
            oa_ref[...] = jnp.dot(h, wa_ref[...], preferred_element_type=F32)

    o_ref[...] = jnp.dot(h_scr[...], w_ref[...], preferred_element_type=F32).astype(o_ref.dtype)


def _norm_matmul(x, g, w, w_aux=None, *, tm=1024, tn=512):
    t, d = x.shape
    n = w.shape[1]
    has_aux = w_aux is not None
    in_specs = [
        pl.BlockSpec((tm, d), lambda i, j: (i, 0)),
        pl.BlockSpec((1, d), lambda i, j: (0, 0)),
        pl.BlockSpec((d, tn), lambda i, j: (0, j)),
    ]
    out_shape = [jax.ShapeDtypeStruct((t, n), BF16)]
    out_specs = [pl.BlockSpec((tm, tn), lambda i, j: (i, j))]
    args = [x, g.reshape(1, d), w]
    if has_aux:
        na = w_aux.shape[1]
        in_specs.append(pl.BlockSpec((d, na), lambda i, j: (0, 0)))
        out_shape.append(jax.ShapeDtypeStruct((t, na), F32))
        out_specs.append(pl.BlockSpec((tm, na), lambda i, j: (i, 0)))
        args.append(w_aux)
    res = pl.pallas_call(
        functools.partial(_norm_matmul_kernel, has_aux=has_aux),
        grid=(t // tm, n // tn),
        in_specs=in_specs,
        out_specs=out_specs,
        out_shape=out_shape,
        scratch_shapes=[pltpu.VMEM((tm, d), BF16)],
        compiler_params=_cparams(("parallel", "arbitrary")),
        name="norm_matmul",
    )(*args)
    return res if has_aux else res[0]


def _matmul_residual_kernel(a_ref, w_ref, r_ref, o_ref):
    o_ref[...] = r_ref[...] + jnp.dot(a_ref[...], w_ref[...], preferred_element_type=F32)


def _matmul_residual(a, w, res, *, tm=512):
    t, k = a.shape
    n = w.shape[1]
    return pl.pallas_call(
        _matmul_residual_kernel,
        grid=(t // tm,),
        in_specs=[
            pl.BlockSpec((tm, k), lambda i: (i, 0)),
            pl.BlockSpec((k, n), lambda i: (0, 0)),
            pl.BlockSpec((tm, n), lambda i: (i, 0)),
        ],
        out_specs=pl.BlockSpec((tm, n), lambda i: (i, 0)),
        out_shape=jax.ShapeDtypeStruct((t, n), F32),
        compiler_params=_cparams(("parallel",)),
        name="matmul_residual",
    )(a, w, res)


def _attn_kernel(sinks_ref, q_ref, kp_ref, kc_ref, vp_ref, vc_ref, bias_ref, o_ref):
    table = jnp.minimum(pl.program_id(1), 1)
    for h in range(N_KV_HEADS):
        ks = slice(h * HEAD_DIM, (h + 1) * HEAD_DIM)
        kb = jnp.concatenate([kp_ref[:, ks], kc_ref[:, ks]], axis=0)
        vb = jnp.concatenate([vp_ref[:, ks], vc_ref[:, ks]], axis=0)
        for g in range(GQA_GROUP):
            hh = h * GQA_GROUP + g
            qh = q_ref[:, hh * HEAD_DIM:(hh + 1) * HEAD_DIM]
            s = lax.dot_general(qh, kb, (((1,), (1,)), ((), ())), preferred_element_type=F32)
            logits = s + bias_ref[table, hh]
            sink = sinks_ref[hh]
            m = jnp.maximum(jnp.max(logits, axis=-1, keepdims=True), sink)
            p = jnp.exp2(logits - m)
            denom = jnp.sum(p, axis=-1, keepdims=True) + jnp.exp2(sink - m)
            o = jnp.dot(p.astype(BF16), vb, preferred_element_type=F32) / denom
            o_ref[:, hh * HEAD_DIM:(hh + 1) * HEAD_DIM] = o.astype(o_ref.dtype)


def _t5_causal_bucket(dist):
    n = jnp.maximum(dist, 0)
    max_exact = REL_BUCKETS // 2
    nf = jnp.maximum(n, 1).astype(F32)
    large = max_exact + (jnp.log(nf / max_exact) / math.log(REL_MAX_DIST / max_exact)
                         * (REL_BUCKETS - max_exact)).astype(jnp.int32)
    large = jnp.minimum(large, REL_BUCKETS - 1)
    return jnp.where(n < max_exact, n, large)


def _attention(qkv, sinks, rel_bias, batch, seq):
    t = qkv.shape[0]
    nb = seq // ATTN_BLOCK
    qi = jnp.arange(ATTN_BLOCK)[:, None]
    ki = jnp.arange(2 * ATTN_BLOCK)[None, :]
    dist = qi + ATTN_BLOCK - ki
    in_window = (dist >= 0) & (dist < WINDOW)
    onehot = (_t5_causal_bucket(dist)[None] == jnp.arange(REL_BUCKETS)[:, None, None]).astype(F32)
    bias = jnp.einsum('hr,rqk->hqk', rel_bias.astype(F32).T, onehot, precision=lax.Precision.HIGHEST)
    bias = jnp.where(in_window[None], bias * LOG2E, NEG_BIG)
    bias_first = jnp.where((ki < ATTN_BLOCK)[None], NEG_BIG, bias)
    bias2 = jnp.stack([bias_first, bias])

    kcol = Q_DIM // KV_DIM
    vcol = kcol + 1

    def prev(b, n):
        return b * nb + jnp.maximum(n - 1, 0)

    return pl.pallas_call(
        _attn_kernel,
        grid=(batch, nb),
        in_specs=[
            pl.BlockSpec(memory_space=pltpu.SMEM),
            pl.BlockSpec((ATTN_BLOCK, Q_DIM), lambda b, n: (b * nb + n, 0)),
            pl.BlockSpec((ATTN_BLOCK, KV_DIM), lambda b, n: (prev(b, n), kcol)),
            pl.BlockSpec((ATTN_BLOCK, KV_DIM), lambda b, n: (b * nb + n, kcol)),
            pl.BlockSpec((ATTN_BLOCK, KV_DIM), lambda b, n: (prev(b, n), vcol)),
            pl.BlockSpec((ATTN_BLOCK, KV_DIM), lambda b, n: (b * nb + n, vcol)),
            pl.BlockSpec((2, N_HEADS, ATTN_BLOCK, 2 * ATTN_BLOCK), lambda b, n: (0, 0, 0, 0)),
        ],
        out_specs=pl.BlockSpec((ATTN_BLOCK, Q_DIM), lambda b, n: (b * nb + n, 0)),
        out_shape=jax.ShapeDtypeStruct((t, Q_DIM), BF16),
        compiler_params=_cparams(("parallel", "arbitrary")),
        name="swa_attention",
    )((sinks.astype(F32) * LOG2E), qkv, qkv, qkv, qkv, qkv, bias2)


def _split3(v):
    hi = v.astype(BF16)
    r = v - hi.astype(F32)
    mid = r.astype(BF16)
    lo = (r - mid.astype(F32)).astype(BF16)
    return hi, mid, lo


def _ssd_kernel(z_ref, x_ref, bc_ref, dt_ref, convw_ref, convb_ref, dtb_ref, alog_ref,
                dskip_ref, normw_ref, shift_ref, shift_tail_ref, expand_ref,
                o_ref, state, xtail, bctail):
    @pl.when(pl.program_id(1) == 0)
    def _():
        state[...] = jnp.zeros_like(state)
        xtail[...] = jnp.zeros_like(xtail)
        bctail[...] = jnp.zeros_like(bctail)

    for k in range(z_ref.shape[0] // SSM_CHUNK):
        _ssd_chunk(pl.ds(k * SSM_CHUNK, SSM_CHUNK), z_ref, x_ref, bc_ref, dt_ref, convw_ref, convb_ref,
                   dtb_ref, alog_ref, dskip_ref, normw_ref, shift_ref, shift_tail_ref, expand_ref,
                   o_ref, state, xtail, bctail)


def _ssd_chunk(rows, z_ref, x_ref, bc_ref, dt_ref, convw_ref, convb_ref, dtb_ref, alog_ref,
               dskip_ref, normw_ref, shift_ref, shift_tail_ref, expand_ref,
               o_ref, state, xtail, bctail):
    L = SSM_CHUNK

    def conv_silu(u_ref, tail, w_lo, width):
        cur = u_ref[rows, :]
        prev = tail[...]
        w = [convw_ref[k:k + 1, w_lo:w_lo + width].astype(BF16) for k in range(CONV_WIDTH)]
        taps = jnp.concatenate([cur * w[k] for k in range(CONV_WIDTH)], axis=0)
        taps_prev = jnp.concatenate([prev * w[k] for k in range(CONV_WIDTH)], axis=0)
        y = (jnp.dot(shift_ref[...], taps, preferred_element_type=F32)
             + jnp.dot(shift_tail_ref[...], taps_prev, preferred_element_type=F32)
             + convb_ref[:, w_lo:w_lo + width])
        tail[...] = cur[L - CONV_TAIL:L]
        return _silu(y)

    xs = conv_silu(x_ref, xtail, 0, D_INNER)
    bcm = conv_silu(bc_ref, bctail, D_INNER, BC_DIM)

    xdt = dt_ref[rows, :] + dtb_ref[...]
    dt = jnp.maximum(xdt, 0.0) + jnp.log1p(jnp.exp(-jnp.abs(xdt)))
    a = dt * (-jnp.exp(alog_ref[...]) * LOG2E)
    row = lax.broadcasted_iota(jnp.int32, (L, L), 0)
    colm = lax.broadcasted_iota(jnp.int32, (L, L), 1)
    causal = row >= colm
    cs3 = jnp.dot(causal.astype(BF16), jnp.concatenate(_split3(a), axis=1), preferred_element_type=F32)
    a_cs = cs3[:, :LANES] + cs3[:, LANES:2 * LANES] + cs3[:, 2 * LANES:]
    a_cs_t = a_cs.T
    dt_t = dt.T
    a_last_col = a_cs_t[:, L - 1:L]
    w_end_t = jnp.exp2(a_last_col - a_cs_t) * dt_t
    exp_acs = jnp.exp2(a_cs)

    cd3 = [jnp.broadcast_to(part.astype(F32), (16, LANES)).astype(BF16)
           for part in _split3(exp_acs[L - 1:L, :])]
    cdx = jnp.dot(jnp.concatenate(cd3, axis=0), expand_ref[...], preferred_element_type=F32)
    cd_exp = cdx[0:1] + cdx[16:17] + cdx[32:33]

    lane = lax.broadcasted_iota(jnp.int32, (L, LANES), 1)
    low_half = lane < SSM_HEAD_DIM

    y_tiles = []
    for g in range(SSM_GROUPS):
        b_g = bcm[:, g * D_STATE:(g + 1) * D_STATE]
        c_g = bcm[:, (SSM_GROUPS + g) * D_STATE:(SSM_GROUPS + g + 1) * D_STATE]
        b_gb = b_g.astype(BF16)
        c_gb = c_g.astype(BF16)
        cb = lax.dot_general(c_gb, b_gb, (((1,), (1,)), ((), ())), preferred_element_type=F32)
        b_gt = b_g.T
        s_g = state[:, g * GROUP_CH:(g + 1) * GROUP_CH]
        y_off_g = jnp.dot(c_gb, s_g.astype(BF16), preferred_element_type=F32)
        for jj in range(GROUP_CH // LANES):
            j = g * (GROUP_CH // LANES) + jj
            lhs_top = []
            lhs_bot = []
            escale = []
            for e in (2 * j, 2 * j + 1):
                acs_b = jnp.broadcast_to(a_cs[:, e:e + 1], (L, L))
                seg = acs_b - a_cs_t[e:e + 1, :]
                dec = jnp.exp2(jnp.where(causal, seg, -jnp.inf))
                lhs_top.append((cb * dec * dt_t[e:e + 1, :]).astype(BF16))
                lhs_bot.append((b_gt * w_end_t[e:e + 1, :]).astype(BF16))
                escale.append(jnp.broadcast_to(exp_acs[:, e:e + 1], (L, LANES)))
            x_tile = xs[:, j * LANES:(j + 1) * LANES]
            x_lo = jnp.where(low_half, x_tile, 0.0).astype(BF16)
            x_hi = jnp.where(low_half, 0.0, x_tile).astype(BF16)
            lhs = jnp.concatenate([jnp.concatenate(lhs_top, axis=1),
                                   jnp.concatenate(lhs_bot, axis=1)], axis=0)
            rhs = jnp.concatenate([x_lo, x_hi], axis=0)
            r = jnp.dot(lhs, rhs, preferred_element_type=F32)
            y_off = y_off_g[:, jj * LANES:(jj + 1) * LANES]
            y_tiles.append(r[0:L] + jnp.where(low_half, escale[0], escale[1]) * y_off)
            sl = slice(j * LANES, (j + 1) * LANES)
            state[:, sl] = cd_exp[:, sl] * state[:, sl] + r[L:2 * L]

    y = jnp.concatenate(y_tiles, axis=1)
    y = y + xs * dskip_ref[...]
    gated = y * _silu(z_ref[rows, :].astype(F32))
    outs = []
    for g in range(SSM_GROUPS):
        gg = gated[:, g * GROUP_CH:(g + 1) * GROUP_CH]
        ms = jnp.mean(gg * gg, axis=-1, keepdims=True)
        outs.append(gg * lax.rsqrt(ms + NORM_EPS))
    o_ref[rows, :] = (jnp.concatenate(outs, axis=1) * normw_ref[...]).astype(o_ref.dtype)


def _ssd(zxbc, dt_raw, conv_w, conv_b, dt_bias, a_log, d_skip, norm_w, batch, seq):
    t = zxbc.shape[0]
    L = SSM_CHUNK
    rows = SSD_STEP_CHUNKS * L
    nc = seq // rows

    def pad_heads(v):
        return jnp.pad(v.astype(F32), (0, LANES - SSM_HEADS)).reshape(1, LANES)

    def rowmap(col):
        return lambda b, c: (b * nc + c, col)

    def const2(b, c):
        return (0, 0)

    l_idx = jnp.arange(L)[:, None]
    shift = jnp.concatenate(
        [(jnp.arange(L)[None, :] == l_idx - (CONV_WIDTH - 1) + k) for k in range(CONV_WIDTH)],
        axis=1).astype(BF16)
    shift_tail = jnp.concatenate(
        [(jnp.arange(-CONV_TAIL, 0)[None, :] == l_idx - (CONV_WIDTH - 1) + k) for k in range(CONV_WIDTH)],
        axis=1).astype(BF16)
    expand = (jnp.arange(D_INNER)[None, :] // SSM_HEAD_DIM == jnp.arange(LANES)[:, None]).astype(BF16)

    return pl.pallas_call(
        _ssd_kernel,
        grid=(batch, nc),
        in_specs=[
            pl.BlockSpec((rows, D_INNER), rowmap(0)),
            pl.BlockSpec((rows, D_INNER), rowmap(1)),
            pl.BlockSpec((rows, BC_DIM), rowmap(2 * D_INNER // BC_DIM)),
            pl.BlockSpec((rows, LANES), rowmap(0)),
            pl.BlockSpec((CONV_WIDTH, CONV_DIM), const2),
            pl.BlockSpec((1, CONV_DIM), const2),
            pl.BlockSpec((1, LANES), const2),
            pl.BlockSpec((1, LANES), const2),
            pl.BlockSpec((1, D_INNER), const2),
            pl.BlockSpec((1, D_INNER), const2),
            pl.BlockSpec((L, CONV_WIDTH * L), const2),
            pl.BlockSpec((L, CONV_WIDTH * CONV_TAIL), const2),
            pl.BlockSpec((LANES, D_INNER), const2),
        ],
        out_specs=pl.BlockSpec((rows, D_INNER), rowmap(0)),
        out_shape=jax.ShapeDtypeStruct((t, D_INNER), BF16),
        scratch_shapes=[
            pltpu.VMEM((D_STATE, D_INNER), F32),
            pltpu.VMEM((CONV_TAIL, D_INNER), BF16),
            pltpu.VMEM((CONV_TAIL, BC_DIM), BF16),
        ],
        compiler_params=_cparams(("parallel", "arbitrary")),
        name="ssd_core",
    )(zxbc, zxbc, zxbc, dt_raw, conv_w.astype(F32), conv_b.astype(F32).reshape(1, CONV_DIM),
      pad_heads(dt_bias), pad_heads(a_log),
      jnp.repeat(d_skip.astype(F32), SSM_HEAD_DIM).reshape(1, D_INNER),
      norm_w.astype(F32).reshape(1, D_INNER), shift, shift_tail, expand)


GROUP_ROW0 = N_EXPERTS


def _router_kernel(a_ref, wp_ref, res_ref, g_ref, w_ref, b_ref, earlier_ref,
                   x_ref, ids_ref, gates_ref, cnt_ref, carry, wt_split):
    i = pl.program_id(0)
    x_ref[...] = res_ref[...] + jnp.dot(a_ref[...], wp_ref[...], preferred_element_type=F32)

    @pl.when(i == 0)
    def _():
        carry[...] = jnp.zeros_like(carry)
        wt = w_ref[...].T
        wt_hi = wt.astype(BF16)
        wt_split[0:LANES, :] = wt_hi
        wt_split[LANES:, :] = (wt - wt_hi.astype(F32)).astype(BF16)

    h = _rms(x_ref[...], g_ref[...])
    h_hi = h.astype(BF16)
    h_mid = (h - h_hi.astype(F32)).astype(BF16)
    nt = (((1,), (1,)), ((), ()))
    top = lax.dot_general(wt_split[...], h_hi, nt, preferred_element_type=F32)
    cross = lax.dot_general(wt_split[0:LANES, :], h_mid, nt, preferred_element_type=F32)
    logits = top[:LANES] + (top[LANES:] + cross) + b_ref[...]
    tm = logits.shape[1]
    row = lax.broadcasted_iota(jnp.int32, (LANES, tm), 0)
    ninf = -jnp.inf

    def first_argmax(v, vmax):
        return jnp.min(jnp.where(v == vmax, row, LANES), axis=0, keepdims=True)

    is_group = (row >= GROUP_ROW0) & (row < GROUP_ROW0 + N_EXPERT_GROUPS)
    glog = jnp.where(is_group, logits, ninf)
    gmax = jnp.max(glog, axis=0, keepdims=True)
    g_idx = first_argmax(glog, gmax) - GROUP_ROW0
    g_p = 1.0 / jnp.sum(jnp.exp(glog - gmax), axis=0, keepdims=True)

    in_group = (row < N_EXPERTS) & (jnp.right_shift(row, 3) == g_idx)
    elog = jnp.where(in_group, logits, ninf)
    m1 = jnp.max(elog, axis=0, keepdims=True)
    e1 = first_argmax(elog, m1)
    elog2 = jnp.where(row == e1, ninf, elog)
    m2 = jnp.max(elog2, axis=0, keepdims=True)
    e2 = first_argmax(elog2, m2)
    zsum = jnp.sum(jnp.exp(elog - m1), axis=0, keepdims=True)
    p1 = 1.0 / zsum
    p2 = jnp.exp(m2 - m1) / zsum
    psum = p1 + p2
    w1 = p1 / psum * g_p
    w2 = p2 / psum * g_p

    oh1 = row == e1
    oh2 = row == e2
    onehot = jnp.logical_or(oh1, oh2).astype(F32)
    before = jnp.dot(onehot.astype(BF16), earlier_ref[...], preferred_element_type=F32) + carry[:, 0:1]
    r1 = jnp.sum(jnp.where(oh1, before, 0.0), axis=0, keepdims=True).astype(jnp.int32)
    r2 = jnp.sum(jnp.where(oh2, before, 0.0), axis=0, keepdims=True).astype(jnp.int32)
    carry[...] = carry[...] + jnp.sum(onehot, axis=1, keepdims=True)

    row8 = lax.broadcasted_iota(jnp.int32, (8, tm), 0)
    ids_ref[...] = jnp.where(row8 == 0, e1, jnp.where(row8 == 1, e2,
                             jnp.where(row8 == 2, r1, jnp.where(row8 == 3, r2, 0))))
    gates_ref[...] = jnp.where(row == 0, w1, jnp.where(row == 1, w2, 0.0)).T
    cnt_ref[...] = carry[...]


def _router(a, w_proj, res, ln_w, w_group, b_group, w_expert, b_expert, *, tm=512):
    t, d = res.shape
    k = a.shape[1]
    pad = LANES - N_EXPERTS - N_EXPERT_GROUPS
    w_r = jnp.concatenate([w_expert, w_group, jnp.zeros((d, pad), F32)], axis=1).astype(F32)
    b_r = jnp.concatenate([b_expert, b_group, jnp.zeros((pad,), F32)]).astype(F32).reshape(LANES, 1)
    earlier = (jnp.arange(tm)[:, None] < jnp.arange(tm)[None, :]).astype(BF16)
    return pl.pallas_call(
        _router_kernel,
        grid=(t // tm,),
        in_specs=[
            pl.BlockSpec((tm, k), lambda i: (i, 0)),
            pl.BlockSpec((k, d), lambda i: (0, 0)),
            pl.BlockSpec((tm, d), lambda i: (i, 0)),
            pl.BlockSpec((1, d), lambda i: (0, 0)),
            pl.BlockSpec((d, LANES), lambda i: (0, 0)),
            pl.BlockSpec((LANES, 1), lambda i: (0, 0)),
            pl.BlockSpec((tm, tm), lambda i: (0, 0)),
        ],
        out_specs=[
            pl.BlockSpec((tm, d), lambda i: (i, 0)),
            pl.BlockSpec((8, tm), lambda i: (0, i)),
            pl.BlockSpec((tm, LANES), lambda i: (i, 0)),
            pl.BlockSpec((LANES, LANES), lambda i: (0, 0)),
        ],
        out_shape=[
            jax.ShapeDtypeStruct((t, d), F32),
            jax.ShapeDtypeStruct((8, t), jnp.int32),
            jax.ShapeDtypeStruct((t, LANES), F32),
            jax.ShapeDtypeStruct((LANES, LANES), F32),
        ],
        scratch_shapes=[pltpu.VMEM((LANES, LANES), F32), pltpu.VMEM((2 * LANES, d), BF16)],
        compiler_params=_cparams(("arbitrary",)),
        name="proj_moe_router",
    )(a, w_proj, res, ln_w.reshape(1, d), w_r, b_r, earlier)


U32 = jnp.uint32
ROW_WORDS = D_MODEL // 2
ROW_SUB = ROW_WORDS // LANES
HIGH_HALF = 0xFFFF0000
ROW_MOVE_TOKENS = 512


def _pack_rows(v):
    lo = lax.bitcast_convert_type(v[:, :ROW_WORDS].astype(BF16).astype(F32), U32)
    hi = lax.bitcast_convert_type(v[:, ROW_WORDS:].astype(BF16).astype(F32), U32)
    return jnp.right_shift(lo, jnp.uint32(16)) | (hi & jnp.uint32(HIGH_HALF))


def _unpack_rows(w):
    lo = lax.bitcast_convert_type(jnp.left_shift(w, jnp.uint32(16)), F32)
    hi = lax.bitcast_convert_type(w & jnp.uint32(HIGH_HALF), F32)
    return lo, hi


def _store_rows(ref, packed):
    n = packed.shape[0]
    for c in range(ROW_SUB):
        ref[pl.ds(c, n, stride=ROW_SUB), :] = packed[:, c * LANES:(c + 1) * LANES]


def _load_rows(ref, n):
    return jnp.concatenate([ref[pl.ds(c, n, stride=ROW_SUB), :] for c in range(ROW_SUB)], axis=1)


def _row_slice(ref, row):
    return ref.at[pl.ds(pl.multiple_of(row * ROW_SUB, ROW_SUB), ROW_SUB)]


def _dispatch_kernel(zf_ref, dest_ref, x_ref, g_ref, xs_ref, hbuf, zbuf, sem, zsem):
    tb = x_ref.shape[0]
    n_blocks = zf_ref.shape[0]
    blk_rows = MOE_BLOCK * ROW_SUB

    @pl.when(pl.program_id(0) == 0)
    def _():
        zbuf[...] = jnp.zeros_like(zbuf)

        def zero_copy(b):
            start = pl.multiple_of(b * blk_rows, blk_rows)
            return pltpu.make_async_copy(zbuf, xs_ref.at[pl.ds(start, blk_rows)], zsem)

        def zissue(b, carry):
            @pl.when(zf_ref[b] != 0)
            def _():
                zero_copy(b).start()
            return carry

        def zdrain(b, carry):
            @pl.when(zf_ref[b] != 0)
            def _():
                zero_copy(b).wait()
            return carry

        lax.fori_loop(0, n_blocks, zissue, 0)
        lax.fori_loop(0, n_blocks, zdrain, 0)

    _store_rows(hbuf, _pack_rows(_rms(x_ref[...], g_ref[...])))
    for r in range(tb):
        for k in range(2):
            d = dest_ref[k, r]
            pltpu.make_async_copy(_row_slice(hbuf, r), _row_slice(xs_ref, d), sem).start(priority=k)
    for k in range(2):
        pltpu.make_async_copy(hbuf, xs_ref.at[pl.ds(0, tb * ROW_SUB)], sem).wait()


def _dispatch(x, ln_w, dest, zero_flag, n_slots, *, tb=ROW_MOVE_TOKENS):
    t, d = x.shape
    nblk = t // tb
    return pl.pallas_call(
        _dispatch_kernel,
        grid_spec=pltpu.PrefetchScalarGridSpec(
            num_scalar_prefetch=1,
            grid=(nblk,),
            in_specs=[
                pl.BlockSpec((2, tb), lambda i, zf: (0, i), memory_space=pltpu.SMEM),
                pl.BlockSpec((tb, d), lambda i, zf: (i, 0)),
                pl.BlockSpec((1, d), lambda i, zf: (0, 0)),
            ],
            out_specs=pl.BlockSpec(memory_space=pl.ANY),
            scratch_shapes=[pltpu.VMEM((tb * ROW_SUB, LANES), U32),
                            pltpu.VMEM((MOE_BLOCK * ROW_SUB, LANES), U32),
                            pltpu.SemaphoreType.DMA(()), pltpu.SemaphoreType.DMA(())],
        ),
        out_shape=jax.ShapeDtypeStruct((n_slots * ROW_SUB, LANES), U32),
        compiler_params=_cparams(("arbitrary",)),
        name="moe_dispatch",
    )(zero_flag, dest, x, ln_w.reshape(1, d))


def _expert_kernel(be_ref, bpos_ref, elist_ref, meta_ref, xs_ref, wg_hbm, wu_hbm, wd_hbm, y_ref,
                   wg_s, wu_s, wd_s, wg_f, wu_f, wd_f, sems, *, layer):
    i = pl.program_id(0)
    active = i < meta_ref[0]
    pos = bpos_ref[i]
    changed = jnp.logical_or(i == 0, be_ref[i] != be_ref[jnp.maximum(i - 1, 0)])
    streams = ((wg_hbm, wg_f), (wu_hbm, wu_f), (wd_hbm, wd_f))

    def fetch(p, slot):
        e = elist_ref[p]
        return [pltpu.make_async_copy(hbm.at[layer, e], buf.at[slot], sems.at[slot, t])
                for t, (hbm, buf) in enumerate(streams)]

    @pl.when(jnp.logical_and(active, i == 0))
    def _():
        for cp in fetch(0, 0):
            cp.start()

    @pl.when(jnp.logical_and(active, changed))
    def _():
        slot = lax.rem(pos, 2)
        for cp in fetch(pos, slot):
            cp.wait()

        @pl.when(pos + 1 < meta_ref[1])
        def _():
            for cp in fetch(pos + 1, 1 - slot):
                cp.start()

        wg_s[...] = wg_f[slot].astype(BF16)
        wu_s[...] = wu_f[slot].astype(BF16)
        wd_s[...] = wd_f[slot].astype(BF16)

    @pl.when(active)
    def _():
        lo, hi = _unpack_rows(_load_rows(xs_ref, MOE_BLOCK))
        xb = jnp.concatenate([lo.astype(BF16), hi.astype(BF16)], axis=1)
        gate = jnp.dot(xb, wg_s[...], preferred_element_type=F32)
        up = jnp.dot(xb, wu_s[...], preferred_element_type=F32)
        hid = (_silu(gate) * up).astype(BF16)
        _store_rows(y_ref, _pack_rows(jnp.dot(hid, wd_s[...], preferred_element_type=F32)))

    @pl.when(jnp.logical_not(active))
    def _():
        y_ref[...] = jnp.zeros_like(y_ref)


def _experts(xs, block_e, block_pos, expert_list, meta, w_gate, w_up, w_down, layer):
    blk_rows = MOE_BLOCK * ROW_SUB
    n_blocks = xs.shape[0] // blk_rows
    d = D_MODEL

    def xmap(i, be, bpos, elist, meta):
        return (jnp.maximum(jnp.minimum(i, meta[0] - 1), 0), 0)

    return pl.pallas_call(
        functools.partial(_expert_kernel, layer=layer),
        grid_spec=pltpu.PrefetchScalarGridSpec(
            num_scalar_prefetch=4,
            grid=(n_blocks,),
            in_specs=[
                pl.BlockSpec((blk_rows, LANES), xmap),
                pl.BlockSpec(memory_space=pl.ANY),
                pl.BlockSpec(memory_space=pl.ANY),
                pl.BlockSpec(memory_space=pl.ANY),
            ],
            out_specs=pl.BlockSpec((blk_rows, LANES), lambda i, be, bpos, elist, meta: (i, 0)),
            scratch_shapes=[
                pltpu.VMEM((d, EXPERT_FF), BF16),
                pltpu.VMEM((d, EXPERT_FF), BF16),
                pltpu.VMEM((EXPERT_FF, d), BF16),
                pltpu.VMEM((2, d, EXPERT_FF), F32),
                pltpu.VMEM((2, d, EXPERT_FF), F32),
                pltpu.VMEM((2, EXPERT_FF, d), F32),
                pltpu.SemaphoreType.DMA((2, 3)),
            ],
        ),
        out_shape=jax.ShapeDtypeStruct(xs.shape, U32),
        compiler_params=_cparams(("arbitrary",)),
        name="moe_experts",
    )(block_e, block_pos, expert_list, meta, xs, w_gate, w_up, w_down)


def _combine_kernel(dest_ref, x_ref, gates_ref, y_ref, *rest, final_norm):
    if final_norm:
        g_ref, o_ref, buf0, buf1, sem = rest
    else:
        o_ref, buf0, buf1, sem = rest
    tb = x_ref.shape[0]
    bufs = (buf0, buf1)

    for r in range(tb):
        for k in range(2):
            d = dest_ref[k, r]
            pltpu.make_async_copy(_row_slice(y_ref, d), _row_slice(bufs[k], r), sem).start(priority=k)
    for k in range(2):
        pltpu.make_async_copy(y_ref.at[pl.ds(0, tb * ROW_SUB)], bufs[k], sem).wait()

    gt = gates_ref[...]
    lo0, hi0 = _unpack_rows(_load_rows(buf0, tb))
    lo1, hi1 = _unpack_rows(_load_rows(buf1, tb))
    w0 = gt[:, 0:1]
    w1 = gt[:, 1:2]
    out = x_ref[...] + jnp.concatenate([lo0 * w0 + lo1 * w1, hi0 * w0 + hi1 * w1], axis=1)
    if final_norm:
        out = _rms(out, g_ref[...])
    o_ref[...] = out


def _combine(x, gates, y, dest, final_w=None, *, tb=ROW_MOVE_TOKENS):
    t, d = x.shape
    nblk = t // tb
    final_norm = final_w is not None
    in_specs = [
        pl.BlockSpec((2, tb), lambda i: (0, i), memory_space=pltpu.SMEM),
        pl.BlockSpec((tb, d), lambda i: (i, 0)),
        pl.BlockSpec((tb, LANES), lambda i: (i, 0)),
        pl.BlockSpec(memory_space=pl.ANY),
    ]
    args = [dest, x, gates, y]
    if final_norm:
        in_specs.append(pl.BlockSpec((1, d), lambda i: (0, 0)))
        args.append(final_w.reshape(1, d))
    return pl.pallas_call(
        functools.partial(_combine_kernel, final_norm=final_norm),
        grid=(nblk,),
        in_specs=in_specs,
        out_specs=pl.BlockSpec((tb, d), lambda i: (i, 0)),
        out_shape=jax.ShapeDtypeStruct((t, d), F32),
        scratch_shapes=[pltpu.VMEM((tb * ROW_SUB, LANES), U32), pltpu.VMEM((tb * ROW_SUB, LANES), U32),
                        pltpu.SemaphoreType.DMA(())],
        compiler_params=_cparams(("arbitrary",)),
        name="moe_combine",
    )(*args)


def _proj_moe(a, w_proj, res, ln_w, w_group, b_group, w_expert, b_expert, w_gate, w_up, w_down, layer,
              final_w=None):
    t = res.shape[0]
    n_assign = t * 2
    n_blocks = -(-n_assign // MOE_BLOCK) + N_EXPERTS
    n_slots = n_blocks * MOE_BLOCK

    x, ids, gates, cnt = _router(a, w_proj, res, ln_w, w_group, b_group, w_expert, b_expert)
    counts = cnt[:N_EXPERTS, 0].astype(jnp.int32)
    padded = (counts + MOE_BLOCK - 1) // MOE_BLOCK * MOE_BLOCK
    p_ends = jnp.cumsum(padded)
    p_starts = p_ends - padded
    experts = jnp.arange(N_EXPERTS, dtype=jnp.int32)
    start_of = jnp.sum(jnp.where(ids[0:2, :, None] == experts, p_starts, 0), axis=-1)
    dest = (start_of + ids[2:4]).astype(jnp.int32)
    block_start = jnp.arange(n_blocks, dtype=jnp.int32) * MOE_BLOCK
    block_e = jnp.minimum(jnp.sum((p_ends[None, :] <= block_start[:, None]).astype(jnp.int32), axis=1),
                          N_EXPERTS - 1)
    n_active = (p_ends[-1:] // MOE_BLOCK).astype(jnp.int32)
    valid_end = jnp.sum(jnp.where(block_e[:, None] == experts, p_starts + counts, 0), axis=-1)
    zero_flag = jnp.logical_or(block_start + MOE_BLOCK > valid_end,
                               block_start >= p_ends[-1]).astype(jnp.int32)

    has_tokens = counts > 0
    list_pos = jnp.cumsum(has_tokens.astype(jnp.int32)) - 1
    expert_list = jnp.sum(jnp.where(jnp.logical_and(has_tokens[None, :], list_pos[None, :] == experts[:, None]),
                                    experts[None, :], 0), axis=1).astype(jnp.int32)
    block_pos = jnp.sum(jnp.where(block_e[:, None] == experts, list_pos, 0), axis=-1).astype(jnp.int32)
    meta = jnp.concatenate([n_active, jnp.sum(has_tokens.astype(jnp.int32), keepdims=True)]).astype(jnp.int32)

    xs = _dispatch(x, ln_w, dest, zero_flag, n_slots)
    y = _experts(xs, block_e, block_pos, expert_list, meta, w_gate, w_up, w_down, layer)
    return _combine(x, gates, y, dest, final_w)


def kernel(x, rel_bias, ln_mix, ln_ffn, ln_final, attn_w_qkv, attn_sinks, attn_w_o, ssm_w_in,
           ssm_conv_w, ssm_conv_b, ssm_dt_bias, ssm_a_log, ssm_d, ssm_norm_w, ssm_w_out,
           moe_w_group, moe_b_group, moe_w_expert, moe_b_expert, moe_w_gate, moe_w_up, moe_w_down):
    batch, seq, d = x.shape
    t = batch * seq
    xf = x.reshape(t, d).astype(F32)

    q_scale = jnp.where(jnp.arange(QKV_DIM) < Q_DIM, HEAD_DIM ** -0.5 * LOG2E, 1.0).astype(F32)
    qkv = _norm_matmul(xf, ln_mix[0], (attn_w_qkv[0] * q_scale).astype(BF16), tm=512, tn=QKV_DIM)
    att = _attention(qkv, attn_sinks[0], rel_bias, batch, seq)
    xf = _proj_moe(att, attn_w_o[0].astype(BF16), xf, ln_ffn[0], moe_w_group[0], moe_b_group[0],
                   moe_w_expert[0], moe_b_expert[0], moe_w_gate, moe_w_up, moe_w_down, 0)

    w_in = ssm_w_in[0]
    w_dt = jnp.pad(w_in[:, ZXBC_DIM:], ((0, 0), (0, LANES - SSM_HEADS))).astype(BF16)
    zxbc, dt_raw = _norm_matmul(xf, ln_mix[1], w_in[:, :ZXBC_DIM].astype(BF16), w_dt, tm=512, tn=ZXBC_DIM)
    gated = _ssd(zxbc, dt_raw, ssm_conv_w[0], ssm_conv_b[0], ssm_dt_bias[0], ssm_a_log[0],
                 ssm_d[0], ssm_norm_w[0], batch, seq)
    xf = _proj_moe(gated, ssm_w_out[0].astype(BF16), xf, ln_ffn[1], moe_w_group[1], moe_b_group[1],
                   moe_w_expert[1], moe_b_expert[1], moe_w_gate, moe_w_up, moe_w_down, 1, final_w=ln_final)
    return xf.reshape(batch, seq, d).astype(x.dtype)
```

```python
import functools
import math

import jax
import jax.numpy as jnp
from jax import lax
from jax.experimental import pallas as pl
from jax.experimental.pallas import tpu as pltpu

F32 = jnp.float32
BF16 = jnp.bfloat16

D_MODEL = 1024
N_HEADS = 16
N_KV_HEADS = 4
HEAD_DIM = 64
GQA_GROUP = N_HEADS // N_KV_HEADS
WINDOW = 128
ATTN_BLOCK = 128
Q_DIM = N_HEADS * HEAD_DIM
KV_DIM = N_KV_HEADS * HEAD_DIM
QKV_DIM = Q_DIM + 2 * KV_DIM
REL_BUCKETS = 32
REL_MAX_DIST = 128

D_INNER = 2048
SSM_HEAD_DIM = 64
SSM_HEADS = D_INNER // SSM_HEAD_DIM
SSM_GROUPS = 4
D_STATE = 128
CONV_WIDTH = 4
BC_DIM = 2 * SSM_GROUPS * D_STATE
CONV_DIM = D_INNER + BC_DIM
ZXBC_DIM = D_INNER + CONV_DIM
SSM_CHUNK = 128
SSD_STEP_CHUNKS = 4
GROUP_CH = D_INNER // SSM_GROUPS

N_EXPERT_GROUPS = 8
EXPERTS_PER_GROUP = 8
N_EXPERTS = 64
EXPERT_FF = 512
MOE_BLOCK = 256

NORM_EPS = 1e-6
LOG2E = 1.4426950408889634
CONV_TAIL = 16
LANES = 128
NEG_BIG = -1e30
VMEM_LIMIT = 56 * 1024 * 1024


def _cparams(sem, flags=None):
    return pltpu.CompilerParams(dimension_semantics=sem, vmem_limit_bytes=VMEM_LIMIT, flags=flags)


def _rms(x, g):
    ms = jnp.mean(x * x, axis=-1, keepdims=True)
    return x * lax.rsqrt(ms + NORM_EPS) * g


def _silu(x):
    h = 0.5 * x
    return h + h * jnp.tanh(h)


def _norm_matmul_kernel(x_ref, g_ref, w_ref, *rest, has_aux):
    if has_aux:
        wa_ref, o_ref, oa_ref, h_scr = rest
    else:
        o_ref, h_scr = rest

    @pl.when(pl.program_id(1) == 0)
    def _():
        h = _rms(x_ref[...], g_ref[...]).astype(BF16)
        h_scr[...] = h
        if has_aux:
            oa_ref[...] = jnp.dot(h, wa_ref[...], preferred_element_type=F32)

    o_ref[...] = jnp.dot(h_scr[...], w_ref[...], preferred_element_type=F32).astype(o_ref.dtype)


def _norm_matmul(x, g, w, w_aux=None, *, tm=1024, tn=512):
    t, d = x.shape
    n = w.shape[1]
    has_aux = w_aux is not None
    in_specs = [
        pl.BlockSpec((tm, d), lambda i, j: (i, 0)),
        pl.BlockSpec((1, d), lambda i, j: (0, 0)),
        pl.BlockSpec((d, tn), lambda i, j: (0, j)),
    ]
    out_shape = [jax.ShapeDtypeStruct((t, n), BF16)]
    out_specs = [pl.BlockSpec((tm, tn), lambda i, j: (i, j))]
    args = [x, g.reshape(1, d), w]
    if has_aux:
        na = w_aux.shape[1]
        in_specs.append(pl.BlockSpec((d, na), lambda i, j: (0, 0)))
        out_shape.append(jax.ShapeDtypeStruct((t, na), F32))
        out_specs.append(pl.BlockSpec((tm, na), lambda i, j: (i, 0)))
        args.append(w_aux)
    res = pl.pallas_call(
        functools.partial(_norm_matmul_kernel, has_aux=has_aux),
        grid=(t // tm, n // tn),
        in_specs=in_specs,
        out_specs=out_specs,
        out_shape=out_shape,
        scratch_shapes=[pltpu.VMEM((tm, d), BF16)],
        compiler_params=_cparams(("parallel", "arbitrary")),
        name="norm_matmul",
    )(*args)
    return res if has_aux else res[0]


def _matmul_residual_kernel(a_ref, w_ref, r_ref, o_ref):
    o_ref[...] = r_ref[...] + jnp.dot(a_ref[...], w_ref[...], preferred_element_type=F32)


def _matmul_residual(a, w, res, *, tm=512):
    t, k = a.shape
    n = w.shape[1]
    return pl.pallas_call(
        _matmul_residual_kernel,
        grid=(t // tm,),
        in_specs=[
            pl.BlockSpec((tm, k), lambda i: (i, 0)),
            pl.BlockSpec((k, n), lambda i: (0, 0)),
            pl.BlockSpec((tm, n), lambda i: (i, 0)),
        ],
        out_specs=pl.BlockSpec((tm, n), lambda i: (i, 0)),
        out_shape=jax.ShapeDtypeStruct((t, n), F32),
        compiler_params=_cparams(("parallel",)),
        name="matmul_residual",
    )(a, w, res)


def _attn_kernel(sinks_ref, q_ref, kp_ref, kc_ref, vp_ref, vc_ref, bias_ref, o_ref):
    table = jnp.minimum(pl.program_id(1), 1)
    for h in range(N_KV_HEADS):
        ks = slice(h * HEAD_DIM, (h + 1) * HEAD_DIM)
        kb = jnp.concatenate([kp_ref[:, ks], kc_ref[:, ks]], axis=0)
        vb = jnp.concatenate([vp_ref[:, ks], vc_ref[:, ks]], axis=0)
        for g in range(GQA_GROUP):
            hh = h * GQA_GROUP + g
            qh = q_ref[:, hh * HEAD_DIM:(hh + 1) * HEAD_DIM]
            s = lax.dot_general(qh, kb, (((1,), (1,)), ((), ())), preferred_element_type=F32)
            logits = s + bias_ref[table, hh]
            sink = sinks_ref[hh]
            m = jnp.maximum(jnp.max(logits, axis=-1, keepdims=True), sink)
            p = jnp.exp2(logits - m)
            denom = jnp.sum(p, axis=-1, keepdims=True) + jnp.exp2(sink - m)
            o = jnp.dot(p.astype(BF16), vb, preferred_element_type=F32) / denom
            o_ref[:, hh * HEAD_DIM:(hh + 1) * HEAD_DIM] = o.astype(o_ref.dtype)


def _t5_causal_bucket(dist):
    n = jnp.maximum(dist, 0)
    max_exact = REL_BUCKETS // 2
    nf = jnp.maximum(n, 1).astype(F32)
    large = max_exact + (jnp.log(nf / max_exact) / math.log(REL_MAX_DIST / max_exact)
                         * (REL_BUCKETS - max_exact)).astype(jnp.int32)
    large = jnp.minimum(large, REL_BUCKETS - 1)
    return jnp.where(n < max_exact, n, large)


def _attention(qkv, sinks, rel_bias, batch, seq):
    t = qkv.shape[0]
    nb = seq // ATTN_BLOCK
    qi = jnp.arange(ATTN_BLOCK)[:, None]
    ki = jnp.arange(2 * ATTN_BLOCK)[None, :]
    dist = qi + ATTN_BLOCK - ki
    in_window = (dist >= 0) & (dist < WINDOW)
    onehot = (_t5_causal_bucket(dist)[None] == jnp.arange(REL_BUCKETS)[:, None, None]).astype(F32)
    bias = jnp.einsum('hr,rqk->hqk', rel_bias.astype(F32).T, onehot, precision=lax.Precision.HIGHEST)
    bias = jnp.where(in_window[None], bias * LOG2E, NEG_BIG)
    bias_first = jnp.where((ki < ATTN_BLOCK)[None], NEG_BIG, bias)
    bias2 = jnp.stack([bias_first, bias])

    kcol = Q_DIM // KV_DIM
    vcol = kcol + 1

    def prev(b, n):
        return b * nb + jnp.maximum(n - 1, 0)

    return pl.pallas_call(
        _attn_kernel,
        grid=(batch, nb),
        in_specs=[
            pl.BlockSpec(memory_space=pltpu.SMEM),
            pl.BlockSpec((ATTN_BLOCK, Q_DIM), lambda b, n: (b * nb + n, 0)),
            pl.BlockSpec((ATTN_BLOCK, KV_DIM), lambda b, n: (prev(b, n), kcol)),
            pl.BlockSpec((ATTN_BLOCK, KV_DIM), lambda b, n: (b * nb + n, kcol)),
            pl.BlockSpec((ATTN_BLOCK, KV_DIM), lambda b, n: (prev(b, n), vcol)),
            pl.BlockSpec((ATTN_BLOCK, KV_DIM), lambda b, n: (b * nb + n, vcol)),
            pl.BlockSpec((2, N_HEADS, ATTN_BLOCK, 2 * ATTN_BLOCK), lambda b, n: (0, 0, 0, 0)),
        ],
        out_specs=pl.BlockSpec((ATTN_BLOCK, Q_DIM), lambda b, n: (b * nb + n, 0)),
        out_shape=jax.ShapeDtypeStruct((t, Q_DIM), BF16),
        compiler_params=_cparams(("parallel", "arbitrary")),
        name="swa_attention",
    )((sinks.astype(F32) * LOG2E), qkv, qkv, qkv, qkv, qkv, bias2)


def _split3(v):
    hi = v.astype(BF16)
    r = v - hi.astype(F32)
    mid = r.astype(BF16)
    lo = (r - mid.astype(F32)).astype(BF16)
    return hi, mid, lo


def _ssd_kernel(z_ref, x_ref, bc_ref, dt_ref, convw_ref, convb_ref, dtb_ref, alog_ref,
                dskip_ref, normw_ref, shift_ref, shift_tail_ref, expand_ref,
                o_ref, state, xtail, bctail):
    @pl.when(pl.program_id(1) == 0)
    def _():
        state[...] = jnp.zeros_like(state)
        xtail[...] = jnp.zeros_like(xtail)
        bctail[...] = jnp.zeros_like(bctail)

    for k in range(z_ref.shape[0] // SSM_CHUNK):
        _ssd_chunk(pl.ds(k * SSM_CHUNK, SSM_CHUNK), z_ref, x_ref, bc_ref, dt_ref, convw_ref, convb_ref,
                   dtb_ref, alog_ref, dskip_ref, normw_ref, shift_ref, shift_tail_ref, expand_ref,
                   o_ref, state, xtail, bctail)


def _ssd_chunk(rows, z_ref, x_ref, bc_ref, dt_ref, convw_ref, convb_ref, dtb_ref, alog_ref,
               dskip_ref, normw_ref, shift_ref, shift_tail_ref, expand_ref,
               o_ref, state, xtail, bctail):
    L = SSM_CHUNK

    def conv_silu(u_ref, tail, w_lo, width):
        cur = u_ref[rows, :]
        prev = tail[...]
        w = [convw_ref[k:k + 1, w_lo:w_lo + width].astype(BF16) for k in range(CONV_WIDTH)]
        taps = jnp.concatenate([cur * w[k] for k in range(CONV_WIDTH)], axis=0)
        taps_prev = jnp.concatenate([prev * w[k] for k in range(CONV_WIDTH)], axis=0)
        y = (jnp.dot(shift_ref[...], taps, preferred_element_type=F32)
             + jnp.dot(shift_tail_ref[...], taps_prev, preferred_element_type=F32)
             + convb_ref[:, w_lo:w_lo + width])
        tail[...] = cur[L - CONV_TAIL:L]
        return _silu(y)

    xs = conv_silu(x_ref, xtail, 0, D_INNER)
    bcm = conv_silu(bc_ref, bctail, D_INNER, BC_DIM)

    xdt = dt_ref[rows, :] + dtb_ref[...]
    dt = jnp.maximum(xdt, 0.0) + jnp.log1p(jnp.exp(-jnp.abs(xdt)))
    a = dt * (-jnp.exp(alog_ref[...]) * LOG2E)
    row = lax.broadcasted_iota(jnp.int32, (L, L), 0)
    colm = lax.broadcasted_iota(jnp.int32, (L, L), 1)
    causal = row >= colm
    cs3 = jnp.dot(causal.astype(BF16), jnp.concatenate(_split3(a), axis=1), preferred_element_type=F32)
    a_cs = cs3[:, :LANES] + cs3[:, LANES:2 * LANES] + cs3[:, 2 * LANES:]
    a_cs_t = a_cs.T
    dt_t = dt.T
    a_last_col = a_cs_t[:, L - 1:L]
    w_end_t = jnp.exp2(a_last_col - a_cs_t) * dt_t
    exp_acs = jnp.exp2(a_cs)

    cd3 = [jnp.broadcast_to(part.astype(F32), (16, LANES)).astype(BF16)
           for part in _split3(exp_acs[L - 1:L, :])]
    cdx = jnp.dot(jnp.concatenate(cd3, axis=0), expand_ref[...], preferred_element_type=F32)
    cd_exp = cdx[0:1] + cdx[16:17] + cdx[32:33]

    lane = lax.broadcasted_iota(jnp.int32, (L, LANES), 1)
    low_half = lane < SSM_HEAD_DIM

    y_tiles = []
    for g in range(SSM_GROUPS):
        b_g = bcm[:, g * D_STATE:(g + 1) * D_STATE]
        c_g = bcm[:, (SSM_GROUPS + g) * D_STATE:(SSM_GROUPS + g + 1) * D_STATE]
        b_gb = b_g.astype(BF16)
        c_gb = c_g.astype(BF16)
        cb = lax.dot_general(c_gb, b_gb, (((1,), (1,)), ((), ())), preferred_element_type=F32)
        b_gt = b_g.T
        s_g = state[:, g * GROUP_CH:(g + 1) * GROUP_CH]
        y_off_g = jnp.dot(c_gb, s_g.astype(BF16), preferred_element_type=F32)
        for jj in range(GROUP_CH // LANES):
            j = g * (GROUP_CH // LANES) + jj
            lhs_top = []
            lhs_bot = []
            escale = []
            for e in (2 * j, 2 * j + 1):
                acs_b = jnp.broadcast_to(a_cs[:, e:e + 1], (L, L))
                seg = acs_b - a_cs_t[e:e + 1, :]
                dec = jnp.exp2(jnp.where(causal, seg, -jnp.inf))
                lhs_top.append((cb * dec * dt_t[e:e + 1, :]).astype(BF16))
                lhs_bot.append((b_gt * w_end_t[e:e + 1, :]).astype(BF16))
                escale.append(jnp.broadcast_to(exp_acs[:, e:e + 1], (L, LANES)))
            x_tile = xs[:, j * LANES:(j + 1) * LANES]
            x_lo = jnp.where(low_half, x_tile, 0.0).astype(BF16)
            x_hi = jnp.where(low_half, 0.0, x_tile).astype(BF16)
            lhs = jnp.concatenate([jnp.concatenate(lhs_top, axis=1),
                                   jnp.concatenate(lhs_bot, axis=1)], axis=0)
            rhs = jnp.concatenate([x_lo, x_hi], axis=0)
            r = jnp.dot(lhs, rhs, preferred_element_type=F32)
            y_off = y_off_g[:, jj * LANES:(jj + 1) * LANES]
            y_tiles.append(r[0:L] + jnp.where(low_half, escale[0], escale[1]) * y_off)
            sl = slice(j * LANES, (j + 1) * LANES)
            state[:, sl] = cd_exp[:, sl] * state[:, sl] + r[L:2 * L]

    y = jnp.concatenate(y_tiles, axis=1)
    y = y + xs * dskip_ref[...]
    gated = y * _silu(z_ref[rows, :].astype(F32))
    outs = []
    for g in range(SSM_GROUPS):
        gg = gated[:, g * GROUP_CH:(g + 1) * GROUP_CH]
        ms = jnp.mean(gg * gg, axis=-1, keepdims=True)
        outs.append(gg * lax.rsqrt(ms + NORM_EPS))
    o_ref[rows, :] = (jnp.concatenate(outs, axis=1) * normw_ref[...]).astype(o_ref.dtype)


def _ssd(zxbc, dt_raw, conv_w, conv_b, dt_bias, a_log, d_skip, norm_w, batch, seq):
    t = zxbc.shape[0]
    L = SSM_CHUNK
    rows = SSD_STEP_CHUNKS * L
    nc = seq // rows

    def pad_heads(v):
        return jnp.pad(v.astype(F32), (0, LANES - SSM_HEADS)).reshape(1, LANES)

    def rowmap(col):
        return lambda b, c: (b * nc + c, col)

    def const2(b, c):
        return (0, 0)

    l_idx = jnp.arange(L)[:, None]
    shift = jnp.concatenate(
        [(jnp.arange(L)[None, :] == l_idx - (CONV_WIDTH - 1) + k) for k in range(CONV_WIDTH)],
        axis=1).astype(BF16)
    shift_tail = jnp.concatenate(
        [(jnp.arange(-CONV_TAIL, 0)[None, :] == l_idx - (CONV_WIDTH - 1) + k) for k in range(CONV_WIDTH)],
        axis=1).astype(BF16)
    expand = (jnp.arange(D_INNER)[None, :] // SSM_HEAD_DIM == jnp.arange(LANES)[:, None]).astype(BF16)

    return pl.pallas_call(
        _ssd_kernel,
        grid=(batch, nc),
        in_specs=[
            pl.BlockSpec((rows, D_INNER), rowmap(0)),
            pl.BlockSpec((rows, D_INNER), rowmap(1)),
            pl.BlockSpec((rows, BC_DIM), rowmap(2 * D_INNER // BC_DIM)),
            pl.BlockSpec((rows, LANES), rowmap(0)),
            pl.BlockSpec((CONV_WIDTH, CONV_DIM), const2),
            pl.BlockSpec((1, CONV_DIM), const2),
            pl.BlockSpec((1, LANES), const2),
            pl.BlockSpec((1, LANES), const2),
            pl.BlockSpec((1, D_INNER), const2),
            pl.BlockSpec((1, D_INNER), const2),
            pl.BlockSpec((L, CONV_WIDTH * L), const2),
            pl.BlockSpec((L, CONV_WIDTH * CONV_TAIL), const2),
            pl.BlockSpec((LANES, D_INNER), const2),
        ],
        out_specs=pl.BlockSpec((rows, D_INNER), rowmap(0)),
        out_shape=jax.ShapeDtypeStruct((t, D_INNER), BF16),
        scratch_shapes=[
            pltpu.VMEM((D_STATE, D_INNER), F32),
            pltpu.VMEM((CONV_TAIL, D_INNER), BF16),
            pltpu.VMEM((CONV_TAIL, BC_DIM), BF16),
        ],
        compiler_params=_cparams(("parallel", "arbitrary")),
        name="ssd_core",
    )(zxbc, zxbc, zxbc, dt_raw, conv_w.astype(F32), conv_b.astype(F32).reshape(1, CONV_DIM),
      pad_heads(dt_bias), pad_heads(a_log),
      jnp.repeat(d_skip.astype(F32), SSM_HEAD_DIM).reshape(1, D_INNER),
      norm_w.astype(F32).reshape(1, D_INNER), shift, shift_tail, expand)


GROUP_ROW0 = N_EXPERTS


def _router_kernel(a_ref, wp_ref, res_ref, g_ref, w_ref, b_ref, earlier_ref,
                   x_ref, ids_ref, gates_ref, cnt_ref, carry, wt_split):
    i = pl.program_id(0)
    x_ref[...] = res_ref[...] + jnp.dot(a_ref[...], wp_ref[...], preferred_element_type=F32)

    @pl.when(i == 0)
    def _():
        carry[...] = jnp.zeros_like(carry)
        wt = w_ref[...].T
        wt_hi = wt.astype(BF16)
        wt_split[0:LANES, :] = wt_hi
        wt_split[LANES:, :] = (wt - wt_hi.astype(F32)).astype(BF16)

    h = _rms(x_ref[...], g_ref[...])
    h_hi = h.astype(BF16)
    h_mid = (h - h_hi.astype(F32)).astype(BF16)
    nt = (((1,), (1,)), ((), ()))
    top = lax.dot_general(wt_split[...], h_hi, nt, preferred_element_type=F32)
    cross = lax.dot_general(wt_split[0:LANES, :], h_mid, nt, preferred_element_type=F32)
    logits = top[:LANES] + (top[LANES:] + cross) + b_ref[...]
    tm = logits.shape[1]
    row = lax.broadcasted_iota(jnp.int32, (LANES, tm), 0)
    ninf = -jnp.inf

    def first_argmax(v, vmax):
        return jnp.min(jnp.where(v == vmax, row, LANES), axis=0, keepdims=True)

    is_group = (row >= GROUP_ROW0) & (row < GROUP_ROW0 + N_EXPERT_GROUPS)
    glog = jnp.where(is_group, logits, ninf)
    gmax = jnp.max(glog, axis=0, keepdims=True)
    g_idx = first_argmax(glog, gmax) - GROUP_ROW0
    g_p = 1.0 / jnp.sum(jnp.exp(glog - gmax), axis=0, keepdims=True)

    in_group = (row < N_EXPERTS) & (jnp.right_shift(row, 3) == g_idx)
    elog = jnp.where(in_group, logits, ninf)
    m1 = jnp.max(elog, axis=0, keepdims=True)
    e1 = first_argmax(elog, m1)
    elog2 = jnp.where(row == e1, ninf, elog)
    m2 = jnp.max(elog2, axis=0, keepdims=True)
    e2 = first_argmax(elog2, m2)
    zsum = jnp.sum(jnp.exp(elog - m1), axis=0, keepdims=True)
    p1 = 1.0 / zsum
    p2 = jnp.exp(m2 - m1) / zsum
    psum = p1 + p2
    w1 = p1 / psum * g_p
    w2 = p2 / psum * g_p

    oh1 = row == e1
    oh2 = row == e2
    onehot = jnp.logical_or(oh1, oh2).astype(F32)
    before = jnp.dot(onehot.astype(BF16), earlier_ref[...], preferred_element_type=F32) + carry[:, 0:1]
    r1 = jnp.sum(jnp.where(oh1, before, 0.0), axis=0, keepdims=True).astype(jnp.int32)
    r2 = jnp.sum(jnp.where(oh2, before, 0.0), axis=0, keepdims=True).astype(jnp.int32)
    carry[...] = carry[...] + jnp.sum(onehot, axis=1, keepdims=True)

    row8 = lax.broadcasted_iota(jnp.int32, (8, tm), 0)
    ids_ref[...] = jnp.where(row8 == 0, e1, jnp.where(row8 == 1, e2,
                             jnp.where(row8 == 2, r1, jnp.where(row8 == 3, r2, 0))))
    gates_ref[...] = jnp.where(row == 0, w1, jnp.where(row == 1, w2, 0.0)).T
    cnt_ref[...] = carry[...]


def _router(a, w_proj, res, ln_w, w_group, b_group, w_expert, b_expert, *, tm=512):
    t, d = res.shape
    k = a.shape[1]
    pad = LANES - N_EXPERTS - N_EXPERT_GROUPS
    w_r = jnp.concatenate([w_expert, w_group, jnp.zeros((d, pad), F32)], axis=1).astype(F32)
    b_r = jnp.concatenate([b_expert, b_group, jnp.zeros((pad,), F32)]).astype(F32).reshape(LANES, 1)
    earlier = (jnp.arange(tm)[:, None] < jnp.arange(tm)[None, :]).astype(BF16)
    return pl.pallas_call(
        _router_kernel,
        grid=(t // tm,),
        in_specs=[
            pl.BlockSpec((tm, k), lambda i: (i, 0)),
            pl.BlockSpec((k, d), lambda i: (0, 0)),
            pl.BlockSpec((tm, d), lambda i: (i, 0)),
            pl.BlockSpec((1, d), lambda i: (0, 0)),
            pl.BlockSpec((d, LANES), lambda i: (0, 0)),
            pl.BlockSpec((LANES, 1), lambda i: (0, 0)),
            pl.BlockSpec((tm, tm), lambda i: (0, 0)),
        ],
        out_specs=[
            pl.BlockSpec((tm, d), lambda i: (i, 0)),
            pl.BlockSpec((8, tm), lambda i: (0, i)),
            pl.BlockSpec((tm, LANES), lambda i: (i, 0)),
            pl.BlockSpec((LANES, LANES), lambda i: (0, 0)),
        ],
        out_shape=[
            jax.ShapeDtypeStruct((t, d), F32),
            jax.ShapeDtypeStruct((8, t), jnp.int32),
            jax.ShapeDtypeStruct((t, LANES), F32),
            jax.ShapeDtypeStruct((LANES, LANES), F32),
        ],
        scratch_shapes=[pltpu.VMEM((LANES, LANES), F32), pltpu.VMEM((2 * LANES, d), BF16)],
        compiler_params=_cparams(("arbitrary",)),
        name="proj_moe_router",
    )(a, w_proj, res, ln_w.reshape(1, d), w_r, b_r, earlier)


U32 = jnp.uint32
ROW_WORDS = D_MODEL // 2
ROW_SUB = ROW_WORDS // LANES
HIGH_HALF = 0xFFFF0000
ROW_MOVE_TOKENS = 512


def _pack_rows(v):
    lo = lax.bitcast_convert_type(v[:, :ROW_WORDS].astype(BF16).astype(F32), U32)
    hi = lax.bitcast_convert_type(v[:, ROW_WORDS:].astype(BF16).astype(F32), U32)
    return jnp.right_shift(lo, jnp.uint32(16)) | (hi & jnp.uint32(HIGH_HALF))


def _unpack_rows(w):
    lo = lax.bitcast_convert_type(jnp.left_shift(w, jnp.uint32(16)), F32)
    hi = lax.bitcast_convert_type(w & jnp.uint32(HIGH_HALF), F32)
    return lo, hi


def _store_rows(ref, packed):
    n = packed.shape[0]
    for c in range(ROW_SUB):
        ref[pl.ds(c, n, stride=ROW_SUB), :] = packed[:, c * LANES:(c + 1) * LANES]


def _load_rows(ref, n):
    return jnp.concatenate([ref[pl.ds(c, n, stride=ROW_SUB), :] for c in range(ROW_SUB)], axis=1)


def _row_slice(ref, row):
    return ref.at[pl.ds(pl.multiple_of(row * ROW_SUB, ROW_SUB), ROW_SUB)]


def _dispatch_kernel(zf_ref, dest_ref, x_ref, g_ref, xs_ref, hbuf, zbuf, sem, zsem):
    tb = x_ref.shape[0]
    n_blocks = zf_ref.shape[0]
    blk_rows = MOE_BLOCK * ROW_SUB

    @pl.when(pl.program_id(0) == 0)
    def _():
        zbuf[...] = jnp.zeros_like(zbuf)

        def zero_copy(b):
            start = pl.multiple_of(b * blk_rows, blk_rows)
            return pltpu.make_async_copy(zbuf, xs_ref.at[pl.ds(start, blk_rows)], zsem)

        def zissue(b, carry):
            @pl.when(zf_ref[b] != 0)
            def _():
                zero_copy(b).start()
            return carry

        def zdrain(b, carry):
            @pl.when(zf_ref[b] != 0)
            def _():
                zero_copy(b).wait()
            return carry

        lax.fori_loop(0, n_blocks, zissue, 0)
        lax.fori_loop(0, n_blocks, zdrain, 0)

    _store_rows(hbuf, _pack_rows(_rms(x_ref[...], g_ref[...])))
    for r in range(tb):
        for k in range(2):
            d = dest_ref[k, r]
            pltpu.make_async_copy(_row_slice(hbuf, r), _row_slice(xs_ref, d), sem).start(priority=k)
    for k in range(2):
        pltpu.make_async_copy(hbuf, xs_ref.at[pl.ds(0, tb * ROW_SUB)], sem).wait()


def _dispatch(x, ln_w, dest, zero_flag, n_slots, *, tb=ROW_MOVE_TOKENS):
    t, d = x.shape
    nblk = t // tb
    return pl.pallas_call(
        _dispatch_kernel,
        grid_spec=pltpu.PrefetchScalarGridSpec(
            num_scalar_prefetch=1,
            grid=(nblk,),
            in_specs=[
                pl.BlockSpec((2, tb), lambda i, zf: (0, i), memory_space=pltpu.SMEM),
                pl.BlockSpec((tb, d), lambda i, zf: (i, 0)),
                pl.BlockSpec((1, d), lambda i, zf: (0, 0)),
            ],
            out_specs=pl.BlockSpec(memory_space=pl.ANY),
            scratch_shapes=[pltpu.VMEM((tb * ROW_SUB, LANES), U32),
                            pltpu.VMEM((MOE_BLOCK * ROW_SUB, LANES), U32),
                            pltpu.SemaphoreType.DMA(()), pltpu.SemaphoreType.DMA(())],
        ),
        out_shape=jax.ShapeDtypeStruct((n_slots * ROW_SUB, LANES), U32),
        compiler_params=_cparams(("arbitrary",)),
        name="moe_dispatch",
    )(zero_flag, dest, x, ln_w.reshape(1, d))


def _expert_kernel(elist_ref, first_ref, nblk_ref, meta_ref, xs_hbm, wg_hbm, wu_hbm, wd_hbm, y_hbm,
                   wg_s, wu_s, wd_s, wg_f, wu_f, wd_f, xbuf, ybuf, wsem, xsem, ysem, *, layer, n_blocks):
    p = pl.program_id(0)
    n_list = meta_ref[0]
    active = p < n_list
    blk_rows = MOE_BLOCK * ROW_SUB
    streams = ((wg_hbm, wg_f), (wu_hbm, wu_f), (wd_hbm, wd_f))

    def w_copies(pos, slot):
        e = elist_ref[pos]
        return [pltpu.make_async_copy(hbm.at[layer, e], buf.at[slot], wsem.at[slot, t])
                for t, (hbm, buf) in enumerate(streams)]

    def block_rows(blk):
        return pl.ds(pl.multiple_of(blk * blk_rows, blk_rows), blk_rows)

    def x_copy(blk, slot):
        return pltpu.make_async_copy(xs_hbm.at[block_rows(blk)], xbuf.at[slot], xsem.at[slot])

    def y_copy(blk, slot):
        return pltpu.make_async_copy(ybuf.at[slot], y_hbm.at[block_rows(blk)], ysem.at[slot])

    @pl.when(jnp.logical_and(active, p == 0))
    def _():
        for cp in w_copies(0, 0):
            cp.start()

    @pl.when(active)
    def _():
        slot = lax.rem(p, 2)
        for cp in w_copies(p, slot):
            cp.wait()

        @pl.when(p + 1 < n_list)
        def _():
            for cp in w_copies(p + 1, 1 - slot):
                cp.start()

        wg_s[...] = wg_f[slot].astype(BF16)
        wu_s[...] = wu_f[slot].astype(BF16)
        wd_s[...] = wd_f[slot].astype(BF16)

        b0 = first_ref[p]
        nb = nblk_ref[p]
        x_copy(b0, 0).start()

        def block(j, carry):
            cur = lax.rem(j, 2)
            x_copy(b0 + j, cur).wait()

            @pl.when(j + 1 < nb)
            def _():
                x_copy(b0 + j + 1, 1 - cur).start()

            @pl.when(j >= 2)
            def _():
                y_copy(b0 + j - 2, cur).wait()

            lo, hi = _unpack_rows(_load_rows(xbuf.at[cur], MOE_BLOCK))
            xb = jnp.concatenate([lo.astype(BF16), hi.astype(BF16)], axis=1)
            gate = jnp.dot(xb, wg_s[...], preferred_element_type=F32)
            up = jnp.dot(xb, wu_s[...], preferred_element_type=F32)
            hid = (_silu(gate) * up).astype(BF16)
            _store_rows(ybuf.at[cur], _pack_rows(jnp.dot(hid, wd_s[...], preferred_element_type=F32)))
            y_copy(b0 + j, cur).start()
            return carry

        lax.fori_loop(0, nb, block, 0)

        @pl.when(nb >= 2)
        def _():
            y_copy(b0 + nb - 2, lax.rem(nb, 2)).wait()
        y_copy(b0 + nb - 1, lax.rem(nb - 1, 2)).wait()

    @pl.when(p == pl.num_programs(0) - 1)
    def _():
        ybuf[0] = jnp.zeros(ybuf.shape[1:], ybuf.dtype)

        def zissue(b, carry):
            y_copy(b, 0).start()
            return carry

        def zdrain(b, carry):
            y_copy(b, 0).wait()
            return carry

        lax.fori_loop(meta_ref[1], n_blocks, zissue, 0)
        lax.fori_loop(meta_ref[1], n_blocks, zdrain, 0)


def _experts(xs, expert_list, first_block, num_blocks, meta, w_gate, w_up, w_down, layer):
    blk_rows = MOE_BLOCK * ROW_SUB
    n_blocks = xs.shape[0] // blk_rows
    d = D_MODEL
    return pl.pallas_call(
        functools.partial(_expert_kernel, layer=layer, n_blocks=n_blocks),
        grid_spec=pltpu.PrefetchScalarGridSpec(
            num_scalar_prefetch=4,
            grid=(N_EXPERTS,),
            in_specs=[pl.BlockSpec(memory_space=pl.ANY)] * 4,
            out_specs=pl.BlockSpec(memory_space=pl.ANY),
            scratch_shapes=[
                pltpu.VMEM((d, EXPERT_FF), BF16),
                pltpu.VMEM((d, EXPERT_FF), BF16),
                pltpu.VMEM((EXPERT_FF, d), BF16),
                pltpu.VMEM((2, d, EXPERT_FF), F32),
                pltpu.VMEM((2, d, EXPERT_FF), F32),
                pltpu.VMEM((2, EXPERT_FF, d), F32),
                pltpu.VMEM((2, blk_rows, LANES), U32),
                pltpu.VMEM((2, blk_rows, LANES), U32),
                pltpu.SemaphoreType.DMA((2, 3)),
                pltpu.SemaphoreType.DMA((2,)),
                pltpu.SemaphoreType.DMA((2,)),
            ],
        ),
        out_shape=jax.ShapeDtypeStruct(xs.shape, U32),
        compiler_params=_cparams(("arbitrary",)),
        name="moe_experts",
    )(expert_list, first_block, num_blocks, meta, xs, w_gate, w_up, w_down)


def _combine_kernel(dest_ref, x_ref, gates_ref, y_ref, *rest, final_norm):
    if final_norm:
        g_ref, o_ref, buf0, buf1, sem = rest
    else:
        o_ref, buf0, buf1, sem = rest
    tb = x_ref.shape[0]
    bufs = (buf0, buf1)

    for r in range(tb):
        for k in range(2):
            d = dest_ref[k, r]
            pltpu.make_async_copy(_row_slice(y_ref, d), _row_slice(bufs[k], r), sem).start(priority=k)
    for k in range(2):
        pltpu.make_async_copy(y_ref.at[pl.ds(0, tb * ROW_SUB)], bufs[k], sem).wait()

    gt = gates_ref[...]
    lo0, hi0 = _unpack_rows(_load_rows(buf0, tb))
    lo1, hi1 = _unpack_rows(_load_rows(buf1, tb))
    w0 = gt[:, 0:1]
    w1 = gt[:, 1:2]
    out = x_ref[...] + jnp.concatenate([lo0 * w0 + lo1 * w1, hi0 * w0 + hi1 * w1], axis=1)
    if final_norm:
        out = _rms(out, g_ref[...])
    o_ref[...] = out


def _combine(x, gates, y, dest, final_w=None, *, tb=ROW_MOVE_TOKENS):
    t, d = x.shape
    nblk = t // tb
    final_norm = final_w is not None
    in_specs = [
        pl.BlockSpec((2, tb), lambda i: (0, i), memory_space=pltpu.SMEM),
        pl.BlockSpec((tb, d), lambda i: (i, 0)),
        pl.BlockSpec((tb, LANES), lambda i: (i, 0)),
        pl.BlockSpec(memory_space=pl.ANY),
    ]
    args = [dest, x, gates, y]
    if final_norm:
        in_specs.append(pl.BlockSpec((1, d), lambda i: (0, 0)))
        args.append(final_w.reshape(1, d))
    return pl.pallas_call(
        functools.partial(_combine_kernel, final_norm=final_norm),
        grid=(nblk,),
        in_specs=in_specs,
        out_specs=pl.BlockSpec((tb, d), lambda i: (i, 0)),
        out_shape=jax.ShapeDtypeStruct((t, d), F32),
        scratch_shapes=[pltpu.VMEM((tb * ROW_SUB, LANES), U32), pltpu.VMEM((tb * ROW_SUB, LANES), U32),
                        pltpu.SemaphoreType.DMA(())],
        compiler_params=_cparams(("arbitrary",)),
        name="moe_combine",
    )(*args)


def _proj_moe(a, w_proj, res, ln_w, w_group, b_group, w_expert, b_expert, w_gate, w_up, w_down, layer,
              final_w=None):
    t = res.shape[0]
    n_assign = t * 2
    n_blocks = -(-n_assign // MOE_BLOCK) + N_EXPERTS
    n_slots = n_blocks * MOE_BLOCK

    x, ids, gates, cnt = _router(a, w_proj, res, ln_w, w_group, b_group, w_expert, b_expert)
    counts = cnt[:N_EXPERTS, 0].astype(jnp.int32)
    padded = (counts + MOE_BLOCK - 1) // MOE_BLOCK * MOE_BLOCK
    p_ends = jnp.cumsum(padded)
    p_starts = p_ends - padded
    experts = jnp.arange(N_EXPERTS, dtype=jnp.int32)
    start_of = jnp.sum(jnp.where(ids[0:2, :, None] == experts, p_starts, 0), axis=-1)
    dest = (start_of + ids[2:4]).astype(jnp.int32)
    block_start = jnp.arange(n_blocks, dtype=jnp.int32) * MOE_BLOCK
    block_e = jnp.minimum(jnp.sum((p_ends[None, :] <= block_start[:, None]).astype(jnp.int32), axis=1),
                          N_EXPERTS - 1)
    n_active = (p_ends[-1:] // MOE_BLOCK).astype(jnp.int32)
    valid_end = jnp.sum(jnp.where(block_e[:, None] == experts, p_starts + counts, 0), axis=-1)
    zero_flag = jnp.logical_or(block_start + MOE_BLOCK > valid_end,
                               block_start >= p_ends[-1]).astype(jnp.int32)

    has_tokens = counts > 0
    list_pos = jnp.cumsum(has_tokens.astype(jnp.int32)) - 1
    expert_list = jnp.sum(jnp.where(jnp.logical_and(has_tokens[None, :], list_pos[None, :] == experts[:, None]),
                                    experts[None, :], 0), axis=1).astype(jnp.int32)
    listed = expert_list[:, None] == experts[None, :]
    first_block = jnp.sum(jnp.where(listed, p_starts // MOE_BLOCK, 0), axis=1).astype(jnp.int32)
    num_blocks = jnp.sum(jnp.where(listed, padded // MOE_BLOCK, 0), axis=1).astype(jnp.int32)
    meta = jnp.concatenate([jnp.sum(has_tokens.astype(jnp.int32), keepdims=True), n_active]).astype(jnp.int32)

    xs = _dispatch(x, ln_w, dest, zero_flag, n_slots)
    y = _experts(xs, expert_list, first_block, num_blocks, meta, w_gate, w_up, w_down, layer)
    return _combine(x, gates, y, dest, final_w)


def kernel(x, rel_bias, ln_mix, ln_ffn, ln_final, attn_w_qkv, attn_sinks, attn_w_o, ssm_w_in,
           ssm_conv_w, ssm_conv_b, ssm_dt_bias, ssm_a_log, ssm_d, ssm_norm_w, ssm_w_out,
           moe_w_group, moe_b_group, moe_w_expert, moe_b_expert, moe_w_gate, moe_w_up, moe_w_down):
    batch, seq, d = x.shape
    t = batch * seq
    xf = x.reshape(t, d).astype(F32)

    q_scale = jnp.where(jnp.arange(QKV_DIM) < Q_DIM, HEAD_DIM ** -0.5 * LOG2E, 1.0).astype(F32)
    qkv = _norm_matmul(xf, ln_mix[0], (attn_w_qkv[0] * q_scale).astype(BF16), tm=512, tn=QKV_DIM)
    att = _attention(qkv, attn_sinks[0], rel_bias, batch, seq)
    xf = _proj_moe(att, attn_w_o[0].astype(BF16), xf, ln_ffn[0], moe_w_group[0], moe_b_group[0],
                   moe_w_expert[0], moe_b_expert[0], moe_w_gate, moe_w_up, moe_w_down, 0)

    w_in = ssm_w_in[0]
    w_dt = jnp.pad(w_in[:, ZXBC_DIM:], ((0, 0), (0, LANES - SSM_HEADS))).astype(BF16)
    zxbc, dt_raw = _norm_matmul(xf, ln_mix[1], w_in[:, :ZXBC_DIM].astype(BF16), w_dt, tm=512, tn=ZXBC_DIM)
    gated = _ssd(zxbc, dt_raw, ssm_conv_w[0], ssm_conv_b[0], ssm_dt_bias[0], ssm_a_log[0],
                 ssm_d[0], ssm_norm_w[0], batch, seq)
    xf = _proj_moe(gated, ssm_w_out[0].astype(BF16), xf, ln_ffn[1], moe_w_group[1], moe_b_group[1],
                   moe_w_expert[1], moe_b_expert[1], moe_w_gate, moe_w_up, moe_w_down, 1, final_w=ln_final)
    return xf.reshape(batch, seq, d).astype(x.dtype)
```

```python
import functools
import math

import jax
import jax.numpy as jnp
from jax import lax
from jax.experimental import pallas as pl
from jax.experimental.pallas import tpu as pltpu

F32 = jnp.float32
BF16 = jnp.bfloat16

D_MODEL = 1024
N_HEADS = 16
N_KV_HEADS = 4
HEAD_DIM = 64
GQA_GROUP = N_HEADS // N_KV_HEADS
WINDOW = 128
ATTN_BLOCK = 128
Q_DIM = N_HEADS * HEAD_DIM
KV_DIM = N_KV_HEADS * HEAD_DIM
QKV_DIM = Q_DIM + 2 * KV_DIM
REL_BUCKETS = 32
REL_MAX_DIST = 128

D_INNER = 2048
SSM_HEAD_DIM = 64
SSM_HEADS = D_INNER // SSM_HEAD_DIM
SSM_GROUPS = 4
D_STATE = 128
CONV_WIDTH = 4
BC_DIM = 2 * SSM_GROUPS * D_STATE
CONV_DIM = D_INNER + BC_DIM
ZXBC_DIM = D_INNER + CONV_DIM
SSM_CHUNK = 128
SSD_STEP_CHUNKS = 4
GROUP_CH = D_INNER // SSM_GROUPS

N_EXPERT_GROUPS = 8
EXPERTS_PER_GROUP = 8
N_EXPERTS = 64
EXPERT_FF = 512
MOE_BLOCK = 256

NORM_EPS = 1e-6
LOG2E = 1.4426950408889634
CONV_TAIL = 16
LANES = 128
NEG_BIG = -1e30
VMEM_LIMIT = 56 * 1024 * 1024


def _cparams(sem, flags=None):
    return pltpu.CompilerParams(dimension_semantics=sem, vmem_limit_bytes=VMEM_LIMIT, flags=flags)


def _rms(x, g):
    ms = jnp.mean(x * x, axis=-1, keepdims=True)
    return x * lax.rsqrt(ms + NORM_EPS) * g


def _silu(x):
    h = 0.5 * x
    return h + h * jnp.tanh(h)


def _norm_matmul_kernel(x_ref, g_ref, w_ref, *rest, has_aux):
    if has_aux:
        wa_ref, o_ref, oa_ref, h_scr = rest
    else:
        o_ref, h_scr = rest

    @pl.when(pl.program_id(1) == 0)
    def _():
        h = _rms(x_ref[...], g_ref[...]).astype(BF16)
        h_scr[...] = h
        if has_aux:
            oa_ref[...] = jnp.dot(h, wa_ref[...], preferred_element_type=F32)

    o_ref[...] = jnp.dot(h_scr[...], w_ref[...], preferred_element_type=F32).astype(o_ref.dtype)


def _norm_matmul(x, g, w, w_aux=None, *, tm=1024, tn=512):
    t, d = x.shape
    n = w.shape[1]
    has_aux = w_aux is not None
    in_specs = [
        pl.BlockSpec((tm, d), lambda i, j: (i, 0)),
        pl.BlockSpec((1, d), lambda i, j: (0, 0)),
        pl.BlockSpec((d, tn), lambda i, j: (0, j)),
    ]
    out_shape = [jax.ShapeDtypeStruct((t, n), BF16)]
    out_specs = [pl.BlockSpec((tm, tn), lambda i, j: (i, j))]
    args = [x, g.reshape(1, d), w]
    if has_aux:
        na = w_aux.shape[1]
        in_specs.append(pl.BlockSpec((d, na), lambda i, j: (0, 0)))
        out_shape.append(jax.ShapeDtypeStruct((t, na), F32))
        out_specs.append(pl.BlockSpec((tm, na), lambda i, j: (i, 0)))
        args.append(w_aux)
    res = pl.pallas_call(
        functools.partial(_norm_matmul_kernel, has_aux=has_aux),
        grid=(t // tm, n // tn),
        in_specs=in_specs,
        out_specs=out_specs,
        out_shape=out_shape,
        scratch_shapes=[pltpu.VMEM((tm, d), BF16)],
        compiler_params=_cparams(("parallel", "arbitrary")),
        name="norm_matmul",
    )(*args)
    return res if has_aux else res[0]


def _matmul_residual_kernel(a_ref, w_ref, r_ref, o_ref):
    o_ref[...] = r_ref[...] + jnp.dot(a_ref[...], w_ref[...], preferred_element_type=F32)


def _matmul_residual(a, w, res, *, tm=512):
    t, k = a.shape
    n = w.shape[1]
    return pl.pallas_call(
        _matmul_residual_kernel,
        grid=(t // tm,),
        in_specs=[
            pl.BlockSpec((tm, k), lambda i: (i, 0)),
            pl.BlockSpec((k, n), lambda i: (0, 0)),
            pl.BlockSpec((tm, n), lambda i: (i, 0)),
        ],
        out_specs=pl.BlockSpec((tm, n), lambda i: (i, 0)),
        out_shape=jax.ShapeDtypeStruct((t, n), F32),
        compiler_params=_cparams(("parallel",)),
        name="matmul_residual",
    )(a, w, res)


def _attn_kernel(sinks_ref, q_ref, kp_ref, kc_ref, vp_ref, vc_ref, bias_ref, o_ref):
    table = jnp.minimum(pl.program_id(1), 1)
    for h in range(N_KV_HEADS):
        ks = slice(h * HEAD_DIM, (h + 1) * HEAD_DIM)
        kb = jnp.concatenate([kp_ref[:, ks], kc_ref[:, ks]], axis=0)
        vb = jnp.concatenate([vp_ref[:, ks], vc_ref[:, ks]], axis=0)
        for g in range(GQA_GROUP):
            hh = h * GQA_GROUP + g
            qh = q_ref[:, hh * HEAD_DIM:(hh + 1) * HEAD_DIM]
            s = lax.dot_general(qh, kb, (((1,), (1,)), ((), ())), preferred_element_type=F32)
            logits = s + bias_ref[table, hh]
            sink = sinks_ref[hh]
            m = jnp.maximum(jnp.max(logits, axis=-1, keepdims=True), sink)
            p = jnp.exp2(logits - m)
            denom = jnp.sum(p, axis=-1, keepdims=True) + jnp.exp2(sink - m)
            o = jnp.dot(p.astype(BF16), vb, preferred_element_type=F32) / denom
            o_ref[:, hh * HEAD_DIM:(hh + 1) * HEAD_DIM] = o.astype(o_ref.dtype)


def _t5_causal_bucket(dist):
    n = jnp.maximum(dist, 0)
    max_exact = REL_BUCKETS // 2
    nf = jnp.maximum(n, 1).astype(F32)
    large = max_exact + (jnp.log(nf / max_exact) / math.log(REL_MAX_DIST / max_exact)
                         * (REL_BUCKETS - max_exact)).astype(jnp.int32)
    large = jnp.minimum(large, REL_BUCKETS - 1)
    return jnp.where(n < max_exact, n, large)


def _attention(qkv, sinks, rel_bias, batch, seq):
    t = qkv.shape[0]
    nb = seq // ATTN_BLOCK
    qi = jnp.arange(ATTN_BLOCK)[:, None]
    ki = jnp.arange(2 * ATTN_BLOCK)[None, :]
    dist = qi + ATTN_BLOCK - ki
    in_window = (dist >= 0) & (dist < WINDOW)
    onehot = (_t5_causal_bucket(dist)[None] == jnp.arange(REL_BUCKETS)[:, None, None]).astype(F32)
    bias = jnp.einsum('hr,rqk->hqk', rel_bias.astype(F32).T, onehot, precision=lax.Precision.HIGHEST)
    bias = jnp.where(in_window[None], bias * LOG2E, NEG_BIG)
    bias_first = jnp.where((ki < ATTN_BLOCK)[None], NEG_BIG, bias)
    bias2 = jnp.stack([bias_first, bias])

    kcol = Q_DIM // KV_DIM
    vcol = kcol + 1

    def prev(b, n):
        return b * nb + jnp.maximum(n - 1, 0)

    return pl.pallas_call(
        _attn_kernel,
        grid=(batch, nb),
        in_specs=[
            pl.BlockSpec(memory_space=pltpu.SMEM),
            pl.BlockSpec((ATTN_BLOCK, Q_DIM), lambda b, n: (b * nb + n, 0)),
            pl.BlockSpec((ATTN_BLOCK, KV_DIM), lambda b, n: (prev(b, n), kcol)),
            pl.BlockSpec((ATTN_BLOCK, KV_DIM), lambda b, n: (b * nb + n, kcol)),
            pl.BlockSpec((ATTN_BLOCK, KV_DIM), lambda b, n: (prev(b, n), vcol)),
            pl.BlockSpec((ATTN_BLOCK, KV_DIM), lambda b, n: (b * nb + n, vcol)),
            pl.BlockSpec((2, N_HEADS, ATTN_BLOCK, 2 * ATTN_BLOCK), lambda b, n: (0, 0, 0, 0)),
        ],
        out_specs=pl.BlockSpec((ATTN_BLOCK, Q_DIM), lambda b, n: (b * nb + n, 0)),
        out_shape=jax.ShapeDtypeStruct((t, Q_DIM), BF16),
        compiler_params=_cparams(("parallel", "arbitrary")),
        name="swa_attention",
    )((sinks.astype(F32) * LOG2E), qkv, qkv, qkv, qkv, qkv, bias2)


def _split3(v):
    hi = v.astype(BF16)
    r = v - hi.astype(F32)
    mid = r.astype(BF16)
    lo = (r - mid.astype(F32)).astype(BF16)
    return hi, mid, lo


def _ssd_kernel(z_ref, x_ref, bc_ref, dt_ref, convw_ref, convb_ref, dtb_ref, alog_ref,
                dskip_ref, normw_ref, shift_ref, shift_tail_ref, expand_ref,
                o_ref, state, xtail, bctail):
    @pl.when(pl.program_id(1) == 0)
    def _():
        state[...] = jnp.zeros_like(state)
        xtail[...] = jnp.zeros_like(xtail)
        bctail[...] = jnp.zeros_like(bctail)

    for k in range(z_ref.shape[0] // SSM_CHUNK):
        _ssd_chunk(pl.ds(k * SSM_CHUNK, SSM_CHUNK), z_ref, x_ref, bc_ref, dt_ref, convw_ref, convb_ref,
                   dtb_ref, alog_ref, dskip_ref, normw_ref, shift_ref, shift_tail_ref, expand_ref,
                   o_ref, state, xtail, bctail)


def _ssd_chunk(rows, z_ref, x_ref, bc_ref, dt_ref, convw_ref, convb_ref, dtb_ref, alog_ref,
               dskip_ref, normw_ref, shift_ref, shift_tail_ref, expand_ref,
               o_ref, state, xtail, bctail):
    L = SSM_CHUNK

    def conv_silu(u_ref, tail, w_lo, width):
        cur = u_ref[rows, :]
        prev = tail[...]
        w = [convw_ref[k:k + 1, w_lo:w_lo + width].astype(BF16) for k in range(CONV_WIDTH)]
        taps = jnp.concatenate([cur * w[k] for k in range(CONV_WIDTH)], axis=0)
        taps_prev = jnp.concatenate([prev * w[k] for k in range(CONV_WIDTH)], axis=0)
        y = (jnp.dot(shift_ref[...], taps, preferred_element_type=F32)
             + jnp.dot(shift_tail_ref[...], taps_prev, preferred_element_type=F32)
             + convb_ref[:, w_lo:w_lo + width])
        tail[...] = cur[L - CONV_TAIL:L]
        return _silu(y)

    xs = conv_silu(x_ref, xtail, 0, D_INNER)
    bcm = conv_silu(bc_ref, bctail, D_INNER, BC_DIM)

    xdt = dt_ref[rows, :] + dtb_ref[...]
    dt = jnp.maximum(xdt, 0.0) + jnp.log1p(jnp.exp(-jnp.abs(xdt)))
    a = dt * (-jnp.exp(alog_ref[...]) * LOG2E)
    row = lax.broadcasted_iota(jnp.int32, (L, L), 0)
    colm = lax.broadcasted_iota(jnp.int32, (L, L), 1)
    causal = row >= colm
    cs3 = jnp.dot(causal.astype(BF16), jnp.concatenate(_split3(a), axis=1), preferred_element_type=F32)
    a_cs = cs3[:, :LANES] + cs3[:, LANES:2 * LANES] + cs3[:, 2 * LANES:]
    a_cs_t = a_cs.T
    dt_t = dt.T
    a_last_col = a_cs_t[:, L - 1:L]
    w_end_t = jnp.exp2(a_last_col - a_cs_t) * dt_t
    exp_acs = jnp.exp2(a_cs)

    cd3 = [jnp.broadcast_to(part.astype(F32), (16, LANES)).astype(BF16)
           for part in _split3(exp_acs[L - 1:L, :])]
    cdx = jnp.dot(jnp.concatenate(cd3, axis=0), expand_ref[...], preferred_element_type=F32)
    cd_exp = cdx[0:1] + cdx[16:17] + cdx[32:33]

    lane = lax.broadcasted_iota(jnp.int32, (L, LANES), 1)
    low_half = lane < SSM_HEAD_DIM

    y_tiles = []
    for g in range(SSM_GROUPS):
        b_g = bcm[:, g * D_STATE:(g + 1) * D_STATE]
        c_g = bcm[:, (SSM_GROUPS + g) * D_STATE:(SSM_GROUPS + g + 1) * D_STATE]
        b_gb = b_g.astype(BF16)
        c_gb = c_g.astype(BF16)
        cb = lax.dot_general(c_gb, b_gb, (((1,), (1,)), ((), ())), preferred_element_type=F32)
        b_gt = b_g.T
        s_g = state[:, g * GROUP_CH:(g + 1) * GROUP_CH]
        y_off_g = jnp.dot(c_gb, s_g.astype(BF16), preferred_element_type=F32)
        for jj in range(GROUP_CH // LANES):
            j = g * (GROUP_CH // LANES) + jj
            lhs_top = []
            lhs_bot = []
            escale = []
            for e in (2 * j, 2 * j + 1):
                acs_b = jnp.broadcast_to(a_cs[:, e:e + 1], (L, L))
                seg = acs_b - a_cs_t[e:e + 1, :]
                dec = jnp.exp2(jnp.where(causal, seg, -jnp.inf))
                lhs_top.append((cb * dec * dt_t[e:e + 1, :]).astype(BF16))
                lhs_bot.append((b_gt * w_end_t[e:e + 1, :]).astype(BF16))
                escale.append(jnp.broadcast_to(exp_acs[:, e:e + 1], (L, LANES)))
            x_tile = xs[:, j * LANES:(j + 1) * LANES]
            x_lo = jnp.where(low_half, x_tile, 0.0).astype(BF16)
            x_hi = jnp.where(low_half, 0.0, x_tile).astype(BF16)
            lhs = jnp.concatenate([jnp.concatenate(lhs_top, axis=1),
                                   jnp.concatenate(lhs_bot, axis=1)], axis=0)
            rhs = jnp.concatenate([x_lo, x_hi], axis=0)
            r = jnp.dot(lhs, rhs, preferred_element_type=F32)
            y_off = y_off_g[:, jj * LANES:(jj + 1) * LANES]
            y_tiles.append(r[0:L] + jnp.where(low_half, escale[0], escale[1]) * y_off)
            sl = slice(j * LANES, (j + 1) * LANES)
            state[:, sl] = cd_exp[:, sl] * state[:, sl] + r[L:2 * L]

    y = jnp.concatenate(y_tiles, axis=1)
    y = y + xs * dskip_ref[...]
    gated = y * _silu(z_ref[rows, :].astype(F32))
    outs = []
    for g in range(SSM_GROUPS):
        gg = gated[:, g * GROUP_CH:(g + 1) * GROUP_CH]
        ms = jnp.mean(gg * gg, axis=-1, keepdims=True)
        outs.append(gg * lax.rsqrt(ms + NORM_EPS))
    o_ref[rows, :] = (jnp.concatenate(outs, axis=1) * normw_ref[...]).astype(o_ref.dtype)


def _ssd(zxbc, dt_raw, conv_w, conv_b, dt_bias, a_log, d_skip, norm_w, batch, seq):
    t = zxbc.shape[0]
    L = SSM_CHUNK
    rows = SSD_STEP_CHUNKS * L
    nc = seq // rows

    def pad_heads(v):
        return jnp.pad(v.astype(F32), (0, LANES - SSM_HEADS)).reshape(1, LANES)

    def rowmap(col):
        return lambda b, c: (b * nc + c, col)

    def const2(b, c):
        return (0, 0)

    l_idx = jnp.arange(L)[:, None]
    shift = jnp.concatenate(
        [(jnp.arange(L)[None, :] == l_idx - (CONV_WIDTH - 1) + k) for k in range(CONV_WIDTH)],
        axis=1).astype(BF16)
    shift_tail = jnp.concatenate(
        [(jnp.arange(-CONV_TAIL, 0)[None, :] == l_idx - (CONV_WIDTH - 1) + k) for k in range(CONV_WIDTH)],
        axis=1).astype(BF16)
    expand = (jnp.arange(D_INNER)[None, :] // SSM_HEAD_DIM == jnp.arange(LANES)[:, None]).astype(BF16)

    return pl.pallas_call(
        _ssd_kernel,
        grid=(batch, nc),
        in_specs=[
            pl.BlockSpec((rows, D_INNER), rowmap(0)),
            pl.BlockSpec((rows, D_INNER), rowmap(1)),
            pl.BlockSpec((rows, BC_DIM), rowmap(2 * D_INNER // BC_DIM)),
            pl.BlockSpec((rows, LANES), rowmap(0)),
            pl.BlockSpec((CONV_WIDTH, CONV_DIM), const2),
            pl.BlockSpec((1, CONV_DIM), const2),
            pl.BlockSpec((1, LANES), const2),
            pl.BlockSpec((1, LANES), const2),
            pl.BlockSpec((1, D_INNER), const2),
            pl.BlockSpec((1, D_INNER), const2),
            pl.BlockSpec((L, CONV_WIDTH * L), const2),
            pl.BlockSpec((L, CONV_WIDTH * CONV_TAIL), const2),
            pl.BlockSpec((LANES, D_INNER), const2),
        ],
        out_specs=pl.BlockSpec((rows, D_INNER), rowmap(0)),
        out_shape=jax.ShapeDtypeStruct((t, D_INNER), BF16),
        scratch_shapes=[
            pltpu.VMEM((D_STATE, D_INNER), F32),
            pltpu.VMEM((CONV_TAIL, D_INNER), BF16),
            pltpu.VMEM((CONV_TAIL, BC_DIM), BF16),
        ],
        compiler_params=_cparams(("parallel", "arbitrary")),
        name="ssd_core",
    )(zxbc, zxbc, zxbc, dt_raw, conv_w.astype(F32), conv_b.astype(F32).reshape(1, CONV_DIM),
      pad_heads(dt_bias), pad_heads(a_log),
      jnp.repeat(d_skip.astype(F32), SSM_HEAD_DIM).reshape(1, D_INNER),
      norm_w.astype(F32).reshape(1, D_INNER), shift, shift_tail, expand)


GROUP_ROW0 = N_EXPERTS


def _router_kernel(a_ref, wp_ref, res_ref, g_ref, w_ref, b_ref, earlier_ref,
                   x_ref, ids_ref, gates_ref, cnt_ref, carry, wt_split):
    i = pl.program_id(0)
    x_ref[...] = res_ref[...] + jnp.dot(a_ref[...], wp_ref[...], preferred_element_type=F32)

    @pl.when(i == 0)
    def _():
        carry[...] = jnp.zeros_like(carry)
        wt = w_ref[...].T
        wt_hi = wt.astype(BF16)
        wt_split[0:LANES, :] = wt_hi
        wt_split[LANES:, :] = (wt - wt_hi.astype(F32)).astype(BF16)

    h = _rms(x_ref[...], g_ref[...])
    h_hi = h.astype(BF16)
    h_mid = (h - h_hi.astype(F32)).astype(BF16)
    nt = (((1,), (1,)), ((), ()))
    top = lax.dot_general(wt_split[...], h_hi, nt, preferred_element_type=F32)
    cross = lax.dot_general(wt_split[0:LANES, :], h_mid, nt, preferred_element_type=F32)
    logits = top[:LANES] + (top[LANES:] + cross) + b_ref[...]
    tm = logits.shape[1]
    row = lax.broadcasted_iota(jnp.int32, (LANES, tm), 0)
    ninf = -jnp.inf

    def first_argmax(v, vmax):
        return jnp.min(jnp.where(v == vmax, row, LANES), axis=0, keepdims=True)

    is_group = (row >= GROUP_ROW0) & (row < GROUP_ROW0 + N_EXPERT_GROUPS)
    glog = jnp.where(is_group, logits, ninf)
    gmax = jnp.max(glog, axis=0, keepdims=True)
    g_idx = first_argmax(glog, gmax) - GROUP_ROW0
    g_p = 1.0 / jnp.sum(jnp.exp(glog - gmax), axis=0, keepdims=True)

    in_group = (row < N_EXPERTS) & (jnp.right_shift(row, 3) == g_idx)
    elog = jnp.where(in_group, logits, ninf)
    m1 = jnp.max(elog, axis=0, keepdims=True)
    e1 = first_argmax(elog, m1)
    elog2 = jnp.where(row == e1, ninf, elog)
    m2 = jnp.max(elog2, axis=0, keepdims=True)
    e2 = first_argmax(elog2, m2)
    zsum = jnp.sum(jnp.exp(elog - m1), axis=0, keepdims=True)
    p1 = 1.0 / zsum
    p2 = jnp.exp(m2 - m1) / zsum
    psum = p1 + p2
    w1 = p1 / psum * g_p
    w2 = p2 / psum * g_p

    oh1 = row == e1
    oh2 = row == e2
    onehot = jnp.logical_or(oh1, oh2).astype(F32)
    before = jnp.dot(onehot.astype(BF16), earlier_ref[...], preferred_element_type=F32) + carry[:, 0:1]
    r1 = jnp.sum(jnp.where(oh1, before, 0.0), axis=0, keepdims=True).astype(jnp.int32)
    r2 = jnp.sum(jnp.where(oh2, before, 0.0), axis=0, keepdims=True).astype(jnp.int32)
    carry[...] = carry[...] + jnp.sum(onehot, axis=1, keepdims=True)

    row8 = lax.broadcasted_iota(jnp.int32, (8, tm), 0)
    ids_ref[...] = jnp.where(row8 == 0, e1, jnp.where(row8 == 1, e2,
                             jnp.where(row8 == 2, r1, jnp.where(row8 == 3, r2, 0))))
    gates_ref[...] = jnp.where(row == 0, w1, jnp.where(row == 1, w2, 0.0)).T
    cnt_ref[...] = carry[...]


def _router(a, w_proj, res, ln_w, w_group, b_group, w_expert, b_expert, *, tm=512):
    t, d = res.shape
    k = a.shape[1]
    pad = LANES - N_EXPERTS - N_EXPERT_GROUPS
    w_r = jnp.concatenate([w_expert, w_group, jnp.zeros((d, pad), F32)], axis=1).astype(F32)
    b_r = jnp.concatenate([b_expert, b_group, jnp.zeros((pad,), F32)]).astype(F32).reshape(LANES, 1)
    earlier = (jnp.arange(tm)[:, None] < jnp.arange(tm)[None, :]).astype(BF16)
    return pl.pallas_call(
        _router_kernel,
        grid=(t // tm,),
        in_specs=[
            pl.BlockSpec((tm, k), lambda i: (i, 0)),
            pl.BlockSpec((k, d), lambda i: (0, 0)),
            pl.BlockSpec((tm, d), lambda i: (i, 0)),
            pl.BlockSpec((1, d), lambda i: (0, 0)),
            pl.BlockSpec((d, LANES), lambda i: (0, 0)),
            pl.BlockSpec((LANES, 1), lambda i: (0, 0)),
            pl.BlockSpec((tm, tm), lambda i: (0, 0)),
        ],
        out_specs=[
            pl.BlockSpec((tm, d), lambda i: (i, 0)),
            pl.BlockSpec((8, tm), lambda i: (0, i)),
            pl.BlockSpec((tm, LANES), lambda i: (i, 0)),
            pl.BlockSpec((LANES, LANES), lambda i: (0, 0)),
        ],
        out_shape=[
            jax.ShapeDtypeStruct((t, d), F32),
            jax.ShapeDtypeStruct((8, t), jnp.int32),
            jax.ShapeDtypeStruct((t, LANES), F32),
            jax.ShapeDtypeStruct((LANES, LANES), F32),
        ],
        scratch_shapes=[pltpu.VMEM((LANES, LANES), F32), pltpu.VMEM((2 * LANES, d), BF16)],
        compiler_params=_cparams(("arbitrary",)),
        name="proj_moe_router",
    )(a, w_proj, res, ln_w.reshape(1, d), w_r, b_r, earlier)


U32 = jnp.uint32
ROW_WORDS = D_MODEL // 2
ROW_SUB = ROW_WORDS // LANES
HIGH_HALF = 0xFFFF0000
ROW_MOVE_TOKENS = 512


def _pack_rows(v):
    lo = lax.bitcast_convert_type(v[:, :ROW_WORDS].astype(BF16).astype(F32), U32)
    hi = lax.bitcast_convert_type(v[:, ROW_WORDS:].astype(BF16).astype(F32), U32)
    return jnp.right_shift(lo, jnp.uint32(16)) | (hi & jnp.uint32(HIGH_HALF))


def _unpack_rows(w):
    lo = lax.bitcast_convert_type(jnp.left_shift(w, jnp.uint32(16)), F32)
    hi = lax.bitcast_convert_type(w & jnp.uint32(HIGH_HALF), F32)
    return lo, hi


def _store_rows(ref, packed):
    n = packed.shape[0]
    for c in range(ROW_SUB):
        ref[pl.ds(c, n, stride=ROW_SUB), :] = packed[:, c * LANES:(c + 1) * LANES]


def _load_rows(ref, n):
    return jnp.concatenate([ref[pl.ds(c, n, stride=ROW_SUB), :] for c in range(ROW_SUB)], axis=1)


def _row_slice(ref, row):
    return ref.at[pl.ds(pl.multiple_of(row * ROW_SUB, ROW_SUB), ROW_SUB)]


def _dispatch_kernel(zf_ref, dest_ref, x_ref, g_ref, xs_ref, hbuf, zbuf, sem, zsem):
    tb = x_ref.shape[0]
    n_blocks = zf_ref.shape[0]
    blk_rows = MOE_BLOCK * ROW_SUB

    @pl.when(pl.program_id(0) == 0)
    def _():
        zbuf[...] = jnp.zeros_like(zbuf)

        def zero_copy(b):
            start = pl.multiple_of(b * blk_rows, blk_rows)
            return pltpu.make_async_copy(zbuf, xs_ref.at[pl.ds(start, blk_rows)], zsem)

        def zissue(b, carry):
            @pl.when(zf_ref[b] != 0)
            def _():
                zero_copy(b).start()
            return carry

        def zdrain(b, carry):
            @pl.when(zf_ref[b] != 0)
            def _():
                zero_copy(b).wait()
            return carry

        lax.fori_loop(0, n_blocks, zissue, 0)
        lax.fori_loop(0, n_blocks, zdrain, 0)

    _store_rows(hbuf, _pack_rows(_rms(x_ref[...], g_ref[...])))
    for r in range(tb):
        for k in range(2):
            d = dest_ref[k, r]
            pltpu.make_async_copy(_row_slice(hbuf, r), _row_slice(xs_ref, d), sem).start(priority=k)
    for k in range(2):
        pltpu.make_async_copy(hbuf, xs_ref.at[pl.ds(0, tb * ROW_SUB)], sem).wait()


def _dispatch(x, ln_w, dest, zero_flag, n_slots, *, tb=ROW_MOVE_TOKENS):
    t, d = x.shape
    nblk = t // tb
    return pl.pallas_call(
        _dispatch_kernel,
        grid_spec=pltpu.PrefetchScalarGridSpec(
            num_scalar_prefetch=1,
            grid=(nblk,),
            in_specs=[
                pl.BlockSpec((2, tb), lambda i, zf: (0, i), memory_space=pltpu.SMEM),
                pl.BlockSpec((tb, d), lambda i, zf: (i, 0)),
                pl.BlockSpec((1, d), lambda i, zf: (0, 0)),
            ],
            out_specs=pl.BlockSpec(memory_space=pl.ANY),
            scratch_shapes=[pltpu.VMEM((tb * ROW_SUB, LANES), U32),
                            pltpu.VMEM((MOE_BLOCK * ROW_SUB, LANES), U32),
                            pltpu.SemaphoreType.DMA(()), pltpu.SemaphoreType.DMA(())],
        ),
        out_shape=jax.ShapeDtypeStruct((n_slots * ROW_SUB, LANES), U32),
        compiler_params=_cparams(("arbitrary",)),
        name="moe_dispatch",
    )(zero_flag, dest, x, ln_w.reshape(1, d))


def _expert_kernel(elist_ref, first_ref, nblk_ref, meta_ref, xs_hbm, wg_hbm, wu_hbm, wd_hbm, y_hbm,
                   wg_s, wu_s, wd_s, wg_f, wu_f, wd_f, xbuf, ybuf, wsem, xsem, ysem, *, layer, n_blocks):
    p = pl.program_id(0)
    n_list = meta_ref[0]
    active = p < n_list
    blk_rows = MOE_BLOCK * ROW_SUB
    streams = ((wg_hbm, wg_f), (wu_hbm, wu_f), (wd_hbm, wd_f))

    def w_copies(pos, slot):
        e = elist_ref[pos]
        return [pltpu.make_async_copy(hbm.at[layer, e], buf.at[slot], wsem.at[slot, t])
                for t, (hbm, buf) in enumerate(streams)]

    def block_rows(blk):
        return pl.ds(pl.multiple_of(blk * blk_rows, blk_rows), blk_rows)

    def x_copy(blk, slot):
        return pltpu.make_async_copy(xs_hbm.at[block_rows(blk)], xbuf.at[slot], xsem.at[slot])

    def y_copy(blk, slot):
        return pltpu.make_async_copy(ybuf.at[slot], y_hbm.at[block_rows(blk)], ysem.at[slot])

    n_act = meta_ref[1]

    @pl.when(jnp.logical_and(active, p == 0))
    def _():
        for cp in w_copies(0, 0):
            cp.start()
        x_copy(0, 0).start()

    @pl.when(active)
    def _():
        slot = lax.rem(p, 2)
        for cp in w_copies(p, slot):
            cp.wait()

        @pl.when(p + 1 < n_list)
        def _():
            for cp in w_copies(p + 1, 1 - slot):
                cp.start()

        wg_s[...] = wg_f[slot].astype(BF16)
        wu_s[...] = wu_f[slot].astype(BF16)
        wd_s[...] = wd_f[slot].astype(BF16)

        def block(blk, carry):
            cur = lax.rem(blk, 2)
            x_copy(blk, cur).wait()

            @pl.when(blk + 1 < n_act)
            def _():
                x_copy(blk + 1, 1 - cur).start()

            @pl.when(blk >= 2)
            def _():
                y_copy(blk - 2, cur).wait()

            lo, hi = _unpack_rows(_load_rows(xbuf.at[cur], MOE_BLOCK))
            xb = jnp.concatenate([lo.astype(BF16), hi.astype(BF16)], axis=1)
            gate = jnp.dot(xb, wg_s[...], preferred_element_type=F32)
            up = jnp.dot(xb, wu_s[...], preferred_element_type=F32)
            hid = (_silu(gate) * up).astype(BF16)
            _store_rows(ybuf.at[cur], _pack_rows(jnp.dot(hid, wd_s[...], preferred_element_type=F32)))
            y_copy(blk, cur).start()
            return carry

        b0 = first_ref[p]
        lax.fori_loop(b0, b0 + nblk_ref[p], block, 0)

        @pl.when(p == n_list - 1)
        def _():
            @pl.when(n_act >= 2)
            def _():
                y_copy(n_act - 2, lax.rem(n_act, 2)).wait()
            y_copy(n_act - 1, lax.rem(n_act - 1, 2)).wait()

    @pl.when(p == pl.num_programs(0) - 1)
    def _():
        ybuf[0] = jnp.zeros(ybuf.shape[1:], ybuf.dtype)

        def zissue(b, carry):
            y_copy(b, 0).start()
            return carry

        def zdrain(b, carry):
            y_copy(b, 0).wait()
            return carry

        lax.fori_loop(n_act, n_blocks, zissue, 0)
        lax.fori_loop(n_act, n_blocks, zdrain, 0)


def _experts(xs, expert_list, first_block, num_blocks, meta, w_gate, w_up, w_down, layer):
    blk_rows = MOE_BLOCK * ROW_SUB
    n_blocks = xs.shape[0] // blk_rows
    d = D_MODEL
    return pl.pallas_call(
        functools.partial(_expert_kernel, layer=layer, n_blocks=n_blocks),
        grid_spec=pltpu.PrefetchScalarGridSpec(
            num_scalar_prefetch=4,
            grid=(N_EXPERTS,),
            in_specs=[pl.BlockSpec(memory_space=pl.ANY)] * 4,
            out_specs=pl.BlockSpec(memory_space=pl.ANY),
            scratch_shapes=[
                pltpu.VMEM((d, EXPERT_FF), BF16),
                pltpu.VMEM((d, EXPERT_FF), BF16),
                pltpu.VMEM((EXPERT_FF, d), BF16),
                pltpu.VMEM((2, d, EXPERT_FF), F32),
                pltpu.VMEM((2, d, EXPERT_FF), F32),
                pltpu.VMEM((2, EXPERT_FF, d), F32),
                pltpu.VMEM((2, blk_rows, LANES), U32),
                pltpu.VMEM((2, blk_rows, LANES), U32),
                pltpu.SemaphoreType.DMA((2, 3)),
                pltpu.SemaphoreType.DMA((2,)),
                pltpu.SemaphoreType.DMA((2,)),
            ],
        ),
        out_shape=jax.ShapeDtypeStruct(xs.shape, U32),
        compiler_params=_cparams(("arbitrary",)),
        name="moe_experts",
    )(expert_list, first_block, num_blocks, meta, xs, w_gate, w_up, w_down)


def _combine_kernel(dest_ref, x_ref, gates_ref, y_ref, *rest, final_norm):
    if final_norm:
        g_ref, o_ref, buf0, buf1, sem = rest
    else:
        o_ref, buf0, buf1, sem = rest
    tb = x_ref.shape[0]
    bufs = (buf0, buf1)

    for r in range(tb):
        for k in range(2):
            d = dest_ref[k, r]
            pltpu.make_async_copy(_row_slice(y_ref, d), _row_slice(bufs[k], r), sem).start(priority=k)
    for k in range(2):
        pltpu.make_async_copy(y_ref.at[pl.ds(0, tb * ROW_SUB)], bufs[k], sem).wait()

    gt = gates_ref[...]
    lo0, hi0 = _unpack_rows(_load_rows(buf0, tb))
    lo1, hi1 = _unpack_rows(_load_rows(buf1, tb))
    w0 = gt[:, 0:1]
    w1 = gt[:, 1:2]
    out = x_ref[...] + jnp.concatenate([lo0 * w0 + lo1 * w1, hi0 * w0 + hi1 * w1], axis=1)
    if final_norm:
        out = _rms(out, g_ref[...])
    o_ref[...] = out


def _combine(x, gates, y, dest, final_w=None, *, tb=ROW_MOVE_TOKENS):
    t, d = x.shape
    nblk = t // tb
    final_norm = final_w is not None
    in_specs = [
        pl.BlockSpec((2, tb), lambda i: (0, i), memory_space=pltpu.SMEM),
        pl.BlockSpec((tb, d), lambda i: (i, 0)),
        pl.BlockSpec((tb, LANES), lambda i: (i, 0)),
        pl.BlockSpec(memory_space=pl.ANY),
    ]
    args = [dest, x, gates, y]
    if final_norm:
        in_specs.append(pl.BlockSpec((1, d), lambda i: (0, 0)))
        args.append(final_w.reshape(1, d))
    return pl.pallas_call(
        functools.partial(_combine_kernel, final_norm=final_norm),
        grid=(nblk,),
        in_specs=in_specs,
        out_specs=pl.BlockSpec((tb, d), lambda i: (i, 0)),
        out_shape=jax.ShapeDtypeStruct((t, d), F32),
        scratch_shapes=[pltpu.VMEM((tb * ROW_SUB, LANES), U32), pltpu.VMEM((tb * ROW_SUB, LANES), U32),
                        pltpu.SemaphoreType.DMA(())],
        compiler_params=_cparams(("arbitrary",)),
        name="moe_combine",
    )(*args)


def _proj_moe(a, w_proj, res, ln_w, w_group, b_group, w_expert, b_expert, w_gate, w_up, w_down, layer,
              final_w=None):
    t = res.shape[0]
    n_assign = t * 2
    n_blocks = -(-n_assign // MOE_BLOCK) + N_EXPERTS
    n_slots = n_blocks * MOE_BLOCK

    x, ids, gates, cnt = _router(a, w_proj, res, ln_w, w_group, b_group, w_expert, b_expert)
    counts = cnt[:N_EXPERTS, 0].astype(jnp.int32)
    padded = (counts + MOE_BLOCK - 1) // MOE_BLOCK * MOE_BLOCK
    p_ends = jnp.cumsum(padded)
    p_starts = p_ends - padded
    experts = jnp.arange(N_EXPERTS, dtype=jnp.int32)
    start_of = jnp.sum(jnp.where(ids[0:2, :, None] == experts, p_starts, 0), axis=-1)
    dest = (start_of + ids[2:4]).astype(jnp.int32)
    block_start = jnp.arange(n_blocks, dtype=jnp.int32) * MOE_BLOCK
    block_e = jnp.minimum(jnp.sum((p_ends[None, :] <= block_start[:, None]).astype(jnp.int32), axis=1),
                          N_EXPERTS - 1)
    n_active = (p_ends[-1:] // MOE_BLOCK).astype(jnp.int32)
    valid_end = jnp.sum(jnp.where(block_e[:, None] == experts, p_starts + counts, 0), axis=-1)
    zero_flag = jnp.logical_or(block_start + MOE_BLOCK > valid_end,
                               block_start >= p_ends[-1]).astype(jnp.int32)

    has_tokens = counts > 0
    list_pos = jnp.cumsum(has_tokens.astype(jnp.int32)) - 1
    expert_list = jnp.sum(jnp.where(jnp.logical_and(has_tokens[None, :], list_pos[None, :] == experts[:, None]),
                                    experts[None, :], 0), axis=1).astype(jnp.int32)
    listed = expert_list[:, None] == experts[None, :]
    first_block = jnp.sum(jnp.where(listed, p_starts // MOE_BLOCK, 0), axis=1).astype(jnp.int32)
    num_blocks = jnp.sum(jnp.where(listed, padded // MOE_BLOCK, 0), axis=1).astype(jnp.int32)
    meta = jnp.concatenate([jnp.sum(has_tokens.astype(jnp.int32), keepdims=True), n_active]).astype(jnp.int32)

    xs = _dispatch(x, ln_w, dest, zero_flag, n_slots)
    y = _experts(xs, expert_list, first_block, num_blocks, meta, w_gate, w_up, w_down, layer)
    return _combine(x, gates, y, dest, final_w)


def kernel(x, rel_bias, ln_mix, ln_ffn, ln_final, attn_w_qkv, attn_sinks, attn_w_o, ssm_w_in,
           ssm_conv_w, ssm_conv_b, ssm_dt_bias, ssm_a_log, ssm_d, ssm_norm_w, ssm_w_out,
           moe_w_group, moe_b_group, moe_w_expert, moe_b_expert, moe_w_gate, moe_w_up, moe_w_down):
    batch, seq, d = x.shape
    t = batch * seq
    xf = x.reshape(t, d).astype(F32)

    q_scale = jnp.where(jnp.arange(QKV_DIM) < Q_DIM, HEAD_DIM ** -0.5 * LOG2E, 1.0).astype(F32)
    qkv = _norm_matmul(xf, ln_mix[0], (attn_w_qkv[0] * q_scale).astype(BF16), tm=512, tn=QKV_DIM)
    att = _attention(qkv, attn_sinks[0], rel_bias, batch, seq)
    xf = _proj_moe(att, attn_w_o[0].astype(BF16), xf, ln_ffn[0], moe_w_group[0], moe_b_group[0],
                   moe_w_expert[0], moe_b_expert[0], moe_w_gate, moe_w_up, moe_w_down, 0)

    w_in = ssm_w_in[0]
    w_dt = jnp.pad(w_in[:, ZXBC_DIM:], ((0, 0), (0, LANES - SSM_HEADS))).astype(BF16)
    zxbc, dt_raw = _norm_matmul(xf, ln_mix[1], w_in[:, :ZXBC_DIM].astype(BF16), w_dt, tm=512, tn=ZXBC_DIM)
    gated = _ssd(zxbc, dt_raw, ssm_conv_w[0], ssm_conv_b[0], ssm_dt_bias[0], ssm_a_log[0],
                 ssm_d[0], ssm_norm_w[0], batch, seq)
    xf = _proj_moe(gated, ssm_w_out[0].astype(BF16), xf, ln_ffn[1], moe_w_group[1], moe_b_group[1],
                   moe_w_expert[1], moe_b_expert[1], moe_w_gate, moe_w_up, moe_w_down, 1, final_w=ln_final)
    return xf.reshape(batch, seq, d).astype(x.dtype)
```

```python
import functools
import math

import jax
import jax.numpy as jnp
from jax import lax
from jax.experimental import pallas as pl
from jax.experimental.pallas import tpu as pltpu

F32 = jnp.float32
BF16 = jnp.bfloat16

D_MODEL = 1024
N_HEADS = 16
N_KV_HEADS = 4
HEAD_DIM = 64
GQA_GROUP = N_HEADS // N_KV_HEADS
WINDOW = 128
ATTN_BLOCK = 128
Q_DIM = N_HEADS * HEAD_DIM
KV_DIM = N_KV_HEADS * HEAD_DIM
QKV_DIM = Q_DIM + 2 * KV_DIM
REL_BUCKETS = 32
REL_MAX_DIST = 128

D_INNER = 2048
SSM_HEAD_DIM = 64
SSM_HEADS = D_INNER // SSM_HEAD_DIM
SSM_GROUPS = 4
D_STATE = 128
CONV_WIDTH = 4
BC_DIM = 2 * SSM_GROUPS * D_STATE
CONV_DIM = D_INNER + BC_DIM
ZXBC_DIM = D_INNER + CONV_DIM
SSM_CHUNK = 128
SSD_STEP_CHUNKS = 4
GROUP_CH = D_INNER // SSM_GROUPS

N_EXPERT_GROUPS = 8
EXPERTS_PER_GROUP = 8
N_EXPERTS = 64
EXPERT_FF = 512
MOE_BLOCK = 512

NORM_EPS = 1e-6
LOG2E = 1.4426950408889634
CONV_TAIL = 16
LANES = 128
NEG_BIG = -1e30
VMEM_LIMIT = 56 * 1024 * 1024


def _cparams(sem, flags=None):
    return pltpu.CompilerParams(dimension_semantics=sem, vmem_limit_bytes=VMEM_LIMIT, flags=flags)


def _rms(x, g):
    ms = jnp.mean(x * x, axis=-1, keepdims=True)
    return x * lax.rsqrt(ms + NORM_EPS) * g


def _silu(x):
    h = 0.5 * x
    return h + h * jnp.tanh(h)


def _norm_matmul_kernel(x_ref, g_ref, w_ref, *rest, has_aux):
    if has_aux:
        wa_ref, o_ref, oa_ref, h_scr = rest
    else:
        o_ref, h_scr = rest

    @pl.when(pl.program_id(1) == 0)
    def _():
        h = _rms(x_ref[...], g_ref[...]).astype(BF16)
        h_scr[...] = h
        if has_aux:
            oa_ref[...] = jnp.dot(h, wa_ref[...], preferred_element_type=F32)

    o_ref[...] = jnp.dot(h_scr[...], w_ref[...], preferred_element_type=F32).astype(o_ref.dtype)


def _norm_matmul(x, g, w, w_aux=None, *, tm=1024, tn=512):
    t, d = x.shape
    n = w.shape[1]
    has_aux = w_aux is not None
    in_specs = [
        pl.BlockSpec((tm, d), lambda i, j: (i, 0)),
        pl.BlockSpec((1, d), lambda i, j: (0, 0)),
        pl.BlockSpec((d, tn), lambda i, j: (0, j)),
    ]
    out_shape = [jax.ShapeDtypeStruct((t, n), BF16)]
    out_specs = [pl.BlockSpec((tm, tn), lambda i, j: (i, j))]
    args = [x, g.reshape(1, d), w]
    if has_aux:
        na = w_aux.shape[1]
        in_specs.append(pl.BlockSpec((d, na), lambda i, j: (0, 0)))
        out_shape.append(jax.ShapeDtypeStruct((t, na), F32))
        out_specs.append(pl.BlockSpec((tm, na), lambda i, j: (i, 0)))
        args.append(w_aux)
    res = pl.pallas_call(
        functools.partial(_norm_matmul_kernel, has_aux=has_aux),
        grid=(t // tm, n // tn),
        in_specs=in_specs,
        out_specs=out_specs,
        out_shape=out_shape,
        scratch_shapes=[pltpu.VMEM((tm, d), BF16)],
        compiler_params=_cparams(("parallel", "arbitrary")),
        name="norm_matmul",
    )(*args)
    return res if has_aux else res[0]


def _attn_kernel(sinks_ref, q_ref, kp_ref, kc_ref, vp_ref, vc_ref, bias_ref, o_ref):
    table = jnp.minimum(pl.program_id(1), 1)
    for h in range(N_KV_HEADS):
        ks = slice(h * HEAD_DIM, (h + 1) * HEAD_DIM)
        kb = jnp.concatenate([kp_ref[:, ks], kc_ref[:, ks]], axis=0)
        vb = jnp.concatenate([vp_ref[:, ks], vc_ref[:, ks]], axis=0)
        for g in range(GQA_GROUP):
            hh = h * GQA_GROUP + g
            qh = q_ref[:, hh * HEAD_DIM:(hh + 1) * HEAD_DIM]
            s = lax.dot_general(qh, kb, (((1,), (1,)), ((), ())), preferred_element_type=F32)
            logits = s + bias_ref[table, hh]
            sink = sinks_ref[hh]
            m = jnp.maximum(jnp.max(logits, axis=-1, keepdims=True), sink)
            p = jnp.exp2(logits - m)
            denom = jnp.sum(p, axis=-1, keepdims=True) + jnp.exp2(sink - m)
            o = jnp.dot(p.astype(BF16), vb, preferred_element_type=F32) / denom
            o_ref[:, hh * HEAD_DIM:(hh + 1) * HEAD_DIM] = o.astype(o_ref.dtype)


def _t5_causal_bucket(dist):
    n = jnp.maximum(dist, 0)
    max_exact = REL_BUCKETS // 2
    nf = jnp.maximum(n, 1).astype(F32)
    large = max_exact + (jnp.log(nf / max_exact) / math.log(REL_MAX_DIST / max_exact)
                         * (REL_BUCKETS - max_exact)).astype(jnp.int32)
    large = jnp.minimum(large, REL_BUCKETS - 1)
    return jnp.where(n < max_exact, n, large)


def _attention(qkv, sinks, rel_bias, batch, seq):
    t = qkv.shape[0]
    nb = seq // ATTN_BLOCK
    qi = jnp.arange(ATTN_BLOCK)[:, None]
    ki = jnp.arange(2 * ATTN_BLOCK)[None, :]
    dist = qi + ATTN_BLOCK - ki
    in_window = (dist >= 0) & (dist < WINDOW)
    onehot = (_t5_causal_bucket(dist)[None] == jnp.arange(REL_BUCKETS)[:, None, None]).astype(F32)
    bias = jnp.einsum('hr,rqk->hqk', rel_bias.astype(F32).T, onehot, precision=lax.Precision.HIGHEST)
    bias = jnp.where(in_window[None], bias * LOG2E, NEG_BIG)
    bias_first = jnp.where((ki < ATTN_BLOCK)[None], NEG_BIG, bias)
    bias2 = jnp.stack([bias_first, bias])

    kcol = Q_DIM // KV_DIM
    vcol = kcol + 1

    def prev(b, n):
        return b * nb + jnp.maximum(n - 1, 0)

    return pl.pallas_call(
        _attn_kernel,
        grid=(batch, nb),
        in_specs=[
            pl.BlockSpec(memory_space=pltpu.SMEM),
            pl.BlockSpec((ATTN_BLOCK, Q_DIM), lambda b, n: (b * nb + n, 0)),
            pl.BlockSpec((ATTN_BLOCK, KV_DIM), lambda b, n: (prev(b, n), kcol)),
            pl.BlockSpec((ATTN_BLOCK, KV_DIM), lambda b, n: (b * nb + n, kcol)),
            pl.BlockSpec((ATTN_BLOCK, KV_DIM), lambda b, n: (prev(b, n), vcol)),
            pl.BlockSpec((ATTN_BLOCK, KV_DIM), lambda b, n: (b * nb + n, vcol)),
            pl.BlockSpec((2, N_HEADS, ATTN_BLOCK, 2 * ATTN_BLOCK), lambda b, n: (0, 0, 0, 0)),
        ],
        out_specs=pl.BlockSpec((ATTN_BLOCK, Q_DIM), lambda b, n: (b * nb + n, 0)),
        out_shape=jax.ShapeDtypeStruct((t, Q_DIM), BF16),
        compiler_params=_cparams(("parallel", "arbitrary")),
        name="swa_attention",
    )((sinks.astype(F32) * LOG2E), qkv, qkv, qkv, qkv, qkv, bias2)


def _split3(v):
    hi = v.astype(BF16)
    r = v - hi.astype(F32)
    mid = r.astype(BF16)
    lo = (r - mid.astype(F32)).astype(BF16)
    return hi, mid, lo


def _ssd_kernel(z_ref, x_ref, bc_ref, dt_ref, convw_ref, convb_ref, dtb_ref, alog_ref,
                dskip_ref, normw_ref, shift_ref, shift_tail_ref, expand_ref,
                o_ref, state, xtail, bctail):
    @pl.when(pl.program_id(1) == 0)
    def _():
        state[...] = jnp.zeros_like(state)
        xtail[...] = jnp.zeros_like(xtail)
        bctail[...] = jnp.zeros_like(bctail)

    for k in range(z_ref.shape[0] // SSM_CHUNK):
        _ssd_chunk(pl.ds(k * SSM_CHUNK, SSM_CHUNK), z_ref, x_ref, bc_ref, dt_ref, convw_ref, convb_ref,
                   dtb_ref, alog_ref, dskip_ref, normw_ref, shift_ref, shift_tail_ref, expand_ref,
                   o_ref, state, xtail, bctail)


def _ssd_chunk(rows, z_ref, x_ref, bc_ref, dt_ref, convw_ref, convb_ref, dtb_ref, alog_ref,
               dskip_ref, normw_ref, shift_ref, shift_tail_ref, expand_ref,
               o_ref, state, xtail, bctail):
    L = SSM_CHUNK

    def conv_silu(u_ref, tail, w_lo, width):
        cur = u_ref[rows, :]
        prev = tail[...]
        w = [convw_ref[k:k + 1, w_lo:w_lo + width].astype(BF16) for k in range(CONV_WIDTH)]
        taps = jnp.concatenate([cur * w[k] for k in range(CONV_WIDTH)], axis=0)
        taps_prev = jnp.concatenate([prev * w[k] for k in range(CONV_WIDTH)], axis=0)
        y = (jnp.dot(shift_ref[...], taps, preferred_element_type=F32)
             + jnp.dot(shift_tail_ref[...], taps_prev, preferred_element_type=F32)
             + convb_ref[:, w_lo:w_lo + width])
        tail[...] = cur[L - CONV_TAIL:L]
        return _silu(y)

    xs = conv_silu(x_ref, xtail, 0, D_INNER)
    bcm = conv_silu(bc_ref, bctail, D_INNER, BC_DIM)

    xdt = dt_ref[rows, :] + dtb_ref[...]
    dt = jnp.maximum(xdt, 0.0) + jnp.log1p(jnp.exp(-jnp.abs(xdt)))
    a = dt * (-jnp.exp(alog_ref[...]) * LOG2E)
    row = lax.broadcasted_iota(jnp.int32, (L, L), 0)
    colm = lax.broadcasted_iota(jnp.int32, (L, L), 1)
    causal = row >= colm
    cs3 = jnp.dot(causal.astype(BF16), jnp.concatenate(_split3(a), axis=1), preferred_element_type=F32)
    a_cs = cs3[:, :LANES] + cs3[:, LANES:2 * LANES] + cs3[:, 2 * LANES:]
    a_cs_t = a_cs.T
    dt_t = dt.T
    a_last_col = a_cs_t[:, L - 1:L]
    w_end_t = jnp.exp2(a_last_col - a_cs_t) * dt_t
    exp_acs = jnp.exp2(a_cs)

    cd3 = [jnp.broadcast_to(part.astype(F32), (16, LANES)).astype(BF16)
           for part in _split3(exp_acs[L - 1:L, :])]
    cdx = jnp.dot(jnp.concatenate(cd3, axis=0), expand_ref[...], preferred_element_type=F32)
    cd_exp = cdx[0:1] + cdx[16:17] + cdx[32:33]

    lane = lax.broadcasted_iota(jnp.int32, (L, LANES), 1)
    low_half = lane < SSM_HEAD_DIM

    y_tiles = []
    for g in range(SSM_GROUPS):
        b_g = bcm[:, g * D_STATE:(g + 1) * D_STATE]
        c_g = bcm[:, (SSM_GROUPS + g) * D_STATE:(SSM_GROUPS + g + 1) * D_STATE]
        b_gb = b_g.astype(BF16)
        c_gb = c_g.astype(BF16)
        cb = lax.dot_general(c_gb, b_gb, (((1,), (1,)), ((), ())), preferred_element_type=F32)
        b_gt = b_g.T
        s_g = state[:, g * GROUP_CH:(g + 1) * GROUP_CH]
        y_off_g = jnp.dot(c_gb, s_g.astype(BF16), preferred_element_type=F32)
        for jj in range(GROUP_CH // LANES):
            j = g * (GROUP_CH // LANES) + jj
            lhs_top = []
            lhs_bot = []
            escale = []
            for e in (2 * j, 2 * j + 1):
                acs_b = jnp.broadcast_to(a_cs[:, e:e + 1], (L, L))
                seg = acs_b - a_cs_t[e:e + 1, :]
                dec = jnp.exp2(jnp.where(causal, seg, -jnp.inf))
                lhs_top.append((cb * dec * dt_t[e:e + 1, :]).astype(BF16))
                lhs_bot.append((b_gt * w_end_t[e:e + 1, :]).astype(BF16))
                escale.append(jnp.broadcast_to(exp_acs[:, e:e + 1], (L, LANES)))
            x_tile = xs[:, j * LANES:(j + 1) * LANES]
            x_lo = jnp.where(low_half, x_tile, 0.0).astype(BF16)
            x_hi = jnp.where(low_half, 0.0, x_tile).astype(BF16)
            lhs = jnp.concatenate([jnp.concatenate(lhs_top, axis=1),
                                   jnp.concatenate(lhs_bot, axis=1)], axis=0)
            rhs = jnp.concatenate([x_lo, x_hi], axis=0)
            r = jnp.dot(lhs, rhs, preferred_element_type=F32)
            y_off = y_off_g[:, jj * LANES:(jj + 1) * LANES]
            y_tiles.append(r[0:L] + jnp.where(low_half, escale[0], escale[1]) * y_off)
            sl = slice(j * LANES, (j + 1) * LANES)
            state[:, sl] = cd_exp[:, sl] * state[:, sl] + r[L:2 * L]

    y = jnp.concatenate(y_tiles, axis=1)
    y = y + xs * dskip_ref[...]
    gated = y * _silu(z_ref[rows, :].astype(F32))
    outs = []
    for g in range(SSM_GROUPS):
        gg = gated[:, g * GROUP_CH:(g + 1) * GROUP_CH]
        ms = jnp.mean(gg * gg, axis=-1, keepdims=True)
        outs.append(gg * lax.rsqrt(ms + NORM_EPS))
    o_ref[rows, :] = (jnp.concatenate(outs, axis=1) * normw_ref[...]).astype(o_ref.dtype)


def _ssd(zxbc, dt_raw, conv_w, conv_b, dt_bias, a_log, d_skip, norm_w, batch, seq):
    t = zxbc.shape[0]
    L = SSM_CHUNK
    rows = SSD_STEP_CHUNKS * L
    nc = seq // rows

    def pad_heads(v):
        return jnp.pad(v.astype(F32), (0, LANES - SSM_HEADS)).reshape(1, LANES)

    def rowmap(col):
        return lambda b, c: (b * nc + c, col)

    def const2(b, c):
        return (0, 0)

    l_idx = jnp.arange(L)[:, None]
    shift = jnp.concatenate(
        [(jnp.arange(L)[None, :] == l_idx - (CONV_WIDTH - 1) + k) for k in range(CONV_WIDTH)],
        axis=1).astype(BF16)
    shift_tail = jnp.concatenate(
        [(jnp.arange(-CONV_TAIL, 0)[None, :] == l_idx - (CONV_WIDTH - 1) + k) for k in range(CONV_WIDTH)],
        axis=1).astype(BF16)
    expand = (jnp.arange(D_INNER)[None, :] // SSM_HEAD_DIM == jnp.arange(LANES)[:, None]).astype(BF16)

    return pl.pallas_call(
        _ssd_kernel,
        grid=(batch, nc),
        in_specs=[
            pl.BlockSpec((rows, D_INNER), rowmap(0)),
            pl.BlockSpec((rows, D_INNER), rowmap(1)),
            pl.BlockSpec((rows, BC_DIM), rowmap(2 * D_INNER // BC_DIM)),
            pl.BlockSpec((rows, LANES), rowmap(0)),
            pl.BlockSpec((CONV_WIDTH, CONV_DIM), const2),
            pl.BlockSpec((1, CONV_DIM), const2),
            pl.BlockSpec((1, LANES), const2),
            pl.BlockSpec((1, LANES), const2),
            pl.BlockSpec((1, D_INNER), const2),
            pl.BlockSpec((1, D_INNER), const2),
            pl.BlockSpec((L, CONV_WIDTH * L), const2),
            pl.BlockSpec((L, CONV_WIDTH * CONV_TAIL), const2),
            pl.BlockSpec((LANES, D_INNER), const2),
        ],
        out_specs=pl.BlockSpec((rows, D_INNER), rowmap(0)),
        out_shape=jax.ShapeDtypeStruct((t, D_INNER), BF16),
        scratch_shapes=[
            pltpu.VMEM((D_STATE, D_INNER), F32),
            pltpu.VMEM((CONV_TAIL, D_INNER), BF16),
            pltpu.VMEM((CONV_TAIL, BC_DIM), BF16),
        ],
        compiler_params=_cparams(("parallel", "arbitrary")),
        name="ssd_core",
    )(zxbc, zxbc, zxbc, dt_raw, conv_w.astype(F32), conv_b.astype(F32).reshape(1, CONV_DIM),
      pad_heads(dt_bias), pad_heads(a_log),
      jnp.repeat(d_skip.astype(F32), SSM_HEAD_DIM).reshape(1, D_INNER),
      norm_w.astype(F32).reshape(1, D_INNER), shift, shift_tail, expand)


GROUP_ROW0 = N_EXPERTS


def _router_kernel(a_ref, wp_ref, res_ref, g_ref, w_ref, b_ref, earlier_ref,
                   x_ref, ids_ref, gates_ref, cnt_ref, carry, wt_split):
    i = pl.program_id(0)
    x_ref[...] = res_ref[...] + jnp.dot(a_ref[...], wp_ref[...], preferred_element_type=F32)

    @pl.when(i == 0)
    def _():
        carry[...] = jnp.zeros_like(carry)
        wt = w_ref[...].T
        wt_hi = wt.astype(BF16)
        wt_split[0:LANES, :] = wt_hi
        wt_split[LANES:, :] = (wt - wt_hi.astype(F32)).astype(BF16)

    h = _rms(x_ref[...], g_ref[...])
    h_hi = h.astype(BF16)
    h_mid = (h - h_hi.astype(F32)).astype(BF16)
    nt = (((1,), (1,)), ((), ()))
    top = lax.dot_general(wt_split[...], h_hi, nt, preferred_element_type=F32)
    cross = lax.dot_general(wt_split[0:LANES, :], h_mid, nt, preferred_element_type=F32)
    logits = top[:LANES] + (top[LANES:] + cross) + b_ref[...]
    tm = logits.shape[1]
    row = lax.broadcasted_iota(jnp.int32, (LANES, tm), 0)
    ninf = -jnp.inf

    def first_argmax(v, vmax):
        return jnp.min(jnp.where(v == vmax, row, LANES), axis=0, keepdims=True)

    is_group = (row >= GROUP_ROW0) & (row < GROUP_ROW0 + N_EXPERT_GROUPS)
    glog = jnp.where(is_group, logits, ninf)
    gmax = jnp.max(glog, axis=0, keepdims=True)
    g_idx = first_argmax(glog, gmax) - GROUP_ROW0
    g_p = 1.0 / jnp.sum(jnp.exp(glog - gmax), axis=0, keepdims=True)

    in_group = (row < N_EXPERTS) & (jnp.right_shift(row, 3) == g_idx)
    elog = jnp.where(in_group, logits, ninf)
    m1 = jnp.max(elog, axis=0, keepdims=True)
    e1 = first_argmax(elog, m1)
    elog2 = jnp.where(row == e1, ninf, elog)
    m2 = jnp.max(elog2, axis=0, keepdims=True)
    e2 = first_argmax(elog2, m2)
    zsum = jnp.sum(jnp.exp(elog - m1), axis=0, keepdims=True)
    p1 = 1.0 / zsum
    p2 = jnp.exp(m2 - m1) / zsum
    psum = p1 + p2
    w1 = p1 / psum * g_p
    w2 = p2 / psum * g_p

    oh1 = row == e1
    oh2 = row == e2
    onehot = jnp.logical_or(oh1, oh2).astype(F32)
    before = jnp.dot(onehot.astype(BF16), earlier_ref[...], preferred_element_type=F32) + carry[:, 0:1]
    r1 = jnp.sum(jnp.where(oh1, before, 0.0), axis=0, keepdims=True).astype(jnp.int32)
    r2 = jnp.sum(jnp.where(oh2, before, 0.0), axis=0, keepdims=True).astype(jnp.int32)
    carry[...] = carry[...] + jnp.sum(onehot, axis=1, keepdims=True)

    row8 = lax.broadcasted_iota(jnp.int32, (8, tm), 0)
    ids_ref[...] = jnp.where(row8 == 0, e1, jnp.where(row8 == 1, e2,
                             jnp.where(row8 == 2, r1, jnp.where(row8 == 3, r2, 0))))
    gates_ref[...] = jnp.where(row == 0, w1, jnp.where(row == 1, w2, 0.0)).T
    cnt_ref[...] = carry[...]


def _router(a, w_proj, res, ln_w, w_group, b_group, w_expert, b_expert, *, tm=512):
    t, d = res.shape
    k = a.shape[1]
    pad = LANES - N_EXPERTS - N_EXPERT_GROUPS
    w_r = jnp.concatenate([w_expert, w_group, jnp.zeros((d, pad), F32)], axis=1).astype(F32)
    b_r = jnp.concatenate([b_expert, b_group, jnp.zeros((pad,), F32)]).astype(F32).reshape(LANES, 1)
    earlier = (jnp.arange(tm)[:, None] < jnp.arange(tm)[None, :]).astype(BF16)
    return pl.pallas_call(
        _router_kernel,
        grid=(t // tm,),
        in_specs=[
            pl.BlockSpec((tm, k), lambda i: (i, 0)),
            pl.BlockSpec((k, d), lambda i: (0, 0)),
            pl.BlockSpec((tm, d), lambda i: (i, 0)),
            pl.BlockSpec((1, d), lambda i: (0, 0)),
            pl.BlockSpec((d, LANES), lambda i: (0, 0)),
            pl.BlockSpec((LANES, 1), lambda i: (0, 0)),
            pl.BlockSpec((tm, tm), lambda i: (0, 0)),
        ],
        out_specs=[
            pl.BlockSpec((tm, d), lambda i: (i, 0)),
            pl.BlockSpec((8, tm), lambda i: (0, i)),
            pl.BlockSpec((tm, LANES), lambda i: (i, 0)),
            pl.BlockSpec((LANES, LANES), lambda i: (0, 0)),
        ],
        out_shape=[
            jax.ShapeDtypeStruct((t, d), F32),
            jax.ShapeDtypeStruct((8, t), jnp.int32),
            jax.ShapeDtypeStruct((t, LANES), F32),
            jax.ShapeDtypeStruct((LANES, LANES), F32),
        ],
        scratch_shapes=[pltpu.VMEM((LANES, LANES), F32), pltpu.VMEM((2 * LANES, d), BF16)],
        compiler_params=_cparams(("arbitrary",)),
        name="proj_moe_router",
    )(a, w_proj, res, ln_w.reshape(1, d), w_r, b_r, earlier)


U32 = jnp.uint32
ROW_WORDS = D_MODEL // 2
ROW_SUB = ROW_WORDS // LANES
HIGH_HALF = 0xFFFF0000
ROW_MOVE_TOKENS = 1024


def _pack_rows(v):
    lo = lax.bitcast_convert_type(v[:, :ROW_WORDS].astype(BF16).astype(F32), U32)
    hi = lax.bitcast_convert_type(v[:, ROW_WORDS:].astype(BF16).astype(F32), U32)
    return jnp.right_shift(lo, jnp.uint32(16)) | (hi & jnp.uint32(HIGH_HALF))


def _unpack_rows(w):
    lo = lax.bitcast_convert_type(jnp.left_shift(w, jnp.uint32(16)), F32)
    hi = lax.bitcast_convert_type(w & jnp.uint32(HIGH_HALF), F32)
    return lo, hi


def _store_rows(ref, packed):
    n = packed.shape[0]
    for c in range(ROW_SUB):
        ref[pl.ds(c, n, stride=ROW_SUB), :] = packed[:, c * LANES:(c + 1) * LANES]


def _load_rows(ref, n):
    return jnp.concatenate([ref[pl.ds(c, n, stride=ROW_SUB), :] for c in range(ROW_SUB)], axis=1)


def _row_slice(ref, row):
    return ref.at[pl.ds(pl.multiple_of(row * ROW_SUB, ROW_SUB), ROW_SUB)]


def _dispatch_kernel(zf_ref, dest_ref, x_ref, g_ref, xs_ref, hbuf, zbuf, sem, zsem):
    tb = x_ref.shape[0]
    n_blocks = zf_ref.shape[0]
    blk_rows = MOE_BLOCK * ROW_SUB

    @pl.when(pl.program_id(0) == 0)
    def _():
        zbuf[...] = jnp.zeros_like(zbuf)

        def zero_copy(b):
            start = pl.multiple_of(b * blk_rows, blk_rows)
            return pltpu.make_async_copy(zbuf, xs_ref.at[pl.ds(start, blk_rows)], zsem)

        def zissue(b, carry):
            @pl.when(zf_ref[b] != 0)
            def _():
                zero_copy(b).start()
            return carry

        def zdrain(b, carry):
            @pl.when(zf_ref[b] != 0)
            def _():
                zero_copy(b).wait()
            return carry

        lax.fori_loop(0, n_blocks, zissue, 0)
        lax.fori_loop(0, n_blocks, zdrain, 0)

    _store_rows(hbuf, _pack_rows(_rms(x_ref[...], g_ref[...])))
    for r in range(tb):
        for k in range(2):
            d = dest_ref[k, r]
            pltpu.make_async_copy(_row_slice(hbuf, r), _row_slice(xs_ref, d), sem).start(priority=k)
    for k in range(2):
        pltpu.make_async_copy(hbuf, xs_ref.at[pl.ds(0, tb * ROW_SUB)], sem).wait()


def _dispatch(x, ln_w, dest, zero_flag, n_slots, *, tb=ROW_MOVE_TOKENS):
    t, d = x.shape
    nblk = t // tb
    return pl.pallas_call(
        _dispatch_kernel,
        grid_spec=pltpu.PrefetchScalarGridSpec(
            num_scalar_prefetch=1,
            grid=(nblk,),
            in_specs=[
                pl.BlockSpec((2, tb), lambda i, zf: (0, i), memory_space=pltpu.SMEM),
                pl.BlockSpec((tb, d), lambda i, zf: (i, 0)),
                pl.BlockSpec((1, d), lambda i, zf: (0, 0)),
            ],
            out_specs=pl.BlockSpec(memory_space=pl.ANY),
            scratch_shapes=[pltpu.VMEM((tb * ROW_SUB, LANES), U32),
                            pltpu.VMEM((MOE_BLOCK * ROW_SUB, LANES), U32),
                            pltpu.SemaphoreType.DMA(()), pltpu.SemaphoreType.DMA(())],
        ),
        out_shape=jax.ShapeDtypeStruct((n_slots * ROW_SUB, LANES), U32),
        compiler_params=_cparams(("arbitrary",)),
        name="moe_dispatch",
    )(zero_flag, dest, x, ln_w.reshape(1, d))


def _expert_kernel(be_ref, bpos_ref, elist_ref, meta_ref, xs_ref, wg_hbm, wu_hbm, wd_hbm, y_ref,
                   wg_s, wu_s, wd_s, wg_f, wu_f, wd_f, sems, *, layer):
    i = pl.program_id(0)
    active = i < meta_ref[0]
    pos = bpos_ref[i]
    changed = jnp.logical_or(i == 0, be_ref[i] != be_ref[jnp.maximum(i - 1, 0)])
    streams = ((wg_hbm, wg_f), (wu_hbm, wu_f), (wd_hbm, wd_f))

    def fetch(p, slot):
        e = elist_ref[p]
        return [pltpu.make_async_copy(hbm.at[layer, e], buf.at[slot], sems.at[slot, t])
                for t, (hbm, buf) in enumerate(streams)]

    @pl.when(jnp.logical_and(active, i == 0))
    def _():
        for cp in fetch(0, 0):
            cp.start()

    @pl.when(jnp.logical_and(active, changed))
    def _():
        slot = lax.rem(pos, 2)
        for cp in fetch(pos, slot):
            cp.wait()

        @pl.when(pos + 1 < meta_ref[1])
        def _():
            for cp in fetch(pos + 1, 1 - slot):
                cp.start()

        wg_s[...] = wg_f[slot].astype(BF16)
        wu_s[...] = wu_f[slot].astype(BF16)
        wd_s[...] = wd_f[slot].astype(BF16)

    @pl.when(active)
    def _():
        lo, hi = _unpack_rows(_load_rows(xs_ref, MOE_BLOCK))
        xb = jnp.concatenate([lo.astype(BF16), hi.astype(BF16)], axis=1)
        gate = jnp.dot(xb, wg_s[...], preferred_element_type=F32)
        up = jnp.dot(xb, wu_s[...], preferred_element_type=F32)
        hid = (_silu(gate) * up).astype(BF16)
        _store_rows(y_ref, _pack_rows(jnp.dot(hid, wd_s[...], preferred_element_type=F32)))

    @pl.when(jnp.logical_not(active))
    def _():
        y_ref[...] = jnp.zeros_like(y_ref)


def _experts(xs, block_e, block_pos, expert_list, meta, w_gate, w_up, w_down, layer):
    blk_rows = MOE_BLOCK * ROW_SUB
    n_blocks = xs.shape[0] // blk_rows
    d = D_MODEL

    def xmap(i, be, bpos, elist, meta):
        return (jnp.maximum(jnp.minimum(i, meta[0] - 1), 0), 0)

    return pl.pallas_call(
        functools.partial(_expert_kernel, layer=layer),
        grid_spec=pltpu.PrefetchScalarGridSpec(
            num_scalar_prefetch=4,
            grid=(n_blocks,),
            in_specs=[
                pl.BlockSpec((blk_rows, LANES), xmap),
                pl.BlockSpec(memory_space=pl.ANY),
                pl.BlockSpec(memory_space=pl.ANY),
                pl.BlockSpec(memory_space=pl.ANY),
            ],
            out_specs=pl.BlockSpec((blk_rows, LANES), lambda i, be, bpos, elist, meta: (i, 0)),
            scratch_shapes=[
                pltpu.VMEM((d, EXPERT_FF), BF16),
                pltpu.VMEM((d, EXPERT_FF), BF16),
                pltpu.VMEM((EXPERT_FF, d), BF16),
                pltpu.VMEM((2, d, EXPERT_FF), F32),
                pltpu.VMEM((2, d, EXPERT_FF), F32),
                pltpu.VMEM((2, EXPERT_FF, d), F32),
                pltpu.SemaphoreType.DMA((2, 3)),
            ],
        ),
        out_shape=jax.ShapeDtypeStruct(xs.shape, U32),
        compiler_params=_cparams(("arbitrary",)),
        name="moe_experts",
    )(block_e, block_pos, expert_list, meta, xs, w_gate, w_up, w_down)


def _combine_kernel(dest_ref, x_ref, gates_ref, y_ref, *rest, final_norm):
    if final_norm:
        g_ref, o_ref, buf0, buf1, sem = rest
    else:
        o_ref, buf0, buf1, sem = rest
    tb = x_ref.shape[0]
    bufs = (buf0, buf1)

    for r in range(tb):
        for k in range(2):
            d = dest_ref[k, r]
            pltpu.make_async_copy(_row_slice(y_ref, d), _row_slice(bufs[k], r), sem).start(priority=k)
    for k in range(2):
        pltpu.make_async_copy(y_ref.at[pl.ds(0, tb * ROW_SUB)], bufs[k], sem).wait()

    gt = gates_ref[...]
    lo0, hi0 = _unpack_rows(_load_rows(buf0, tb))
    lo1, hi1 = _unpack_rows(_load_rows(buf1, tb))
    w0 = gt[:, 0:1]
    w1 = gt[:, 1:2]
    out = x_ref[...] + jnp.concatenate([lo0 * w0 + lo1 * w1, hi0 * w0 + hi1 * w1], axis=1)
    if final_norm:
        out = _rms(out, g_ref[...])
    o_ref[...] = out


def _combine(x, gates, y, dest, final_w=None, *, tb=ROW_MOVE_TOKENS):
    t, d = x.shape
    nblk = t // tb
    final_norm = final_w is not None
    in_specs = [
        pl.BlockSpec((2, tb), lambda i: (0, i), memory_space=pltpu.SMEM),
        pl.BlockSpec((tb, d), lambda i: (i, 0)),
        pl.BlockSpec((tb, LANES), lambda i: (i, 0)),
        pl.BlockSpec(memory_space=pl.ANY),
    ]
    args = [dest, x, gates, y]
    if final_norm:
        in_specs.append(pl.BlockSpec((1, d), lambda i: (0, 0)))
        args.append(final_w.reshape(1, d))
    return pl.pallas_call(
        functools.partial(_combine_kernel, final_norm=final_norm),
        grid=(nblk,),
        in_specs=in_specs,
        out_specs=pl.BlockSpec((tb, d), lambda i: (i, 0)),
        out_shape=jax.ShapeDtypeStruct((t, d), F32),
        scratch_shapes=[pltpu.VMEM((tb * ROW_SUB, LANES), U32), pltpu.VMEM((tb * ROW_SUB, LANES), U32),
                        pltpu.SemaphoreType.DMA(())],
        compiler_params=_cparams(("arbitrary",)),
        name="moe_combine",
    )(*args)


def _proj_moe(a, w_proj, res, ln_w, w_group, b_group, w_expert, b_expert, w_gate, w_up, w_down, layer,
              final_w=None):
    t = res.shape[0]
    n_assign = t * 2
    n_blocks = -(-n_assign // MOE_BLOCK) + N_EXPERTS
    n_slots = n_blocks * MOE_BLOCK

    x, ids, gates, cnt = _router(a, w_proj, res, ln_w, w_group, b_group, w_expert, b_expert)
    counts = cnt[:N_EXPERTS, 0].astype(jnp.int32)
    padded = (counts + MOE_BLOCK - 1) // MOE_BLOCK * MOE_BLOCK
    p_ends = jnp.cumsum(padded)
    p_starts = p_ends - padded
    experts = jnp.arange(N_EXPERTS, dtype=jnp.int32)
    start_of = jnp.sum(jnp.where(ids[0:2, :, None] == experts, p_starts, 0), axis=-1)
    dest = (start_of + ids[2:4]).astype(jnp.int32)
    block_start = jnp.arange(n_blocks, dtype=jnp.int32) * MOE_BLOCK
    block_e = jnp.minimum(jnp.sum((p_ends[None, :] <= block_start[:, None]).astype(jnp.int32), axis=1),
                          N_EXPERTS - 1)
    n_active = (p_ends[-1:] // MOE_BLOCK).astype(jnp.int32)
    valid_end = jnp.sum(jnp.where(block_e[:, None] == experts, p_starts + counts, 0), axis=-1)
    zero_flag = jnp.logical_or(block_start + MOE_BLOCK > valid_end,
                               block_start >= p_ends[-1]).astype(jnp.int32)

    has_tokens = counts > 0
    list_pos = jnp.cumsum(has_tokens.astype(jnp.int32)) - 1
    expert_list = jnp.sum(jnp.where(jnp.logical_and(has_tokens[None, :], list_pos[None, :] == experts[:, None]),
                                    experts[None, :], 0), axis=1).astype(jnp.int32)
    block_pos = jnp.sum(jnp.where(block_e[:, None] == experts, list_pos, 0), axis=-1).astype(jnp.int32)
    meta = jnp.concatenate([n_active, jnp.sum(has_tokens.astype(jnp.int32), keepdims=True)]).astype(jnp.int32)

    xs = _dispatch(x, ln_w, dest, zero_flag, n_slots)
    y = _experts(xs, block_e, block_pos, expert_list, meta, w_gate, w_up, w_down, layer)
    return _combine(x, gates, y, dest, final_w)


def kernel(x, rel_bias, ln_mix, ln_ffn, ln_final, attn_w_qkv, attn_sinks, attn_w_o, ssm_w_in,
           ssm_conv_w, ssm_conv_b, ssm_dt_bias, ssm_a_log, ssm_d, ssm_norm_w, ssm_w_out,
           moe_w_group, moe_b_group, moe_w_expert, moe_b_expert, moe_w_gate, moe_w_up, moe_w_down):
    batch, seq, d = x.shape
    t = batch * seq
    xf = x.reshape(t, d).astype(F32)

    q_scale = jnp.where(jnp.arange(QKV_DIM) < Q_DIM, HEAD_DIM ** -0.5 * LOG2E, 1.0).astype(F32)
    qkv = _norm_matmul(xf, ln_mix[0], (attn_w_qkv[0] * q_scale).astype(BF16), tm=512, tn=QKV_DIM)
    att = _attention(qkv, attn_sinks[0], rel_bias, batch, seq)
    xf = _proj_moe(att, attn_w_o[0].astype(BF16), xf, ln_ffn[0], moe_w_group[0], moe_b_group[0],
                   moe_w_expert[0], moe_b_expert[0], moe_w_gate, moe_w_up, moe_w_down, 0)

    w_in = ssm_w_in[0]
    w_dt = jnp.pad(w_in[:, ZXBC_DIM:], ((0, 0), (0, LANES - SSM_HEADS))).astype(BF16)
    zxbc, dt_raw = _norm_matmul(xf, ln_mix[1], w_in[:, :ZXBC_DIM].astype(BF16), w_dt, tm=512, tn=ZXBC_DIM)
    gated = _ssd(zxbc, dt_raw, ssm_conv_w[0], ssm_conv_b[0], ssm_dt_bias[0], ssm_a_log[0],
                 ssm_d[0], ssm_norm_w[0], batch, seq)
    xf = _proj_moe(gated, ssm_w_out[0].astype(BF16), xf, ln_ffn[1], moe_w_group[1], moe_b_group[1],
                   moe_w_expert[1], moe_b_expert[1], moe_w_gate, moe_w_up, moe_w_down, 1, final_w=ln_final)
    return xf.reshape(batch, seq, d).astype(x.dtype)
```

```python
import functools
import math

import jax
import jax.numpy as jnp
from jax import lax
from jax.experimental import pallas as pl
from jax.experimental.pallas import tpu as pltpu

F32 = jnp.float32
BF16 = jnp.bfloat16

D_MODEL = 1024
N_HEADS = 16
N_KV_HEADS = 4
HEAD_DIM = 64
GQA_GROUP = N_HEADS // N_KV_HEADS
WINDOW = 128
ATTN_BLOCK = 128
Q_DIM = N_HEADS * HEAD_DIM
KV_DIM = N_KV_HEADS * HEAD_DIM
QKV_DIM = Q_DIM + 2 * KV_DIM
REL_BUCKETS = 32
REL_MAX_DIST = 128

D_INNER = 2048
SSM_HEAD_DIM = 64
SSM_HEADS = D_INNER // SSM_HEAD_DIM
SSM_GROUPS = 4
D_STATE = 128
CONV_WIDTH = 4
BC_DIM = 2 * SSM_GROUPS * D_STATE
CONV_DIM = D_INNER + BC_DIM
ZXBC_DIM = D_INNER + CONV_DIM
SSM_CHUNK = 128
SSD_STEP_CHUNKS = 4
GROUP_CH = D_INNER // SSM_GROUPS

N_EXPERT_GROUPS = 8
EXPERTS_PER_GROUP = 8
N_EXPERTS = 64
EXPERT_FF = 512
MOE_BLOCK = 1024

NORM_EPS = 1e-6
LOG2E = 1.4426950408889634
CONV_TAIL = 16
LANES = 128
NEG_BIG = -1e30
VMEM_LIMIT = 56 * 1024 * 1024


def _cparams(sem, flags=None):
    return pltpu.CompilerParams(dimension_semantics=sem, vmem_limit_bytes=VMEM_LIMIT, flags=flags)


def _rms(x, g):
    ms = jnp.mean(x * x, axis=-1, keepdims=True)
    return x * lax.rsqrt(ms + NORM_EPS) * g


def _silu(x):
    h = 0.5 * x
    return h + h * jnp.tanh(h)


def _norm_matmul_kernel(x_ref, g_ref, w_ref, *rest, has_aux):
    if has_aux:
        wa_ref, o_ref, oa_ref, h_scr = rest
    else:
        o_ref, h_scr = rest

    @pl.when(pl.program_id(1) == 0)
    def _():
        h = _rms(x_ref[...], g_ref[...]).astype(BF16)
        h_scr[...] = h
        if has_aux:
            oa_ref[...] = jnp.dot(h, wa_ref[...], preferred_element_type=F32)

    o_ref[...] = jnp.dot(h_scr[...], w_ref[...], preferred_element_type=F32).astype(o_ref.dtype)


def _norm_matmul(x, g, w, w_aux=None, *, tm=1024, tn=512):
    t, d = x.shape
    n = w.shape[1]
    has_aux = w_aux is not None
    in_specs = [
        pl.BlockSpec((tm, d), lambda i, j: (i, 0)),
        pl.BlockSpec((1, d), lambda i, j: (0, 0)),
        pl.BlockSpec((d, tn), lambda i, j: (0, j)),
    ]
    out_shape = [jax.ShapeDtypeStruct((t, n), BF16)]
    out_specs = [pl.BlockSpec((tm, tn), lambda i, j: (i, j))]
    args = [x, g.reshape(1, d), w]
    if has_aux:
        na = w_aux.shape[1]
        in_specs.append(pl.BlockSpec((d, na), lambda i, j: (0, 0)))
        out_shape.append(jax.ShapeDtypeStruct((t, na), F32))
        out_specs.append(pl.BlockSpec((tm, na), lambda i, j: (i, 0)))
        args.append(w_aux)
    res = pl.pallas_call(
        functools.partial(_norm_matmul_kernel, has_aux=has_aux),
        grid=(t // tm, n // tn),
        in_specs=in_specs,
        out_specs=out_specs,
        out_shape=out_shape,
        scratch_shapes=[pltpu.VMEM((tm, d), BF16)],
        compiler_params=_cparams(("parallel", "arbitrary")),
        name="norm_matmul",
    )(*args)
    return res if has_aux else res[0]


def _attn_kernel(sinks_ref, q_ref, kp_ref, kc_ref, vp_ref, vc_ref, bias_ref, o_ref):
    table = jnp.minimum(pl.program_id(1), 1)
    for h in range(N_KV_HEADS):
        ks = slice(h * HEAD_DIM, (h + 1) * HEAD_DIM)
        kb = jnp.concatenate([kp_ref[:, ks], kc_ref[:, ks]], axis=0)
        vb = jnp.concatenate([vp_ref[:, ks], vc_ref[:, ks]], axis=0)
        for g in range(GQA_GROUP):
            hh = h * GQA_GROUP + g
            qh = q_ref[:, hh * HEAD_DIM:(hh + 1) * HEAD_DIM]
            s = lax.dot_general(qh, kb, (((1,), (1,)), ((), ())), preferred_element_type=F32)
            logits = s + bias_ref[table, hh]
            sink = sinks_ref[hh]
            m = jnp.maximum(jnp.max(logits, axis=-1, keepdims=True), sink)
            p = jnp.exp2(logits - m)
            denom = jnp.sum(p, axis=-1, keepdims=True) + jnp.exp2(sink - m)
            o = jnp.dot(p.astype(BF16), vb, preferred_element_type=F32) / denom
            o_ref[:, hh * HEAD_DIM:(hh + 1) * HEAD_DIM] = o.astype(o_ref.dtype)


def _t5_causal_bucket(dist):
    n = jnp.maximum(dist, 0)
    max_exact = REL_BUCKETS // 2
    nf = jnp.maximum(n, 1).astype(F32)
    large = max_exact + (jnp.log(nf / max_exact) / math.log(REL_MAX_DIST / max_exact)
                         * (REL_BUCKETS - max_exact)).astype(jnp.int32)
    large = jnp.minimum(large, REL_BUCKETS - 1)
    return jnp.where(n < max_exact, n, large)


def _attention(qkv, sinks, rel_bias, batch, seq):
    t = qkv.shape[0]
    nb = seq // ATTN_BLOCK
    qi = jnp.arange(ATTN_BLOCK)[:, None]
    ki = jnp.arange(2 * ATTN_BLOCK)[None, :]
    dist = qi + ATTN_BLOCK - ki
    in_window = (dist >= 0) & (dist < WINDOW)
    onehot = (_t5_causal_bucket(dist)[None] == jnp.arange(REL_BUCKETS)[:, None, None]).astype(F32)
    bias = jnp.einsum('hr,rqk->hqk', rel_bias.astype(F32).T, onehot, precision=lax.Precision.HIGHEST)
    bias = jnp.where(in_window[None], bias * LOG2E, NEG_BIG)
    bias_first = jnp.where((ki < ATTN_BLOCK)[None], NEG_BIG, bias)
    bias2 = jnp.stack([bias_first, bias])

    kcol = Q_DIM // KV_DIM
    vcol = kcol + 1

    def prev(b, n):
        return b * nb + jnp.maximum(n - 1, 0)

    return pl.pallas_call(
        _attn_kernel,
        grid=(batch, nb),
        in_specs=[
            pl.BlockSpec(memory_space=pltpu.SMEM),
            pl.BlockSpec((ATTN_BLOCK, Q_DIM), lambda b, n: (b * nb + n, 0)),
            pl.BlockSpec((ATTN_BLOCK, KV_DIM), lambda b, n: (prev(b, n), kcol)),
            pl.BlockSpec((ATTN_BLOCK, KV_DIM), lambda b, n: (b * nb + n, kcol)),
            pl.BlockSpec((ATTN_BLOCK, KV_DIM), lambda b, n: (prev(b, n), vcol)),
            pl.BlockSpec((ATTN_BLOCK, KV_DIM), lambda b, n: (b * nb + n, vcol)),
            pl.BlockSpec((2, N_HEADS, ATTN_BLOCK, 2 * ATTN_BLOCK), lambda b, n: (0, 0, 0, 0)),
        ],
        out_specs=pl.BlockSpec((ATTN_BLOCK, Q_DIM), lambda b, n: (b * nb + n, 0)),
        out_shape=jax.ShapeDtypeStruct((t, Q_DIM), BF16),
        compiler_params=_cparams(("parallel", "arbitrary")),
        name="swa_attention",
    )((sinks.astype(F32) * LOG2E), qkv, qkv, qkv, qkv, qkv, bias2)


def _split3(v):
    hi = v.astype(BF16)
    r = v - hi.astype(F32)
    mid = r.astype(BF16)
    lo = (r - mid.astype(F32)).astype(BF16)
    return hi, mid, lo


def _ssd_kernel(z_ref, x_ref, bc_ref, dt_ref, convw_ref, convb_ref, dtb_ref, alog_ref,
                dskip_ref, normw_ref, shift_ref, shift_tail_ref, expand_ref,
                o_ref, state, xtail, bctail):
    @pl.when(pl.program_id(1) == 0)
    def _():
        state[...] = jnp.zeros_like(state)
        xtail[...] = jnp.zeros_like(xtail)
        bctail[...] = jnp.zeros_like(bctail)

    for k in range(z_ref.shape[0] // SSM_CHUNK):
        _ssd_chunk(pl.ds(k * SSM_CHUNK, SSM_CHUNK), z_ref, x_ref, bc_ref, dt_ref, convw_ref, convb_ref,
                   dtb_ref, alog_ref, dskip_ref, normw_ref, shift_ref, shift_tail_ref, expand_ref,
                   o_ref, state, xtail, bctail)


def _ssd_chunk(rows, z_ref, x_ref, bc_ref, dt_ref, convw_ref, convb_ref, dtb_ref, alog_ref,
               dskip_ref, normw_ref, shift_ref, shift_tail_ref, expand_ref,
               o_ref, state, xtail, bctail):
    L = SSM_CHUNK

    def conv_silu(u_ref, tail, w_lo, width):
        cur = u_ref[rows, :]
        prev = tail[...]
        w = [convw_ref[k:k + 1, w_lo:w_lo + width].astype(BF16) for k in range(CONV_WIDTH)]
        taps = jnp.concatenate([cur * w[k] for k in range(CONV_WIDTH)], axis=0)
        taps_prev = jnp.concatenate([prev * w[k] for k in range(CONV_WIDTH)], axis=0)
        y = (jnp.dot(shift_ref[...], taps, preferred_element_type=F32)
             + jnp.dot(shift_tail_ref[...], taps_prev, preferred_element_type=F32)
             + convb_ref[:, w_lo:w_lo + width])
        tail[...] = cur[L - CONV_TAIL:L]
        return _silu(y)

    xs = conv_silu(x_ref, xtail, 0, D_INNER)
    bcm = conv_silu(bc_ref, bctail, D_INNER, BC_DIM)

    xdt = dt_ref[rows, :] + dtb_ref[...]
    dt = jnp.maximum(xdt, 0.0) + jnp.log1p(jnp.exp(-jnp.abs(xdt)))
    a = dt * (-jnp.exp(alog_ref[...]) * LOG2E)
    row = lax.broadcasted_iota(jnp.int32, (L, L), 0)
    colm = lax.broadcasted_iota(jnp.int32, (L, L), 1)
    causal = row >= colm
    cs3 = jnp.dot(causal.astype(BF16), jnp.concatenate(_split3(a), axis=1), preferred_element_type=F32)
    a_cs = cs3[:, :LANES] + cs3[:, LANES:2 * LANES] + cs3[:, 2 * LANES:]
    a_cs_t = a_cs.T
    dt_t = dt.T
    a_last_col = a_cs_t[:, L - 1:L]
    w_end_t = jnp.exp2(a_last_col - a_cs_t) * dt_t
    exp_acs = jnp.exp2(a_cs)

    cd3 = [jnp.broadcast_to(part.astype(F32), (16, LANES)).astype(BF16)
           for part in _split3(exp_acs[L - 1:L, :])]
    cdx = jnp.dot(jnp.concatenate(cd3, axis=0), expand_ref[...], preferred_element_type=F32)
    cd_exp = cdx[0:1] + cdx[16:17] + cdx[32:33]

    lane = lax.broadcasted_iota(jnp.int32, (L, LANES), 1)
    low_half = lane < SSM_HEAD_DIM

    y_tiles = []
    for g in range(SSM_GROUPS):
        b_g = bcm[:, g * D_STATE:(g + 1) * D_STATE]
        c_g = bcm[:, (SSM_GROUPS + g) * D_STATE:(SSM_GROUPS + g + 1) * D_STATE]
        b_gb = b_g.astype(BF16)
        c_gb = c_g.astype(BF16)
        cb = lax.dot_general(c_gb, b_gb, (((1,), (1,)), ((), ())), preferred_element_type=F32)
        b_gt = b_g.T
        s_g = state[:, g * GROUP_CH:(g + 1) * GROUP_CH]
        y_off_g = jnp.dot(c_gb, s_g.astype(BF16), preferred_element_type=F32)
        for jj in range(GROUP_CH // LANES):
            j = g * (GROUP_CH // LANES) + jj
            lhs_top = []
            lhs_bot = []
            escale = []
            for e in (2 * j, 2 * j + 1):
                acs_b = jnp.broadcast_to(a_cs[:, e:e + 1], (L, L))
                seg = acs_b - a_cs_t[e:e + 1, :]
                dec = jnp.exp2(jnp.where(causal, seg, -jnp.inf))
                lhs_top.append((cb * dec * dt_t[e:e + 1, :]).astype(BF16))
                lhs_bot.append((b_gt * w_end_t[e:e + 1, :]).astype(BF16))
                escale.append(jnp.broadcast_to(exp_acs[:, e:e + 1], (L, LANES)))
            x_tile = xs[:, j * LANES:(j + 1) * LANES]
            x_lo = jnp.where(low_half, x_tile, 0.0).astype(BF16)
            x_hi = jnp.where(low_half, 0.0, x_tile).astype(BF16)
            lhs = jnp.concatenate([jnp.concatenate(lhs_top, axis=1),
                                   jnp.concatenate(lhs_bot, axis=1)], axis=0)
            rhs = jnp.concatenate([x_lo, x_hi], axis=0)
            r = jnp.dot(lhs, rhs, preferred_element_type=F32)
            y_off = y_off_g[:, jj * LANES:(jj + 1) * LANES]
            y_tiles.append(r[0:L] + jnp.where(low_half, escale[0], escale[1]) * y_off)
            sl = slice(j * LANES, (j + 1) * LANES)
            state[:, sl] = cd_exp[:, sl] * state[:, sl] + r[L:2 * L]

    y = jnp.concatenate(y_tiles, axis=1)
    y = y + xs * dskip_ref[...]
    gated = y * _silu(z_ref[rows, :].astype(F32))
    outs = []
    for g in range(SSM_GROUPS):
        gg = gated[:, g * GROUP_CH:(g + 1) * GROUP_CH]
        ms = jnp.mean(gg * gg, axis=-1, keepdims=True)
        outs.append(gg * lax.rsqrt(ms + NORM_EPS))
    o_ref[rows, :] = (jnp.concatenate(outs, axis=1) * normw_ref[...]).astype(o_ref.dtype)


def _ssd(zxbc, dt_raw, conv_w, conv_b, dt_bias, a_log, d_skip, norm_w, batch, seq):
    t = zxbc.shape[0]
    L = SSM_CHUNK
    rows = SSD_STEP_CHUNKS * L
    nc = seq // rows

    def pad_heads(v):
        return jnp.pad(v.astype(F32), (0, LANES - SSM_HEADS)).reshape(1, LANES)

    def rowmap(col):
        return lambda b, c: (b * nc + c, col)

    def const2(b, c):
        return (0, 0)

    l_idx = jnp.arange(L)[:, None]
    shift = jnp.concatenate(
        [(jnp.arange(L)[None, :] == l_idx - (CONV_WIDTH - 1) + k) for k in range(CONV_WIDTH)],
        axis=1).astype(BF16)
    shift_tail = jnp.concatenate(
        [(jnp.arange(-CONV_TAIL, 0)[None, :] == l_idx - (CONV_WIDTH - 1) + k) for k in range(CONV_WIDTH)],
        axis=1).astype(BF16)
    expand = (jnp.arange(D_INNER)[None, :] // SSM_HEAD_DIM == jnp.arange(LANES)[:, None]).astype(BF16)

    return pl.pallas_call(
        _ssd_kernel,
        grid=(batch, nc),
        in_specs=[
            pl.BlockSpec((rows, D_INNER), rowmap(0)),
            pl.BlockSpec((rows, D_INNER), rowmap(1)),
            pl.BlockSpec((rows, BC_DIM), rowmap(2 * D_INNER // BC_DIM)),
            pl.BlockSpec((rows, LANES), rowmap(0)),
            pl.BlockSpec((CONV_WIDTH, CONV_DIM), const2),
            pl.BlockSpec((1, CONV_DIM), const2),
            pl.BlockSpec((1, LANES), const2),
            pl.BlockSpec((1, LANES), const2),
            pl.BlockSpec((1, D_INNER), const2),
            pl.BlockSpec((1, D_INNER), const2),
            pl.BlockSpec((L, CONV_WIDTH * L), const2),
            pl.BlockSpec((L, CONV_WIDTH * CONV_TAIL), const2),
            pl.BlockSpec((LANES, D_INNER), const2),
        ],
        out_specs=pl.BlockSpec((rows, D_INNER), rowmap(0)),
        out_shape=jax.ShapeDtypeStruct((t, D_INNER), BF16),
        scratch_shapes=[
            pltpu.VMEM((D_STATE, D_INNER), F32),
            pltpu.VMEM((CONV_TAIL, D_INNER), BF16),
            pltpu.VMEM((CONV_TAIL, BC_DIM), BF16),
        ],
        compiler_params=_cparams(("parallel", "arbitrary")),
        name="ssd_core",
    )(zxbc, zxbc, zxbc, dt_raw, conv_w.astype(F32), conv_b.astype(F32).reshape(1, CONV_DIM),
      pad_heads(dt_bias), pad_heads(a_log),
      jnp.repeat(d_skip.astype(F32), SSM_HEAD_DIM).reshape(1, D_INNER),
      norm_w.astype(F32).reshape(1, D_INNER), shift, shift_tail, expand)


GROUP_ROW0 = N_EXPERTS


def _router_kernel(a_ref, wp_ref, res_ref, g_ref, w_ref, b_ref, earlier_ref,
                   x_ref, ids_ref, gates_ref, cnt_ref, carry, wt_split):
    i = pl.program_id(0)
    x_ref[...] = res_ref[...] + jnp.dot(a_ref[...], wp_ref[...], preferred_element_type=F32)

    @pl.when(i == 0)
    def _():
        carry[...] = jnp.zeros_like(carry)
        wt = w_ref[...].T
        wt_hi = wt.astype(BF16)
        wt_split[0:LANES, :] = wt_hi
        wt_split[LANES:, :] = (wt - wt_hi.astype(F32)).astype(BF16)

    h = _rms(x_ref[...], g_ref[...])
    h_hi = h.astype(BF16)
    h_mid = (h - h_hi.astype(F32)).astype(BF16)
    nt = (((1,), (1,)), ((), ()))
    top = lax.dot_general(wt_split[...], h_hi, nt, preferred_element_type=F32)
    cross = lax.dot_general(wt_split[0:LANES, :], h_mid, nt, preferred_element_type=F32)
    logits = top[:LANES] + (top[LANES:] + cross) + b_ref[...]
    tm = logits.shape[1]
    row = lax.broadcasted_iota(jnp.int32, (LANES, tm), 0)
    ninf = -jnp.inf

    def first_argmax(v, vmax):
        return jnp.min(jnp.where(v == vmax, row, LANES), axis=0, keepdims=True)

    is_group = (row >= GROUP_ROW0) & (row < GROUP_ROW0 + N_EXPERT_GROUPS)
    glog = jnp.where(is_group, logits, ninf)
    gmax = jnp.max(glog, axis=0, keepdims=True)
    g_idx = first_argmax(glog, gmax) - GROUP_ROW0
    g_p = 1.0 / jnp.sum(jnp.exp(glog - gmax), axis=0, keepdims=True)

    in_group = (row < N_EXPERTS) & (jnp.right_shift(row, 3) == g_idx)
    elog = jnp.where(in_group, logits, ninf)
    m1 = jnp.max(elog, axis=0, keepdims=True)
    e1 = first_argmax(elog, m1)
    elog2 = jnp.where(row == e1, ninf, elog)
    m2 = jnp.max(elog2, axis=0, keepdims=True)
    e2 = first_argmax(elog2, m2)
    zsum = jnp.sum(jnp.exp(elog - m1), axis=0, keepdims=True)
    p1 = 1.0 / zsum
    p2 = jnp.exp(m2 - m1) / zsum
    psum = p1 + p2
    w1 = p1 / psum * g_p
    w2 = p2 / psum * g_p

    oh1 = row == e1
    oh2 = row == e2
    onehot = jnp.logical_or(oh1, oh2).astype(F32)
    before = jnp.dot(onehot.astype(BF16), earlier_ref[...], preferred_element_type=F32) + carry[:, 0:1]
    r1 = jnp.sum(jnp.where(oh1, before, 0.0), axis=0, keepdims=True).astype(jnp.int32)
    r2 = jnp.sum(jnp.where(oh2, before, 0.0), axis=0, keepdims=True).astype(jnp.int32)
    carry[...] = carry[...] + jnp.sum(onehot, axis=1, keepdims=True)

    row8 = lax.broadcasted_iota(jnp.int32, (8, tm), 0)
    ids_ref[...] = jnp.where(row8 == 0, e1, jnp.where(row8 == 1, e2,
                             jnp.where(row8 == 2, r1, jnp.where(row8 == 3, r2, 0))))
    gates_ref[...] = jnp.where(row == 0, w1, jnp.where(row == 1, w2, 0.0)).T
    cnt_ref[...] = carry[...]


def _router(a, w_proj, res, ln_w, w_group, b_group, w_expert, b_expert, *, tm=512):
    t, d = res.shape
    k = a.shape[1]
    pad = LANES - N_EXPERTS - N_EXPERT_GROUPS
    w_r = jnp.concatenate([w_expert, w_group, jnp.zeros((d, pad), F32)], axis=1).astype(F32)
    b_r = jnp.concatenate([b_expert, b_group, jnp.zeros((pad,), F32)]).astype(F32).reshape(LANES, 1)
    earlier = (jnp.arange(tm)[:, None] < jnp.arange(tm)[None, :]).astype(BF16)
    return pl.pallas_call(
        _router_kernel,
        grid=(t // tm,),
        in_specs=[
            pl.BlockSpec((tm, k), lambda i: (i, 0)),
            pl.BlockSpec((k, d), lambda i: (0, 0)),
            pl.BlockSpec((tm, d), lambda i: (i, 0)),
            pl.BlockSpec((1, d), lambda i: (0, 0)),
            pl.BlockSpec((d, LANES), lambda i: (0, 0)),
            pl.BlockSpec((LANES, 1), lambda i: (0, 0)),
            pl.BlockSpec((tm, tm), lambda i: (0, 0)),
        ],
        out_specs=[
            pl.BlockSpec((tm, d), lambda i: (i, 0)),
            pl.BlockSpec((8, tm), lambda i: (0, i)),
            pl.BlockSpec((tm, LANES), lambda i: (i, 0)),
            pl.BlockSpec((LANES, LANES), lambda i: (0, 0)),
        ],
        out_shape=[
            jax.ShapeDtypeStruct((t, d), F32),
            jax.ShapeDtypeStruct((8, t), jnp.int32),
            jax.ShapeDtypeStruct((t, LANES), F32),
            jax.ShapeDtypeStruct((LANES, LANES), F32),
        ],
        scratch_shapes=[pltpu.VMEM((LANES, LANES), F32), pltpu.VMEM((2 * LANES, d), BF16)],
        compiler_params=_cparams(("arbitrary",)),
        name="proj_moe_router",
    )(a, w_proj, res, ln_w.reshape(1, d), w_r, b_r, earlier)


U32 = jnp.uint32
ROW_WORDS = D_MODEL // 2
ROW_SUB = ROW_WORDS // LANES
HIGH_HALF = 0xFFFF0000
ROW_MOVE_TOKENS = 1024


def _pack_rows(v):
    lo = lax.bitcast_convert_type(v[:, :ROW_WORDS].astype(BF16).astype(F32), U32)
    hi = lax.bitcast_convert_type(v[:, ROW_WORDS:].astype(BF16).astype(F32), U32)
    return jnp.right_shift(lo, jnp.uint32(16)) | (hi & jnp.uint32(HIGH_HALF))


def _unpack_rows(w):
    lo = lax.bitcast_convert_type(jnp.left_shift(w, jnp.uint32(16)), F32)
    hi = lax.bitcast_convert_type(w & jnp.uint32(HIGH_HALF), F32)
    return lo, hi


def _store_rows(ref, packed):
    n = packed.shape[0]
    for c in range(ROW_SUB):
        ref[pl.ds(c, n, stride=ROW_SUB), :] = packed[:, c * LANES:(c + 1) * LANES]


def _load_rows(ref, n):
    return jnp.concatenate([ref[pl.ds(c, n, stride=ROW_SUB), :] for c in range(ROW_SUB)], axis=1)


def _row_slice(ref, row):
    return ref.at[pl.ds(pl.multiple_of(row * ROW_SUB, ROW_SUB), ROW_SUB)]


def _dispatch_kernel(zf_ref, dest_ref, x_ref, g_ref, xs_ref, hbuf, zbuf, sem, zsem):
    tb = x_ref.shape[0]
    n_blocks = zf_ref.shape[0]
    blk_rows = MOE_BLOCK * ROW_SUB

    @pl.when(pl.program_id(0) == 0)
    def _():
        zbuf[...] = jnp.zeros_like(zbuf)

        def zero_copy(b):
            start = pl.multiple_of(b * blk_rows, blk_rows)
            return pltpu.make_async_copy(zbuf, xs_ref.at[pl.ds(start, blk_rows)], zsem)

        def zissue(b, carry):
            @pl.when(zf_ref[b] != 0)
            def _():
                zero_copy(b).start()
            return carry

        def zdrain(b, carry):
            @pl.when(zf_ref[b] != 0)
            def _():
                zero_copy(b).wait()
            return carry

        lax.fori_loop(0, n_blocks, zissue, 0)
        lax.fori_loop(0, n_blocks, zdrain, 0)

    _store_rows(hbuf, _pack_rows(_rms(x_ref[...], g_ref[...])))
    for r in range(tb):
        for k in range(2):
            d = dest_ref[k, r]
            pltpu.make_async_copy(_row_slice(hbuf, r), _row_slice(xs_ref, d), sem).start(priority=k)
    for k in range(2):
        pltpu.make_async_copy(hbuf, xs_ref.at[pl.ds(0, tb * ROW_SUB)], sem).wait()


def _dispatch(x, ln_w, dest, zero_flag, n_slots, *, tb=ROW_MOVE_TOKENS):
    t, d = x.shape
    nblk = t // tb
    return pl.pallas_call(
        _dispatch_kernel,
        grid_spec=pltpu.PrefetchScalarGridSpec(
            num_scalar_prefetch=1,
            grid=(nblk,),
            in_specs=[
                pl.BlockSpec((2, tb), lambda i, zf: (0, i), memory_space=pltpu.SMEM),
                pl.BlockSpec((tb, d), lambda i, zf: (i, 0)),
                pl.BlockSpec((1, d), lambda i, zf: (0, 0)),
            ],
            out_specs=pl.BlockSpec(memory_space=pl.ANY),
            scratch_shapes=[pltpu.VMEM((tb * ROW_SUB, LANES), U32),
                            pltpu.VMEM((MOE_BLOCK * ROW_SUB, LANES), U32),
                            pltpu.SemaphoreType.DMA(()), pltpu.SemaphoreType.DMA(())],
        ),
        out_shape=jax.ShapeDtypeStruct((n_slots * ROW_SUB, LANES), U32),
        compiler_params=_cparams(("arbitrary",)),
        name="moe_dispatch",
    )(zero_flag, dest, x, ln_w.reshape(1, d))


def _expert_kernel(be_ref, bpos_ref, elist_ref, meta_ref, xs_ref, wg_hbm, wu_hbm, wd_hbm, y_ref,
                   wg_s, wu_s, wd_s, wg_f, wu_f, wd_f, sems, *, layer):
    i = pl.program_id(0)
    active = i < meta_ref[0]
    pos = bpos_ref[i]
    changed = jnp.logical_or(i == 0, be_ref[i] != be_ref[jnp.maximum(i - 1, 0)])
    streams = ((wg_hbm, wg_f), (wu_hbm, wu_f), (wd_hbm, wd_f))

    def fetch(p, slot):
        e = elist_ref[p]
        return [pltpu.make_async_copy(hbm.at[layer, e], buf.at[slot], sems.at[slot, t])
                for t, (hbm, buf) in enumerate(streams)]

    @pl.when(jnp.logical_and(active, i == 0))
    def _():
        for cp in fetch(0, 0):
            cp.start()

    @pl.when(jnp.logical_and(active, changed))
    def _():
        slot = lax.rem(pos, 2)
        for cp in fetch(pos, slot):
            cp.wait()

        @pl.when(pos + 1 < meta_ref[1])
        def _():
            for cp in fetch(pos + 1, 1 - slot):
                cp.start()

        wg_s[...] = wg_f[slot].astype(BF16)
        wu_s[...] = wu_f[slot].astype(BF16)
        wd_s[...] = wd_f[slot].astype(BF16)

    @pl.when(active)
    def _():
        lo, hi = _unpack_rows(_load_rows(xs_ref, MOE_BLOCK))
        xb = jnp.concatenate([lo.astype(BF16), hi.astype(BF16)], axis=1)
        gate = jnp.dot(xb, wg_s[...], preferred_element_type=F32)
        up = jnp.dot(xb, wu_s[...], preferred_element_type=F32)
        hid = (_silu(gate) * up).astype(BF16)
        _store_rows(y_ref, _pack_rows(jnp.dot(hid, wd_s[...], preferred_element_type=F32)))

    @pl.when(jnp.logical_not(active))
    def _():
        y_ref[...] = jnp.zeros_like(y_ref)


def _experts(xs, block_e, block_pos, expert_list, meta, w_gate, w_up, w_down, layer):
    blk_rows = MOE_BLOCK * ROW_SUB
    n_blocks = xs.shape[0] // blk_rows
    d = D_MODEL

    def xmap(i, be, bpos, elist, meta):
        return (jnp.maximum(jnp.minimum(i, meta[0] - 1), 0), 0)

    return pl.pallas_call(
        functools.partial(_expert_kernel, layer=layer),
        grid_spec=pltpu.PrefetchScalarGridSpec(
            num_scalar_prefetch=4,
            grid=(n_blocks,),
            in_specs=[
                pl.BlockSpec((blk_rows, LANES), xmap),
                pl.BlockSpec(memory_space=pl.ANY),
                pl.BlockSpec(memory_space=pl.ANY),
                pl.BlockSpec(memory_space=pl.ANY),
            ],
            out_specs=pl.BlockSpec((blk_rows, LANES), lambda i, be, bpos, elist, meta: (i, 0)),
            scratch_shapes=[
                pltpu.VMEM((d, EXPERT_FF), BF16),
                pltpu.VMEM((d, EXPERT_FF), BF16),
                pltpu.VMEM((EXPERT_FF, d), BF16),
                pltpu.VMEM((2, d, EXPERT_FF), F32),
                pltpu.VMEM((2, d, EXPERT_FF), F32),
                pltpu.VMEM((2, EXPERT_FF, d), F32),
                pltpu.SemaphoreType.DMA((2, 3)),
            ],
        ),
        out_shape=jax.ShapeDtypeStruct(xs.shape, U32),
        compiler_params=_cparams(("arbitrary",)),
        name="moe_experts",
    )(block_e, block_pos, expert_list, meta, xs, w_gate, w_up, w_down)


def _combine_kernel(dest_ref, x_ref, gates_ref, y_ref, *rest, final_norm):
    if final_norm:
        g_ref, o_ref, buf0, buf1, sem = rest
    else:
        o_ref, buf0, buf1, sem = rest
    tb = x_ref.shape[0]
    bufs = (buf0, buf1)

    for r in range(tb):
        for k in range(2):
            d = dest_ref[k, r]
            pltpu.make_async_copy(_row_slice(y_ref, d), _row_slice(bufs[k], r), sem).start(priority=k)
    for k in range(2):
        pltpu.make_async_copy(y_ref.at[pl.ds(0, tb * ROW_SUB)], bufs[k], sem).wait()

    gt = gates_ref[...]
    lo0, hi0 = _unpack_rows(_load_rows(buf0, tb))
    lo1, hi1 = _unpack_rows(_load_rows(buf1, tb))
    w0 = gt[:, 0:1]
    w1 = gt[:, 1:2]
    out = x_ref[...] + jnp.concatenate([lo0 * w0 + lo1 * w1, hi0 * w0 + hi1 * w1], axis=1)
    if final_norm:
        out = _rms(out, g_ref[...])
    o_ref[...] = out


def _combine(x, gates, y, dest, final_w=None, *, tb=ROW_MOVE_TOKENS):
    t, d = x.shape
    nblk = t // tb
    final_norm = final_w is not None
    in_specs = [
        pl.BlockSpec((2, tb), lambda i: (0, i), memory_space=pltpu.SMEM),
        pl.BlockSpec((tb, d), lambda i: (i, 0)),
        pl.BlockSpec((tb, LANES), lambda i: (i, 0)),
        pl.BlockSpec(memory_space=pl.ANY),
    ]
    args = [dest, x, gates, y]
    if final_norm:
        in_specs.append(pl.BlockSpec((1, d), lambda i: (0, 0)))
        args.append(final_w.reshape(1, d))
    return pl.pallas_call(
        functools.partial(_combine_kernel, final_norm=final_norm),
        grid=(nblk,),
        in_specs=in_specs,
        out_specs=pl.BlockSpec((tb, d), lambda i: (i, 0)),
        out_shape=jax.ShapeDtypeStruct((t, d), F32),
        scratch_shapes=[pltpu.VMEM((tb * ROW_SUB, LANES), U32), pltpu.VMEM((tb * ROW_SUB, LANES), U32),
                        pltpu.SemaphoreType.DMA(())],
        compiler_params=_cparams(("arbitrary",)),
        name="moe_combine",
    )(*args)


def _proj_moe(a, w_proj, res, ln_w, w_group, b_group, w_expert, b_expert, w_gate, w_up, w_down, layer,
              final_w=None):
    t = res.shape[0]
    n_assign = t * 2
    n_blocks = -(-n_assign // MOE_BLOCK) + N_EXPERTS
    n_slots = n_blocks * MOE_BLOCK

    x, ids, gates, cnt = _router(a, w_proj, res, ln_w, w_group, b_group, w_expert, b_expert)
    counts = cnt[:N_EXPERTS, 0].astype(jnp.int32)
    padded = (counts + MOE_BLOCK - 1) // MOE_BLOCK * MOE_BLOCK
    p_ends = jnp.cumsum(padded)
    p_starts = p_ends - padded
    experts = jnp.arange(N_EXPERTS, dtype=jnp.int32)
    start_of = jnp.sum(jnp.where(ids[0:2, :, None] == experts, p_starts, 0), axis=-1)
    dest = (start_of + ids[2:4]).astype(jnp.int32)
    block_start = jnp.arange(n_blocks, dtype=jnp.int32) * MOE_BLOCK
    block_e = jnp.minimum(jnp.sum((p_ends[None, :] <= block_start[:, None]).astype(jnp.int32), axis=1),
                          N_EXPERTS - 1)
    n_active = (p_ends[-1:] // MOE_BLOCK).astype(jnp.int32)
    valid_end = jnp.sum(jnp.where(block_e[:, None] == experts, p_starts + counts, 0), axis=-1)
    zero_flag = jnp.logical_or(block_start + MOE_BLOCK > valid_end,
                               block_start >= p_ends[-1]).astype(jnp.int32)

    has_tokens = counts > 0
    list_pos = jnp.cumsum(has_tokens.astype(jnp.int32)) - 1
    expert_list = jnp.sum(jnp.where(jnp.logical_and(has_tokens[None, :], list_pos[None, :] == experts[:, None]),
                                    experts[None, :], 0), axis=1).astype(jnp.int32)
    block_pos = jnp.sum(jnp.where(block_e[:, None] == experts, list_pos, 0), axis=-1).astype(jnp.int32)
    meta = jnp.concatenate([n_active, jnp.sum(has_tokens.astype(jnp.int32), keepdims=True)]).astype(jnp.int32)

    xs = _dispatch(x, ln_w, dest, zero_flag, n_slots)
    y = _experts(xs, block_e, block_pos, expert_list, meta, w_gate, w_up, w_down, layer)
    return _combine(x, gates, y, dest, final_w)


def kernel(x, rel_bias, ln_mix, ln_ffn, ln_final, attn_w_qkv, attn_sinks, attn_w_o, ssm_w_in,
           ssm_conv_w, ssm_conv_b, ssm_dt_bias, ssm_a_log, ssm_d, ssm_norm_w, ssm_w_out,
           moe_w_group, moe_b_group, moe_w_expert, moe_b_expert, moe_w_gate, moe_w_up, moe_w_down):
    batch, seq, d = x.shape
    t = batch * seq
    xf = x.reshape(t, d).astype(F32)

    q_scale = jnp.where(jnp.arange(QKV_DIM) < Q_DIM, HEAD_DIM ** -0.5 * LOG2E, 1.0).astype(F32)
    qkv = _norm_matmul(xf, ln_mix[0], (attn_w_qkv[0] * q_scale).astype(BF16), tm=512, tn=QKV_DIM)
    att = _attention(qkv, attn_sinks[0], rel_bias, batch, seq)
    xf = _proj_moe(att, attn_w_o[0].astype(BF16), xf, ln_ffn[0], moe_w_group[0], moe_b_group[0],
                   moe_w_expert[0], moe_b_expert[0], moe_w_gate, moe_w_up, moe_w_down, 0)

    w_in = ssm_w_in[0]
    w_dt = jnp.pad(w_in[:, ZXBC_DIM:], ((0, 0), (0, LANES - SSM_HEADS))).astype(BF16)
    zxbc, dt_raw = _norm_matmul(xf, ln_mix[1], w_in[:, :ZXBC_DIM].astype(BF16), w_dt, tm=512, tn=ZXBC_DIM)
    gated = _ssd(zxbc, dt_raw, ssm_conv_w[0], ssm_conv_b[0], ssm_dt_bias[0], ssm_a_log[0],
                 ssm_d[0], ssm_norm_w[0], batch, seq)
    xf = _proj_moe(gated, ssm_w_out[0].astype(BF16), xf, ln_ffn[1], moe_w_group[1], moe_b_group[1],
                   moe_w_expert[1], moe_b_expert[1], moe_w_gate, moe_w_up, moe_w_down, 1, final_w=ln_final)
    return xf.reshape(batch, seq, d).astype(x.dtype)
```

```python
import functools
import math

import jax
import jax.numpy as jnp
from jax import lax
from jax.experimental import pallas as pl
from jax.experimental.pallas import tpu as pltpu

F32 = jnp.float32
BF16 = jnp.bfloat16

D_MODEL = 1024
N_HEADS = 16
N_KV_HEADS = 4
HEAD_DIM = 64
GQA_GROUP = N_HEADS // N_KV_HEADS
WINDOW = 128
ATTN_BLOCK = 128
Q_DIM = N_HEADS * HEAD_DIM
KV_DIM = N_KV_HEADS * HEAD_DIM
QKV_DIM = Q_DIM + 2 * KV_DIM
REL_BUCKETS = 32
REL_MAX_DIST = 128

D_INNER = 2048
SSM_HEAD_DIM = 64
SSM_HEADS = D_INNER // SSM_HEAD_DIM
SSM_GROUPS = 4
D_STATE = 128
CONV_WIDTH = 4
BC_DIM = 2 * SSM_GROUPS * D_STATE
CONV_DIM = D_INNER + BC_DIM
ZXBC_DIM = D_INNER + CONV_DIM
SSM_CHUNK = 128
SSD_STEP_CHUNKS = 8
GROUP_CH = D_INNER // SSM_GROUPS

N_EXPERT_GROUPS = 8
EXPERTS_PER_GROUP = 8
N_EXPERTS = 64
EXPERT_FF = 512
MOE_BLOCK = 512

NORM_EPS = 1e-6
LOG2E = 1.4426950408889634
CONV_TAIL = 16
LANES = 128
NEG_BIG = -1e30
VMEM_LIMIT = 56 * 1024 * 1024


def _cparams(sem, flags=None):
    return pltpu.CompilerParams(dimension_semantics=sem, vmem_limit_bytes=VMEM_LIMIT, flags=flags)


def _rms(x, g):
    ms = jnp.mean(x * x, axis=-1, keepdims=True)
    return x * lax.rsqrt(ms + NORM_EPS) * g


def _silu(x):
    h = 0.5 * x
    return h + h * jnp.tanh(h)


def _norm_matmul_kernel(x_ref, g_ref, w_ref, *rest, has_aux):
    if has_aux:
        wa_ref, o_ref, oa_ref, h_scr = rest
    else:
        o_ref, h_scr = rest

    @pl.when(pl.program_id(1) == 0)
    def _():
        h = _rms(x_ref[...], g_ref[...]).astype(BF16)
        h_scr[...] = h
        if has_aux:
            oa_ref[...] = jnp.dot(h, wa_ref[...], preferred_element_type=F32)

    o_ref[...] = jnp.dot(h_scr[...], w_ref[...], preferred_element_type=F32).astype(o_ref.dtype)


def _norm_matmul(x, g, w, w_aux=None, *, tm=1024, tn=512):
    t, d = x.shape
    n = w.shape[1]
    has_aux = w_aux is not None
    in_specs = [
        pl.BlockSpec((tm, d), lambda i, j: (i, 0)),
        pl.BlockSpec((1, d), lambda i, j: (0, 0)),
        pl.BlockSpec((d, tn), lambda i, j: (0, j), pipeline_mode=pl.Buffered(1) if tn == n else None),
    ]
    out_shape = [jax.ShapeDtypeStruct((t, n), BF16)]
    out_specs = [pl.BlockSpec((tm, tn), lambda i, j: (i, j))]
    args = [x, g.reshape(1, d), w]
    if has_aux:
        na = w_aux.shape[1]
        in_specs.append(pl.BlockSpec((d, na), lambda i, j: (0, 0)))
        out_shape.append(jax.ShapeDtypeStruct((t, na), F32))
        out_specs.append(pl.BlockSpec((tm, na), lambda i, j: (i, 0)))
        args.append(w_aux)
    res = pl.pallas_call(
        functools.partial(_norm_matmul_kernel, has_aux=has_aux),
        grid=(t // tm, n // tn),
        in_specs=in_specs,
        out_specs=out_specs,
        out_shape=out_shape,
        scratch_shapes=[pltpu.VMEM((tm, d), BF16)],
        compiler_params=_cparams(("parallel", "arbitrary")),
        name="norm_matmul",
    )(*args)
    return res if has_aux else res[0]


def _attn_kernel(sinks_ref, q_ref, kp_ref, kc_ref, vp_ref, vc_ref, bias_ref, o_ref):
    table = jnp.minimum(pl.program_id(1), 1)
    for h in range(N_KV_HEADS):
        ks = slice(h * HEAD_DIM, (h + 1) * HEAD_DIM)
        kb = jnp.concatenate([kp_ref[:, ks], kc_ref[:, ks]], axis=0)
        vb = jnp.concatenate([vp_ref[:, ks], vc_ref[:, ks]], axis=0)
        for g in range(GQA_GROUP):
            hh = h * GQA_GROUP + g
            qh = q_ref[:, hh * HEAD_DIM:(hh + 1) * HEAD_DIM]
            s = lax.dot_general(qh, kb, (((1,), (1,)), ((), ())), preferred_element_type=F32)
            logits = s + bias_ref[table, hh]
            sink = sinks_ref[hh]
            m = jnp.maximum(jnp.max(logits, axis=-1, keepdims=True), sink)
            p = jnp.exp2(logits - m)
            denom = jnp.sum(p, axis=-1, keepdims=True) + jnp.exp2(sink - m)
            o = jnp.dot(p.astype(BF16), vb, preferred_element_type=F32) / denom
            o_ref[:, hh * HEAD_DIM:(hh + 1) * HEAD_DIM] = o.astype(o_ref.dtype)


def _t5_causal_bucket(dist):
    n = jnp.maximum(dist, 0)
    max_exact = REL_BUCKETS // 2
    nf = jnp.maximum(n, 1).astype(F32)
    large = max_exact + (jnp.log(nf / max_exact) / math.log(REL_MAX_DIST / max_exact)
                         * (REL_BUCKETS - max_exact)).astype(jnp.int32)
    large = jnp.minimum(large, REL_BUCKETS - 1)
    return jnp.where(n < max_exact, n, large)


def _attention(qkv, sinks, rel_bias, batch, seq):
    t = qkv.shape[0]
    nb = seq // ATTN_BLOCK
    qi = jnp.arange(ATTN_BLOCK)[:, None]
    ki = jnp.arange(2 * ATTN_BLOCK)[None, :]
    dist = qi + ATTN_BLOCK - ki
    in_window = (dist >= 0) & (dist < WINDOW)
    onehot = (_t5_causal_bucket(dist)[None] == jnp.arange(REL_BUCKETS)[:, None, None]).astype(F32)
    bias = jnp.einsum('hr,rqk->hqk', rel_bias.astype(F32).T, onehot, precision=lax.Precision.HIGHEST)
    bias = jnp.where(in_window[None], bias * LOG2E, NEG_BIG)
    bias_first = jnp.where((ki < ATTN_BLOCK)[None], NEG_BIG, bias)
    bias2 = jnp.stack([bias_first, bias])

    kcol = Q_DIM // KV_DIM
    vcol = kcol + 1

    def prev(b, n):
        return b * nb + jnp.maximum(n - 1, 0)

    return pl.pallas_call(
        _attn_kernel,
        grid=(batch, nb),
        in_specs=[
            pl.BlockSpec(memory_space=pltpu.SMEM),
            pl.BlockSpec((ATTN_BLOCK, Q_DIM), lambda b, n: (b * nb + n, 0)),
            pl.BlockSpec((ATTN_BLOCK, KV_DIM), lambda b, n: (prev(b, n), kcol)),
            pl.BlockSpec((ATTN_BLOCK, KV_DIM), lambda b, n: (b * nb + n, kcol)),
            pl.BlockSpec((ATTN_BLOCK, KV_DIM), lambda b, n: (prev(b, n), vcol)),
            pl.BlockSpec((ATTN_BLOCK, KV_DIM), lambda b, n: (b * nb + n, vcol)),
            pl.BlockSpec((2, N_HEADS, ATTN_BLOCK, 2 * ATTN_BLOCK), lambda b, n: (0, 0, 0, 0)),
        ],
        out_specs=pl.BlockSpec((ATTN_BLOCK, Q_DIM), lambda b, n: (b * nb + n, 0)),
        out_shape=jax.ShapeDtypeStruct((t, Q_DIM), BF16),
        compiler_params=_cparams(("parallel", "arbitrary")),
        name="swa_attention",
    )((sinks.astype(F32) * LOG2E), qkv, qkv, qkv, qkv, qkv, bias2)


def _split3(v):
    hi = v.astype(BF16)
    r = v - hi.astype(F32)
    mid = r.astype(BF16)
    lo = (r - mid.astype(F32)).astype(BF16)
    return hi, mid, lo


def _ssd_kernel(z_ref, x_ref, bc_ref, dt_ref, convw_ref, convb_ref, dtb_ref, alog_ref,
                dskip_ref, normw_ref, shift_ref, shift_tail_ref, expand_ref,
                o_ref, state, xtail, bctail):
    @pl.when(pl.program_id(1) == 0)
    def _():
        state[...] = jnp.zeros_like(state)
        xtail[...] = jnp.zeros_like(xtail)
        bctail[...] = jnp.zeros_like(bctail)

    for k in range(z_ref.shape[0] // SSM_CHUNK):
        _ssd_chunk(pl.ds(k * SSM_CHUNK, SSM_CHUNK), z_ref, x_ref, bc_ref, dt_ref, convw_ref, convb_ref,
                   dtb_ref, alog_ref, dskip_ref, normw_ref, shift_ref, shift_tail_ref, expand_ref,
                   o_ref, state, xtail, bctail)


def _ssd_chunk(rows, z_ref, x_ref, bc_ref, dt_ref, convw_ref, convb_ref, dtb_ref, alog_ref,
               dskip_ref, normw_ref, shift_ref, shift_tail_ref, expand_ref,
               o_ref, state, xtail, bctail):
    L = SSM_CHUNK

    def conv_silu(u_ref, tail, w_lo, width):
        cur = u_ref[rows, :]
        prev = tail[...]
        w = [convw_ref[k:k + 1, w_lo:w_lo + width].astype(BF16) for k in range(CONV_WIDTH)]
        taps = jnp.concatenate([cur * w[k] for k in range(CONV_WIDTH)], axis=0)
        taps_prev = jnp.concatenate([prev * w[k] for k in range(CONV_WIDTH)], axis=0)
        y = (jnp.dot(shift_ref[...], taps, preferred_element_type=F32)
             + jnp.dot(shift_tail_ref[...], taps_prev, preferred_element_type=F32)
             + convb_ref[:, w_lo:w_lo + width])
        tail[...] = cur[L - CONV_TAIL:L]
        return _silu(y)

    xs = conv_silu(x_ref, xtail, 0, D_INNER)
    bcm = conv_silu(bc_ref, bctail, D_INNER, BC_DIM)

    xdt = dt_ref[rows, :] + dtb_ref[...]
    dt = jnp.maximum(xdt, 0.0) + jnp.log1p(jnp.exp(-jnp.abs(xdt)))
    a = dt * (-jnp.exp(alog_ref[...]) * LOG2E)
    row = lax.broadcasted_iota(jnp.int32, (L, L), 0)
    colm = lax.broadcasted_iota(jnp.int32, (L, L), 1)
    causal = row >= colm
    cs3 = jnp.dot(causal.astype(BF16), jnp.concatenate(_split3(a), axis=1), preferred_element_type=F32)
    a_cs = cs3[:, :LANES] + cs3[:, LANES:2 * LANES] + cs3[:, 2 * LANES:]
    a_cs_t = a_cs.T
    dt_t = dt.T
    a_last_col = a_cs_t[:, L - 1:L]
    w_end_t = jnp.exp2(a_last_col - a_cs_t) * dt_t
    exp_acs = jnp.exp2(a_cs)

    cd3 = [jnp.broadcast_to(part.astype(F32), (16, LANES)).astype(BF16)
           for part in _split3(exp_acs[L - 1:L, :])]
    cdx = jnp.dot(jnp.concatenate(cd3, axis=0), expand_ref[...], preferred_element_type=F32)
    cd_exp = cdx[0:1] + cdx[16:17] + cdx[32:33]

    lane = lax.broadcasted_iota(jnp.int32, (L, LANES), 1)
    low_half = lane < SSM_HEAD_DIM

    y_tiles = []
    for g in range(SSM_GROUPS):
        b_g = bcm[:, g * D_STATE:(g + 1) * D_STATE]
        c_g = bcm[:, (SSM_GROUPS + g) * D_STATE:(SSM_GROUPS + g + 1) * D_STATE]
        b_gb = b_g.astype(BF16)
        c_gb = c_g.astype(BF16)
        cb = lax.dot_general(c_gb, b_gb, (((1,), (1,)), ((), ())), preferred_element_type=F32)
        b_gt = b_g.T
        s_g = state[:, g * GROUP_CH:(g + 1) * GROUP_CH]
        y_off_g = jnp.dot(c_gb, s_g.astype(BF16), preferred_element_type=F32)
        for jj in range(GROUP_CH // LANES):
            j = g * (GROUP_CH // LANES) + jj
            lhs_top = []
            lhs_bot = []
            escale = []
            for e in (2 * j, 2 * j + 1):
                acs_b = jnp.broadcast_to(a_cs[:, e:e + 1], (L, L))
                seg = acs_b - a_cs_t[e:e + 1, :]
                dec = jnp.exp2(jnp.where(causal, seg, -jnp.inf))
                lhs_top.append((cb * dec * dt_t[e:e + 1, :]).astype(BF16))
                lhs_bot.append((b_gt * w_end_t[e:e + 1, :]).astype(BF16))
                escale.append(jnp.broadcast_to(exp_acs[:, e:e + 1], (L, LANES)))
            x_tile = xs[:, j * LANES:(j + 1) * LANES]
            x_lo = jnp.where(low_half, x_tile, 0.0).astype(BF16)
            x_hi = jnp.where(low_half, 0.0, x_tile).astype(BF16)
            lhs = jnp.concatenate([jnp.concatenate(lhs_top, axis=1),
                                   jnp.concatenate(lhs_bot, axis=1)], axis=0)
            rhs = jnp.concatenate([x_lo, x_hi], axis=0)
            r = jnp.dot(lhs, rhs, preferred_element_type=F32)
            y_off = y_off_g[:, jj * LANES:(jj + 1) * LANES]
            y_tiles.append(r[0:L] + jnp.where(low_half, escale[0], escale[1]) * y_off)
            sl = slice(j * LANES, (j + 1) * LANES)
            state[:, sl] = cd_exp[:, sl] * state[:, sl] + r[L:2 * L]

    y = jnp.concatenate(y_tiles, axis=1)
    y = y + xs * dskip_ref[...]
    gated = y * _silu(z_ref[rows, :].astype(F32))
    outs = []
    for g in range(SSM_GROUPS):
        gg = gated[:, g * GROUP_CH:(g + 1) * GROUP_CH]
        ms = jnp.mean(gg * gg, axis=-1, keepdims=True)
        outs.append(gg * lax.rsqrt(ms + NORM_EPS))
    o_ref[rows, :] = (jnp.concatenate(outs, axis=1) * normw_ref[...]).astype(o_ref.dtype)


def _ssd(zxbc, dt_raw, conv_w, conv_b, dt_bias, a_log, d_skip, norm_w, batch, seq):
    t = zxbc.shape[0]
    L = SSM_CHUNK
    rows = SSD_STEP_CHUNKS * L
    nc = seq // rows

    def pad_heads(v):
        return jnp.pad(v.astype(F32), (0, LANES - SSM_HEADS)).reshape(1, LANES)

    def rowmap(col):
        return lambda b, c: (b * nc + c, col)

    def const2(b, c):
        return (0, 0)

    l_idx = jnp.arange(L)[:, None]
    shift = jnp.concatenate(
        [(jnp.arange(L)[None, :] == l_idx - (CONV_WIDTH - 1) + k) for k in range(CONV_WIDTH)],
        axis=1).astype(BF16)
    shift_tail = jnp.concatenate(
        [(jnp.arange(-CONV_TAIL, 0)[None, :] == l_idx - (CONV_WIDTH - 1) + k) for k in range(CONV_WIDTH)],
        axis=1).astype(BF16)
    expand = (jnp.arange(D_INNER)[None, :] // SSM_HEAD_DIM == jnp.arange(LANES)[:, None]).astype(BF16)

    return pl.pallas_call(
        _ssd_kernel,
        grid=(batch, nc),
        in_specs=[
            pl.BlockSpec((rows, D_INNER), rowmap(0)),
            pl.BlockSpec((rows, D_INNER), rowmap(1)),
            pl.BlockSpec((rows, BC_DIM), rowmap(2 * D_INNER // BC_DIM)),
            pl.BlockSpec((rows, LANES), rowmap(0)),
            pl.BlockSpec((CONV_WIDTH, CONV_DIM), const2),
            pl.BlockSpec((1, CONV_DIM), const2),
            pl.BlockSpec((1, LANES), const2),
            pl.BlockSpec((1, LANES), const2),
            pl.BlockSpec((1, D_INNER), const2),
            pl.BlockSpec((1, D_INNER), const2),
            pl.BlockSpec((L, CONV_WIDTH * L), const2),
            pl.BlockSpec((L, CONV_WIDTH * CONV_TAIL), const2),
            pl.BlockSpec((LANES, D_INNER), const2),
        ],
        out_specs=pl.BlockSpec((rows, D_INNER), rowmap(0)),
        out_shape=jax.ShapeDtypeStruct((t, D_INNER), BF16),
        scratch_shapes=[
            pltpu.VMEM((D_STATE, D_INNER), F32),
            pltpu.VMEM((CONV_TAIL, D_INNER), BF16),
            pltpu.VMEM((CONV_TAIL, BC_DIM), BF16),
        ],
        compiler_params=_cparams(("parallel", "arbitrary")),
        name="ssd_core",
    )(zxbc, zxbc, zxbc, dt_raw, conv_w.astype(F32), conv_b.astype(F32).reshape(1, CONV_DIM),
      pad_heads(dt_bias), pad_heads(a_log),
      jnp.repeat(d_skip.astype(F32), SSM_HEAD_DIM).reshape(1, D_INNER),
      norm_w.astype(F32).reshape(1, D_INNER), shift, shift_tail, expand)


GROUP_ROW0 = N_EXPERTS
GROUP_SHIFT = EXPERTS_PER_GROUP.bit_length() - 1
assert 1 << GROUP_SHIFT == EXPERTS_PER_GROUP


def _router_kernel(a_ref, wp_ref, res_ref, g_ref, w_ref, b_ref, earlier_ref,
                   x_ref, ids_ref, gates_ref, cnt_ref, carry, wt_split):
    i = pl.program_id(0)
    x_ref[...] = res_ref[...] + jnp.dot(a_ref[...], wp_ref[...], preferred_element_type=F32)

    @pl.when(i == 0)
    def _():
        carry[...] = jnp.zeros_like(carry)
        wt = w_ref[...].T
        wt_hi = wt.astype(BF16)
        wt_split[0:LANES, :] = wt_hi
        wt_split[LANES:, :] = (wt - wt_hi.astype(F32)).astype(BF16)

    h = _rms(x_ref[...], g_ref[...])
    h_hi = h.astype(BF16)
    h_mid = (h - h_hi.astype(F32)).astype(BF16)
    nt = (((1,), (1,)), ((), ()))
    top = lax.dot_general(wt_split[...], h_hi, nt, preferred_element_type=F32)
    cross = lax.dot_general(wt_split[0:LANES, :], h_mid, nt, preferred_element_type=F32)
    logits = top[:LANES] + (top[LANES:] + cross) + b_ref[...]
    tm = logits.shape[1]
    row = lax.broadcasted_iota(jnp.int32, (LANES, tm), 0)
    ninf = -jnp.inf

    def first_argmax(v, vmax):
        return jnp.min(jnp.where(v == vmax, row, LANES), axis=0, keepdims=True)

    is_group = (row >= GROUP_ROW0) & (row < GROUP_ROW0 + N_EXPERT_GROUPS)
    glog = jnp.where(is_group, logits, ninf)
    gmax = jnp.max(glog, axis=0, keepdims=True)
    g_idx = first_argmax(glog, gmax) - GROUP_ROW0
    g_p = 1.0 / jnp.sum(jnp.exp(glog - gmax), axis=0, keepdims=True)

    in_group = (row < N_EXPERTS) & (jnp.right_shift(row, GROUP_SHIFT) == g_idx)
    elog = jnp.where(in_group, logits, ninf)
    m1 = jnp.max(elog, axis=0, keepdims=True)
    e1 = first_argmax(elog, m1)
    elog2 = jnp.where(row == e1, ninf, elog)
    m2 = jnp.max(elog2, axis=0, keepdims=True)
    e2 = first_argmax(elog2, m2)
    zsum = jnp.sum(jnp.exp(elog - m1), axis=0, keepdims=True)
    p1 = 1.0 / zsum
    p2 = jnp.exp(m2 - m1) / zsum
    psum = p1 + p2
    w1 = p1 / psum * g_p
    w2 = p2 / psum * g_p

    oh1 = row == e1
    oh2 = row == e2
    onehot = jnp.logical_or(oh1, oh2).astype(F32)
    before = jnp.dot(onehot.astype(BF16), earlier_ref[...], preferred_element_type=F32) + carry[:, 0:1]
    r1 = jnp.sum(jnp.where(oh1, before, 0.0), axis=0, keepdims=True).astype(jnp.int32)
    r2 = jnp.sum(jnp.where(oh2, before, 0.0), axis=0, keepdims=True).astype(jnp.int32)
    carry[...] = carry[...] + jnp.sum(onehot, axis=1, keepdims=True)

    row8 = lax.broadcasted_iota(jnp.int32, (8, tm), 0)
    ids_ref[...] = jnp.where(row8 == 0, e1, jnp.where(row8 == 1, e2,
                             jnp.where(row8 == 2, r1, jnp.where(row8 == 3, r2, 0))))
    gates_ref[...] = jnp.where(row == 0, w1, jnp.where(row == 1, w2, 0.0)).T
    cnt_ref[...] = carry[...]


def _router(a, w_proj, res, ln_w, w_group, b_group, w_expert, b_expert, *, tm=512):
    t, d = res.shape
    k = a.shape[1]
    pad = LANES - N_EXPERTS - N_EXPERT_GROUPS
    w_r = jnp.concatenate([w_expert, w_group, jnp.zeros((d, pad), F32)], axis=1).astype(F32)
    b_r = jnp.concatenate([b_expert, b_group, jnp.zeros((pad,), F32)]).astype(F32).reshape(LANES, 1)
    earlier = (jnp.arange(tm)[:, None] < jnp.arange(tm)[None, :]).astype(BF16)
    return pl.pallas_call(
        _router_kernel,
        grid=(t // tm,),
        in_specs=[
            pl.BlockSpec((tm, k), lambda i: (i, 0)),
            pl.BlockSpec((k, d), lambda i: (0, 0)),
            pl.BlockSpec((tm, d), lambda i: (i, 0)),
            pl.BlockSpec((1, d), lambda i: (0, 0)),
            pl.BlockSpec((d, LANES), lambda i: (0, 0)),
            pl.BlockSpec((LANES, 1), lambda i: (0, 0)),
            pl.BlockSpec((tm, tm), lambda i: (0, 0)),
        ],
        out_specs=[
            pl.BlockSpec((tm, d), lambda i: (i, 0)),
            pl.BlockSpec((8, tm), lambda i: (0, i)),
            pl.BlockSpec((tm, LANES), lambda i: (i, 0)),
            pl.BlockSpec((LANES, LANES), lambda i: (0, 0)),
        ],
        out_shape=[
            jax.ShapeDtypeStruct((t, d), F32),
            jax.ShapeDtypeStruct((8, t), jnp.int32),
            jax.ShapeDtypeStruct((t, LANES), F32),
            jax.ShapeDtypeStruct((LANES, LANES), F32),
        ],
        scratch_shapes=[pltpu.VMEM((LANES, LANES), F32), pltpu.VMEM((2 * LANES, d), BF16)],
        compiler_params=_cparams(("arbitrary",)),
        name="proj_moe_router",
    )(a, w_proj, res, ln_w.reshape(1, d), w_r, b_r, earlier)


U32 = jnp.uint32
ROW_WORDS = D_MODEL // 2
ROW_SUB = ROW_WORDS // LANES
HIGH_HALF = 0xFFFF0000
ROW_MOVE_TOKENS = 1024


def _pack_rows(v):
    lo = lax.bitcast_convert_type(v[:, :ROW_WORDS].astype(BF16).astype(F32), U32)
    hi = lax.bitcast_convert_type(v[:, ROW_WORDS:].astype(BF16).astype(F32), U32)
    return jnp.right_shift(lo, jnp.uint32(16)) | (hi & jnp.uint32(HIGH_HALF))


def _unpack_rows(w):
    lo = lax.bitcast_convert_type(jnp.left_shift(w, jnp.uint32(16)), F32)
    hi = lax.bitcast_convert_type(w & jnp.uint32(HIGH_HALF), F32)
    return lo, hi


def _store_rows(ref, packed):
    n = packed.shape[0]
    for c in range(ROW_SUB):
        ref[pl.ds(c, n, stride=ROW_SUB), :] = packed[:, c * LANES:(c + 1) * LANES]


def _load_rows(ref, n):
    return jnp.concatenate([ref[pl.ds(c, n, stride=ROW_SUB), :] for c in range(ROW_SUB)], axis=1)


def _row_slice(ref, row):
    return ref.at[pl.ds(pl.multiple_of(row * ROW_SUB, ROW_SUB), ROW_SUB)]


def _dispatch_kernel(zf_ref, dest_ref, x_ref, g_ref, xs_ref, hbuf, zbuf, sem, zsem):
    tb = x_ref.shape[0]
    n_blocks = zf_ref.shape[0]
    blk_rows = MOE_BLOCK * ROW_SUB

    @pl.when(pl.program_id(0) == 0)
    def _():
        zbuf[...] = jnp.zeros_like(zbuf)

        def zero_copy(b):
            start = pl.multiple_of(b * blk_rows, blk_rows)
            return pltpu.make_async_copy(zbuf, xs_ref.at[pl.ds(start, blk_rows)], zsem)

        def zissue(b, carry):
            @pl.when(zf_ref[b] != 0)
            def _():
                zero_copy(b).start()
            return carry

        def zdrain(b, carry):
            @pl.when(zf_ref[b] != 0)
            def _():
                zero_copy(b).wait()
            return carry

        lax.fori_loop(0, n_blocks, zissue, 0)
        lax.fori_loop(0, n_blocks, zdrain, 0)

    _store_rows(hbuf, _pack_rows(_rms(x_ref[...], g_ref[...])))
    for r in range(tb):
        for k in range(2):
            d = dest_ref[k, r]
            pltpu.make_async_copy(_row_slice(hbuf, r), _row_slice(xs_ref, d), sem).start(priority=k)
    for k in range(2):
        pltpu.make_async_copy(hbuf, xs_ref.at[pl.ds(0, tb * ROW_SUB)], sem).wait()


def _dispatch(x, ln_w, dest, zero_flag, n_slots, *, tb=ROW_MOVE_TOKENS):
    t, d = x.shape
    nblk = t // tb
    return pl.pallas_call(
        _dispatch_kernel,
        grid_spec=pltpu.PrefetchScalarGridSpec(
            num_scalar_prefetch=1,
            grid=(nblk,),
            in_specs=[
                pl.BlockSpec((2, tb), lambda i, zf: (0, i), memory_space=pltpu.SMEM),
                pl.BlockSpec((tb, d), lambda i, zf: (i, 0)),
                pl.BlockSpec((1, d), lambda i, zf: (0, 0)),
            ],
            out_specs=pl.BlockSpec(memory_space=pl.ANY),
            scratch_shapes=[pltpu.VMEM((tb * ROW_SUB, LANES), U32),
                            pltpu.VMEM((MOE_BLOCK * ROW_SUB, LANES), U32),
                            pltpu.SemaphoreType.DMA(()), pltpu.SemaphoreType.DMA(())],
        ),
        out_shape=jax.ShapeDtypeStruct((n_slots * ROW_SUB, LANES), U32),
        compiler_params=_cparams(("arbitrary",)),
        name="moe_dispatch",
    )(zero_flag, dest, x, ln_w.reshape(1, d))


def _expert_kernel(be_ref, bpos_ref, elist_ref, meta_ref, xs_ref, wg_hbm, wu_hbm, wd_hbm, y_ref,
                   wg_s, wu_s, wd_s, wg_f, wu_f, wd_f, sems, *, layer):
    i = pl.program_id(0)
    active = i < meta_ref[0]
    pos = bpos_ref[i]
    changed = jnp.logical_or(i == 0, be_ref[i] != be_ref[jnp.maximum(i - 1, 0)])
    streams = ((wg_hbm, wg_f), (wu_hbm, wu_f), (wd_hbm, wd_f))

    def fetch(p, slot):
        e = elist_ref[p]
        return [pltpu.make_async_copy(hbm.at[layer, e], buf.at[slot], sems.at[slot, t])
                for t, (hbm, buf) in enumerate(streams)]

    @pl.when(jnp.logical_and(active, i == 0))
    def _():
        for cp in fetch(0, 0):
            cp.start()

    @pl.when(jnp.logical_and(active, changed))
    def _():
        slot = lax.rem(pos, 2)
        for cp in fetch(pos, slot):
            cp.wait()

        @pl.when(pos + 1 < meta_ref[1])
        def _():
            for cp in fetch(pos + 1, 1 - slot):
                cp.start()

        wg_s[...] = wg_f[slot].astype(BF16)
        wu_s[...] = wu_f[slot].astype(BF16)
        wd_s[...] = wd_f[slot].astype(BF16)

    @pl.when(active)
    def _():
        lo, hi = _unpack_rows(_load_rows(xs_ref, MOE_BLOCK))
        xb = jnp.concatenate([lo.astype(BF16), hi.astype(BF16)], axis=1)
        gate = jnp.dot(xb, wg_s[...], preferred_element_type=F32)
        up = jnp.dot(xb, wu_s[...], preferred_element_type=F32)
        hid = (_silu(gate) * up).astype(BF16)
        _store_rows(y_ref, _pack_rows(jnp.dot(hid, wd_s[...], preferred_element_type=F32)))

    @pl.when(jnp.logical_not(active))
    def _():
        y_ref[...] = jnp.zeros_like(y_ref)


def _experts(xs, block_e, block_pos, expert_list, meta, w_gate, w_up, w_down, layer):
    blk_rows = MOE_BLOCK * ROW_SUB
    n_blocks = xs.shape[0] // blk_rows
    d = D_MODEL

    def xmap(i, be, bpos, elist, meta):
        return (jnp.maximum(jnp.minimum(i, meta[0] - 1), 0), 0)

    return pl.pallas_call(
        functools.partial(_expert_kernel, layer=layer),
        grid_spec=pltpu.PrefetchScalarGridSpec(
            num_scalar_prefetch=4,
            grid=(n_blocks,),
            in_specs=[
                pl.BlockSpec((blk_rows, LANES), xmap),
                pl.BlockSpec(memory_space=pl.ANY),
                pl.BlockSpec(memory_space=pl.ANY),
                pl.BlockSpec(memory_space=pl.ANY),
            ],
            out_specs=pl.BlockSpec((blk_rows, LANES), lambda i, be, bpos, elist, meta: (i, 0)),
            scratch_shapes=[
                pltpu.VMEM((d, EXPERT_FF), BF16),
                pltpu.VMEM((d, EXPERT_FF), BF16),
                pltpu.VMEM((EXPERT_FF, d), BF16),
                pltpu.VMEM((2, d, EXPERT_FF), F32),
                pltpu.VMEM((2, d, EXPERT_FF), F32),
                pltpu.VMEM((2, EXPERT_FF, d), F32),
                pltpu.SemaphoreType.DMA((2, 3)),
            ],
        ),
        out_shape=jax.ShapeDtypeStruct(xs.shape, U32),
        compiler_params=_cparams(("arbitrary",)),
        name="moe_experts",
    )(block_e, block_pos, expert_list, meta, xs, w_gate, w_up, w_down)


def _combine_kernel(dest_ref, x_ref, gates_ref, y_ref, *rest, final_norm):
    if final_norm:
        g_ref, o_ref, buf0, buf1, sem = rest
    else:
        o_ref, buf0, buf1, sem = rest
    tb = x_ref.shape[0]
    bufs = (buf0, buf1)

    for r in range(tb):
        for k in range(2):
            d = dest_ref[k, r]
            pltpu.make_async_copy(_row_slice(y_ref, d), _row_slice(bufs[k], r), sem).start(priority=k)
    for k in range(2):
        pltpu.make_async_copy(y_ref.at[pl.ds(0, tb * ROW_SUB)], bufs[k], sem).wait()

    gt = gates_ref[...]
    lo0, hi0 = _unpack_rows(_load_rows(buf0, tb))
    lo1, hi1 = _unpack_rows(_load_rows(buf1, tb))
    w0 = gt[:, 0:1]
    w1 = gt[:, 1:2]
    out = x_ref[...] + jnp.concatenate([lo0 * w0 + lo1 * w1, hi0 * w0 + hi1 * w1], axis=1)
    if final_norm:
        out = _rms(out, g_ref[...])
    o_ref[...] = out


def _combine(x, gates, y, dest, final_w=None, *, tb=ROW_MOVE_TOKENS):
    t, d = x.shape
    nblk = t // tb
    final_norm = final_w is not None
    in_specs = [
        pl.BlockSpec((2, tb), lambda i: (0, i), memory_space=pltpu.SMEM),
        pl.BlockSpec((tb, d), lambda i: (i, 0)),
        pl.BlockSpec((tb, LANES), lambda i: (i, 0)),
        pl.BlockSpec(memory_space=pl.ANY),
    ]
    args = [dest, x, gates, y]
    if final_norm:
        in_specs.append(pl.BlockSpec((1, d), lambda i: (0, 0)))
        args.append(final_w.reshape(1, d))
    return pl.pallas_call(
        functools.partial(_combine_kernel, final_norm=final_norm),
        grid=(nblk,),
        in_specs=in_specs,
        out_specs=pl.BlockSpec((tb, d), lambda i: (i, 0)),
        out_shape=jax.ShapeDtypeStruct((t, d), F32),
        scratch_shapes=[pltpu.VMEM((tb * ROW_SUB, LANES), U32), pltpu.VMEM((tb * ROW_SUB, LANES), U32),
                        pltpu.SemaphoreType.DMA(())],
        compiler_params=_cparams(("arbitrary",)),
        name="moe_combine",
    )(*args)


def _proj_moe(a, w_proj, res, ln_w, w_group, b_group, w_expert, b_expert, w_gate, w_up, w_down, layer,
              final_w=None):
    t = res.shape[0]
    n_assign = t * 2
    n_blocks = -(-n_assign // MOE_BLOCK) + N_EXPERTS
    n_slots = n_blocks * MOE_BLOCK

    x, ids, gates, cnt = _router(a, w_proj, res, ln_w, w_group, b_group, w_expert, b_expert)
    counts = cnt[:N_EXPERTS, 0].astype(jnp.int32)
    padded = (counts + MOE_BLOCK - 1) // MOE_BLOCK * MOE_BLOCK
    p_ends = jnp.cumsum(padded)
    p_starts = p_ends - padded
    experts = jnp.arange(N_EXPERTS, dtype=jnp.int32)
    start_of = jnp.sum(jnp.where(ids[0:2, :, None] == experts, p_starts, 0), axis=-1)
    dest = (start_of + ids[2:4]).astype(jnp.int32)
    block_start = jnp.arange(n_blocks, dtype=jnp.int32) * MOE_BLOCK
    block_e = jnp.minimum(jnp.sum((p_ends[None, :] <= block_start[:, None]).astype(jnp.int32), axis=1),
                          N_EXPERTS - 1)
    n_active = (p_ends[-1:] // MOE_BLOCK).astype(jnp.int32)
    valid_end = jnp.sum(jnp.where(block_e[:, None] == experts, p_starts + counts, 0), axis=-1)
    zero_flag = jnp.logical_or(block_start + MOE_BLOCK > valid_end,
                               block_start >= p_ends[-1]).astype(jnp.int32)

    has_tokens = counts > 0
    list_pos = jnp.cumsum(has_tokens.astype(jnp.int32)) - 1
    expert_list = jnp.sum(jnp.where(jnp.logical_and(has_tokens[None, :], list_pos[None, :] == experts[:, None]),
                                    experts[None, :], 0), axis=1).astype(jnp.int32)
    block_pos = jnp.sum(jnp.where(block_e[:, None] == experts, list_pos, 0), axis=-1).astype(jnp.int32)
    meta = jnp.concatenate([n_active, jnp.sum(has_tokens.astype(jnp.int32), keepdims=True)]).astype(jnp.int32)

    xs = _dispatch(x, ln_w, dest, zero_flag, n_slots)
    y = _experts(xs, block_e, block_pos, expert_list, meta, w_gate, w_up, w_down, layer)
    return _combine(x, gates, y, dest, final_w)


def kernel(x, rel_bias, ln_mix, ln_ffn, ln_final, attn_w_qkv, attn_sinks, attn_w_o, ssm_w_in,
           ssm_conv_w, ssm_conv_b, ssm_dt_bias, ssm_a_log, ssm_d, ssm_norm_w, ssm_w_out,
           moe_w_group, moe_b_group, moe_w_expert, moe_b_expert, moe_w_gate, moe_w_up, moe_w_down):
    batch, seq, d = x.shape
    t = batch * seq
    xf = x.reshape(t, d).astype(F32)

    q_scale = jnp.where(jnp.arange(QKV_DIM) < Q_DIM, HEAD_DIM ** -0.5 * LOG2E, 1.0).astype(F32)
    qkv = _norm_matmul(xf, ln_mix[0], (attn_w_qkv[0] * q_scale).astype(BF16), tm=512, tn=QKV_DIM)
    att = _attention(qkv, attn_sinks[0], rel_bias, batch, seq)
    xf = _proj_moe(att, attn_w_o[0].astype(BF16), xf, ln_ffn[0], moe_w_group[0], moe_b_group[0],
                   moe_w_expert[0], moe_b_expert[0], moe_w_gate, moe_w_up, moe_w_down, 0)

    w_in = ssm_w_in[0]
    w_dt = jnp.pad(w_in[:, ZXBC_DIM:], ((0, 0), (0, LANES - SSM_HEADS))).astype(BF16)
    zxbc, dt_raw = _norm_matmul(xf, ln_mix[1], w_in[:, :ZXBC_DIM].astype(BF16), w_dt, tm=1024, tn=ZXBC_DIM)
    gated = _ssd(zxbc, dt_raw, ssm_conv_w[0], ssm_conv_b[0], ssm_dt_bias[0], ssm_a_log[0],
                 ssm_d[0], ssm_norm_w[0], batch, seq)
    xf = _proj_moe(gated, ssm_w_out[0].astype(BF16), xf, ln_ffn[1], moe_w_group[1], moe_b_group[1],
                   moe_w_expert[1], moe_b_expert[1], moe_w_gate, moe_w_up, moe_w_down, 1, final_w=ln_final)
    return xf.reshape(batch, seq, d).astype(x.dtype)
```

```python
import functools
import math

import jax
import jax.numpy as jnp
from jax import lax
from jax.experimental import pallas as pl
from jax.experimental.pallas import tpu as pltpu

F32 = jnp.float32
BF16 = jnp.bfloat16

D_MODEL = 1024
N_HEADS = 16
N_KV_HEADS = 4
HEAD_DIM = 64
GQA_GROUP = N_HEADS // N_KV_HEADS
WINDOW = 128
ATTN_BLOCK = 128
Q_DIM = N_HEADS * HEAD_DIM
KV_DIM = N_KV_HEADS * HEAD_DIM
QKV_DIM = Q_DIM + 2 * KV_DIM
REL_BUCKETS = 32
REL_MAX_DIST = 128

D_INNER = 2048
SSM_HEAD_DIM = 64
SSM_HEADS = D_INNER // SSM_HEAD_DIM
SSM_GROUPS = 4
D_STATE = 128
CONV_WIDTH = 4
BC_DIM = 2 * SSM_GROUPS * D_STATE
CONV_DIM = D_INNER + BC_DIM
ZXBC_DIM = D_INNER + CONV_DIM
SSM_CHUNK = 128
SSD_STEP_CHUNKS = 8
GROUP_CH = D_INNER // SSM_GROUPS

N_EXPERT_GROUPS = 8
EXPERTS_PER_GROUP = 8
N_EXPERTS = 64
EXPERT_FF = 512
MOE_BLOCK = 512

NORM_EPS = 1e-6
LOG2E = 1.4426950408889634
CONV_TAIL = 16
LANES = 128
NEG_BIG = -1e30
VMEM_LIMIT = 56 * 1024 * 1024


def _cparams(sem, flags=None):
    return pltpu.CompilerParams(dimension_semantics=sem, vmem_limit_bytes=VMEM_LIMIT, flags=flags)


def _rms(x, g):
    ms = jnp.mean(x * x, axis=-1, keepdims=True)
    return x * lax.rsqrt(ms + NORM_EPS) * g


def _silu(x):
    h = 0.5 * x
    return h + h * jnp.tanh(h)


def _norm_matmul_kernel(x_ref, g_ref, w_ref, *rest, has_aux):
    if has_aux:
        wa_ref, o_ref, oa_ref, h_scr = rest
    else:
        o_ref, h_scr = rest

    @pl.when(pl.program_id(1) == 0)
    def _():
        h = _rms(x_ref[...], g_ref[...]).astype(BF16)
        h_scr[...] = h
        if has_aux:
            oa_ref[...] = jnp.dot(h, wa_ref[...], preferred_element_type=F32)

    o_ref[...] = jnp.dot(h_scr[...], w_ref[...], preferred_element_type=F32).astype(o_ref.dtype)


def _norm_matmul(x, g, w, w_aux=None, *, tm=1024, tn=512):
    t, d = x.shape
    n = w.shape[1]
    has_aux = w_aux is not None
    in_specs = [
        pl.BlockSpec((tm, d), lambda i, j: (i, 0)),
        pl.BlockSpec((1, d), lambda i, j: (0, 0)),
        pl.BlockSpec((d, tn), lambda i, j: (0, j), pipeline_mode=pl.Buffered(1) if tn == n else None),
    ]
    out_shape = [jax.ShapeDtypeStruct((t, n), BF16)]
    out_specs = [pl.BlockSpec((tm, tn), lambda i, j: (i, j))]
    args = [x, g.reshape(1, d), w]
    if has_aux:
        na = w_aux.shape[1]
        in_specs.append(pl.BlockSpec((d, na), lambda i, j: (0, 0)))
        out_shape.append(jax.ShapeDtypeStruct((t, na), F32))
        out_specs.append(pl.BlockSpec((tm, na), lambda i, j: (i, 0)))
        args.append(w_aux)
    res = pl.pallas_call(
        functools.partial(_norm_matmul_kernel, has_aux=has_aux),
        grid=(t // tm, n // tn),
        in_specs=in_specs,
        out_specs=out_specs,
        out_shape=out_shape,
        scratch_shapes=[pltpu.VMEM((tm, d), BF16)],
        compiler_params=_cparams(("parallel", "arbitrary")),
        name="norm_matmul",
    )(*args)
    return res if has_aux else res[0]


def _attn_kernel(sinks_ref, q_ref, kp_ref, kc_ref, vp_ref, vc_ref, bias_ref, o_ref):
    table = jnp.minimum(pl.program_id(1), 1)
    for h in range(N_KV_HEADS):
        ks = slice(h * HEAD_DIM, (h + 1) * HEAD_DIM)
        kb = jnp.concatenate([kp_ref[:, ks], kc_ref[:, ks]], axis=0)
        vb = jnp.concatenate([vp_ref[:, ks], vc_ref[:, ks]], axis=0)
        for g in range(GQA_GROUP):
            hh = h * GQA_GROUP + g
            qh = q_ref[:, hh * HEAD_DIM:(hh + 1) * HEAD_DIM]
            s = lax.dot_general(qh, kb, (((1,), (1,)), ((), ())), preferred_element_type=F32)
            logits = s + bias_ref[table, hh]
            sink = sinks_ref[hh]
            m = jnp.maximum(jnp.max(logits, axis=-1, keepdims=True), sink)
            p = jnp.exp2(logits - m)
            denom = jnp.sum(p, axis=-1, keepdims=True) + jnp.exp2(sink - m)
            o = jnp.dot(p.astype(BF16), vb, preferred_element_type=F32) / denom
            o_ref[:, hh * HEAD_DIM:(hh + 1) * HEAD_DIM] = o.astype(o_ref.dtype)


def _t5_causal_bucket(dist):
    n = jnp.maximum(dist, 0)
    max_exact = REL_BUCKETS // 2
    nf = jnp.maximum(n, 1).astype(F32)
    large = max_exact + (jnp.log(nf / max_exact) / math.log(REL_MAX_DIST / max_exact)
                         * (REL_BUCKETS - max_exact)).astype(jnp.int32)
    large = jnp.minimum(large, REL_BUCKETS - 1)
    return jnp.where(n < max_exact, n, large)


def _attention(qkv, sinks, rel_bias, batch, seq):
    t = qkv.shape[0]
    nb = seq // ATTN_BLOCK
    qi = jnp.arange(ATTN_BLOCK)[:, None]
    ki = jnp.arange(2 * ATTN_BLOCK)[None, :]
    dist = qi + ATTN_BLOCK - ki
    in_window = (dist >= 0) & (dist < WINDOW)
    onehot = (_t5_causal_bucket(dist)[None] == jnp.arange(REL_BUCKETS)[:, None, None]).astype(F32)
    bias = jnp.einsum('hr,rqk->hqk', rel_bias.astype(F32).T, onehot, precision=lax.Precision.HIGHEST)
    bias = jnp.where(in_window[None], bias * LOG2E, NEG_BIG)
    bias_first = jnp.where((ki < ATTN_BLOCK)[None], NEG_BIG, bias)
    bias2 = jnp.stack([bias_first, bias])

    kcol = Q_DIM // KV_DIM
    vcol = kcol + 1

    def prev(b, n):
        return b * nb + jnp.maximum(n - 1, 0)

    return pl.pallas_call(
        _attn_kernel,
        grid=(batch, nb),
        in_specs=[
            pl.BlockSpec(memory_space=pltpu.SMEM),
            pl.BlockSpec((ATTN_BLOCK, Q_DIM), lambda b, n: (b * nb + n, 0)),
            pl.BlockSpec((ATTN_BLOCK, KV_DIM), lambda b, n: (prev(b, n), kcol)),
            pl.BlockSpec((ATTN_BLOCK, KV_DIM), lambda b, n: (b * nb + n, kcol)),
            pl.BlockSpec((ATTN_BLOCK, KV_DIM), lambda b, n: (prev(b, n), vcol)),
            pl.BlockSpec((ATTN_BLOCK, KV_DIM), lambda b, n: (b * nb + n, vcol)),
            pl.BlockSpec((2, N_HEADS, ATTN_BLOCK, 2 * ATTN_BLOCK), lambda b, n: (0, 0, 0, 0)),
        ],
        out_specs=pl.BlockSpec((ATTN_BLOCK, Q_DIM), lambda b, n: (b * nb + n, 0)),
        out_shape=jax.ShapeDtypeStruct((t, Q_DIM), BF16),
        compiler_params=_cparams(("parallel", "arbitrary")),
        name="swa_attention",
    )((sinks.astype(F32) * LOG2E), qkv, qkv, qkv, qkv, qkv, bias2)


def _split3(v):
    hi = v.astype(BF16)
    r = v - hi.astype(F32)
    mid = r.astype(BF16)
    lo = (r - mid.astype(F32)).astype(BF16)
    return hi, mid, lo


def _ssd_kernel(z_ref, x_ref, bc_ref, dt_ref, convw_ref, convb_ref, dtb_ref, alog_ref,
                dskip_ref, normw_ref, shift_ref, shift_tail_ref, expand_ref,
                o_ref, state, xtail, bctail):
    @pl.when(pl.program_id(1) == 0)
    def _():
        state[...] = jnp.zeros_like(state)
        xtail[...] = jnp.zeros_like(xtail)
        bctail[...] = jnp.zeros_like(bctail)

    for k in range(z_ref.shape[0] // SSM_CHUNK):
        _ssd_chunk(pl.ds(k * SSM_CHUNK, SSM_CHUNK), z_ref, x_ref, bc_ref, dt_ref, convw_ref, convb_ref,
                   dtb_ref, alog_ref, dskip_ref, normw_ref, shift_ref, shift_tail_ref, expand_ref,
                   o_ref, state, xtail, bctail)


def _ssd_chunk(rows, z_ref, x_ref, bc_ref, dt_ref, convw_ref, convb_ref, dtb_ref, alog_ref,
               dskip_ref, normw_ref, shift_ref, shift_tail_ref, expand_ref,
               o_ref, state, xtail, bctail):
    L = SSM_CHUNK

    def conv_silu(u_ref, tail, w_lo, width):
        cur = u_ref[rows, :]
        prev = tail[...]
        w = [convw_ref[k:k + 1, w_lo:w_lo + width].astype(BF16) for k in range(CONV_WIDTH)]
        taps = jnp.concatenate([cur * w[k] for k in range(CONV_WIDTH)], axis=0)
        taps_prev = jnp.concatenate([prev * w[k] for k in range(CONV_WIDTH)], axis=0)
        y = (jnp.dot(shift_ref[...], taps, preferred_element_type=F32)
             + jnp.dot(shift_tail_ref[...], taps_prev, preferred_element_type=F32)
             + convb_ref[:, w_lo:w_lo + width])
        tail[...] = cur[L - CONV_TAIL:L]
        return _silu(y)

    xs = conv_silu(x_ref, xtail, 0, D_INNER)
    bcm = conv_silu(bc_ref, bctail, D_INNER, BC_DIM)

    xdt = dt_ref[rows, :] + dtb_ref[...]
    dt = jnp.maximum(xdt, 0.0) + jnp.log1p(jnp.exp(-jnp.abs(xdt)))
    a = dt * (-jnp.exp(alog_ref[...]) * LOG2E)
    row = lax.broadcasted_iota(jnp.int32, (L, L), 0)
    colm = lax.broadcasted_iota(jnp.int32, (L, L), 1)
    causal = row >= colm
    cs3 = jnp.dot(causal.astype(BF16), jnp.concatenate(_split3(a), axis=1), preferred_element_type=F32)
    a_cs = cs3[:, :LANES] + cs3[:, LANES:2 * LANES] + cs3[:, 2 * LANES:]
    a_cs_t = a_cs.T
    dt_t = dt.T
    a_last_col = a_cs_t[:, L - 1:L]
    w_end_t = jnp.exp2(a_last_col - a_cs_t) * dt_t
    exp_acs = jnp.exp2(a_cs)

    cd3 = [jnp.broadcast_to(part.astype(F32), (16, LANES)).astype(BF16)
           for part in _split3(exp_acs[L - 1:L, :])]
    cdx = jnp.dot(jnp.concatenate(cd3, axis=0), expand_ref[...], preferred_element_type=F32)
    cd_exp = cdx[0:1] + cdx[16:17] + cdx[32:33]

    lane = lax.broadcasted_iota(jnp.int32, (L, LANES), 1)
    low_half = lane < SSM_HEAD_DIM

    y_tiles = []
    for g in range(SSM_GROUPS):
        b_g = bcm[:, g * D_STATE:(g + 1) * D_STATE]
        c_g = bcm[:, (SSM_GROUPS + g) * D_STATE:(SSM_GROUPS + g + 1) * D_STATE]
        b_gb = b_g.astype(BF16)
        c_gb = c_g.astype(BF16)
        cb = lax.dot_general(c_gb, b_gb, (((1,), (1,)), ((), ())), preferred_element_type=F32)
        b_gt = b_g.T
        s_g = state[:, g * GROUP_CH:(g + 1) * GROUP_CH]
        y_off_g = jnp.dot(c_gb, s_g.astype(BF16), preferred_element_type=F32)
        for jj in range(GROUP_CH // LANES):
            j = g * (GROUP_CH // LANES) + jj
            lhs_top = []
            lhs_bot = []
            escale = []
            for e in (2 * j, 2 * j + 1):
                acs_b = jnp.broadcast_to(a_cs[:, e:e + 1], (L, L))
                seg = acs_b - a_cs_t[e:e + 1, :]
                dec = jnp.exp2(jnp.where(causal, seg, -jnp.inf))
                lhs_top.append((cb * dec * dt_t[e:e + 1, :]).astype(BF16))
                lhs_bot.append((b_gt * w_end_t[e:e + 1, :]).astype(BF16))
                escale.append(jnp.broadcast_to(exp_acs[:, e:e + 1], (L, LANES)))
            x_tile = xs[:, j * LANES:(j + 1) * LANES]
            x_lo = jnp.where(low_half, x_tile, 0.0).astype(BF16)
            x_hi = jnp.where(low_half, 0.0, x_tile).astype(BF16)
            lhs = jnp.concatenate([jnp.concatenate(lhs_top, axis=1),
                                   jnp.concatenate(lhs_bot, axis=1)], axis=0)
            rhs = jnp.concatenate([x_lo, x_hi], axis=0)
            r = jnp.dot(lhs, rhs, preferred_element_type=F32)
            y_off = y_off_g[:, jj * LANES:(jj + 1) * LANES]
            y_tiles.append(r[0:L] + jnp.where(low_half, escale[0], escale[1]) * y_off)
            sl = slice(j * LANES, (j + 1) * LANES)
            state[:, sl] = cd_exp[:, sl] * state[:, sl] + r[L:2 * L]

    y = jnp.concatenate(y_tiles, axis=1)
    y = y + xs * dskip_ref[...]
    gated = y * _silu(z_ref[rows, :].astype(F32))
    outs = []
    for g in range(SSM_GROUPS):
        gg = gated[:, g * GROUP_CH:(g + 1) * GROUP_CH]
        ms = jnp.mean(gg * gg, axis=-1, keepdims=True)
        outs.append(gg * lax.rsqrt(ms + NORM_EPS))
    o_ref[rows, :] = (jnp.concatenate(outs, axis=1) * normw_ref[...]).astype(o_ref.dtype)


def _ssd(zxbc, dt_raw, conv_w, conv_b, dt_bias, a_log, d_skip, norm_w, batch, seq):
    t = zxbc.shape[0]
    L = SSM_CHUNK
    rows = SSD_STEP_CHUNKS * L
    nc = seq // rows

    def pad_heads(v):
        return jnp.pad(v.astype(F32), (0, LANES - SSM_HEADS)).reshape(1, LANES)

    def rowmap(col):
        return lambda b, c: (b * nc + c, col)

    def const2(b, c):
        return (0, 0)

    l_idx = jnp.arange(L)[:, None]
    shift = jnp.concatenate(
        [(jnp.arange(L)[None, :] == l_idx - (CONV_WIDTH - 1) + k) for k in range(CONV_WIDTH)],
        axis=1).astype(BF16)
    shift_tail = jnp.concatenate(
        [(jnp.arange(-CONV_TAIL, 0)[None, :] == l_idx - (CONV_WIDTH - 1) + k) for k in range(CONV_WIDTH)],
        axis=1).astype(BF16)
    expand = (jnp.arange(D_INNER)[None, :] // SSM_HEAD_DIM == jnp.arange(LANES)[:, None]).astype(BF16)

    return pl.pallas_call(
        _ssd_kernel,
        grid=(batch, nc),
        in_specs=[
            pl.BlockSpec((rows, D_INNER), rowmap(0)),
            pl.BlockSpec((rows, D_INNER), rowmap(1)),
            pl.BlockSpec((rows, BC_DIM), rowmap(2 * D_INNER // BC_DIM)),
            pl.BlockSpec((rows, LANES), rowmap(0)),
            pl.BlockSpec((CONV_WIDTH, CONV_DIM), const2),
            pl.BlockSpec((1, CONV_DIM), const2),
            pl.BlockSpec((1, LANES), const2),
            pl.BlockSpec((1, LANES), const2),
            pl.BlockSpec((1, D_INNER), const2),
            pl.BlockSpec((1, D_INNER), const2),
            pl.BlockSpec((L, CONV_WIDTH * L), const2),
            pl.BlockSpec((L, CONV_WIDTH * CONV_TAIL), const2),
            pl.BlockSpec((LANES, D_INNER), const2),
        ],
        out_specs=pl.BlockSpec((rows, D_INNER), rowmap(0)),
        out_shape=jax.ShapeDtypeStruct((t, D_INNER), BF16),
        scratch_shapes=[
            pltpu.VMEM((D_STATE, D_INNER), F32),
            pltpu.VMEM((CONV_TAIL, D_INNER), BF16),
            pltpu.VMEM((CONV_TAIL, BC_DIM), BF16),
        ],
        compiler_params=_cparams(("parallel", "arbitrary")),
        name="ssd_core",
    )(zxbc, zxbc, zxbc, dt_raw, conv_w.astype(F32), conv_b.astype(F32).reshape(1, CONV_DIM),
      pad_heads(dt_bias), pad_heads(a_log),
      jnp.repeat(d_skip.astype(F32), SSM_HEAD_DIM).reshape(1, D_INNER),
      norm_w.astype(F32).reshape(1, D_INNER), shift, shift_tail, expand)


GROUP_ROW0 = N_EXPERTS
GROUP_SHIFT = EXPERTS_PER_GROUP.bit_length() - 1
assert 1 << GROUP_SHIFT == EXPERTS_PER_GROUP


def _router_kernel(a_ref, wp_ref, res_ref, g_ref, w_ref, b_ref, earlier_ref,
                   x_ref, ids_ref, gates_ref, cnt_ref, carry, wt_split):
    i = pl.program_id(0)
    x_ref[...] = res_ref[...] + jnp.dot(a_ref[...], wp_ref[...], preferred_element_type=F32)

    @pl.when(i == 0)
    def _():
        carry[...] = jnp.zeros_like(carry)
        wt = w_ref[...].T
        wt_hi = wt.astype(BF16)
        wt_split[0:LANES, :] = wt_hi
        wt_split[LANES:, :] = (wt - wt_hi.astype(F32)).astype(BF16)

    h = _rms(x_ref[...], g_ref[...])
    h_hi = h.astype(BF16)
    h_mid = (h - h_hi.astype(F32)).astype(BF16)
    nt = (((1,), (1,)), ((), ()))
    top = lax.dot_general(wt_split[...], h_hi, nt, preferred_element_type=F32)
    cross = lax.dot_general(wt_split[0:LANES, :], h_mid, nt, preferred_element_type=F32)
    logits = top[:LANES] + (top[LANES:] + cross) + b_ref[...]
    tm = logits.shape[1]
    row = lax.broadcasted_iota(jnp.int32, (LANES, tm), 0)
    ninf = -jnp.inf

    def first_argmax(v, vmax):
        return jnp.min(jnp.where(v == vmax, row, LANES), axis=0, keepdims=True)

    is_group = (row >= GROUP_ROW0) & (row < GROUP_ROW0 + N_EXPERT_GROUPS)
    glog = jnp.where(is_group, logits, ninf)
    gmax = jnp.max(glog, axis=0, keepdims=True)
    g_idx = first_argmax(glog, gmax) - GROUP_ROW0
    g_p = 1.0 / jnp.sum(jnp.exp(glog - gmax), axis=0, keepdims=True)

    in_group = (row < N_EXPERTS) & (jnp.right_shift(row, GROUP_SHIFT) == g_idx)
    elog = jnp.where(in_group, logits, ninf)
    m1 = jnp.max(elog, axis=0, keepdims=True)
    e1 = first_argmax(elog, m1)
    elog2 = jnp.where(row == e1, ninf, elog)
    m2 = jnp.max(elog2, axis=0, keepdims=True)
    e2 = first_argmax(elog2, m2)
    zsum = jnp.sum(jnp.exp(elog - m1), axis=0, keepdims=True)
    p1 = 1.0 / zsum
    p2 = jnp.exp(m2 - m1) / zsum
    psum = p1 + p2
    w1 = p1 / psum * g_p
    w2 = p2 / psum * g_p

    oh1 = row == e1
    oh2 = row == e2
    onehot = jnp.logical_or(oh1, oh2).astype(F32)
    before = jnp.dot(onehot.astype(BF16), earlier_ref[...], preferred_element_type=F32) + carry[:, 0:1]
    r1 = jnp.sum(jnp.where(oh1, before, 0.0), axis=0, keepdims=True).astype(jnp.int32)
    r2 = jnp.sum(jnp.where(oh2, before, 0.0), axis=0, keepdims=True).astype(jnp.int32)
    carry[...] = carry[...] + jnp.sum(onehot, axis=1, keepdims=True)

    row8 = lax.broadcasted_iota(jnp.int32, (8, tm), 0)
    ids_ref[...] = jnp.where(row8 == 0, e1, jnp.where(row8 == 1, e2,
                             jnp.where(row8 == 2, r1, jnp.where(row8 == 3, r2, 0))))
    gates_ref[...] = jnp.where(row == 0, w1, jnp.where(row == 1, w2, 0.0)).T
    cnt_ref[...] = carry[...]


def _router(a, w_proj, res, ln_w, w_group, b_group, w_expert, b_expert, *, tm=512):
    t, d = res.shape
    k = a.shape[1]
    pad = LANES - N_EXPERTS - N_EXPERT_GROUPS
    w_r = jnp.concatenate([w_expert, w_group, jnp.zeros((d, pad), F32)], axis=1).astype(F32)
    b_r = jnp.concatenate([b_expert, b_group, jnp.zeros((pad,), F32)]).astype(F32).reshape(LANES, 1)
    earlier = (jnp.arange(tm)[:, None] < jnp.arange(tm)[None, :]).astype(BF16)
    return pl.pallas_call(
        _router_kernel,
        grid=(t // tm,),
        in_specs=[
            pl.BlockSpec((tm, k), lambda i: (i, 0)),
            pl.BlockSpec((k, d), lambda i: (0, 0)),
            pl.BlockSpec((tm, d), lambda i: (i, 0)),
            pl.BlockSpec((1, d), lambda i: (0, 0)),
            pl.BlockSpec((d, LANES), lambda i: (0, 0)),
            pl.BlockSpec((LANES, 1), lambda i: (0, 0)),
            pl.BlockSpec((tm, tm), lambda i: (0, 0)),
        ],
        out_specs=[
            pl.BlockSpec((tm, d), lambda i: (i, 0)),
            pl.BlockSpec((8, tm), lambda i: (0, i)),
            pl.BlockSpec((tm, LANES), lambda i: (i, 0)),
            pl.BlockSpec((LANES, LANES), lambda i: (0, 0)),
        ],
        out_shape=[
            jax.ShapeDtypeStruct((t, d), F32),
            jax.ShapeDtypeStruct((8, t), jnp.int32),
            jax.ShapeDtypeStruct((t, LANES), F32),
            jax.ShapeDtypeStruct((LANES, LANES), F32),
        ],
        scratch_shapes=[pltpu.VMEM((LANES, LANES), F32), pltpu.VMEM((2 * LANES, d), BF16)],
        compiler_params=_cparams(("arbitrary",)),
        name="proj_moe_router",
    )(a, w_proj, res, ln_w.reshape(1, d), w_r, b_r, earlier)


U32 = jnp.uint32
ROW_WORDS = D_MODEL // 2
ROW_SUB = ROW_WORDS // LANES
HIGH_HALF = 0xFFFF0000
ROW_MOVE_TOKENS = 1024


def _pack_rows(v):
    lo = lax.bitcast_convert_type(v[:, :ROW_WORDS].astype(BF16).astype(F32), U32)
    hi = lax.bitcast_convert_type(v[:, ROW_WORDS:].astype(BF16).astype(F32), U32)
    return jnp.right_shift(lo, jnp.uint32(16)) | (hi & jnp.uint32(HIGH_HALF))


def _unpack_rows(w):
    lo = lax.bitcast_convert_type(jnp.left_shift(w, jnp.uint32(16)), F32)
    hi = lax.bitcast_convert_type(w & jnp.uint32(HIGH_HALF), F32)
    return lo, hi


def _store_rows(ref, packed):
    n = packed.shape[0]
    for c in range(ROW_SUB):
        ref[pl.ds(c, n, stride=ROW_SUB), :] = packed[:, c * LANES:(c + 1) * LANES]


def _load_rows(ref, n):
    return jnp.concatenate([ref[pl.ds(c, n, stride=ROW_SUB), :] for c in range(ROW_SUB)], axis=1)


def _row_slice(ref, row):
    return ref.at[pl.ds(pl.multiple_of(row * ROW_SUB, ROW_SUB), ROW_SUB)]


def _dispatch_kernel(zf_ref, dest_ref, x_ref, g_ref, xs_ref, hbuf, zbuf, sem, zsem):
    tb = x_ref.shape[0]
    n_blocks = zf_ref.shape[0]
    blk_rows = MOE_BLOCK * ROW_SUB

    @pl.when(pl.program_id(0) == 0)
    def _():
        zbuf[...] = jnp.zeros_like(zbuf)

        def zero_copy(b):
            start = pl.multiple_of(b * blk_rows, blk_rows)
            return pltpu.make_async_copy(zbuf, xs_ref.at[pl.ds(start, blk_rows)], zsem)

        def zissue(b, carry):
            @pl.when(zf_ref[b] != 0)
            def _():
                zero_copy(b).start()
            return carry

        def zdrain(b, carry):
            @pl.when(zf_ref[b] != 0)
            def _():
                zero_copy(b).wait()
            return carry

        lax.fori_loop(0, n_blocks, zissue, 0)
        lax.fori_loop(0, n_blocks, zdrain, 0)

    _store_rows(hbuf, _pack_rows(_rms(x_ref[...], g_ref[...])))
    for r in range(tb):
        for k in range(2):
            d = dest_ref[k, r]
            pltpu.make_async_copy(_row_slice(hbuf, r), _row_slice(xs_ref, d), sem).start(priority=k)
    for k in range(2):
        pltpu.make_async_copy(hbuf, xs_ref.at[pl.ds(0, tb * ROW_SUB)], sem).wait()


def _dispatch(x, ln_w, dest, zero_flag, n_slots, *, tb=ROW_MOVE_TOKENS):
    t, d = x.shape
    nblk = t // tb
    return pl.pallas_call(
        _dispatch_kernel,
        grid_spec=pltpu.PrefetchScalarGridSpec(
            num_scalar_prefetch=1,
            grid=(nblk,),
            in_specs=[
                pl.BlockSpec((2, tb), lambda i, zf: (0, i), memory_space=pltpu.SMEM),
                pl.BlockSpec((tb, d), lambda i, zf: (i, 0)),
                pl.BlockSpec((1, d), lambda i, zf: (0, 0)),
            ],
            out_specs=pl.BlockSpec(memory_space=pl.ANY),
            scratch_shapes=[pltpu.VMEM((tb * ROW_SUB, LANES), U32),
                            pltpu.VMEM((MOE_BLOCK * ROW_SUB, LANES), U32),
                            pltpu.SemaphoreType.DMA(()), pltpu.SemaphoreType.DMA(())],
        ),
        out_shape=jax.ShapeDtypeStruct((n_slots * ROW_SUB, LANES), U32),
        compiler_params=_cparams(("arbitrary",)),
        name="moe_dispatch",
    )(zero_flag, dest, x, ln_w.reshape(1, d))


def _expert_kernel(be_ref, bpos_ref, elist_ref, meta_ref, xs_ref, wg_hbm, wu_hbm, wd_hbm, y_ref,
                   wg_s, wu_s, wd_s, wg_f, wu_f, wd_f, sems, *, layer):
    i = pl.program_id(0)
    active = i < meta_ref[0]
    pos = bpos_ref[i]
    changed = jnp.logical_or(i == 0, be_ref[i] != be_ref[jnp.maximum(i - 1, 0)])
    streams = ((wg_hbm, wg_f), (wu_hbm, wu_f), (wd_hbm, wd_f))

    def fetch(p, slot):
        e = elist_ref[p]
        return [pltpu.make_async_copy(hbm.at[layer, e], buf.at[slot], sems.at[slot, t])
                for t, (hbm, buf) in enumerate(streams)]

    @pl.when(jnp.logical_and(active, i == 0))
    def _():
        for cp in fetch(0, 0):
            cp.start()

    @pl.when(jnp.logical_and(active, changed))
    def _():
        slot = lax.rem(pos, 2)
        for cp in fetch(pos, slot):
            cp.wait()

        @pl.when(pos + 1 < meta_ref[1])
        def _():
            for cp in fetch(pos + 1, 1 - slot):
                cp.start()

        wg_s[...] = wg_f[slot].astype(BF16)
        wu_s[...] = wu_f[slot].astype(BF16)
        wd_s[...] = wd_f[slot].astype(BF16)

    @pl.when(active)
    def _():
        lo, hi = _unpack_rows(_load_rows(xs_ref, MOE_BLOCK))
        xb = jnp.concatenate([lo.astype(BF16), hi.astype(BF16)], axis=1)
        gate = jnp.dot(xb, wg_s[...], preferred_element_type=F32)
        up = jnp.dot(xb, wu_s[...], preferred_element_type=F32)
        hid = (_silu(gate) * up).astype(BF16)
        _store_rows(y_ref, _pack_rows(jnp.dot(hid, wd_s[...], preferred_element_type=F32)))

    @pl.when(jnp.logical_not(active))
    def _():
        y_ref[...] = jnp.zeros_like(y_ref)


def _experts(xs, block_e, block_pos, expert_list, meta, w_gate, w_up, w_down, layer):
    blk_rows = MOE_BLOCK * ROW_SUB
    n_blocks = xs.shape[0] // blk_rows
    d = D_MODEL

    def xmap(i, be, bpos, elist, meta):
        return (jnp.maximum(jnp.minimum(i, meta[0] - 1), 0), 0)

    return pl.pallas_call(
        functools.partial(_expert_kernel, layer=layer),
        grid_spec=pltpu.PrefetchScalarGridSpec(
            num_scalar_prefetch=4,
            grid=(n_blocks,),
            in_specs=[
                pl.BlockSpec((blk_rows, LANES), xmap),
                pl.BlockSpec(memory_space=pl.ANY),
                pl.BlockSpec(memory_space=pl.ANY),
                pl.BlockSpec(memory_space=pl.ANY),
            ],
            out_specs=pl.BlockSpec((blk_rows, LANES), lambda i, be, bpos, elist, meta: (i, 0)),
            scratch_shapes=[
                pltpu.VMEM((d, EXPERT_FF), BF16),
                pltpu.VMEM((d, EXPERT_FF), BF16),
                pltpu.VMEM((EXPERT_FF, d), BF16),
                pltpu.VMEM((2, d, EXPERT_FF), F32),
                pltpu.VMEM((2, d, EXPERT_FF), F32),
                pltpu.VMEM((2, EXPERT_FF, d), F32),
                pltpu.SemaphoreType.DMA((2, 3)),
            ],
        ),
        out_shape=jax.ShapeDtypeStruct(xs.shape, U32),
        compiler_params=_cparams(("arbitrary",)),
        name="moe_experts",
    )(block_e, block_pos, expert_list, meta, xs, w_gate, w_up, w_down)


def _combine_kernel(dest_ref, dest_next_ref, x_ref, gates_ref, y_ref, *rest, final_norm):
    if final_norm:
        g_ref, o_ref, bufs, sem = rest
    else:
        o_ref, bufs, sem = rest
    tb = x_ref.shape[0]
    i = pl.program_id(0)

    def start_gathers(dref, slot):
        for r in range(tb):
            for k in range(2):
                d = dref[k, r]
                pltpu.make_async_copy(_row_slice(y_ref, d), _row_slice(bufs.at[slot, k], r),
                                      sem.at[slot]).start(priority=k)

    @pl.when(i == 0)
    def _():
        start_gathers(dest_ref, 0)

    @pl.when(i + 1 < pl.num_programs(0))
    def _():
        start_gathers(dest_next_ref, lax.rem(i + 1, 2))

    slot = lax.rem(i, 2)
    for k in range(2):
        pltpu.make_async_copy(y_ref.at[pl.ds(0, tb * ROW_SUB)], bufs.at[slot, k], sem.at[slot]).wait()

    gt = gates_ref[...]
    lo0, hi0 = _unpack_rows(_load_rows(bufs.at[slot, 0], tb))
    lo1, hi1 = _unpack_rows(_load_rows(bufs.at[slot, 1], tb))
    w0 = gt[:, 0:1]
    w1 = gt[:, 1:2]
    out = x_ref[...] + jnp.concatenate([lo0 * w0 + lo1 * w1, hi0 * w0 + hi1 * w1], axis=1)
    if final_norm:
        out = _rms(out, g_ref[...])
    o_ref[...] = out


def _combine(x, gates, y, dest, final_w=None, *, tb=ROW_MOVE_TOKENS):
    t, d = x.shape
    nblk = t // tb
    final_norm = final_w is not None
    in_specs = [
        pl.BlockSpec((2, tb), lambda i: (0, i), memory_space=pltpu.SMEM),
        pl.BlockSpec((2, tb), lambda i: (0, jnp.minimum(i + 1, nblk - 1)), memory_space=pltpu.SMEM),
        pl.BlockSpec((tb, d), lambda i: (i, 0)),
        pl.BlockSpec((tb, LANES), lambda i: (i, 0)),
        pl.BlockSpec(memory_space=pl.ANY),
    ]
    args = [dest, dest, x, gates, y]
    if final_norm:
        in_specs.append(pl.BlockSpec((1, d), lambda i: (0, 0)))
        args.append(final_w.reshape(1, d))
    return pl.pallas_call(
        functools.partial(_combine_kernel, final_norm=final_norm),
        grid=(nblk,),
        in_specs=in_specs,
        out_specs=pl.BlockSpec((tb, d), lambda i: (i, 0)),
        out_shape=jax.ShapeDtypeStruct((t, d), F32),
        scratch_shapes=[pltpu.VMEM((2, 2, tb * ROW_SUB, LANES), U32),
                        pltpu.SemaphoreType.DMA((2,))],
        compiler_params=_cparams(("arbitrary",)),
        name="moe_combine",
    )(*args)


def _proj_moe(a, w_proj, res, ln_w, w_group, b_group, w_expert, b_expert, w_gate, w_up, w_down, layer,
              final_w=None):
    t = res.shape[0]
    n_assign = t * 2
    n_blocks = -(-n_assign // MOE_BLOCK) + N_EXPERTS
    n_slots = n_blocks * MOE_BLOCK

    x, ids, gates, cnt = _router(a, w_proj, res, ln_w, w_group, b_group, w_expert, b_expert)
    counts = cnt[:N_EXPERTS, 0].astype(jnp.int32)
    padded = (counts + MOE_BLOCK - 1) // MOE_BLOCK * MOE_BLOCK
    p_ends = jnp.cumsum(padded)
    p_starts = p_ends - padded
    experts = jnp.arange(N_EXPERTS, dtype=jnp.int32)
    start_of = jnp.sum(jnp.where(ids[0:2, :, None] == experts, p_starts, 0), axis=-1)
    dest = (start_of + ids[2:4]).astype(jnp.int32)
    block_start = jnp.arange(n_blocks, dtype=jnp.int32) * MOE_BLOCK
    block_e = jnp.minimum(jnp.sum((p_ends[None, :] <= block_start[:, None]).astype(jnp.int32), axis=1),
                          N_EXPERTS - 1)
    n_active = (p_ends[-1:] // MOE_BLOCK).astype(jnp.int32)
    valid_end = jnp.sum(jnp.where(block_e[:, None] == experts, p_starts + counts, 0), axis=-1)
    zero_flag = jnp.logical_or(block_start + MOE_BLOCK > valid_end,
                               block_start >= p_ends[-1]).astype(jnp.int32)

    has_tokens = counts > 0
    list_pos = jnp.cumsum(has_tokens.astype(jnp.int32)) - 1
    expert_list = jnp.sum(jnp.where(jnp.logical_and(has_tokens[None, :], list_pos[None, :] == experts[:, None]),
                                    experts[None, :], 0), axis=1).astype(jnp.int32)
    block_pos = jnp.sum(jnp.where(block_e[:, None] == experts, list_pos, 0), axis=-1).astype(jnp.int32)
    meta = jnp.concatenate([n_active, jnp.sum(has_tokens.astype(jnp.int32), keepdims=True)]).astype(jnp.int32)

    xs = _dispatch(x, ln_w, dest, zero_flag, n_slots)
    y = _experts(xs, block_e, block_pos, expert_list, meta, w_gate, w_up, w_down, layer)
    return _combine(x, gates, y, dest, final_w)


def kernel(x, rel_bias, ln_mix, ln_ffn, ln_final, attn_w_qkv, attn_sinks, attn_w_o, ssm_w_in,
           ssm_conv_w, ssm_conv_b, ssm_dt_bias, ssm_a_log, ssm_d, ssm_norm_w, ssm_w_out,
           moe_w_group, moe_b_group, moe_w_expert, moe_b_expert, moe_w_gate, moe_w_up, moe_w_down):
    batch, seq, d = x.shape
    t = batch * seq
    xf = x.reshape(t, d).astype(F32)

    q_scale = jnp.where(jnp.arange(QKV_DIM) < Q_DIM, HEAD_DIM ** -0.5 * LOG2E, 1.0).astype(F32)
    qkv = _norm_matmul(xf, ln_mix[0], (attn_w_qkv[0] * q_scale).astype(BF16), tm=512, tn=QKV_DIM)
    att = _attention(qkv, attn_sinks[0], rel_bias, batch, seq)
    xf = _proj_moe(att, attn_w_o[0].astype(BF16), xf, ln_ffn[0], moe_w_group[0], moe_b_group[0],
                   moe_w_expert[0], moe_b_expert[0], moe_w_gate, moe_w_up, moe_w_down, 0)

    w_in = ssm_w_in[0]
    w_dt = jnp.pad(w_in[:, ZXBC_DIM:], ((0, 0), (0, LANES - SSM_HEADS))).astype(BF16)
    zxbc, dt_raw = _norm_matmul(xf, ln_mix[1], w_in[:, :ZXBC_DIM].astype(BF16), w_dt, tm=1024, tn=ZXBC_DIM)
    gated = _ssd(zxbc, dt_raw, ssm_conv_w[0], ssm_conv_b[0], ssm_dt_bias[0], ssm_a_log[0],
                 ssm_d[0], ssm_norm_w[0], batch, seq)
    xf = _proj_moe(gated, ssm_w_out[0].astype(BF16), xf, ln_ffn[1], moe_w_group[1], moe_b_group[1],
                   moe_w_expert[1], moe_b_expert[1], moe_w_gate, moe_w_up, moe_w_down, 1, final_w=ln_final)
    return xf.reshape(batch, seq, d).astype(x.dtype)
```

```python
import functools
import math

import jax
import jax.numpy as jnp
from jax import lax
from jax.experimental import pallas as pl
from jax.experimental.pallas import tpu as pltpu

F32 = jnp.float32
BF16 = jnp.bfloat16

D_MODEL = 1024
N_HEADS = 16
N_KV_HEADS = 4
HEAD_DIM = 64
GQA_GROUP = N_HEADS // N_KV_HEADS
WINDOW = 128
ATTN_BLOCK = 128
Q_DIM = N_HEADS * HEAD_DIM
KV_DIM = N_KV_HEADS * HEAD_DIM
QKV_DIM = Q_DIM + 2 * KV_DIM
REL_BUCKETS = 32
REL_MAX_DIST = 128

D_INNER = 2048
SSM_HEAD_DIM = 64
SSM_HEADS = D_INNER // SSM_HEAD_DIM
SSM_GROUPS = 4
D_STATE = 128
CONV_WIDTH = 4
BC_DIM = 2 * SSM_GROUPS * D_STATE
CONV_DIM = D_INNER + BC_DIM
ZXBC_DIM = D_INNER + CONV_DIM
SSM_CHUNK = 128
SSD_STEP_CHUNKS = 8
GROUP_CH = D_INNER // SSM_GROUPS

N_EXPERT_GROUPS = 8
EXPERTS_PER_GROUP = 8
N_EXPERTS = 64
EXPERT_FF = 512
MOE_BLOCK = 512

NORM_EPS = 1e-6
LOG2E = 1.4426950408889634
CONV_TAIL = 16
LANES = 128
NEG_BIG = -1e30
VMEM_LIMIT = 56 * 1024 * 1024


def _cparams(sem, flags=None):
    return pltpu.CompilerParams(dimension_semantics=sem, vmem_limit_bytes=VMEM_LIMIT, flags=flags)


def _rms(x, g):
    ms = jnp.mean(x * x, axis=-1, keepdims=True)
    return x * lax.rsqrt(ms + NORM_EPS) * g


def _silu(x):
    h = 0.5 * x
    return h + h * jnp.tanh(h)


def _norm_matmul_kernel(x_ref, g_ref, w_ref, *rest, has_aux):
    if has_aux:
        wa_ref, o_ref, oa_ref, h_scr = rest
    else:
        o_ref, h_scr = rest

    @pl.when(pl.program_id(1) == 0)
    def _():
        h = _rms(x_ref[...], g_ref[...]).astype(BF16)
        h_scr[...] = h
        if has_aux:
            oa_ref[...] = jnp.dot(h, wa_ref[...], preferred_element_type=F32)

    o_ref[...] = jnp.dot(h_scr[...], w_ref[...], preferred_element_type=F32).astype(o_ref.dtype)


def _norm_matmul(x, g, w, w_aux=None, *, tm=1024, tn=512):
    t, d = x.shape
    n = w.shape[1]
    has_aux = w_aux is not None
    in_specs = [
        pl.BlockSpec((tm, d), lambda i, j: (i, 0)),
        pl.BlockSpec((1, d), lambda i, j: (0, 0)),
        pl.BlockSpec((d, tn), lambda i, j: (0, j), pipeline_mode=pl.Buffered(1) if tn == n else None),
    ]
    out_shape = [jax.ShapeDtypeStruct((t, n), BF16)]
    out_specs = [pl.BlockSpec((tm, tn), lambda i, j: (i, j))]
    args = [x, g.reshape(1, d), w]
    if has_aux:
        na = w_aux.shape[1]
        in_specs.append(pl.BlockSpec((d, na), lambda i, j: (0, 0)))
        out_shape.append(jax.ShapeDtypeStruct((t, na), F32))
        out_specs.append(pl.BlockSpec((tm, na), lambda i, j: (i, 0)))
        args.append(w_aux)
    res = pl.pallas_call(
        functools.partial(_norm_matmul_kernel, has_aux=has_aux),
        grid=(t // tm, n // tn),
        in_specs=in_specs,
        out_specs=out_specs,
        out_shape=out_shape,
        scratch_shapes=[pltpu.VMEM((tm, d), BF16)],
        compiler_params=_cparams(("parallel", "arbitrary")),
        name="norm_matmul",
    )(*args)
    return res if has_aux else res[0]


def _attn_kernel(sinks_ref, q_ref, kp_ref, kc_ref, vp_ref, vc_ref, bias_ref, o_ref):
    table = jnp.minimum(pl.program_id(1), 1)
    for h in range(N_KV_HEADS):
        ks = slice(h * HEAD_DIM, (h + 1) * HEAD_DIM)
        kb = jnp.concatenate([kp_ref[:, ks], kc_ref[:, ks]], axis=0)
        vb = jnp.concatenate([vp_ref[:, ks], vc_ref[:, ks]], axis=0)
        for g in range(GQA_GROUP):
            hh = h * GQA_GROUP + g
            qh = q_ref[:, hh * HEAD_DIM:(hh + 1) * HEAD_DIM]
            s = lax.dot_general(qh, kb, (((1,), (1,)), ((), ())), preferred_element_type=F32)
            logits = s + bias_ref[table, hh]
            sink = sinks_ref[hh]
            m = jnp.maximum(jnp.max(logits, axis=-1, keepdims=True), sink)
            p = jnp.exp2(logits - m)
            denom = jnp.sum(p, axis=-1, keepdims=True) + jnp.exp2(sink - m)
            o = jnp.dot(p.astype(BF16), vb, preferred_element_type=F32) / denom
            o_ref[:, hh * HEAD_DIM:(hh + 1) * HEAD_DIM] = o.astype(o_ref.dtype)


def _t5_causal_bucket(dist):
    n = jnp.maximum(dist, 0)
    max_exact = REL_BUCKETS // 2
    nf = jnp.maximum(n, 1).astype(F32)
    large = max_exact + (jnp.log(nf / max_exact) / math.log(REL_MAX_DIST / max_exact)
                         * (REL_BUCKETS - max_exact)).astype(jnp.int32)
    large = jnp.minimum(large, REL_BUCKETS - 1)
    return jnp.where(n < max_exact, n, large)


def _attention(qkv, sinks, rel_bias, batch, seq):
    t = qkv.shape[0]
    nb = seq // ATTN_BLOCK
    qi = jnp.arange(ATTN_BLOCK)[:, None]
    ki = jnp.arange(2 * ATTN_BLOCK)[None, :]
    dist = qi + ATTN_BLOCK - ki
    in_window = (dist >= 0) & (dist < WINDOW)
    onehot = (_t5_causal_bucket(dist)[None] == jnp.arange(REL_BUCKETS)[:, None, None]).astype(F32)
    bias = jnp.einsum('hr,rqk->hqk', rel_bias.astype(F32).T, onehot, precision=lax.Precision.HIGHEST)
    bias = jnp.where(in_window[None], bias * LOG2E, NEG_BIG)
    bias_first = jnp.where((ki < ATTN_BLOCK)[None], NEG_BIG, bias)
    bias2 = jnp.stack([bias_first, bias])

    kcol = Q_DIM // KV_DIM
    vcol = kcol + 1

    def prev(b, n):
        return b * nb + jnp.maximum(n - 1, 0)

    return pl.pallas_call(
        _attn_kernel,
        grid=(batch, nb),
        in_specs=[
            pl.BlockSpec(memory_space=pltpu.SMEM),
            pl.BlockSpec((ATTN_BLOCK, Q_DIM), lambda b, n: (b * nb + n, 0)),
            pl.BlockSpec((ATTN_BLOCK, KV_DIM), lambda b, n: (prev(b, n), kcol)),
            pl.BlockSpec((ATTN_BLOCK, KV_DIM), lambda b, n: (b * nb + n, kcol)),
            pl.BlockSpec((ATTN_BLOCK, KV_DIM), lambda b, n: (prev(b, n), vcol)),
            pl.BlockSpec((ATTN_BLOCK, KV_DIM), lambda b, n: (b * nb + n, vcol)),
            pl.BlockSpec((2, N_HEADS, ATTN_BLOCK, 2 * ATTN_BLOCK), lambda b, n: (0, 0, 0, 0)),
        ],
        out_specs=pl.BlockSpec((ATTN_BLOCK, Q_DIM), lambda b, n: (b * nb + n, 0)),
        out_shape=jax.ShapeDtypeStruct((t, Q_DIM), BF16),
        compiler_params=_cparams(("parallel", "arbitrary")),
        name="swa_attention",
    )((sinks.astype(F32) * LOG2E), qkv, qkv, qkv, qkv, qkv, bias2)


def _split3(v):
    hi = v.astype(BF16)
    r = v - hi.astype(F32)
    mid = r.astype(BF16)
    lo = (r - mid.astype(F32)).astype(BF16)
    return hi, mid, lo


def _ssd_kernel(z_ref, x_ref, bc_ref, dt_ref, convw_ref, convb_ref, dtb_ref, alog_ref,
                dskip_ref, normw_ref, shift_ref, shift_tail_ref, expand_ref,
                o_ref, state, xtail, bctail):
    @pl.when(pl.program_id(1) == 0)
    def _():
        state[...] = jnp.zeros_like(state)
        xtail[...] = jnp.zeros_like(xtail)
        bctail[...] = jnp.zeros_like(bctail)

    for k in range(z_ref.shape[0] // SSM_CHUNK):
        _ssd_chunk(pl.ds(k * SSM_CHUNK, SSM_CHUNK), z_ref, x_ref, bc_ref, dt_ref, convw_ref, convb_ref,
                   dtb_ref, alog_ref, dskip_ref, normw_ref, shift_ref, shift_tail_ref, expand_ref,
                   o_ref, state, xtail, bctail)


def _ssd_chunk(rows, z_ref, x_ref, bc_ref, dt_ref, convw_ref, convb_ref, dtb_ref, alog_ref,
               dskip_ref, normw_ref, shift_ref, shift_tail_ref, expand_ref,
               o_ref, state, xtail, bctail):
    L = SSM_CHUNK

    def conv_silu(u_ref, tail, w_lo, width):
        cur = u_ref[rows, :]
        prev = tail[...]
        w = [convw_ref[k:k + 1, w_lo:w_lo + width].astype(BF16) for k in range(CONV_WIDTH)]
        taps = jnp.concatenate([cur * w[k] for k in range(CONV_WIDTH)], axis=0)
        taps_prev = jnp.concatenate([prev * w[k] for k in range(CONV_WIDTH)], axis=0)
        y = (jnp.dot(shift_ref[...], taps, preferred_element_type=F32)
             + jnp.dot(shift_tail_ref[...], taps_prev, preferred_element_type=F32)
             + convb_ref[:, w_lo:w_lo + width])
        tail[...] = cur[L - CONV_TAIL:L]
        return _silu(y)

    xs = conv_silu(x_ref, xtail, 0, D_INNER)
    bcm = conv_silu(bc_ref, bctail, D_INNER, BC_DIM)

    xdt = dt_ref[rows, :] + dtb_ref[...]
    dt = jnp.maximum(xdt, 0.0) + jnp.log1p(jnp.exp(-jnp.abs(xdt)))
    a = dt * (-jnp.exp(alog_ref[...]) * LOG2E)
    row = lax.broadcasted_iota(jnp.int32, (L, L), 0)
    colm = lax.broadcasted_iota(jnp.int32, (L, L), 1)
    causal = row >= colm
    cs3 = jnp.dot(causal.astype(BF16), jnp.concatenate(_split3(a), axis=1), preferred_element_type=F32)
    a_cs = cs3[:, :LANES] + cs3[:, LANES:2 * LANES] + cs3[:, 2 * LANES:]
    a_cs_t = a_cs.T
    dt_t = dt.T
    a_last_col = a_cs_t[:, L - 1:L]
    w_end_t = jnp.exp2(a_last_col - a_cs_t) * dt_t
    exp_acs = jnp.exp2(a_cs)

    cd3 = [jnp.broadcast_to(part.astype(F32), (16, LANES)).astype(BF16)
           for part in _split3(exp_acs[L - 1:L, :])]
    cdx = jnp.dot(jnp.concatenate(cd3, axis=0), expand_ref[...], preferred_element_type=F32)
    cd_exp = cdx[0:1] + cdx[16:17] + cdx[32:33]

    lane = lax.broadcasted_iota(jnp.int32, (L, LANES), 1)
    low_half = lane < SSM_HEAD_DIM

    y_tiles = []
    for g in range(SSM_GROUPS):
        b_g = bcm[:, g * D_STATE:(g + 1) * D_STATE]
        c_g = bcm[:, (SSM_GROUPS + g) * D_STATE:(SSM_GROUPS + g + 1) * D_STATE]
        b_gb = b_g.astype(BF16)
        c_gb = c_g.astype(BF16)
        cb = lax.dot_general(c_gb, b_gb, (((1,), (1,)), ((), ())), preferred_element_type=F32)
        b_gt = b_g.T
        s_g = state[:, g * GROUP_CH:(g + 1) * GROUP_CH]
        y_off_g = jnp.dot(c_gb, s_g.astype(BF16), preferred_element_type=F32)
        for jj in range(GROUP_CH // LANES):
            j = g * (GROUP_CH // LANES) + jj
            lhs_top = []
            lhs_bot = []
            escale = []
            for e in (2 * j, 2 * j + 1):
                acs_b = jnp.broadcast_to(a_cs[:, e:e + 1], (L, L))
                seg = acs_b - a_cs_t[e:e + 1, :]
                dec = jnp.exp2(jnp.where(causal, seg, -jnp.inf))
                lhs_top.append((cb * dec * dt_t[e:e + 1, :]).astype(BF16))
                lhs_bot.append((b_gt * w_end_t[e:e + 1, :]).astype(BF16))
                escale.append(jnp.exp2(acs_b))
            x_tile = xs[:, j * LANES:(j + 1) * LANES]
            x_lo = jnp.where(low_half, x_tile, 0.0).astype(BF16)
            x_hi = jnp.where(low_half, 0.0, x_tile).astype(BF16)
            lhs = jnp.concatenate([jnp.concatenate(lhs_top, axis=1),
                                   jnp.concatenate(lhs_bot, axis=1)], axis=0)
            rhs = jnp.concatenate([x_lo, x_hi], axis=0)
            r = jnp.dot(lhs, rhs, preferred_element_type=F32)
            y_off = y_off_g[:, jj * LANES:(jj + 1) * LANES]
            y_tiles.append(r[0:L] + jnp.where(low_half, escale[0], escale[1]) * y_off)
            sl = slice(j * LANES, (j + 1) * LANES)
            state[:, sl] = cd_exp[:, sl] * state[:, sl] + r[L:2 * L]

    y = jnp.concatenate(y_tiles, axis=1)
    y = y + xs * dskip_ref[...]
    gated = y * _silu(z_ref[rows, :].astype(F32))
    outs = []
    for g in range(SSM_GROUPS):
        gg = gated[:, g * GROUP_CH:(g + 1) * GROUP_CH]
        ms = jnp.mean(gg * gg, axis=-1, keepdims=True)
        outs.append(gg * lax.rsqrt(ms + NORM_EPS))
    o_ref[rows, :] = (jnp.concatenate(outs, axis=1) * normw_ref[...]).astype(o_ref.dtype)


def _ssd(zxbc, dt_raw, conv_w, conv_b, dt_bias, a_log, d_skip, norm_w, batch, seq):
    t = zxbc.shape[0]
    L = SSM_CHUNK
    rows = SSD_STEP_CHUNKS * L
    nc = seq // rows

    def pad_heads(v):
        return jnp.pad(v.astype(F32), (0, LANES - SSM_HEADS)).reshape(1, LANES)

    def rowmap(col):
        return lambda b, c: (b * nc + c, col)

    def const2(b, c):
        return (0, 0)

    l_idx = jnp.arange(L)[:, None]
    shift = jnp.concatenate(
        [(jnp.arange(L)[None, :] == l_idx - (CONV_WIDTH - 1) + k) for k in range(CONV_WIDTH)],
        axis=1).astype(BF16)
    shift_tail = jnp.concatenate(
        [(jnp.arange(-CONV_TAIL, 0)[None, :] == l_idx - (CONV_WIDTH - 1) + k) for k in range(CONV_WIDTH)],
        axis=1).astype(BF16)
    expand = (jnp.arange(D_INNER)[None, :] // SSM_HEAD_DIM == jnp.arange(LANES)[:, None]).astype(BF16)

    return pl.pallas_call(
        _ssd_kernel,
        grid=(batch, nc),
        in_specs=[
            pl.BlockSpec((rows, D_INNER), rowmap(0)),
            pl.BlockSpec((rows, D_INNER), rowmap(1)),
            pl.BlockSpec((rows, BC_DIM), rowmap(2 * D_INNER // BC_DIM)),
            pl.BlockSpec((rows, LANES), rowmap(0)),
            pl.BlockSpec((CONV_WIDTH, CONV_DIM), const2),
            pl.BlockSpec((1, CONV_DIM), const2),
            pl.BlockSpec((1, LANES), const2),
            pl.BlockSpec((1, LANES), const2),
            pl.BlockSpec((1, D_INNER), const2),
            pl.BlockSpec((1, D_INNER), const2),
            pl.BlockSpec((L, CONV_WIDTH * L), const2),
            pl.BlockSpec((L, CONV_WIDTH * CONV_TAIL), const2),
            pl.BlockSpec((LANES, D_INNER), const2),
        ],
        out_specs=pl.BlockSpec((rows, D_INNER), rowmap(0)),
        out_shape=jax.ShapeDtypeStruct((t, D_INNER), BF16),
        scratch_shapes=[
            pltpu.VMEM((D_STATE, D_INNER), F32),
            pltpu.VMEM((CONV_TAIL, D_INNER), BF16),
            pltpu.VMEM((CONV_TAIL, BC_DIM), BF16),
        ],
        compiler_params=_cparams(("parallel", "arbitrary")),
        name="ssd_core",
    )(zxbc, zxbc, zxbc, dt_raw, conv_w.astype(F32), conv_b.astype(F32).reshape(1, CONV_DIM),
      pad_heads(dt_bias), pad_heads(a_log),
      jnp.repeat(d_skip.astype(F32), SSM_HEAD_DIM).reshape(1, D_INNER),
      norm_w.astype(F32).reshape(1, D_INNER), shift, shift_tail, expand)


GROUP_ROW0 = N_EXPERTS
GROUP_SHIFT = EXPERTS_PER_GROUP.bit_length() - 1
assert 1 << GROUP_SHIFT == EXPERTS_PER_GROUP


def _router_kernel(a_ref, wp_ref, res_ref, g_ref, w_ref, b_ref, earlier_ref,
                   x_ref, ids_ref, gates_ref, cnt_ref, carry, wt_split):
    i = pl.program_id(0)
    x_ref[...] = res_ref[...] + jnp.dot(a_ref[...], wp_ref[...], preferred_element_type=F32)

    @pl.when(i == 0)
    def _():
        carry[...] = jnp.zeros_like(carry)
        wt = w_ref[...].T
        wt_hi = wt.astype(BF16)
        wt_split[0:LANES, :] = wt_hi
        wt_split[LANES:, :] = (wt - wt_hi.astype(F32)).astype(BF16)

    h = _rms(x_ref[...], g_ref[...])
    h_hi = h.astype(BF16)
    h_mid = (h - h_hi.astype(F32)).astype(BF16)
    nt = (((1,), (1,)), ((), ()))
    top = lax.dot_general(wt_split[...], h_hi, nt, preferred_element_type=F32)
    cross = lax.dot_general(wt_split[0:LANES, :], h_mid, nt, preferred_element_type=F32)
    logits = top[:LANES] + (top[LANES:] + cross) + b_ref[...]
    tm = logits.shape[1]
    row = lax.broadcasted_iota(jnp.int32, (LANES, tm), 0)
    ninf = -jnp.inf

    def first_argmax(v, vmax):
        return jnp.min(jnp.where(v == vmax, row, LANES), axis=0, keepdims=True)

    is_group = (row >= GROUP_ROW0) & (row < GROUP_ROW0 + N_EXPERT_GROUPS)
    glog = jnp.where(is_group, logits, ninf)
    gmax = jnp.max(glog, axis=0, keepdims=True)
    g_idx = first_argmax(glog, gmax) - GROUP_ROW0
    g_p = 1.0 / jnp.sum(jnp.exp(glog - gmax), axis=0, keepdims=True)

    in_group = (row < N_EXPERTS) & (jnp.right_shift(row, GROUP_SHIFT) == g_idx)
    elog = jnp.where(in_group, logits, ninf)
    m1 = jnp.max(elog, axis=0, keepdims=True)
    e1 = first_argmax(elog, m1)
    elog2 = jnp.where(row == e1, ninf, elog)
    m2 = jnp.max(elog2, axis=0, keepdims=True)
    e2 = first_argmax(elog2, m2)
    zsum = jnp.sum(jnp.exp(elog - m1), axis=0, keepdims=True)
    p1 = 1.0 / zsum
    p2 = jnp.exp(m2 - m1) / zsum
    psum = p1 + p2
    w1 = p1 / psum * g_p
    w2 = p2 / psum * g_p

    oh1 = row == e1
    oh2 = row == e2
    onehot = jnp.logical_or(oh1, oh2).astype(F32)
    before = jnp.dot(onehot.astype(BF16), earlier_ref[...], preferred_element_type=F32) + carry[:, 0:1]
    r1 = jnp.sum(jnp.where(oh1, before, 0.0), axis=0, keepdims=True).astype(jnp.int32)
    r2 = jnp.sum(jnp.where(oh2, before, 0.0), axis=0, keepdims=True).astype(jnp.int32)
    carry[...] = carry[...] + jnp.sum(onehot, axis=1, keepdims=True)

    row8 = lax.broadcasted_iota(jnp.int32, (8, tm), 0)
    ids_ref[...] = jnp.where(row8 == 0, e1, jnp.where(row8 == 1, e2,
                             jnp.where(row8 == 2, r1, jnp.where(row8 == 3, r2, 0))))
    gates_ref[...] = jnp.where(row == 0, w1, jnp.where(row == 1, w2, 0.0)).T
    cnt_ref[...] = carry[...]


def _router(a, w_proj, res, ln_w, w_group, b_group, w_expert, b_expert, *, tm=512):
    t, d = res.shape
    k = a.shape[1]
    pad = LANES - N_EXPERTS - N_EXPERT_GROUPS
    w_r = jnp.concatenate([w_expert, w_group, jnp.zeros((d, pad), F32)], axis=1).astype(F32)
    b_r = jnp.concatenate([b_expert, b_group, jnp.zeros((pad,), F32)]).astype(F32).reshape(LANES, 1)
    earlier = (jnp.arange(tm)[:, None] < jnp.arange(tm)[None, :]).astype(BF16)
    return pl.pallas_call(
        _router_kernel,
        grid=(t // tm,),
        in_specs=[
            pl.BlockSpec((tm, k), lambda i: (i, 0)),
            pl.BlockSpec((k, d), lambda i: (0, 0)),
            pl.BlockSpec((tm, d), lambda i: (i, 0)),
            pl.BlockSpec((1, d), lambda i: (0, 0)),
            pl.BlockSpec((d, LANES), lambda i: (0, 0)),
            pl.BlockSpec((LANES, 1), lambda i: (0, 0)),
            pl.BlockSpec((tm, tm), lambda i: (0, 0)),
        ],
        out_specs=[
            pl.BlockSpec((tm, d), lambda i: (i, 0)),
            pl.BlockSpec((8, tm), lambda i: (0, i)),
            pl.BlockSpec((tm, LANES), lambda i: (i, 0)),
            pl.BlockSpec((LANES, LANES), lambda i: (0, 0)),
        ],
        out_shape=[
            jax.ShapeDtypeStruct((t, d), F32),
            jax.ShapeDtypeStruct((8, t), jnp.int32),
            jax.ShapeDtypeStruct((t, LANES), F32),
            jax.ShapeDtypeStruct((LANES, LANES), F32),
        ],
        scratch_shapes=[pltpu.VMEM((LANES, LANES), F32), pltpu.VMEM((2 * LANES, d), BF16)],
        compiler_params=_cparams(("arbitrary",)),
        name="proj_moe_router",
    )(a, w_proj, res, ln_w.reshape(1, d), w_r, b_r, earlier)


U32 = jnp.uint32
ROW_WORDS = D_MODEL // 2
ROW_SUB = ROW_WORDS // LANES
HIGH_HALF = 0xFFFF0000
ROW_MOVE_TOKENS = 1024


def _pack_rows(v):
    lo = lax.bitcast_convert_type(v[:, :ROW_WORDS].astype(BF16).astype(F32), U32)
    hi = lax.bitcast_convert_type(v[:, ROW_WORDS:].astype(BF16).astype(F32), U32)
    return jnp.right_shift(lo, jnp.uint32(16)) | (hi & jnp.uint32(HIGH_HALF))


def _unpack_rows(w):
    lo = lax.bitcast_convert_type(jnp.left_shift(w, jnp.uint32(16)), F32)
    hi = lax.bitcast_convert_type(w & jnp.uint32(HIGH_HALF), F32)
    return lo, hi


def _store_rows(ref, packed):
    n = packed.shape[0]
    for c in range(ROW_SUB):
        ref[pl.ds(c, n, stride=ROW_SUB), :] = packed[:, c * LANES:(c + 1) * LANES]


def _load_rows(ref, n):
    return jnp.concatenate([ref[pl.ds(c, n, stride=ROW_SUB), :] for c in range(ROW_SUB)], axis=1)


def _row_slice(ref, row):
    return ref.at[pl.ds(pl.multiple_of(row * ROW_SUB, ROW_SUB), ROW_SUB)]


def _dispatch_kernel(zf_ref, dest_ref, x_ref, g_ref, xs_ref, hbuf, zbuf, sem, zsem):
    tb = x_ref.shape[0]
    n_blocks = zf_ref.shape[0]
    blk_rows = MOE_BLOCK * ROW_SUB

    @pl.when(pl.program_id(0) == 0)
    def _():
        zbuf[...] = jnp.zeros_like(zbuf)

        def zero_copy(b):
            start = pl.multiple_of(b * blk_rows, blk_rows)
            return pltpu.make_async_copy(zbuf, xs_ref.at[pl.ds(start, blk_rows)], zsem)

        def zissue(b, carry):
            @pl.when(zf_ref[b] != 0)
            def _():
                zero_copy(b).start()
            return carry

        def zdrain(b, carry):
            @pl.when(zf_ref[b] != 0)
            def _():
                zero_copy(b).wait()
            return carry

        lax.fori_loop(0, n_blocks, zissue, 0)
        lax.fori_loop(0, n_blocks, zdrain, 0)

    _store_rows(hbuf, _pack_rows(_rms(x_ref[...], g_ref[...])))
    for r in range(tb):
        for k in range(2):
            d = dest_ref[k, r]
            pltpu.make_async_copy(_row_slice(hbuf, r), _row_slice(xs_ref, d), sem).start(priority=k)
    for k in range(2):
        pltpu.make_async_copy(hbuf, xs_ref.at[pl.ds(0, tb * ROW_SUB)], sem).wait()


def _dispatch(x, ln_w, dest, zero_flag, n_slots, *, tb=ROW_MOVE_TOKENS):
    t, d = x.shape
    nblk = t // tb
    return pl.pallas_call(
        _dispatch_kernel,
        grid_spec=pltpu.PrefetchScalarGridSpec(
            num_scalar_prefetch=1,
            grid=(nblk,),
            in_specs=[
                pl.BlockSpec((2, tb), lambda i, zf: (0, i), memory_space=pltpu.SMEM),
                pl.BlockSpec((tb, d), lambda i, zf: (i, 0)),
                pl.BlockSpec((1, d), lambda i, zf: (0, 0)),
            ],
            out_specs=pl.BlockSpec(memory_space=pl.ANY),
            scratch_shapes=[pltpu.VMEM((tb * ROW_SUB, LANES), U32),
                            pltpu.VMEM((MOE_BLOCK * ROW_SUB, LANES), U32),
                            pltpu.SemaphoreType.DMA(()), pltpu.SemaphoreType.DMA(())],
        ),
        out_shape=jax.ShapeDtypeStruct((n_slots * ROW_SUB, LANES), U32),
        compiler_params=_cparams(("arbitrary",)),
        name="moe_dispatch",
    )(zero_flag, dest, x, ln_w.reshape(1, d))


def _expert_kernel(be_ref, bpos_ref, elist_ref, meta_ref, xs_ref, wg_hbm, wu_hbm, wd_hbm, y_ref,
                   wg_s, wu_s, wd_s, wg_f, wu_f, wd_f, sems, *, layer):
    i = pl.program_id(0)
    active = i < meta_ref[0]
    pos = bpos_ref[i]
    changed = jnp.logical_or(i == 0, be_ref[i] != be_ref[jnp.maximum(i - 1, 0)])
    streams = ((wg_hbm, wg_f), (wu_hbm, wu_f), (wd_hbm, wd_f))

    def fetch(p, slot):
        e = elist_ref[p]
        return [pltpu.make_async_copy(hbm.at[layer, e], buf.at[slot], sems.at[slot, t])
                for t, (hbm, buf) in enumerate(streams)]

    @pl.when(jnp.logical_and(active, i == 0))
    def _():
        for cp in fetch(0, 0):
            cp.start()

    @pl.when(jnp.logical_and(active, changed))
    def _():
        slot = lax.rem(pos, 2)
        for cp in fetch(pos, slot):
            cp.wait()

        @pl.when(pos + 1 < meta_ref[1])
        def _():
            for cp in fetch(pos + 1, 1 - slot):
                cp.start()

        wg_s[...] = wg_f[slot].astype(BF16)
        wu_s[...] = wu_f[slot].astype(BF16)
        wd_s[...] = wd_f[slot].astype(BF16)

    @pl.when(active)
    def _():
        lo, hi = _unpack_rows(_load_rows(xs_ref, MOE_BLOCK))
        xb = jnp.concatenate([lo.astype(BF16), hi.astype(BF16)], axis=1)
        gate = jnp.dot(xb, wg_s[...], preferred_element_type=F32)
        up = jnp.dot(xb, wu_s[...], preferred_element_type=F32)
        hid = (_silu(gate) * up).astype(BF16)
        _store_rows(y_ref, _pack_rows(jnp.dot(hid, wd_s[...], preferred_element_type=F32)))

    @pl.when(jnp.logical_not(active))
    def _():
        y_ref[...] = jnp.zeros_like(y_ref)


def _experts(xs, block_e, block_pos, expert_list, meta, w_gate, w_up, w_down, layer):
    blk_rows = MOE_BLOCK * ROW_SUB
    n_blocks = xs.shape[0] // blk_rows
    d = D_MODEL

    def xmap(i, be, bpos, elist, meta):
        return (jnp.maximum(jnp.minimum(i, meta[0] - 1), 0), 0)

    return pl.pallas_call(
        functools.partial(_expert_kernel, layer=layer),
        grid_spec=pltpu.PrefetchScalarGridSpec(
            num_scalar_prefetch=4,
            grid=(n_blocks,),
            in_specs=[
                pl.BlockSpec((blk_rows, LANES), xmap),
                pl.BlockSpec(memory_space=pl.ANY),
                pl.BlockSpec(memory_space=pl.ANY),
                pl.BlockSpec(memory_space=pl.ANY),
            ],
            out_specs=pl.BlockSpec((blk_rows, LANES), lambda i, be, bpos, elist, meta: (i, 0)),
            scratch_shapes=[
                pltpu.VMEM((d, EXPERT_FF), BF16),
                pltpu.VMEM((d, EXPERT_FF), BF16),
                pltpu.VMEM((EXPERT_FF, d), BF16),
                pltpu.VMEM((2, d, EXPERT_FF), F32),
                pltpu.VMEM((2, d, EXPERT_FF), F32),
                pltpu.VMEM((2, EXPERT_FF, d), F32),
                pltpu.SemaphoreType.DMA((2, 3)),
            ],
        ),
        out_shape=jax.ShapeDtypeStruct(xs.shape, U32),
        compiler_params=_cparams(("arbitrary",)),
        name="moe_experts",
    )(block_e, block_pos, expert_list, meta, xs, w_gate, w_up, w_down)


def _combine_kernel(dest_ref, dest_next_ref, x_ref, gates_ref, y_ref, *rest, final_norm):
    if final_norm:
        g_ref, o_ref, bufs, sem = rest
    else:
        o_ref, bufs, sem = rest
    tb = x_ref.shape[0]
    i = pl.program_id(0)

    def start_gathers(dref, slot):
        for r in range(tb):
            for k in range(2):
                d = dref[k, r]
                pltpu.make_async_copy(_row_slice(y_ref, d), _row_slice(bufs.at[slot, k], r),
                                      sem.at[slot]).start(priority=k)

    @pl.when(i == 0)
    def _():
        start_gathers(dest_ref, 0)

    @pl.when(i + 1 < pl.num_programs(0))
    def _():
        start_gathers(dest_next_ref, lax.rem(i + 1, 2))

    slot = lax.rem(i, 2)
    for k in range(2):
        pltpu.make_async_copy(y_ref.at[pl.ds(0, tb * ROW_SUB)], bufs.at[slot, k], sem.at[slot]).wait()

    gt = gates_ref[...]
    lo0, hi0 = _unpack_rows(_load_rows(bufs.at[slot, 0], tb))
    lo1, hi1 = _unpack_rows(_load_rows(bufs.at[slot, 1], tb))
    w0 = gt[:, 0:1]
    w1 = gt[:, 1:2]
    out = x_ref[...] + jnp.concatenate([lo0 * w0 + lo1 * w1, hi0 * w0 + hi1 * w1], axis=1)
    if final_norm:
        out = _rms(out, g_ref[...])
    o_ref[...] = out


def _combine(x, gates, y, dest, final_w=None, *, tb=ROW_MOVE_TOKENS):
    t, d = x.shape
    nblk = t // tb
    final_norm = final_w is not None
    in_specs = [
        pl.BlockSpec((2, tb), lambda i: (0, i), memory_space=pltpu.SMEM),
        pl.BlockSpec((2, tb), lambda i: (0, jnp.minimum(i + 1, nblk - 1)), memory_space=pltpu.SMEM),
        pl.BlockSpec((tb, d), lambda i: (i, 0)),
        pl.BlockSpec((tb, LANES), lambda i: (i, 0)),
        pl.BlockSpec(memory_space=pl.ANY),
    ]
    args = [dest, dest, x, gates, y]
    if final_norm:
        in_specs.append(pl.BlockSpec((1, d), lambda i: (0, 0)))
        args.append(final_w.reshape(1, d))
    return pl.pallas_call(
        functools.partial(_combine_kernel, final_norm=final_norm),
        grid=(nblk,),
        in_specs=in_specs,
        out_specs=pl.BlockSpec((tb, d), lambda i: (i, 0)),
        out_shape=jax.ShapeDtypeStruct((t, d), F32),
        scratch_shapes=[pltpu.VMEM((2, 2, tb * ROW_SUB, LANES), U32),
                        pltpu.SemaphoreType.DMA((2,))],
        compiler_params=_cparams(("arbitrary",)),
        name="moe_combine",
    )(*args)


def _proj_moe(a, w_proj, res, ln_w, w_group, b_group, w_expert, b_expert, w_gate, w_up, w_down, layer,
              final_w=None):
    t = res.shape[0]
    n_assign = t * 2
    n_blocks = -(-n_assign // MOE_BLOCK) + N_EXPERTS
    n_slots = n_blocks * MOE_BLOCK

    x, ids, gates, cnt = _router(a, w_proj, res, ln_w, w_group, b_group, w_expert, b_expert)
    counts = cnt[:N_EXPERTS, 0].astype(jnp.int32)
    padded = (counts + MOE_BLOCK - 1) // MOE_BLOCK * MOE_BLOCK
    p_ends = jnp.cumsum(padded)
    p_starts = p_ends - padded
    experts = jnp.arange(N_EXPERTS, dtype=jnp.int32)
    start_of = jnp.sum(jnp.where(ids[0:2, :, None] == experts, p_starts, 0), axis=-1)
    dest = (start_of + ids[2:4]).astype(jnp.int32)
    block_start = jnp.arange(n_blocks, dtype=jnp.int32) * MOE_BLOCK
    block_e = jnp.minimum(jnp.sum((p_ends[None, :] <= block_start[:, None]).astype(jnp.int32), axis=1),
                          N_EXPERTS - 1)
    n_active = (p_ends[-1:] // MOE_BLOCK).astype(jnp.int32)
    valid_end = jnp.sum(jnp.where(block_e[:, None] == experts, p_starts + counts, 0), axis=-1)
    zero_flag = jnp.logical_or(block_start + MOE_BLOCK > valid_end,
                               block_start >= p_ends[-1]).astype(jnp.int32)

    has_tokens = counts > 0
    list_pos = jnp.cumsum(has_tokens.astype(jnp.int32)) - 1
    expert_list = jnp.sum(jnp.where(jnp.logical_and(has_tokens[None, :], list_pos[None, :] == experts[:, None]),
                                    experts[None, :], 0), axis=1).astype(jnp.int32)
    block_pos = jnp.sum(jnp.where(block_e[:, None] == experts, list_pos, 0), axis=-1).astype(jnp.int32)
    meta = jnp.concatenate([n_active, jnp.sum(has_tokens.astype(jnp.int32), keepdims=True)]).astype(jnp.int32)

    xs = _dispatch(x, ln_w, dest, zero_flag, n_slots)
    y = _experts(xs, block_e, block_pos, expert_list, meta, w_gate, w_up, w_down, layer)
    return _combine(x, gates, y, dest, final_w)


def kernel(x, rel_bias, ln_mix, ln_ffn, ln_final, attn_w_qkv, attn_sinks, attn_w_o, ssm_w_in,
           ssm_conv_w, ssm_conv_b, ssm_dt_bias, ssm_a_log, ssm_d, ssm_norm_w, ssm_w_out,
           moe_w_group, moe_b_group, moe_w_expert, moe_b_expert, moe_w_gate, moe_w_up, moe_w_down):
    batch, seq, d = x.shape
    t = batch * seq
    xf = x.reshape(t, d).astype(F32)

    q_scale = jnp.where(jnp.arange(QKV_DIM) < Q_DIM, HEAD_DIM ** -0.5 * LOG2E, 1.0).astype(F32)
    qkv = _norm_matmul(xf, ln_mix[0], (attn_w_qkv[0] * q_scale).astype(BF16), tm=512, tn=QKV_DIM)
    att = _attention(qkv, attn_sinks[0], rel_bias, batch, seq)
    xf = _proj_moe(att, attn_w_o[0].astype(BF16), xf, ln_ffn[0], moe_w_group[0], moe_b_group[0],
                   moe_w_expert[0], moe_b_expert[0], moe_w_gate, moe_w_up, moe_w_down, 0)

    w_in = ssm_w_in[0]
    w_dt = jnp.pad(w_in[:, ZXBC_DIM:], ((0, 0), (0, LANES - SSM_HEADS))).astype(BF16)
    zxbc, dt_raw = _norm_matmul(xf, ln_mix[1], w_in[:, :ZXBC_DIM].astype(BF16), w_dt, tm=1024, tn=ZXBC_DIM)
    gated = _ssd(zxbc, dt_raw, ssm_conv_w[0], ssm_conv_b[0], ssm_dt_bias[0], ssm_a_log[0],
                 ssm_d[0], ssm_norm_w[0], batch, seq)
    xf = _proj_moe(gated, ssm_w_out[0].astype(BF16), xf, ln_ffn[1], moe_w_group[1], moe_b_group[1],
                   moe_w_expert[1], moe_b_expert[1], moe_w_gate, moe_w_up, moe_w_down, 1, final_w=ln_final)
    return xf.reshape(batch, seq, d).astype(x.dtype)
```
